```python
import jax, jax.numpy as jnp
from jax import lax
import numpy as np

D_MODEL = 1024
BATCH = 4
SEQ = 8192
DEPTH = 2
DEC_BATCH = 32
DEC_SEQ = 64
PAST_LEN = 1024

CHUNK = 64
Q_BLOCK = 128
N_MIXERS = 2
EPS = 1e-6
NEG_INF = -1e30

FOX_HEADS = 16
FOX_HEAD_DIM = 64
FOX_WIDTH = FOX_HEADS * FOX_HEAD_DIM
FOX_IN = 4 * FOX_WIDTH + FOX_HEADS

GLA_HEADS = 4
GLA_KEY_WIDTH = D_MODEL // 2
GLA_VAL_WIDTH = D_MODEL
GLA_HEAD_K = GLA_KEY_WIDTH // GLA_HEADS
GLA_HEAD_V = GLA_VAL_WIDTH // GLA_HEADS
GLA_GATE_RANK = 16
GLA_GATE_TEMP = 16.0
GLA_IN = 2 * GLA_KEY_WIDTH + 2 * GLA_VAL_WIDTH + GLA_GATE_RANK

kernel_name = "fox_gla_streaming_encoder_step"


def rmsnorm(x, g):
    xf = x.astype(jnp.float32)
    y = xf * lax.rsqrt(jnp.mean(xf * xf, axis=-1, keepdims=True) + EPS)
    return (y * g.astype(jnp.float32)).astype(x.dtype)


def fox_project(h, w_in, b_f):
    B, L, _ = h.shape
    z = h @ w_in
    q, k, v, gate, fl = jnp.split(z, [FOX_WIDTH, 2 * FOX_WIDTH, 3 * FOX_WIDTH, 4 * FOX_WIDTH], axis=-1)
    shp = (B, L, FOX_HEADS, FOX_HEAD_DIM)
    logf = jax.nn.log_sigmoid((fl + b_f).astype(jnp.float32))
    return q.reshape(shp), k.reshape(shp), v.reshape(shp), gate, logf


def fox_attend(q, cq, qpos, k, v, ck, kpos):
    s = jnp.einsum('bqhd,bkhd->bhqk', q, k, preferred_element_type=jnp.float32) * (FOX_HEAD_DIM ** -0.5)
    s = s + jnp.swapaxes(cq, 1, 2)[..., :, None] - jnp.swapaxes(ck, 1, 2)[..., None, :]
    s = jnp.where(kpos[None, :] <= qpos[:, None], s, NEG_INF)
    p = jax.nn.softmax(s, axis=-1)
    return jnp.einsum('bhqk,bkhd->bqhd', p.astype(v.dtype), v)


def fox_attend_prompt(q, k, v, logf):
    B, L, H, Dh = q.shape
    c = jnp.cumsum(logf.astype(jnp.float32), axis=1)
    nb = L // Q_BLOCK
    pos = jnp.arange(L)
    qb = q.reshape(B, nb, Q_BLOCK, H, Dh).transpose(1, 0, 2, 3, 4)
    cb = c.reshape(B, nb, Q_BLOCK, H).transpose(1, 0, 2, 3)
    pb = pos.reshape(nb, Q_BLOCK)

    def block(args):
        qi, ci, pi = args
        return fox_attend(qi, ci, pi, k, v, c, pos)

    o = lax.map(block, (qb, cb, pb))
    return o.transpose(1, 0, 2, 3, 4).reshape(B, L, H, Dh)


def fox_attend_sample(q, k, v, logf, cache_k, cache_v, cache_logf):
    L = q.shape[1]
    P = cache_k.shape[1]
    k_all = jnp.concatenate([cache_k.astype(k.dtype), k], axis=1)
    v_all = jnp.concatenate([cache_v.astype(v.dtype), v], axis=1)
    lf_all = jnp.concatenate([cache_logf.astype(jnp.float32), logf], axis=1)
    c = jnp.cumsum(lf_all, axis=1)
    kpos = jnp.arange(P + L)
    return fox_attend(q, c[:, P:], P + jnp.arange(L), k_all, v_all, c, kpos)


def gated_out(o, gate, w_out):
    B, L = o.shape[:2]
    return (o.reshape(B, L, -1) * jax.nn.silu(gate)) @ w_out


def gla_project(h, w_in, w_a2, b_a):
    B, L, _ = h.shape
    z = h @ w_in
    q, k, v, gate, a1 = jnp.split(
        z, [GLA_KEY_WIDTH, 2 * GLA_KEY_WIDTH, 2 * GLA_KEY_WIDTH + GLA_VAL_WIDTH,
            2 * GLA_KEY_WIDTH + 2 * GLA_VAL_WIDTH], axis=-1)
    log_alpha = jax.nn.log_sigmoid((a1 @ w_a2 + b_a).astype(jnp.float32)) / GLA_GATE_TEMP
    kshp = (B, L, GLA_HEADS, GLA_HEAD_K)
    q = q.reshape(kshp) * (GLA_HEAD_K ** -0.5)
    return (q, k.reshape(kshp), v.reshape(B, L, GLA_HEADS, GLA_HEAD_V), gate,
            log_alpha.reshape(kshp))


def gla_chunked(q, k, v, g, s0, chunk):
    B, L, H, dk = q.shape
    dv = v.shape[-1]
    n = L // chunk

    def to_chunks(t):
        return t.astype(jnp.float32).reshape(B, n, chunk, H, t.shape[-1]).transpose(1, 0, 3, 2, 4)

    qc, kc, vc, gc = to_chunks(q), to_chunks(k), to_chunks(v), to_chunks(g)
    causal = jnp.tril(jnp.ones((chunk, chunk), dtype=bool))

    def step(S, inp):
        qi, ki, vi, gi = inp
        b = jnp.cumsum(gi, axis=2)
        b_last = b[:, :, -1:, :]
        qe = qi * jnp.exp(b)
        ke = ki * jnp.exp(-b)
        a = jnp.where(causal, jnp.einsum('bhtd,bhsd->bhts', qe, ke), 0.0)
        o = jnp.einsum('bhts,bhsv->bhtv', a, vi) + jnp.einsum('bhtd,bhdv->bhtv', qe, S)
        kd = ki * jnp.exp(b_last - b)
        S = jnp.exp(b_last)[:, :, 0, :, None] * S + jnp.einsum('bhsd,bhsv->bhdv', kd, vi)
        return S, o

    S, o = lax.scan(step, s0.astype(jnp.float32), (qc, kc, vc, gc))
    o = o.transpose(1, 3, 0, 2, 4).reshape(B, L, H, dv)
    return o, S


def gla_output(o, g_o, gate, w_out, dtype):
    o = rmsnorm(o, g_o).astype(dtype)
    return gated_out(o, gate, w_out)


def setup_inputs(seed: int = 0) -> dict:
    key = jax.random.key(seed)
    ks = jax.random.split(key, 20)
    f32 = jnp.float32

    def nrm(k, shape, scale=1.0):
        return jax.random.normal(k, shape, f32) * scale

    return {
        "x_prompt": nrm(ks[0], (BATCH, SEQ, D_MODEL)),
        "x_sample": nrm(ks[1], (DEC_BATCH, DEC_SEQ, D_MODEL)),
        "cache_fox_k": nrm(ks[2], (DEC_BATCH, PAST_LEN, FOX_HEADS, FOX_HEAD_DIM)),
        "cache_fox_v": nrm(ks[3], (DEC_BATCH, PAST_LEN, FOX_HEADS, FOX_HEAD_DIM)),
        "cache_fox_logf": jax.nn.log_sigmoid(2.0 + nrm(ks[4], (DEC_BATCH, PAST_LEN, FOX_HEADS))),
        "state_gla": nrm(ks[5], (DEC_BATCH, GLA_HEADS, GLA_HEAD_K, GLA_HEAD_V), 0.5),
        "g_norm_fox": 1.0 + nrm(ks[6], (D_MODEL,), 0.02),
        "w_in_fox": nrm(ks[7], (D_MODEL, FOX_IN), D_MODEL ** -0.5),
        "b_fox_f": 1.0 + 2.0 * jax.random.uniform(ks[8], (FOX_HEADS,), f32),
        "w_out_fox": nrm(ks[9], (FOX_WIDTH, D_MODEL), FOX_WIDTH ** -0.5),
        "g_norm_gla": 1.0 + nrm(ks[10], (D_MODEL,), 0.02),
        "w_in_gla": nrm(ks[11], (D_MODEL, GLA_IN), D_MODEL ** -0.5),
        "w_gla_a2": nrm(ks[12], (GLA_GATE_RANK, GLA_KEY_WIDTH), GLA_GATE_RANK ** -0.5),
        "b_gla_a": nrm(ks[13], (GLA_KEY_WIDTH,), 0.02),
        "g_gla_o": 1.0 + nrm(ks[14], (GLA_HEAD_V,), 0.02),
        "w_out_gla": nrm(ks[15], (GLA_VAL_WIDTH, D_MODEL), GLA_VAL_WIDTH ** -0.5),
        "g_final": 1.0 + nrm(ks[16], (D_MODEL,), 0.02),
    }


def reference(x_prompt, x_sample, cache_fox_k, cache_fox_v, cache_fox_logf, state_gla,
              g_norm_fox, w_in_fox, b_fox_f, w_out_fox,
              g_norm_gla, w_in_gla, w_gla_a2, b_gla_a, g_gla_o, w_out_gla, g_final):
    yp, ys = x_prompt, x_sample
    Lp, Ls = yp.shape[1], ys.shape[1]
    for i in range(DEPTH):
        if i % N_MIXERS == 0:
            hp = rmsnorm(yp, g_norm_fox)
            q, k, v, gate, logf = fox_project(hp, w_in_fox, b_fox_f)
            o = fox_attend_prompt(q, k, v, logf)
            yp = yp + gated_out(o, gate, w_out_fox)
            fox_k_prompt, fox_v_prompt, fox_logf_prompt = k, v, logf

            hs = rmsnorm(ys, g_norm_fox)
            q, k, v, gate, logf = fox_project(hs, w_in_fox, b_fox_f)
            o = fox_attend_sample(q, k, v, logf, cache_fox_k, cache_fox_v, cache_fox_logf)
            ys = ys + gated_out(o, gate, w_out_fox)
            fox_k_sample, fox_v_sample, fox_logf_sample = k, v, logf
        else:
            hp = rmsnorm(yp, g_norm_gla)
            q, k, v, gate, ga = gla_project(hp, w_in_gla, w_gla_a2, b_gla_a)
            s0 = jnp.zeros((yp.shape[0], GLA_HEADS, GLA_HEAD_K, GLA_HEAD_V), jnp.float32)
            o, S = gla_chunked(q, k, v, ga, s0, CHUNK)
            yp = yp + gla_output(o, g_gla_o, gate, w_out_gla, yp.dtype)
            gla_state_prompt = S.astype(state_gla.dtype)

            hs = rmsnorm(ys, g_norm_gla)
            q, k, v, gate, ga = gla_project(hs, w_in_gla, w_gla_a2, b_gla_a)
            o, S = gla_chunked(q, k, v, ga, state_gla, Ls)
            ys = ys + gla_output(o, g_gla_o, gate, w_out_gla, ys.dtype)
            gla_state_sample = S.astype(state_gla.dtype)
    y_prompt = rmsnorm(yp, g_final)
    y_sample = rmsnorm(ys, g_final)
    return (y_prompt, y_sample,
            fox_k_prompt, fox_v_prompt, fox_logf_prompt, gla_state_prompt,
            fox_k_sample, fox_v_sample, fox_logf_sample, gla_state_sample)
```

```python
import functools

import jax
import jax.numpy as jnp
from jax import lax
from jax.experimental import pallas as pl
from jax.experimental.pallas import tpu as pltpu

F32 = jnp.float32
BF16 = jnp.bfloat16

D_MODEL = 1024
EPS = 1e-6
NEG_INF = -1e30

FOX_HEADS = 16
FOX_HEAD_DIM = 64
FOX_WIDTH = FOX_HEADS * FOX_HEAD_DIM
FOX_PAIRS = FOX_HEADS // 2

GLA_HEADS = 4
GLA_KEY_WIDTH = 512
GLA_VAL_WIDTH = 1024
GLA_HEAD_K = 128
GLA_HEAD_V = 256
GLA_GATE_RANK = 16
GLA_GATE_TEMP = 16.0
GLA_CHUNK = 64

LANES = 128
VMEM_LIMIT = 56 * 1024 * 1024

PROJ_TM = 512
ATTN_BLOCK = 512
GLA_TC = 256


def _params(n_axes):
    return pltpu.CompilerParams(
        dimension_semantics=("arbitrary",) * n_axes,
        vmem_limit_bytes=VMEM_LIMIT,
    )


def _dot(a, b):
    return jnp.dot(a, b, preferred_element_type=F32)


def _dot_nt(a, b):
    return lax.dot_general(a, b, (((1,), (1,)), ((), ())), preferred_element_type=F32)


def _dot_tn(a, b):
    return lax.dot_general(a, b, (((0,), (0,)), ((), ())), preferred_element_type=F32)


def _log_sigmoid(z):
    return jnp.minimum(z, 0.0) - jnp.log1p(jnp.exp(-jnp.abs(z)))


def _rms(x, g):
    ms = jnp.mean(x * x, axis=-1, keepdims=True)
    return (x * lax.rsqrt(ms + EPS)) * g


def _split3(x):
    hi = x.astype(BF16)
    r1 = x - hi.astype(F32)
    mid = r1.astype(BF16)
    lo = (r1 - mid.astype(F32)).astype(BF16)
    return hi, mid, lo


def _tril_cumsum(tril_bf16, x):
    hi, mid, lo = _split3(x)
    return (_dot(tril_bf16, hi) + _dot(tril_bf16, mid)) + _dot(tril_bf16, lo)


def _tril(n):
    r = lax.broadcasted_iota(jnp.int32, (n, n), 0)
    c = lax.broadcasted_iota(jnp.int32, (n, n), 1)
    return jnp.where(r >= c, 1.0, 0.0).astype(BF16)


def _fox_proj_kernel(x_ref, g_ref, w_ref, wf_ref, bf_ref, *rest, want_vt):
    if want_vt:
        wvt_ref, q_ref, k_ref, v_ref, kb_ref, sg_ref, lf_ref, vt_ref = rest
    else:
        q_ref, k_ref, v_ref, kb_ref, sg_ref, lf_ref = rest
    hb = _rms(x_ref[...], g_ref[...]).astype(BF16)
    q = _dot(hb, w_ref[:, 0:FOX_WIDTH])
    q_ref[...] = (q * (FOX_HEAD_DIM ** -0.5)).astype(BF16)
    k = _dot(hb, w_ref[:, FOX_WIDTH:2 * FOX_WIDTH])
    k_ref[...] = k
    kb_ref[...] = k.astype(BF16)
    v_ref[...] = _dot(hb, w_ref[:, 2 * FOX_WIDTH:3 * FOX_WIDTH])
    gate = _dot(hb, w_ref[:, 3 * FOX_WIDTH:4 * FOX_WIDTH])
    sg_ref[...] = gate * jax.nn.sigmoid(gate)
    fl = _dot(hb, wf_ref[...]) + bf_ref[...]
    lf_ref[...] = _log_sigmoid(fl)[:, :FOX_HEADS]
    if want_vt:
        vt_ref[...] = _dot_nt(wvt_ref[...], hb).astype(BF16)


def _fox_proj(x, g, w_main, w_f, b_f, w_vt, *, tm, want_vt):
    B, L, _ = x.shape
    n = L // tm
    tok = lambda w: pl.BlockSpec((None, tm, w), lambda b, i: (b, i, 0))
    full = lambda a: pl.BlockSpec(a.shape, lambda b, i: (0,) * a.ndim)
    in_specs = [tok(D_MODEL), full(g), full(w_main), full(w_f), full(b_f)]
    args = [x, g, w_main, w_f, b_f]
    out_specs = [tok(FOX_WIDTH)] * 5 + [tok(FOX_HEADS)]
    out_shape = [
        jax.ShapeDtypeStruct((B, L, FOX_WIDTH), BF16),
        jax.ShapeDtypeStruct((B, L, FOX_WIDTH), F32),
        jax.ShapeDtypeStruct((B, L, FOX_WIDTH), F32),
        jax.ShapeDtypeStruct((B, L, FOX_WIDTH), BF16),
        jax.ShapeDtypeStruct((B, L, FOX_WIDTH), F32),
        jax.ShapeDtypeStruct((B, L, FOX_HEADS), F32),
    ]
    if want_vt:
        in_specs.append(full(w_vt))
        args.append(w_vt)
        out_specs.append(pl.BlockSpec((None, None, FOX_WIDTH, tm), lambda b, i: (b, i, 0, 0)))
        out_shape.append(jax.ShapeDtypeStruct((B, n, FOX_WIDTH, tm), BF16))
    return pl.pallas_call(
        functools.partial(_fox_proj_kernel, want_vt=want_vt),
        grid=(B, n),
        in_specs=in_specs,
        out_specs=out_specs,
        out_shape=out_shape,
        compiler_params=_params(2),
        name="fox_proj_vt" if want_vt else "fox_proj",
    )(*args)


def _cumsum_kernel(lf_ref, c_ref, *, rows, n_chunks):
    tril = _tril(rows)

    def body(i, carry):
        r0 = pl.multiple_of(i * rows, rows)
        cs = _tril_cumsum(tril, lf_ref[pl.ds(r0, rows), :]) + carry
        c_ref[pl.ds(r0, rows), :] = cs
        return cs[rows - 1:rows, :]

    lax.fori_loop(0, n_chunks, body, jnp.zeros((1, FOX_HEADS), F32))


def _cumsum(lf, *, rows):
    B, L, H = lf.shape
    spec = pl.BlockSpec((None, L, H), lambda b: (b, 0, 0))
    return pl.pallas_call(
        functools.partial(_cumsum_kernel, rows=rows, n_chunks=L // rows),
        grid=(B,),
        in_specs=[spec],
        out_specs=spec,
        out_shape=jax.ShapeDtypeStruct((B, L, H), F32),
        compiler_params=_params(1),
        name="fox_cumsum",
    )(lf)


def _stack_heads(q2):
    lane = lax.broadcasted_iota(jnp.int32, q2.shape, 1)
    zero = jnp.zeros_like(q2)
    return jnp.concatenate(
        [jnp.where(lane < FOX_HEAD_DIM, q2, zero), jnp.where(lane >= FOX_HEAD_DIM, q2, zero)], axis=0)


def _attn_prompt_kernel(q_ref, k_ref, vt_ref, cq_ref, c_ref, o_ref,
                        ckb_ref, m_ref, l_ref, acc_ref, *, blk, seq):
    p = pl.program_id(1)
    qi = pl.program_id(2)
    n = 2 * blk
    rep = blk // LANES
    fill = 512

    @pl.when(qi == 0)
    def _():
        def body(i, carry):
            r0 = pl.multiple_of(i * fill, fill)
            cb = c_ref[pl.ds(r0, fill), :]
            lane = lax.broadcasted_iota(jnp.int32, cb.shape, 1)
            for h2 in range(2):
                col = jnp.sum(jnp.where(lane == 2 * p + h2, cb, 0.0), axis=1, keepdims=True)
                ckb_ref[h2, pl.ds(r0, fill), :] = jnp.broadcast_to(col, (fill, LANES))
            return carry
        lax.fori_loop(0, seq // fill, body, 0)

    qst = _stack_heads(q_ref[...])
    cq = jnp.concatenate([cq_ref[pl.ds(2 * p, 1), :], cq_ref[pl.ds(2 * p + 1, 1), :]], axis=1)

    m_ref[...] = jnp.full((1, n), NEG_INF, F32)
    l_ref[...] = jnp.zeros((1, n), F32)
    acc_ref[...] = jnp.zeros((LANES, blk), F32)

    def step(j, masked):
        k0 = pl.multiple_of(j * blk, blk)
        st = _dot_nt(k_ref[pl.ds(k0, blk), :], qst)
        ck = jnp.concatenate([ckb_ref[0, pl.ds(k0, blk), :]] * rep
                             + [ckb_ref[1, pl.ds(k0, blk), :]] * rep, axis=1)
        st = st + (cq - ck)
        if masked:
            r = lax.broadcasted_iota(jnp.int32, (blk, n), 0)
            c = lax.broadcasted_iota(jnp.int32, (blk, n), 1) & (blk - 1)
            st = jnp.where(r <= c, st, NEG_INF)
        m_old = m_ref[...]
        m_new = jnp.maximum(m_old, jnp.max(st, axis=0, keepdims=True))
        alpha = jnp.exp(m_old - m_new)
        pt = jnp.exp(st - m_new)
        l_ref[...] = alpha * l_ref[...] + jnp.sum(pt, axis=0, keepdims=True)
        m_ref[...] = m_new
        pv = _dot(vt_ref[j], pt.astype(BF16))
        hd = FOX_HEAD_DIM
        acc_ref[0:hd, :] = acc_ref[0:hd, :] * alpha[:, :blk] + pv[0:hd, :blk]
        acc_ref[hd:, :] = acc_ref[hd:, :] * alpha[:, blk:] + pv[hd:, blk:]

    def loop_body(j, carry):
        step(j, False)
        return carry

    lax.fori_loop(0, qi, loop_body, 0)
    step(qi, True)

    l = l_ref[...]
    hd = FOX_HEAD_DIM
    ot = jnp.concatenate([acc_ref[0:hd, :] / l[:, :blk], acc_ref[hd:, :] / l[:, blk:]], axis=0)
    o_ref[...] = ot.T


def _attn_prompt(q, kb, vt, c, ct, *, blk):
    B, L, _ = q.shape
    nb = L // blk
    return pl.pallas_call(
        functools.partial(_attn_prompt_kernel, blk=blk, seq=L),
        grid=(B, FOX_PAIRS, nb),
        in_specs=[
            pl.BlockSpec((None, blk, LANES), lambda b, p, i: (b, i, p)),
            pl.BlockSpec((None, L, LANES), lambda b, p, i: (b, 0, p)),
            pl.BlockSpec((None, nb, LANES, blk), lambda b, p, i: (b, 0, p, 0)),
            pl.BlockSpec((None, FOX_HEADS, blk), lambda b, p, i: (b, 0, i)),
            pl.BlockSpec((None, L, FOX_HEADS), lambda b, p, i: (b, 0, 0)),
        ],
        out_specs=pl.BlockSpec((None, blk, LANES), lambda b, p, i: (b, i, p)),
        out_shape=jax.ShapeDtypeStruct((B, L, FOX_WIDTH), F32),
        scratch_shapes=[
            pltpu.VMEM((2, L, LANES), F32),
            pltpu.VMEM((1, 2 * blk), F32),
            pltpu.VMEM((1, 2 * blk), F32),
            pltpu.VMEM((LANES, blk), F32),
        ],
        compiler_params=_params(3),
        name="fox_attn_prompt",
    )(q, kb, vt, ct, c)


def _attn_sample_kernel(q_ref, kn_ref, vn_ref, ck_ref, cv_ref, c_ref, ct_ref, o_ref, *, past, ls):
    p = pl.program_id(1)
    hd = FOX_HEAD_DIM
    qst = _stack_heads(q_ref[...])
    kc = ck_ref[...].astype(BF16)
    vc = cv_ref[...].astype(BF16)
    pad_k = jnp.zeros((LANES - ls, LANES), BF16)
    kn = jnp.concatenate([kn_ref[...], pad_k], axis=0)
    vn = jnp.concatenate([vn_ref[...].astype(BF16), pad_k], axis=0)

    cqb = c_ref[past:past + ls, :]
    lane = lax.broadcasted_iota(jnp.int32, cqb.shape, 1)
    cq = jnp.concatenate(
        [jnp.sum(jnp.where(lane == 2 * p + h2, cqb, 0.0), axis=1, keepdims=True) for h2 in range(2)],
        axis=0)
    cka = ct_ref[pl.ds(2 * p, 1), :]
    ckb = ct_ref[pl.ds(2 * p + 1, 1), :]

    def bias(lo, hi):
        return jnp.concatenate([cq[:ls] - cka[:, lo:hi], cq[ls:] - ckb[:, lo:hi]], axis=0)

    s_c = _dot_nt(qst, kc) + bias(0, past)
    s_n = _dot_nt(qst, kn) + bias(past, past + LANES)
    r = lax.broadcasted_iota(jnp.int32, s_n.shape, 0) & (ls - 1)
    c = lax.broadcasted_iota(jnp.int32, s_n.shape, 1)
    s_n = jnp.where(c <= r, s_n, NEG_INF)
    m = jnp.maximum(jnp.max(s_c, axis=1, keepdims=True), jnp.max(s_n, axis=1, keepdims=True))
    p_c = jnp.exp(s_c - m)
    p_n = jnp.exp(s_n - m)
    l = jnp.sum(p_c, axis=1, keepdims=True) + jnp.sum(p_n, axis=1, keepdims=True)
    o = (_dot(p_c.astype(BF16), vc) + _dot(p_n.astype(BF16), vn)) / l
    lane_o = lax.broadcasted_iota(jnp.int32, (ls, LANES), 1)
    o_ref[...] = jnp.where(lane_o < hd, o[:ls], o[ls:])


def _attn_sample(q, kb, v, cache_k, cache_v, c, ct, *, past):
    B, ls, _ = q.shape
    tot = c.shape[1]
    tile = lambda rows: pl.BlockSpec((None, rows, LANES), lambda b, p: (b, 0, p))
    return pl.pallas_call(
        functools.partial(_attn_sample_kernel, past=past, ls=ls),
        grid=(B, FOX_PAIRS),
        in_specs=[
            tile(ls), tile(ls), tile(ls), tile(past), tile(past),
            pl.BlockSpec((None, tot, FOX_HEADS), lambda b, p: (b, 0, 0)),
            pl.BlockSpec((None, FOX_HEADS, tot), lambda b, p: (b, 0, 0)),
        ],
        out_specs=tile(ls),
        out_shape=jax.ShapeDtypeStruct((B, ls, FOX_WIDTH), F32),
        compiler_params=_params(2),
        name="fox_attn_sample",
    )(q, kb, v, cache_k, cache_v, c, ct)


def _mid_kernel(o_ref, sg_ref, x_ref, wo_ref, g_ref, w_ref, wa1_ref, wa2_ref, ba_ref,
                y_ref, q_ref, k_ref, v_ref, sg2_ref, ga_ref):
    u = (o_ref[...] * sg_ref[...]).astype(BF16)
    y = x_ref[...] + _dot(u, wo_ref[...])
    y_ref[...] = y
    hb = _rms(y, g_ref[...]).astype(BF16)
    kw = GLA_KEY_WIDTH
    q_ref[...] = _dot(hb, w_ref[:, 0:kw]) * (GLA_HEAD_K ** -0.5)
    k_ref[...] = _dot(hb, w_ref[:, kw:2 * kw])
    v_ref[...] = _dot(hb, w_ref[:, 2 * kw:2 * kw + GLA_VAL_WIDTH]).astype(BF16)
    gate = _dot(hb, w_ref[:, 2 * kw + GLA_VAL_WIDTH:2 * kw + 2 * GLA_VAL_WIDTH])
    sg2_ref[...] = gate * jax.nn.sigmoid(gate)
    a1 = _dot(hb, wa1_ref[...]).astype(BF16)
    a = _dot(a1, wa2_ref[...]) + ba_ref[...]
    ga_ref[...] = _log_sigmoid(a) * (1.0 / GLA_GATE_TEMP)


def _mid(o, sg, x, w_out, g, w_main, w_a1, w_a2, b_a, *, tm):
    T = o.shape[0]
    tok = lambda w: pl.BlockSpec((tm, w), lambda i: (i, 0))
    full = lambda a: pl.BlockSpec(a.shape, lambda i: (0,) * a.ndim)
    return pl.pallas_call(
        _mid_kernel,
        grid=(T // tm,),
        in_specs=[tok(D_MODEL), tok(D_MODEL), tok(D_MODEL), full(w_out), full(g), full(w_main),
                  full(w_a1), full(w_a2), full(b_a)],
        out_specs=[tok(D_MODEL), tok(GLA_KEY_WIDTH), tok(GLA_KEY_WIDTH), tok(GLA_VAL_WIDTH),
                   tok(GLA_VAL_WIDTH), tok(GLA_KEY_WIDTH)],
        out_shape=[
            jax.ShapeDtypeStruct((T, D_MODEL), F32),
            jax.ShapeDtypeStruct((T, GLA_KEY_WIDTH), F32),
            jax.ShapeDtypeStruct((T, GLA_KEY_WIDTH), F32),
            jax.ShapeDtypeStruct((T, GLA_VAL_WIDTH), BF16),
            jax.ShapeDtypeStruct((T, GLA_VAL_WIDTH), F32),
            jax.ShapeDtypeStruct((T, GLA_KEY_WIDTH), F32),
        ],
        compiler_params=_params(1),
        name="fox_out_gla_proj",
    )(o, sg, x, w_out, g, w_main, w_a1, w_a2, b_a)


def _gla_kernel(q_ref, k_ref, v_ref, ga_ref, s0_ref, o_ref, s_ref, *, n_chunks):
    i = pl.program_id(1)
    ch = GLA_CHUNK
    dk, dv = GLA_HEAD_K, GLA_HEAD_V

    @pl.when(i == 0)
    def _():
        s_ref[...] = s0_ref[...]

    tril = _tril(ch)
    r = lax.broadcasted_iota(jnp.int32, (ch, ch), 0)
    c = lax.broadcasted_iota(jnp.int32, (ch, ch), 1)
    causal = r >= c
    eye = (lax.broadcasted_iota(jnp.int32, (dk, dk), 0) == lax.broadcasted_iota(jnp.int32, (dk, dk), 1))

    for ci in range(n_chunks):
        rows = slice(ci * ch, (ci + 1) * ch)
        b = _tril_cumsum(tril, ga_ref[rows, :])
        b_last = b[ch - 1:ch, :]
        eb = jnp.exp(b)
        qe = (q_ref[rows, :] * eb).astype(BF16)
        kk = k_ref[rows, :]
        ke = (kk * jnp.exp(-b)).astype(BF16)
        kd = (kk * jnp.exp(b_last - b)).astype(BF16)
        dec = jnp.exp(b_last)
        for h in range(GLA_HEADS):
            ks = slice(h * dk, (h + 1) * dk)
            vs = slice(h * dv, (h + 1) * dv)
            vh = v_ref[rows, vs]
            a = jnp.where(causal, _dot_nt(qe[:, ks], ke[:, ks]), 0.0)
            s_old = s_ref[h]
            o_ref[rows, vs] = _dot(a.astype(BF16), vh) + _dot(qe[:, ks], s_old.astype(BF16))
            dec_col = jnp.sum(jnp.where(eye, jnp.broadcast_to(dec[:, ks], (dk, dk)), 0.0),
                              axis=1, keepdims=True)
            s_ref[h] = dec_col * s_old + _dot_tn(kd[:, ks], vh)


def _gla(q, k, v, ga, s0, *, tc):
    B, L, _ = q.shape
    tok = lambda w: pl.BlockSpec((None, tc, w), lambda b, i: (b, i, 0))
    st = pl.BlockSpec((None, GLA_HEADS, GLA_HEAD_K, GLA_HEAD_V), lambda b, i: (b, 0, 0, 0))
    return pl.pallas_call(
        functools.partial(_gla_kernel, n_chunks=tc // GLA_CHUNK),
        grid=(B, L // tc),
        in_specs=[tok(GLA_KEY_WIDTH), tok(GLA_KEY_WIDTH), tok(GLA_VAL_WIDTH), tok(GLA_KEY_WIDTH), st],
        out_specs=[tok(GLA_VAL_WIDTH), st],
        out_shape=[
            jax.ShapeDtypeStruct((B, L, GLA_VAL_WIDTH), F32),
            jax.ShapeDtypeStruct((B, GLA_HEADS, GLA_HEAD_K, GLA_HEAD_V), F32),
        ],
        compiler_params=_params(2),
        name="gla_chunk",
    )(q, k, v, ga, s0)


def _gla_out_kernel(*refs, n_parts):
    o_refs = refs[:n_parts]
    sg_ref, y_ref, go_ref, wo_ref, gf_ref, out_ref = refs[n_parts:]
    dv = GLA_HEAD_V
    o = jnp.concatenate([r[...] for r in o_refs], axis=0) if n_parts > 1 else o_refs[0][...]
    parts = []
    for h in range(GLA_HEADS):
        vs = slice(h * dv, (h + 1) * dv)
        parts.append((_rms(o[:, vs], go_ref[...]) * sg_ref[:, vs]).astype(BF16))
    u = jnp.concatenate(parts, axis=1)
    y = y_ref[...] + _dot(u, wo_ref[...])
    out_ref[...] = _rms(y, gf_ref[...])


def _gla_out(o, sg, y, g_o, w_out, g_f, *, tm):
    B, L, _ = o.shape
    n = L // GLA_CHUNK
    if n == 1:
        flat = lambda a: a.reshape(1, B * L, D_MODEL)
        o, sg, y = flat(o), flat(sg), flat(y)
        n_parts = 1
        o_specs = [pl.BlockSpec((None, tm, D_MODEL), lambda b, i: (b, i, 0))]
    else:
        n_parts = tm // n
        o = o.reshape(B, n, GLA_CHUNK * D_MODEL)
        o_specs = [pl.BlockSpec((None, n, D_MODEL), functools.partial(_part_index, n_parts=n_parts, part=kk))
                   for kk in range(n_parts)]
    nb, rows, _ = sg.shape
    tok = pl.BlockSpec((None, tm, D_MODEL), lambda b, i: (b, i, 0))
    full = lambda a: pl.BlockSpec(a.shape, lambda b, i: (0,) * a.ndim)
    return pl.pallas_call(
        functools.partial(_gla_out_kernel, n_parts=n_parts),
        grid=(nb, rows // tm),
        in_specs=o_specs + [tok, tok, full(g_o), full(w_out), full(g_f)],
        out_specs=tok,
        out_shape=jax.ShapeDtypeStruct((nb, rows, D_MODEL), F32),
        compiler_params=_params(2),
        name="gla_out",
    )(*([o] * n_parts), sg, y, g_o, w_out, g_f)


def _part_index(b, i, *, n_parts, part):
    return (b, 0, n_parts * i + part)


def _pad_cols(w, n):
    return jnp.pad(w, ((0, 0), (0, n - w.shape[1])))


def kernel(x_prompt, x_sample, cache_fox_k, cache_fox_v, cache_fox_logf, state_gla,
           g_norm_fox, w_in_fox, b_fox_f, w_out_fox,
           g_norm_gla, w_in_gla, w_gla_a2, b_gla_a, g_gla_o, w_out_gla, g_final):
    B, L, D = x_prompt.shape
    Bs, Ls, _ = x_sample.shape
    P = cache_fox_k.shape[1]
    H, hd = FOX_HEADS, FOX_HEAD_DIM

    row = lambda a: a.reshape(1, -1).astype(F32)
    w_fox = w_in_fox[:, :4 * FOX_WIDTH].astype(BF16)
    w_fox_f = _pad_cols(w_in_fox[:, 4 * FOX_WIDTH:], LANES).astype(BF16)
    b_f = _pad_cols(row(b_fox_f), LANES)
    w_fox_vt = w_in_fox[:, 2 * FOX_WIDTH:3 * FOX_WIDTH].T.astype(BF16)
    w_o_fox = w_out_fox.astype(BF16)
    n_main = 2 * GLA_KEY_WIDTH + 2 * GLA_VAL_WIDTH
    w_gla = w_in_gla[:, :n_main].astype(BF16)
    w_a1 = _pad_cols(w_in_gla[:, n_main:], LANES).astype(BF16)
    w_a2 = jnp.pad(w_gla_a2, ((0, LANES - GLA_GATE_RANK), (0, 0))).astype(BF16)
    w_o_gla = w_out_gla.astype(BF16)

    q_p, k_p, v_p, kb_p, sg_p, lf_p, vt_p = _fox_proj(
        x_prompt, row(g_norm_fox), w_fox, w_fox_f, b_f, w_fox_vt, tm=ATTN_BLOCK, want_vt=True)
    c_p = _cumsum(lf_p, rows=256)
    o_p = _attn_prompt(q_p, kb_p, vt_p, c_p, jnp.swapaxes(c_p, 1, 2), blk=ATTN_BLOCK)

    xs = x_sample.reshape(1, Bs * Ls, D)
    q_s, k_s, v_s, kb_s, sg_s, lf_s = _fox_proj(
        xs, row(g_norm_fox), w_fox, w_fox_f, b_f, None, tm=PROJ_TM, want_vt=False)
    sh = lambda a: a.reshape(Bs, Ls, a.shape[-1])
    q_s, k_s, v_s, kb_s, lf_s = sh(q_s), sh(k_s), sh(v_s), sh(kb_s), sh(lf_s)
    lf_all = jnp.concatenate(
        [cache_fox_logf.astype(F32), lf_s, jnp.zeros((Bs, LANES - Ls, H), F32)], axis=1)
    c_s = _cumsum(lf_all, rows=LANES)
    o_s = _attn_sample(q_s, kb_s, v_s, cache_fox_k.reshape(Bs, P, FOX_WIDTH),
                       cache_fox_v.reshape(Bs, P, FOX_WIDTH), c_s, jnp.swapaxes(c_s, 1, 2), past=P)

    def gla_layer(o, sg, x, s0, tc):
        b, l, _ = x.shape
        flat = lambda a: a.reshape(b * l, a.shape[-1])
        y1, q, k, v, sg2, ga = _mid(flat(o), flat(sg), flat(x), w_o_fox, row(g_norm_gla), w_gla,
                                    w_a1, w_a2, row(b_gla_a), tm=PROJ_TM)
        un = lambda a: a.reshape(b, l, a.shape[-1])
        og, s_out = _gla(un(q), un(k), un(v), un(ga), s0, tc=tc)
        y = _gla_out(og, un(sg2), un(y1), row(g_gla_o), w_o_gla, row(g_final), tm=PROJ_TM)
        return y.reshape(b, l, D), s_out

    s0_p = jnp.zeros((B, GLA_HEADS, GLA_HEAD_K, GLA_HEAD_V), F32)
    y_p, s_p = gla_layer(o_p, sg_p, x_prompt, s0_p, GLA_TC)
    y_s, s_s = gla_layer(o_s, sg_s.reshape(Bs, Ls, D), x_sample, state_gla.astype(F32), Ls)

    return (y_p, y_s,
            k_p.reshape(B, L, H, hd), v_p.reshape(B, L, H, hd), lf_p, s_p.astype(state_gla.dtype),
            k_s.reshape(Bs, Ls, H, hd), v_s.reshape(Bs, Ls, H, hd), lf_s, s_s.astype(state_gla.dtype))
```

```python
import functools

import jax
import jax.numpy as jnp
from jax import lax
from jax.experimental import pallas as pl
from jax.experimental.pallas import tpu as pltpu

F32 = jnp.float32
BF16 = jnp.bfloat16

D_MODEL = 1024
EPS = 1e-6
NEG_INF = -1e30
LOG2E = 1.4426950408889634

FOX_HEADS = 16
FOX_HEAD_DIM = 64
FOX_WIDTH = FOX_HEADS * FOX_HEAD_DIM
FOX_PAIRS = FOX_HEADS // 2

GLA_HEADS = 4
GLA_KEY_WIDTH = 512
GLA_VAL_WIDTH = 1024
GLA_HEAD_K = 128
GLA_HEAD_V = 256
GLA_GATE_RANK = 16
GLA_GATE_TEMP = 16.0
GLA_CHUNK = 64

LANES = 128
VMEM_LIMIT = 56 * 1024 * 1024

PROJ_TM = 512
ATTN_BLOCK = 512
GLA_TC = 256


def _params(n_axes):
    return pltpu.CompilerParams(
        dimension_semantics=("arbitrary",) * n_axes,
        vmem_limit_bytes=VMEM_LIMIT,
    )


def _dot(a, b):
    return jnp.dot(a, b, preferred_element_type=F32)


def _dot_nt(a, b):
    return lax.dot_general(a, b, (((1,), (1,)), ((), ())), preferred_element_type=F32)


def _dot_tn(a, b):
    return lax.dot_general(a, b, (((0,), (0,)), ((), ())), preferred_element_type=F32)


def _log_sigmoid(z):
    return jnp.minimum(z, 0.0) - jnp.log1p(jnp.exp(-jnp.abs(z)))


def _rms(x, g):
    ms = jnp.mean(x * x, axis=-1, keepdims=True)
    return (x * lax.rsqrt(ms + EPS)) * g


def _split3(x):
    hi = x.astype(BF16)
    r1 = x - hi.astype(F32)
    mid = r1.astype(BF16)
    lo = (r1 - mid.astype(F32)).astype(BF16)
    return hi, mid, lo


def _tril_cumsum(tril_bf16, x):
    hi, mid, lo = _split3(x)
    return (_dot(tril_bf16, hi) + _dot(tril_bf16, mid)) + _dot(tril_bf16, lo)


def _tril(n):
    r = lax.broadcasted_iota(jnp.int32, (n, n), 0)
    c = lax.broadcasted_iota(jnp.int32, (n, n), 1)
    return jnp.where(r >= c, 1.0, 0.0).astype(BF16)


def _fox_proj_kernel(x_ref, g_ref, w_ref, wf_ref, bf_ref, *rest, want_vt):
    if want_vt:
        wvt_ref, q_ref, k_ref, v_ref, kb_ref, sg_ref, lf_ref, vt_ref = rest
    else:
        q_ref, k_ref, v_ref, kb_ref, sg_ref, lf_ref = rest
    hb = _rms(x_ref[...], g_ref[...]).astype(BF16)
    q = _dot(hb, w_ref[:, 0:FOX_WIDTH])
    q_ref[...] = (q * (LOG2E * FOX_HEAD_DIM ** -0.5)).astype(BF16)
    k = _dot(hb, w_ref[:, FOX_WIDTH:2 * FOX_WIDTH])
    k_ref[...] = k
    kb_ref[...] = k.astype(BF16)
    v_ref[...] = _dot(hb, w_ref[:, 2 * FOX_WIDTH:3 * FOX_WIDTH])
    gate = _dot(hb, w_ref[:, 3 * FOX_WIDTH:4 * FOX_WIDTH])
    sg_ref[...] = gate * jax.nn.sigmoid(gate)
    fl = _dot(hb, wf_ref[...]) + bf_ref[...]
    lf_ref[...] = _log_sigmoid(fl)[:, :FOX_HEADS]
    if want_vt:
        vt_ref[...] = _dot_nt(wvt_ref[...], hb).astype(BF16)


def _fox_proj(x, g, w_main, w_f, b_f, w_vt, *, tm, want_vt):
    B, L, _ = x.shape
    n = L // tm
    tok = lambda w: pl.BlockSpec((None, tm, w), lambda b, i: (b, i, 0))
    full = lambda a: pl.BlockSpec(a.shape, lambda b, i: (0,) * a.ndim)
    in_specs = [tok(D_MODEL), full(g), full(w_main), full(w_f), full(b_f)]
    args = [x, g, w_main, w_f, b_f]
    out_specs = [tok(FOX_WIDTH)] * 5 + [tok(FOX_HEADS)]
    out_shape = [
        jax.ShapeDtypeStruct((B, L, FOX_WIDTH), BF16),
        jax.ShapeDtypeStruct((B, L, FOX_WIDTH), F32),
        jax.ShapeDtypeStruct((B, L, FOX_WIDTH), F32),
        jax.ShapeDtypeStruct((B, L, FOX_WIDTH), BF16),
        jax.ShapeDtypeStruct((B, L, FOX_WIDTH), F32),
        jax.ShapeDtypeStruct((B, L, FOX_HEADS), F32),
    ]
    if want_vt:
        in_specs.append(full(w_vt))
        args.append(w_vt)
        out_specs.append(pl.BlockSpec((None, None, FOX_WIDTH, tm), lambda b, i: (b, i, 0, 0)))
        out_shape.append(jax.ShapeDtypeStruct((B, n, FOX_WIDTH, tm), BF16))
    return pl.pallas_call(
        functools.partial(_fox_proj_kernel, want_vt=want_vt),
        grid=(B, n),
        in_specs=in_specs,
        out_specs=out_specs,
        out_shape=out_shape,
        compiler_params=_params(2),
        name="fox_proj_vt" if want_vt else "fox_proj",
    )(*args)


BIAS_SLOT = 8


def _bias_selectors():
    shape = (FOX_HEADS, FOX_WIDTH)
    head = lax.broadcasted_iota(jnp.int32, shape, 0)
    col = lax.broadcasted_iota(jnp.int32, shape, 1)
    slot = LANES * (head >> 1) + BIAS_SLOT * (head & 1)
    sel_f = [jnp.where(col == slot + x, 1.0, 0.0).astype(BF16) for x in range(3)]
    sel_e = [jnp.where(col == slot + 3 + x, -1.0, 0.0).astype(BF16) for x in range(3)]
    lane = lax.broadcasted_iota(jnp.int32, (1, FOX_WIDTH), 1) & (BIAS_SLOT - 1)
    in_slots = (lax.broadcasted_iota(jnp.int32, (1, FOX_WIDTH), 1) & (LANES - 1)) < 2 * BIAS_SLOT
    one_f = jnp.where(in_slots & (lane >= 3) & (lane < 6), 1.0, 0.0)
    one_e = jnp.where(in_slots & (lane < 3), 1.0, 0.0)
    return sel_f, sel_e, one_f, one_e


def _cumsum_kernel(lf_ref, c_ref, *rest, rows, n_chunks, want_bias):
    tril = _tril(rows)
    if want_bias:
        e_ref, f_ref, carry_ref = rest
        sel_f, sel_e, one_f, one_e = _bias_selectors()
    else:
        (carry_ref,) = rest

    @pl.when(pl.program_id(1) == 0)
    def _():
        carry_ref[...] = jnp.zeros_like(carry_ref)

    def body(i, carry):
        r0 = pl.multiple_of(i * rows, rows)
        cs = _tril_cumsum(tril, lf_ref[pl.ds(r0, rows), :]) + carry
        c2 = cs * LOG2E
        c_ref[pl.ds(r0, rows), :] = c2
        if want_bias:
            hi, mid, lo = _split3(c2)
            f_ref[pl.ds(r0, rows), :] = (
                (_dot(hi, sel_f[0]) + _dot(mid, sel_f[1])) + _dot(lo, sel_f[2]) + one_f).astype(BF16)
            e_ref[pl.ds(r0, rows), :] = (
                (_dot(hi, sel_e[0]) + _dot(mid, sel_e[1])) + _dot(lo, sel_e[2]) + one_e).astype(BF16)
        return cs[rows - 1:rows, :]

    carry_ref[...] = lax.fori_loop(0, n_chunks, body, carry_ref[...])


def _cumsum(lf, *, rows, tb, want_bias):
    B, L, H = lf.shape
    spec = pl.BlockSpec((None, tb, H), lambda b, i: (b, i, 0))
    wide = pl.BlockSpec((None, tb, FOX_WIDTH), lambda b, i: (b, i, 0))
    out_specs = [spec]
    out_shape = [jax.ShapeDtypeStruct((B, L, H), F32)]
    if want_bias:
        out_specs += [wide, wide]
        out_shape += [jax.ShapeDtypeStruct((B, L, FOX_WIDTH), BF16)] * 2
    return pl.pallas_call(
        functools.partial(_cumsum_kernel, rows=rows, n_chunks=tb // rows, want_bias=want_bias),
        grid=(B, L // tb),
        in_specs=[spec],
        out_specs=out_specs,
        out_shape=out_shape,
        scratch_shapes=[pltpu.VMEM((1, FOX_HEADS), F32)],
        compiler_params=_params(2),
        name="fox_cumsum_bias" if want_bias else "fox_cumsum",
    )(lf)


def _stack_heads(q2):
    lane = lax.broadcasted_iota(jnp.int32, q2.shape, 1)
    zero = jnp.zeros_like(q2)
    return jnp.concatenate(
        [jnp.where(lane < FOX_HEAD_DIM, q2, zero), jnp.where(lane >= FOX_HEAD_DIM, q2, zero)], axis=0)


ATTN_TILE = 256


def _attn_prompt_kernel(q_ref, f_ref, k_ref, e_ref, vt_ref, o_ref,
                        m_ref, l_ref, acc_ref, sa_ref, xa_ref, sb_ref, xb_ref, *, blk):
    qi = pl.program_id(2)
    hd = FOX_HEAD_DIM
    tw = ATTN_TILE
    per_head = blk // tw
    n_tiles = 2 * per_head

    q2 = q_ref[...]
    f2 = f_ref[...]
    lane = lax.broadcasted_iota(jnp.int32, q2.shape, 1)
    zero = jnp.zeros_like(q2)
    q_heads = [
        jnp.concatenate([jnp.where(lane < hd, q2, zero), jnp.where(lane < BIAS_SLOT, f2, zero)], axis=1),
        jnp.concatenate([jnp.where(lane >= hd, q2, zero), jnp.where(lane >= BIAS_SLOT, f2, zero)], axis=1),
    ]
    q_tiles = [q_heads[t // per_head][(t % per_head) * tw:(t % per_head + 1) * tw] for t in range(n_tiles)]

    m_ref[...] = jnp.full(m_ref.shape, NEG_INF, F32)
    l_ref[...] = jnp.zeros(l_ref.shape, F32)
    acc_ref[...] = jnp.zeros(acc_ref.shape, F32)

    def scores(j, buf, diagonal):
        s_ref, x_ref = buf
        k0 = pl.multiple_of(j * blk, blk)
        kaug = jnp.concatenate([k_ref[pl.ds(k0, blk), :], e_ref[pl.ds(k0, blk), :]], axis=1)
        for t in range(n_tiles):
            s = _dot_nt(kaug, q_tiles[t])
            if diagonal:
                r = lax.broadcasted_iota(jnp.int32, (blk, tw), 0)
                c = lax.broadcasted_iota(jnp.int32, (blk, tw), 1) + (t % per_head) * tw
                s = jnp.where(r <= c, s, NEG_INF)
            s_ref[t] = s
            x_ref[t] = jnp.max(s, axis=0, keepdims=True)

    def absorb(j, buf):
        s_ref, x_ref = buf
        vt = vt_ref[j]
        for t in range(n_tiles):
            head = t // per_head
            m_old = m_ref[t]
            m_new = jnp.maximum(m_old, x_ref[t])
            alpha = jnp.exp2(m_old - m_new)
            p = jnp.exp2(s_ref[t] - m_new)
            l_ref[t] = alpha * l_ref[t] + jnp.sum(p, axis=0, keepdims=True)
            m_ref[t] = m_new
            pv = _dot(vt[head * hd:(head + 1) * hd, :], p.astype(BF16))
            acc_ref[t] = acc_ref[t] * alpha + pv

    buf_a = (sa_ref, xa_ref)
    buf_b = (sb_ref, xb_ref)
    scores(qi, buf_a, True)

    def pair(n, carry):
        j = qi - 2 * n
        scores(j - 1, buf_b, False)
        absorb(j, buf_a)
        scores(j - 2, buf_a, False)
        absorb(j - 1, buf_b)
        return carry

    lax.fori_loop(0, qi // 2, pair, 0)

    @pl.when(qi % 2 == 1)
    def _():
        scores(0, buf_b, False)
        absorb(1, buf_a)
        absorb(0, buf_b)

    @pl.when(qi % 2 == 0)
    def _():
        absorb(0, buf_a)

    halves = [jnp.concatenate([acc_ref[t] / l_ref[t] for t in range(h * per_head, (h + 1) * per_head)], axis=1)
              for h in range(2)]
    o_ref[...] = jnp.concatenate(halves, axis=0).T


def _attn_prompt(q, f, kb, e, vt, *, blk):
    B, L, _ = q.shape
    nb = L // blk
    n_tiles = 2 * blk // ATTN_TILE
    qspec = pl.BlockSpec((None, blk, LANES), lambda b, p, i: (b, i, p))
    kspec = pl.BlockSpec((None, L, LANES), lambda b, p, i: (b, 0, p))
    return pl.pallas_call(
        functools.partial(_attn_prompt_kernel, blk=blk),
        grid=(B, FOX_PAIRS, nb),
        in_specs=[qspec, qspec, kspec, kspec,
                  pl.BlockSpec((None, nb, LANES, blk), lambda b, p, i: (b, 0, p, 0))],
        out_specs=qspec,
        out_shape=jax.ShapeDtypeStruct((B, L, FOX_WIDTH), F32),
        scratch_shapes=[
            pltpu.VMEM((n_tiles, 1, ATTN_TILE), F32),
            pltpu.VMEM((n_tiles, 1, ATTN_TILE), F32),
            pltpu.VMEM((n_tiles, FOX_HEAD_DIM, ATTN_TILE), F32),
            pltpu.VMEM((n_tiles, blk, ATTN_TILE), F32),
            pltpu.VMEM((n_tiles, 1, ATTN_TILE), F32),
            pltpu.VMEM((n_tiles, blk, ATTN_TILE), F32),
            pltpu.VMEM((n_tiles, 1, ATTN_TILE), F32),
        ],
        compiler_params=_params(3),
        name="fox_attn_prompt",
    )(q, f, kb, e, vt)


def _attn_sample_kernel(q_ref, kn_ref, vn_ref, ck_ref, cv_ref, c_ref, ct_ref, o_ref, *, past, ls):
    p = pl.program_id(1)
    hd = FOX_HEAD_DIM
    qst = _stack_heads(q_ref[...])
    kc = ck_ref[...].astype(BF16)
    vc = cv_ref[...].astype(BF16)
    pad_k = jnp.zeros((LANES - ls, LANES), BF16)
    kn = jnp.concatenate([kn_ref[...], pad_k], axis=0)
    vn = jnp.concatenate([vn_ref[...].astype(BF16), pad_k], axis=0)

    cqb = c_ref[past:past + ls, :]
    lane = lax.broadcasted_iota(jnp.int32, cqb.shape, 1)
    cq = jnp.concatenate(
        [jnp.sum(jnp.where(lane == 2 * p + h2, cqb, 0.0), axis=1, keepdims=True) for h2 in range(2)],
        axis=0)
    cka = ct_ref[pl.ds(2 * p, 1), :]
    ckb = ct_ref[pl.ds(2 * p + 1, 1), :]

    def bias(lo, hi):
        return jnp.concatenate([cq[:ls] - cka[:, lo:hi], cq[ls:] - ckb[:, lo:hi]], axis=0)

    s_c = _dot_nt(qst, kc) + bias(0, past)
    s_n = _dot_nt(qst, kn) + bias(past, past + LANES)
    r = lax.broadcasted_iota(jnp.int32, s_n.shape, 0) & (ls - 1)
    c = lax.broadcasted_iota(jnp.int32, s_n.shape, 1)
    s_n = jnp.where(c <= r, s_n, NEG_INF)
    m = jnp.maximum(jnp.max(s_c, axis=1, keepdims=True), jnp.max(s_n, axis=1, keepdims=True))
    p_c = jnp.exp2(s_c - m)
    p_n = jnp.exp2(s_n - m)
    l = jnp.sum(p_c, axis=1, keepdims=True) + jnp.sum(p_n, axis=1, keepdims=True)
    o = (_dot(p_c.astype(BF16), vc) + _dot(p_n.astype(BF16), vn)) / l
    lane_o = lax.broadcasted_iota(jnp.int32, (ls, LANES), 1)
    o_ref[...] = jnp.where(lane_o < hd, o[:ls], o[ls:])


def _attn_sample(q, kb, v, cache_k, cache_v, c, ct, *, past):
    B, ls, _ = q.shape
    tot = c.shape[1]
    tile = lambda rows: pl.BlockSpec((None, rows, LANES), lambda b, p: (b, 0, p))
    return pl.pallas_call(
        functools.partial(_attn_sample_kernel, past=past, ls=ls),
        grid=(B, FOX_PAIRS),
        in_specs=[
            tile(ls), tile(ls), tile(ls), tile(past), tile(past),
            pl.BlockSpec((None, tot, FOX_HEADS), lambda b, p: (b, 0, 0)),
            pl.BlockSpec((None, FOX_HEADS, tot), lambda b, p: (b, 0, 0)),
        ],
        out_specs=tile(ls),
        out_shape=jax.ShapeDtypeStruct((B, ls, FOX_WIDTH), F32),
        compiler_params=_params(2),
        name="fox_attn_sample",
    )(q, kb, v, cache_k, cache_v, c, ct)


def _mid_kernel(o_ref, sg_ref, x_ref, wo_ref, g_ref, w_ref, wa1_ref, wa2_ref, ba_ref,
                y_ref, q_ref, k_ref, v_ref, sg2_ref, ga_ref):
    u = (o_ref[...] * sg_ref[...]).astype(BF16)
    y = x_ref[...] + _dot(u, wo_ref[...])
    y_ref[...] = y
    hb = _rms(y, g_ref[...]).astype(BF16)
    kw = GLA_KEY_WIDTH
    q_ref[...] = _dot(hb, w_ref[:, 0:kw]) * (GLA_HEAD_K ** -0.5)
    k_ref[...] = _dot(hb, w_ref[:, kw:2 * kw])
    v_ref[...] = _dot(hb, w_ref[:, 2 * kw:2 * kw + GLA_VAL_WIDTH]).astype(BF16)
    gate = _dot(hb, w_ref[:, 2 * kw + GLA_VAL_WIDTH:2 * kw + 2 * GLA_VAL_WIDTH])
    sg2_ref[...] = gate * jax.nn.sigmoid(gate)
    a1 = _dot(hb, wa1_ref[...]).astype(BF16)
    a = _dot(a1, wa2_ref[...]) + ba_ref[...]
    ga_ref[...] = _log_sigmoid(a) * (1.0 / GLA_GATE_TEMP)


def _mid(o, sg, x, w_out, g, w_main, w_a1, w_a2, b_a, *, tm):
    T = o.shape[0]
    tok = lambda w: pl.BlockSpec((tm, w), lambda i: (i, 0))
    full = lambda a: pl.BlockSpec(a.shape, lambda i: (0,) * a.ndim)
    return pl.pallas_call(
        _mid_kernel,
        grid=(T // tm,),
        in_specs=[tok(D_MODEL), tok(D_MODEL), tok(D_MODEL), full(w_out), full(g), full(w_main),
                  full(w_a1), full(w_a2), full(b_a)],
        out_specs=[tok(D_MODEL), tok(GLA_KEY_WIDTH), tok(GLA_KEY_WIDTH), tok(GLA_VAL_WIDTH),
                   tok(GLA_VAL_WIDTH), tok(GLA_KEY_WIDTH)],
        out_shape=[
            jax.ShapeDtypeStruct((T, D_MODEL), F32),
            jax.ShapeDtypeStruct((T, GLA_KEY_WIDTH), F32),
            jax.ShapeDtypeStruct((T, GLA_KEY_WIDTH), F32),
            jax.ShapeDtypeStruct((T, GLA_VAL_WIDTH), BF16),
            jax.ShapeDtypeStruct((T, GLA_VAL_WIDTH), F32),
            jax.ShapeDtypeStruct((T, GLA_KEY_WIDTH), F32),
        ],
        compiler_params=_params(1),
        name="fox_out_gla_proj",
    )(o, sg, x, w_out, g, w_main, w_a1, w_a2, b_a)


def _gla_kernel(q_ref, k_ref, v_ref, ga_ref, s0_ref, o_ref, s_ref, *, n_chunks):
    i = pl.program_id(1)
    ch = GLA_CHUNK
    dk, dv = GLA_HEAD_K, GLA_HEAD_V

    @pl.when(i == 0)
    def _():
        s_ref[...] = s0_ref[...]

    tril = _tril(ch)
    r = lax.broadcasted_iota(jnp.int32, (ch, ch), 0)
    c = lax.broadcasted_iota(jnp.int32, (ch, ch), 1)
    causal = r >= c
    eye = (lax.broadcasted_iota(jnp.int32, (dk, dk), 0) == lax.broadcasted_iota(jnp.int32, (dk, dk), 1))

    for ci in range(n_chunks):
        rows = slice(ci * ch, (ci + 1) * ch)
        b = _tril_cumsum(tril, ga_ref[rows, :])
        b_last = b[ch - 1:ch, :]
        eb = jnp.exp(b)
        qe = (q_ref[rows, :] * eb).astype(BF16)
        kk = k_ref[rows, :]
        ke = (kk * jnp.exp(-b)).astype(BF16)
        kd = (kk * jnp.exp(b_last - b)).astype(BF16)
        dec = jnp.exp(b_last)
        for h in range(GLA_HEADS):
            ks = slice(h * dk, (h + 1) * dk)
            vs = slice(h * dv, (h + 1) * dv)
            vh = v_ref[rows, vs]
            a = jnp.where(causal, _dot_nt(qe[:, ks], ke[:, ks]), 0.0)
            s_old = s_ref[h]
            o_ref[rows, vs] = _dot(a.astype(BF16), vh) + _dot(qe[:, ks], s_old.astype(BF16))
            dec_col = jnp.sum(jnp.where(eye, jnp.broadcast_to(dec[:, ks], (dk, dk)), 0.0),
                              axis=1, keepdims=True)
            s_ref[h] = dec_col * s_old + _dot_tn(kd[:, ks], vh)


def _gla(q, k, v, ga, s0, *, tc):
    B, L, _ = q.shape
    tok = lambda w: pl.BlockSpec((None, tc, w), lambda b, i: (b, i, 0))
    st = pl.BlockSpec((None, GLA_HEADS, GLA_HEAD_K, GLA_HEAD_V), lambda b, i: (b, 0, 0, 0))
    return pl.pallas_call(
        functools.partial(_gla_kernel, n_chunks=tc // GLA_CHUNK),
        grid=(B, L // tc),
        in_specs=[tok(GLA_KEY_WIDTH), tok(GLA_KEY_WIDTH), tok(GLA_VAL_WIDTH), tok(GLA_KEY_WIDTH), st],
        out_specs=[tok(GLA_VAL_WIDTH), st],
        out_shape=[
            jax.ShapeDtypeStruct((B, L, GLA_VAL_WIDTH), F32),
            jax.ShapeDtypeStruct((B, GLA_HEADS, GLA_HEAD_K, GLA_HEAD_V), F32),
        ],
        compiler_params=_params(2),
        name="gla_chunk",
    )(q, k, v, ga, s0)


def _gla_out_kernel(*refs, n_parts):
    o_refs = refs[:n_parts]
    sg_ref, y_ref, go_ref, wo_ref, gf_ref, out_ref = refs[n_parts:]
    dv = GLA_HEAD_V
    o = jnp.concatenate([r[...] for r in o_refs], axis=0) if n_parts > 1 else o_refs[0][...]
    parts = []
    for h in range(GLA_HEADS):
        vs = slice(h * dv, (h + 1) * dv)
        parts.append((_rms(o[:, vs], go_ref[...]) * sg_ref[:, vs]).astype(BF16))
    u = jnp.concatenate(parts, axis=1)
    y = y_ref[...] + _dot(u, wo_ref[...])
    out_ref[...] = _rms(y, gf_ref[...])


def _gla_out(o, sg, y, g_o, w_out, g_f, *, tm):
    B, L, _ = o.shape
    n = L // GLA_CHUNK
    if n == 1:
        flat = lambda a: a.reshape(1, B * L, D_MODEL)
        o, sg, y = flat(o), flat(sg), flat(y)
        n_parts = 1
        o_specs = [pl.BlockSpec((None, tm, D_MODEL), lambda b, i: (b, i, 0))]
    else:
        n_parts = tm // n
        o = o.reshape(B, n, GLA_CHUNK * D_MODEL)
        o_specs = [pl.BlockSpec((None, n, D_MODEL), functools.partial(_part_index, n_parts=n_parts, part=kk))
                   for kk in range(n_parts)]
    nb, rows, _ = sg.shape
    tok = pl.BlockSpec((None, tm, D_MODEL), lambda b, i: (b, i, 0))
    full = lambda a: pl.BlockSpec(a.shape, lambda b, i: (0,) * a.ndim)
    return pl.pallas_call(
        functools.partial(_gla_out_kernel, n_parts=n_parts),
        grid=(nb, rows // tm),
        in_specs=o_specs + [tok, tok, full(g_o), full(w_out), full(g_f)],
        out_specs=tok,
        out_shape=jax.ShapeDtypeStruct((nb, rows, D_MODEL), F32),
        compiler_params=_params(2),
        name="gla_out",
    )(*([o] * n_parts), sg, y, g_o, w_out, g_f)


def _part_index(b, i, *, n_parts, part):
    return (b, 0, n_parts * i + part)


def _pad_cols(w, n):
    return jnp.pad(w, ((0, 0), (0, n - w.shape[1])))


def kernel(x_prompt, x_sample, cache_fox_k, cache_fox_v, cache_fox_logf, state_gla,
           g_norm_fox, w_in_fox, b_fox_f, w_out_fox,
           g_norm_gla, w_in_gla, w_gla_a2, b_gla_a, g_gla_o, w_out_gla, g_final):
    B, L, D = x_prompt.shape
    Bs, Ls, _ = x_sample.shape
    P = cache_fox_k.shape[1]
    H, hd = FOX_HEADS, FOX_HEAD_DIM

    row = lambda a: a.reshape(1, -1).astype(F32)
    w_fox = w_in_fox[:, :4 * FOX_WIDTH].astype(BF16)
    w_fox_f = _pad_cols(w_in_fox[:, 4 * FOX_WIDTH:], LANES).astype(BF16)
    b_f = _pad_cols(row(b_fox_f), LANES)
    w_fox_vt = w_in_fox[:, 2 * FOX_WIDTH:3 * FOX_WIDTH].T.astype(BF16)
    w_o_fox = w_out_fox.astype(BF16)
    n_main = 2 * GLA_KEY_WIDTH + 2 * GLA_VAL_WIDTH
    w_gla = w_in_gla[:, :n_main].astype(BF16)
    w_a1 = _pad_cols(w_in_gla[:, n_main:], LANES).astype(BF16)
    w_a2 = jnp.pad(w_gla_a2, ((0, LANES - GLA_GATE_RANK), (0, 0))).astype(BF16)
    w_o_gla = w_out_gla.astype(BF16)

    q_p, k_p, v_p, kb_p, sg_p, lf_p, vt_p = _fox_proj(
        x_prompt, row(g_norm_fox), w_fox, w_fox_f, b_f, w_fox_vt, tm=ATTN_BLOCK, want_vt=True)
    _, e_p, f_p = _cumsum(lf_p, rows=256, tb=1024, want_bias=True)
    o_p = _attn_prompt(q_p, f_p, kb_p, e_p, vt_p, blk=ATTN_BLOCK)

    xs = x_sample.reshape(1, Bs * Ls, D)
    q_s, k_s, v_s, kb_s, sg_s, lf_s = _fox_proj(
        xs, row(g_norm_fox), w_fox, w_fox_f, b_f, None, tm=PROJ_TM, want_vt=False)
    sh = lambda a: a.reshape(Bs, Ls, a.shape[-1])
    q_s, k_s, v_s, kb_s, lf_s = sh(q_s), sh(k_s), sh(v_s), sh(kb_s), sh(lf_s)
    lf_all = jnp.concatenate(
        [cache_fox_logf.astype(F32), lf_s, jnp.zeros((Bs, LANES - Ls, H), F32)], axis=1)
    (c_s,) = _cumsum(lf_all, rows=LANES, tb=P + LANES, want_bias=False)
    o_s = _attn_sample(q_s, kb_s, v_s, cache_fox_k.reshape(Bs, P, FOX_WIDTH),
                       cache_fox_v.reshape(Bs, P, FOX_WIDTH), c_s, jnp.swapaxes(c_s, 1, 2), past=P)

    def gla_layer(o, sg, x, s0, tc):
        b, l, _ = x.shape
        flat = lambda a: a.reshape(b * l, a.shape[-1])
        y1, q, k, v, sg2, ga = _mid(flat(o), flat(sg), flat(x), w_o_fox, row(g_norm_gla), w_gla,
                                    w_a1, w_a2, row(b_gla_a), tm=PROJ_TM)
        un = lambda a: a.reshape(b, l, a.shape[-1])
        og, s_out = _gla(un(q), un(k), un(v), un(ga), s0, tc=tc)
        y = _gla_out(og, un(sg2), un(y1), row(g_gla_o), w_o_gla, row(g_final), tm=PROJ_TM)
        return y.reshape(b, l, D), s_out

    s0_p = jnp.zeros((B, GLA_HEADS, GLA_HEAD_K, GLA_HEAD_V), F32)
    y_p, s_p = gla_layer(o_p, sg_p, x_prompt, s0_p, GLA_TC)
    y_s, s_s = gla_layer(o_s, sg_s.reshape(Bs, Ls, D), x_sample, state_gla.astype(F32), Ls)

    return (y_p, y_s,
            k_p.reshape(B, L, H, hd), v_p.reshape(B, L, H, hd), lf_p, s_p.astype(state_gla.dtype),
            k_s.reshape(Bs, Ls, H, hd), v_s.reshape(Bs, Ls, H, hd), lf_s, s_s.astype(state_gla.dtype))
```

```python
import functools

import jax
import jax.numpy as jnp
from jax import lax
from jax.experimental import pallas as pl
from jax.experimental.pallas import tpu as pltpu

F32 = jnp.float32
BF16 = jnp.bfloat16

D_MODEL = 1024
EPS = 1e-6
NEG_INF = -1e30
LOG2E = 1.4426950408889634
NORM_SLACK = 1.01
SKIP_LOG2 = 160.0

FOX_HEADS = 16
FOX_HEAD_DIM = 64
FOX_WIDTH = FOX_HEADS * FOX_HEAD_DIM
FOX_PAIRS = FOX_HEADS // 2

GLA_HEADS = 4
GLA_KEY_WIDTH = 512
GLA_VAL_WIDTH = 1024
GLA_HEAD_K = 128
GLA_HEAD_V = 256
GLA_GATE_RANK = 16
GLA_GATE_TEMP = 16.0
GLA_CHUNK = 64

LANES = 128
VMEM_LIMIT = 56 * 1024 * 1024

PROJ_TM = 512
ATTN_BLOCK = 512
GLA_TC = 256


def _params(n_axes):
    return pltpu.CompilerParams(
        dimension_semantics=("arbitrary",) * n_axes,
        vmem_limit_bytes=VMEM_LIMIT,
    )


def _dot(a, b):
    return jnp.dot(a, b, preferred_element_type=F32)


def _dot_nt(a, b):
    return lax.dot_general(a, b, (((1,), (1,)), ((), ())), preferred_element_type=F32)


def _dot_tn(a, b):
    return lax.dot_general(a, b, (((0,), (0,)), ((), ())), preferred_element_type=F32)


def _log_sigmoid(z):
    return jnp.minimum(z, 0.0) - jnp.log1p(jnp.exp(-jnp.abs(z)))


def _rms(x, g):
    ms = jnp.mean(x * x, axis=-1, keepdims=True)
    return (x * lax.rsqrt(ms + EPS)) * g


def _split3(x):
    hi = x.astype(BF16)
    r1 = x - hi.astype(F32)
    mid = r1.astype(BF16)
    lo = (r1 - mid.astype(F32)).astype(BF16)
    return hi, mid, lo


def _tril_cumsum(tril_bf16, x):
    hi, mid, lo = _split3(x)
    return (_dot(tril_bf16, hi) + _dot(tril_bf16, mid)) + _dot(tril_bf16, lo)


def _tril(n):
    r = lax.broadcasted_iota(jnp.int32, (n, n), 0)
    c = lax.broadcasted_iota(jnp.int32, (n, n), 1)
    return jnp.where(r >= c, 1.0, 0.0).astype(BF16)


def _fox_proj_kernel(x_ref, g_ref, w_ref, wf_ref, bf_ref, *rest, want_vt):
    if want_vt:
        wvt_ref, q_ref, k_ref, v_ref, kb_ref, sg_ref, lf_ref, vt_ref, nrm_ref = rest
    else:
        q_ref, k_ref, v_ref, kb_ref, sg_ref, lf_ref = rest
    hb = _rms(x_ref[...], g_ref[...]).astype(BF16)
    q = _dot(hb, w_ref[:, 0:FOX_WIDTH])
    qb = (q * (LOG2E * FOX_HEAD_DIM ** -0.5)).astype(BF16)
    q_ref[...] = qb
    k = _dot(hb, w_ref[:, FOX_WIDTH:2 * FOX_WIDTH])
    k_ref[...] = k
    kb = k.astype(BF16)
    kb_ref[...] = kb
    if want_vt:
        row = lax.broadcasted_iota(jnp.int32, (FOX_WIDTH, LANES), 0)
        col = lax.broadcasted_iota(jnp.int32, (FOX_WIDTH, LANES), 1)
        head_of = jnp.where((row >> 6) == col, 1.0, 0.0).astype(BF16)

        def bound(xb):
            xf = xb.astype(F32)
            sq = _dot((xf * xf).astype(BF16), head_of)
            return jnp.sqrt(jnp.max(sq, axis=0, keepdims=True)) * NORM_SLACK

        nrm_ref[...] = jnp.concatenate(
            [bound(kb), bound(qb), jnp.zeros((6, LANES), F32)], axis=0)
    v_ref[...] = _dot(hb, w_ref[:, 2 * FOX_WIDTH:3 * FOX_WIDTH])
    gate = _dot(hb, w_ref[:, 3 * FOX_WIDTH:4 * FOX_WIDTH])
    sg_ref[...] = gate * jax.nn.sigmoid(gate)
    fl = _dot(hb, wf_ref[...]) + bf_ref[...]
    lf_ref[...] = _log_sigmoid(fl)[:, :FOX_HEADS]
    if want_vt:
        vt_ref[...] = _dot_nt(wvt_ref[...], hb).astype(BF16)


def _fox_proj(x, g, w_main, w_f, b_f, w_vt, *, tm, want_vt):
    B, L, _ = x.shape
    n = L // tm
    tok = lambda w: pl.BlockSpec((None, tm, w), lambda b, i: (b, i, 0))
    full = lambda a: pl.BlockSpec(a.shape, lambda b, i: (0,) * a.ndim)
    in_specs = [tok(D_MODEL), full(g), full(w_main), full(w_f), full(b_f)]
    args = [x, g, w_main, w_f, b_f]
    out_specs = [tok(FOX_WIDTH)] * 5 + [tok(FOX_HEADS)]
    out_shape = [
        jax.ShapeDtypeStruct((B, L, FOX_WIDTH), BF16),
        jax.ShapeDtypeStruct((B, L, FOX_WIDTH), F32),
        jax.ShapeDtypeStruct((B, L, FOX_WIDTH), F32),
        jax.ShapeDtypeStruct((B, L, FOX_WIDTH), BF16),
        jax.ShapeDtypeStruct((B, L, FOX_WIDTH), F32),
        jax.ShapeDtypeStruct((B, L, FOX_HEADS), F32),
    ]
    if want_vt:
        in_specs.append(full(w_vt))
        args.append(w_vt)
        out_specs.append(pl.BlockSpec((None, None, FOX_WIDTH, tm), lambda b, i: (b, i, 0, 0)))
        out_shape.append(jax.ShapeDtypeStruct((B, n, FOX_WIDTH, tm), BF16))
        out_specs.append(pl.BlockSpec((None, None, 8, LANES), lambda b, i: (b, i, 0, 0)))
        out_shape.append(jax.ShapeDtypeStruct((B, n, 8, LANES), F32))
    return pl.pallas_call(
        functools.partial(_fox_proj_kernel, want_vt=want_vt),
        grid=(B, n),
        in_specs=in_specs,
        out_specs=out_specs,
        out_shape=out_shape,
        compiler_params=_params(2),
        name="fox_proj_vt" if want_vt else "fox_proj",
    )(*args)


BIAS_SLOT = 8


def _bias_selectors():
    shape = (FOX_HEADS, FOX_WIDTH)
    head = lax.broadcasted_iota(jnp.int32, shape, 0)
    col = lax.broadcasted_iota(jnp.int32, shape, 1)
    slot = LANES * (head >> 1) + BIAS_SLOT * (head & 1)
    sel_f = [jnp.where(col == slot + x, 1.0, 0.0).astype(BF16) for x in range(3)]
    sel_e = [jnp.where(col == slot + 3 + x, -1.0, 0.0).astype(BF16) for x in range(3)]
    lane = lax.broadcasted_iota(jnp.int32, (1, FOX_WIDTH), 1) & (BIAS_SLOT - 1)
    in_slots = (lax.broadcasted_iota(jnp.int32, (1, FOX_WIDTH), 1) & (LANES - 1)) < 2 * BIAS_SLOT
    one_f = jnp.where(in_slots & (lane >= 3) & (lane < 6), 1.0, 0.0)
    one_e = jnp.where(in_slots & (lane < 3), 1.0, 0.0)
    return sel_f, sel_e, one_f, one_e


def _cumsum_kernel(lf_ref, c_ref, *rest, rows, n_chunks, want_bias):
    tril = _tril(rows)
    if want_bias:
        e_ref, f_ref, rng_ref, carry_ref = rest
        sel_f, sel_e, one_f, one_e = _bias_selectors()
    else:
        (carry_ref,) = rest

    @pl.when(pl.program_id(1) == 0)
    def _():
        carry_ref[...] = jnp.zeros_like(carry_ref)

    def body(i, carry):
        r0 = pl.multiple_of(i * rows, rows)
        cs = _tril_cumsum(tril, lf_ref[pl.ds(r0, rows), :]) + carry
        c2 = cs * LOG2E
        c_ref[pl.ds(r0, rows), :] = c2
        if want_bias:
            hi, mid, lo = _split3(c2)
            f_ref[pl.ds(r0, rows), :] = (
                (_dot(hi, sel_f[0]) + _dot(mid, sel_f[1])) + _dot(lo, sel_f[2]) + one_f).astype(BF16)
            e_ref[pl.ds(r0, rows), :] = (
                (_dot(hi, sel_e[0]) + _dot(mid, sel_e[1])) + _dot(lo, sel_e[2]) + one_e).astype(BF16)
        return cs[rows - 1:rows, :]

    carry_ref[...] = lax.fori_loop(0, n_chunks, body, carry_ref[...])
    if want_bias:
        c_all = c_ref[...]
        rng_ref[...] = jnp.concatenate(
            [jnp.min(c_all, axis=0, keepdims=True), jnp.max(c_all, axis=0, keepdims=True),
             jnp.zeros((6, FOX_HEADS), F32)], axis=0)


def _cumsum(lf, *, rows, tb, want_bias):
    B, L, H = lf.shape
    spec = pl.BlockSpec((None, tb, H), lambda b, i: (b, i, 0))
    wide = pl.BlockSpec((None, tb, FOX_WIDTH), lambda b, i: (b, i, 0))
    out_specs = [spec]
    out_shape = [jax.ShapeDtypeStruct((B, L, H), F32)]
    if want_bias:
        out_specs += [wide, wide, pl.BlockSpec((None, None, 8, H), lambda b, i: (b, i, 0, 0))]
        out_shape += [jax.ShapeDtypeStruct((B, L, FOX_WIDTH), BF16)] * 2
        out_shape += [jax.ShapeDtypeStruct((B, L // tb, 8, H), F32)]
    return pl.pallas_call(
        functools.partial(_cumsum_kernel, rows=rows, n_chunks=tb // rows, want_bias=want_bias),
        grid=(B, L // tb),
        in_specs=[spec],
        out_specs=out_specs,
        out_shape=out_shape,
        scratch_shapes=[pltpu.VMEM((1, FOX_HEADS), F32)],
        compiler_params=_params(2),
        name="fox_cumsum_bias" if want_bias else "fox_cumsum",
    )(lf)


def _skip_plan_kernel(kn_ref, cmin_ref, qn_ref, cmax_ref, knq_ref, first_ref, *, nb):
    j = lax.broadcasted_iota(jnp.int32, (nb, nb), 0).astype(F32)
    q = lax.broadcasted_iota(jnp.int32, (nb, nb), 1).astype(F32)
    rows = []
    for p in range(FOX_PAIRS):
        skippable = None
        for h in (2 * p, 2 * p + 1):
            qn = qn_ref[h:h + 1, :]
            bound = (kn_ref[:, h:h + 1] * qn + (cmax_ref[h:h + 1, :] - cmin_ref[:, h:h + 1])
                     + qn * knq_ref[h:h + 1, :])
            ok = bound < -SKIP_LOG2
            skippable = ok if skippable is None else (skippable & ok)
        must = (j < q) & jnp.logical_not(skippable)
        rows.append(jnp.min(jnp.where(must, j, q), axis=0, keepdims=True))
    first_ref[...] = jnp.concatenate(rows, axis=0).astype(jnp.int32)


def _skip_plan(kn, cmin, qn, cmax):
    B, nb, H = kn.shape
    by_block = pl.BlockSpec((None, nb, H), lambda b: (b, 0, 0))
    by_head = pl.BlockSpec((None, H, nb), lambda b: (b, 0, 0))
    t = lambda a: jnp.swapaxes(a, 1, 2)
    return pl.pallas_call(
        functools.partial(_skip_plan_kernel, nb=nb),
        grid=(B,),
        in_specs=[by_block, by_block, by_head, by_head, by_head],
        out_specs=pl.BlockSpec((None, FOX_PAIRS, nb), lambda b: (b, 0, 0)),
        out_shape=jax.ShapeDtypeStruct((B, FOX_PAIRS, nb), jnp.int32),
        compiler_params=_params(1),
        name="fox_skip_plan",
    )(kn, cmin, t(qn), t(cmax), t(kn))


def _stack_heads(q2):
    lane = lax.broadcasted_iota(jnp.int32, q2.shape, 1)
    zero = jnp.zeros_like(q2)
    return jnp.concatenate(
        [jnp.where(lane < FOX_HEAD_DIM, q2, zero), jnp.where(lane >= FOX_HEAD_DIM, q2, zero)], axis=0)


ATTN_TILE = 256


def _attn_prompt_kernel(first_ref, q_ref, f_ref, k_ref, e_ref, vt_ref, o_ref,
                        m_ref, l_ref, acc_ref, sa_ref, xa_ref, sb_ref, xb_ref, *, blk):
    qi = pl.program_id(2)
    j0 = first_ref[pl.program_id(0), pl.program_id(1), qi]
    hd = FOX_HEAD_DIM
    tw = ATTN_TILE
    per_head = blk // tw
    n_tiles = 2 * per_head

    q2 = q_ref[...]
    f2 = f_ref[...]
    lane = lax.broadcasted_iota(jnp.int32, q2.shape, 1)
    zero = jnp.zeros_like(q2)
    q_heads = [
        jnp.concatenate([jnp.where(lane < hd, q2, zero), jnp.where(lane < BIAS_SLOT, f2, zero)], axis=1),
        jnp.concatenate([jnp.where(lane >= hd, q2, zero), jnp.where(lane >= BIAS_SLOT, f2, zero)], axis=1),
    ]
    q_tiles = [q_heads[t // per_head][(t % per_head) * tw:(t % per_head + 1) * tw] for t in range(n_tiles)]

    m_ref[...] = jnp.full(m_ref.shape, NEG_INF, F32)
    l_ref[...] = jnp.zeros(l_ref.shape, F32)
    acc_ref[...] = jnp.zeros(acc_ref.shape, F32)

    def scores(j, buf, diagonal):
        s_ref, x_ref = buf
        k0 = pl.multiple_of(j * blk, blk)
        kaug = jnp.concatenate([k_ref[pl.ds(k0, blk), :], e_ref[pl.ds(k0, blk), :]], axis=1)
        for t in range(n_tiles):
            s = _dot_nt(kaug, q_tiles[t])
            if diagonal:
                r = lax.broadcasted_iota(jnp.int32, (blk, tw), 0)
                c = lax.broadcasted_iota(jnp.int32, (blk, tw), 1) + (t % per_head) * tw
                s = jnp.where(r <= c, s, NEG_INF)
            s_ref[t] = s
            x_ref[t] = jnp.max(s, axis=0, keepdims=True)

    def absorb(j, buf):
        s_ref, x_ref = buf
        vt = vt_ref[j]
        for t in range(n_tiles):
            head = t // per_head
            m_old = m_ref[t]
            m_new = jnp.maximum(m_old, x_ref[t])
            alpha = jnp.exp2(m_old - m_new)
            p = jnp.exp2(s_ref[t] - m_new)
            l_ref[t] = alpha * l_ref[t] + jnp.sum(p, axis=0, keepdims=True)
            m_ref[t] = m_new
            pv = _dot(vt[head * hd:(head + 1) * hd, :], p.astype(BF16))
            acc_ref[t] = acc_ref[t] * alpha + pv

    buf_a = (sa_ref, xa_ref)
    buf_b = (sb_ref, xb_ref)
    scores(qi, buf_a, True)
    behind = qi - j0

    def pair(n, carry):
        j = qi - 2 * n
        scores(j - 1, buf_b, False)
        absorb(j, buf_a)
        scores(j - 2, buf_a, False)
        absorb(j - 1, buf_b)
        return carry

    lax.fori_loop(0, behind // 2, pair, 0)

    @pl.when(behind % 2 == 1)
    def _():
        scores(j0, buf_b, False)
        absorb(j0 + 1, buf_a)
        absorb(j0, buf_b)

    @pl.when(behind % 2 == 0)
    def _():
        absorb(j0, buf_a)

    halves = [jnp.concatenate([acc_ref[t] / l_ref[t] for t in range(h * per_head, (h + 1) * per_head)], axis=1)
              for h in range(2)]
    o_ref[...] = jnp.concatenate(halves, axis=0).T


def _attn_prompt(first, q, f, kb, e, vt, *, blk):
    B, L, _ = q.shape
    nb = L // blk
    n_tiles = 2 * blk // ATTN_TILE
    qspec = pl.BlockSpec((None, blk, LANES), lambda b, p, i, first: (b, i, p))
    kspec = pl.BlockSpec((None, L, LANES), lambda b, p, i, first: (b, 0, p))
    return pl.pallas_call(
        functools.partial(_attn_prompt_kernel, blk=blk),
        grid_spec=pltpu.PrefetchScalarGridSpec(
            num_scalar_prefetch=1,
            grid=(B, FOX_PAIRS, nb),
            in_specs=[qspec, qspec, kspec, kspec,
                      pl.BlockSpec((None, nb, LANES, blk), lambda b, p, i, first: (b, 0, p, 0))],
            out_specs=qspec,
            scratch_shapes=[
                pltpu.VMEM((n_tiles, 1, ATTN_TILE), F32),
                pltpu.VMEM((n_tiles, 1, ATTN_TILE), F32),
                pltpu.VMEM((n_tiles, FOX_HEAD_DIM, ATTN_TILE), F32),
                pltpu.VMEM((n_tiles, blk, ATTN_TILE), F32),
                pltpu.VMEM((n_tiles, 1, ATTN_TILE), F32),
                pltpu.VMEM((n_tiles, blk, ATTN_TILE), F32),
                pltpu.VMEM((n_tiles, 1, ATTN_TILE), F32),
            ],
        ),
        out_shape=jax.ShapeDtypeStruct((B, L, FOX_WIDTH), F32),
        compiler_params=_params(3),
        name="fox_attn_prompt",
    )(first, q, f, kb, e, vt)


def _attn_sample_kernel(q_ref, kn_ref, vn_ref, ck_ref, cv_ref, c_ref, ct_ref, o_ref, *, past, ls):
    p = pl.program_id(1)
    hd = FOX_HEAD_DIM
    qst = _stack_heads(q_ref[...])
    kc = ck_ref[...].astype(BF16)
    vc = cv_ref[...].astype(BF16)
    pad_k = jnp.zeros((LANES - ls, LANES), BF16)
    kn = jnp.concatenate([kn_ref[...], pad_k], axis=0)
    vn = jnp.concatenate([vn_ref[...].astype(BF16), pad_k], axis=0)

    cqb = c_ref[past:past + ls, :]
    lane = lax.broadcasted_iota(jnp.int32, cqb.shape, 1)
    cq = jnp.concatenate(
        [jnp.sum(jnp.where(lane == 2 * p + h2, cqb, 0.0), axis=1, keepdims=True) for h2 in range(2)],
        axis=0)
    cka = ct_ref[pl.ds(2 * p, 1), :]
    ckb = ct_ref[pl.ds(2 * p + 1, 1), :]

    def bias(lo, hi):
        return jnp.concatenate([cq[:ls] - cka[:, lo:hi], cq[ls:] - ckb[:, lo:hi]], axis=0)

    s_c = _dot_nt(qst, kc) + bias(0, past)
    s_n = _dot_nt(qst, kn) + bias(past, past + LANES)
    r = lax.broadcasted_iota(jnp.int32, s_n.shape, 0) & (ls - 1)
    c = lax.broadcasted_iota(jnp.int32, s_n.shape, 1)
    s_n = jnp.where(c <= r, s_n, NEG_INF)
    m = jnp.maximum(jnp.max(s_c, axis=1, keepdims=True), jnp.max(s_n, axis=1, keepdims=True))
    p_c = jnp.exp2(s_c - m)
    p_n = jnp.exp2(s_n - m)
    l = jnp.sum(p_c, axis=1, keepdims=True) + jnp.sum(p_n, axis=1, keepdims=True)
    o = (_dot(p_c.astype(BF16), vc) + _dot(p_n.astype(BF16), vn)) / l
    lane_o = lax.broadcasted_iota(jnp.int32, (ls, LANES), 1)
    o_ref[...] = jnp.where(lane_o < hd, o[:ls], o[ls:])


def _attn_sample(q, kb, v, cache_k, cache_v, c, ct, *, past):
    B, ls, _ = q.shape
    tot = c.shape[1]
    tile = lambda rows: pl.BlockSpec((None, rows, LANES), lambda b, p: (b, 0, p))
    return pl.pallas_call(
        functools.partial(_attn_sample_kernel, past=past, ls=ls),
        grid=(B, FOX_PAIRS),
        in_specs=[
            tile(ls), tile(ls), tile(ls), tile(past), tile(past),
            pl.BlockSpec((None, tot, FOX_HEADS), lambda b, p: (b, 0, 0)),
            pl.BlockSpec((None, FOX_HEADS, tot), lambda b, p: (b, 0, 0)),
        ],
        out_specs=tile(ls),
        out_shape=jax.ShapeDtypeStruct((B, ls, FOX_WIDTH), F32),
        compiler_params=_params(2),
        name="fox_attn_sample",
    )(q, kb, v, cache_k, cache_v, c, ct)


def _mid_kernel(o_ref, sg_ref, x_ref, wo_ref, g_ref, w_ref, wa1_ref, wa2_ref, ba_ref,
                y_ref, q_ref, k_ref, v_ref, sg2_ref, ga_ref):
    u = (o_ref[...] * sg_ref[...]).astype(BF16)
    y = x_ref[...] + _dot(u, wo_ref[...])
    y_ref[...] = y
    hb = _rms(y, g_ref[...]).astype(BF16)
    kw = GLA_KEY_WIDTH
    q_ref[...] = _dot(hb, w_ref[:, 0:kw]) * (GLA_HEAD_K ** -0.5)
    k_ref[...] = _dot(hb, w_ref[:, kw:2 * kw])
    v_ref[...] = _dot(hb, w_ref[:, 2 * kw:2 * kw + GLA_VAL_WIDTH]).astype(BF16)
    gate = _dot(hb, w_ref[:, 2 * kw + GLA_VAL_WIDTH:2 * kw + 2 * GLA_VAL_WIDTH])
    sg2_ref[...] = gate * jax.nn.sigmoid(gate)
    a1 = _dot(hb, wa1_ref[...]).astype(BF16)
    a = _dot(a1, wa2_ref[...]) + ba_ref[...]
    ga_ref[...] = _log_sigmoid(a) * (1.0 / GLA_GATE_TEMP)


def _mid(o, sg, x, w_out, g, w_main, w_a1, w_a2, b_a, *, tm):
    T = o.shape[0]
    tok = lambda w: pl.BlockSpec((tm, w), lambda i: (i, 0))
    full = lambda a: pl.BlockSpec(a.shape, lambda i: (0,) * a.ndim)
    return pl.pallas_call(
        _mid_kernel,
        grid=(T // tm,),
        in_specs=[tok(D_MODEL), tok(D_MODEL), tok(D_MODEL), full(w_out), full(g), full(w_main),
                  full(w_a1), full(w_a2), full(b_a)],
        out_specs=[tok(D_MODEL), tok(GLA_KEY_WIDTH), tok(GLA_KEY_WIDTH), tok(GLA_VAL_WIDTH),
                   tok(GLA_VAL_WIDTH), tok(GLA_KEY_WIDTH)],
        out_shape=[
            jax.ShapeDtypeStruct((T, D_MODEL), F32),
            jax.ShapeDtypeStruct((T, GLA_KEY_WIDTH), F32),
            jax.ShapeDtypeStruct((T, GLA_KEY_WIDTH), F32),
            jax.ShapeDtypeStruct((T, GLA_VAL_WIDTH), BF16),
            jax.ShapeDtypeStruct((T, GLA_VAL_WIDTH), F32),
            jax.ShapeDtypeStruct((T, GLA_KEY_WIDTH), F32),
        ],
        compiler_params=_params(1),
        name="fox_out_gla_proj",
    )(o, sg, x, w_out, g, w_main, w_a1, w_a2, b_a)


def _gla_kernel(q_ref, k_ref, v_ref, ga_ref, s0_ref, o_ref, s_ref, *, n_chunks):
    i = pl.program_id(1)
    ch = GLA_CHUNK
    dk, dv = GLA_HEAD_K, GLA_HEAD_V

    @pl.when(i == 0)
    def _():
        s_ref[...] = s0_ref[...]

    tril = _tril(ch)
    r = lax.broadcasted_iota(jnp.int32, (ch, ch), 0)
    c = lax.broadcasted_iota(jnp.int32, (ch, ch), 1)
    causal = r >= c
    eye = (lax.broadcasted_iota(jnp.int32, (dk, dk), 0) == lax.broadcasted_iota(jnp.int32, (dk, dk), 1))

    for ci in range(n_chunks):
        rows = slice(ci * ch, (ci + 1) * ch)
        b = _tril_cumsum(tril, ga_ref[rows, :])
        b_last = b[ch - 1:ch, :]
        eb = jnp.exp(b)
        qe = (q_ref[rows, :] * eb).astype(BF16)
        kk = k_ref[rows, :]
        ke = (kk * jnp.exp(-b)).astype(BF16)
        kd = (kk * jnp.exp(b_last - b)).astype(BF16)
        dec = jnp.exp(b_last)
        for h in range(GLA_HEADS):
            ks = slice(h * dk, (h + 1) * dk)
            vs = slice(h * dv, (h + 1) * dv)
            vh = v_ref[rows, vs]
            a = jnp.where(causal, _dot_nt(qe[:, ks], ke[:, ks]), 0.0)
            s_old = s_ref[h]
            o_ref[rows, vs] = _dot(a.astype(BF16), vh) + _dot(qe[:, ks], s_old.astype(BF16))
            dec_col = jnp.sum(jnp.where(eye, jnp.broadcast_to(dec[:, ks], (dk, dk)), 0.0),
                              axis=1, keepdims=True)
            s_ref[h] = dec_col * s_old + _dot_tn(kd[:, ks], vh)


def _gla(q, k, v, ga, s0, *, tc):
    B, L, _ = q.shape
    tok = lambda w: pl.BlockSpec((None, tc, w), lambda b, i: (b, i, 0))
    st = pl.BlockSpec((None, GLA_HEADS, GLA_HEAD_K, GLA_HEAD_V), lambda b, i: (b, 0, 0, 0))
    return pl.pallas_call(
        functools.partial(_gla_kernel, n_chunks=tc // GLA_CHUNK),
        grid=(B, L // tc),
        in_specs=[tok(GLA_KEY_WIDTH), tok(GLA_KEY_WIDTH), tok(GLA_VAL_WIDTH), tok(GLA_KEY_WIDTH), st],
        out_specs=[tok(GLA_VAL_WIDTH), st],
        out_shape=[
            jax.ShapeDtypeStruct((B, L, GLA_VAL_WIDTH), F32),
            jax.ShapeDtypeStruct((B, GLA_HEADS, GLA_HEAD_K, GLA_HEAD_V), F32),
        ],
        compiler_params=_params(2),
        name="gla_chunk",
    )(q, k, v, ga, s0)


def _gla_out_kernel(*refs, n_parts):
    o_refs = refs[:n_parts]
    sg_ref, y_ref, go_ref, wo_ref, gf_ref, out_ref = refs[n_parts:]
    dv = GLA_HEAD_V
    o = jnp.concatenate([r[...] for r in o_refs], axis=0) if n_parts > 1 else o_refs[0][...]
    parts = []
    for h in range(GLA_HEADS):
        vs = slice(h * dv, (h + 1) * dv)
        parts.append((_rms(o[:, vs], go_ref[...]) * sg_ref[:, vs]).astype(BF16))
    u = jnp.concatenate(parts, axis=1)
    y = y_ref[...] + _dot(u, wo_ref[...])
    out_ref[...] = _rms(y, gf_ref[...])


def _gla_out(o, sg, y, g_o, w_out, g_f, *, tm):
    B, L, _ = o.shape
    n = L // GLA_CHUNK
    if n == 1:
        flat = lambda a: a.reshape(1, B * L, D_MODEL)
        o, sg, y = flat(o), flat(sg), flat(y)
        n_parts = 1
        o_specs = [pl.BlockSpec((None, tm, D_MODEL), lambda b, i: (b, i, 0))]
    else:
        n_parts = tm // n
        o = o.reshape(B, n, GLA_CHUNK * D_MODEL)
        o_specs = [pl.BlockSpec((None, n, D_MODEL), functools.partial(_part_index, n_parts=n_parts, part=kk))
                   for kk in range(n_parts)]
    nb, rows, _ = sg.shape
    tok = pl.BlockSpec((None, tm, D_MODEL), lambda b, i: (b, i, 0))
    full = lambda a: pl.BlockSpec(a.shape, lambda b, i: (0,) * a.ndim)
    return pl.pallas_call(
        functools.partial(_gla_out_kernel, n_parts=n_parts),
        grid=(nb, rows // tm),
        in_specs=o_specs + [tok, tok, full(g_o), full(w_out), full(g_f)],
        out_specs=tok,
        out_shape=jax.ShapeDtypeStruct((nb, rows, D_MODEL), F32),
        compiler_params=_params(2),
        name="gla_out",
    )(*([o] * n_parts), sg, y, g_o, w_out, g_f)


def _part_index(b, i, *, n_parts, part):
    return (b, 0, n_parts * i + part)


def _pad_cols(w, n):
    return jnp.pad(w, ((0, 0), (0, n - w.shape[1])))


def kernel(x_prompt, x_sample, cache_fox_k, cache_fox_v, cache_fox_logf, state_gla,
           g_norm_fox, w_in_fox, b_fox_f, w_out_fox,
           g_norm_gla, w_in_gla, w_gla_a2, b_gla_a, g_gla_o, w_out_gla, g_final):
    B, L, D = x_prompt.shape
    Bs, Ls, _ = x_sample.shape
    P = cache_fox_k.shape[1]
    H, hd = FOX_HEADS, FOX_HEAD_DIM

    row = lambda a: a.reshape(1, -1).astype(F32)
    w_fox = w_in_fox[:, :4 * FOX_WIDTH].astype(BF16)
    w_fox_f = _pad_cols(w_in_fox[:, 4 * FOX_WIDTH:], LANES).astype(BF16)
    b_f = _pad_cols(row(b_fox_f), LANES)
    w_fox_vt = w_in_fox[:, 2 * FOX_WIDTH:3 * FOX_WIDTH].T.astype(BF16)
    w_o_fox = w_out_fox.astype(BF16)
    n_main = 2 * GLA_KEY_WIDTH + 2 * GLA_VAL_WIDTH
    w_gla = w_in_gla[:, :n_main].astype(BF16)
    w_a1 = _pad_cols(w_in_gla[:, n_main:], LANES).astype(BF16)
    w_a2 = jnp.pad(w_gla_a2, ((0, LANES - GLA_GATE_RANK), (0, 0))).astype(BF16)
    w_o_gla = w_out_gla.astype(BF16)

    q_p, k_p, v_p, kb_p, sg_p, lf_p, vt_p, nrm_p = _fox_proj(
        x_prompt, row(g_norm_fox), w_fox, w_fox_f, b_f, w_fox_vt, tm=ATTN_BLOCK, want_vt=True)
    _, e_p, f_p, rng_p = _cumsum(lf_p, rows=256, tb=ATTN_BLOCK, want_bias=True)
    first_p = _skip_plan(nrm_p[:, :, 0, :H], rng_p[:, :, 0, :], nrm_p[:, :, 1, :H], rng_p[:, :, 1, :])
    o_p = _attn_prompt(first_p, q_p, f_p, kb_p, e_p, vt_p, blk=ATTN_BLOCK)

    xs = x_sample.reshape(1, Bs * Ls, D)
    q_s, k_s, v_s, kb_s, sg_s, lf_s = _fox_proj(
        xs, row(g_norm_fox), w_fox, w_fox_f, b_f, None, tm=PROJ_TM, want_vt=False)
    sh = lambda a: a.reshape(Bs, Ls, a.shape[-1])
    q_s, k_s, v_s, kb_s, lf_s = sh(q_s), sh(k_s), sh(v_s), sh(kb_s), sh(lf_s)
    lf_all = jnp.concatenate(
        [cache_fox_logf.astype(F32), lf_s, jnp.zeros((Bs, LANES - Ls, H), F32)], axis=1)
    (c_s,) = _cumsum(lf_all, rows=LANES, tb=P + LANES, want_bias=False)
    o_s = _attn_sample(q_s, kb_s, v_s, cache_fox_k.reshape(Bs, P, FOX_WIDTH),
                       cache_fox_v.reshape(Bs, P, FOX_WIDTH), c_s, jnp.swapaxes(c_s, 1, 2), past=P)

    def gla_layer(o, sg, x, s0, tc):
        b, l, _ = x.shape
        flat = lambda a: a.reshape(b * l, a.shape[-1])
        y1, q, k, v, sg2, ga = _mid(flat(o), flat(sg), flat(x), w_o_fox, row(g_norm_gla), w_gla,
                                    w_a1, w_a2, row(b_gla_a), tm=PROJ_TM)
        un = lambda a: a.reshape(b, l, a.shape[-1])
        og, s_out = _gla(un(q), un(k), un(v), un(ga), s0, tc=tc)
        y = _gla_out(og, un(sg2), un(y1), row(g_gla_o), w_o_gla, row(g_final), tm=PROJ_TM)
        return y.reshape(b, l, D), s_out

    s0_p = jnp.zeros((B, GLA_HEADS, GLA_HEAD_K, GLA_HEAD_V), F32)
    y_p, s_p = gla_layer(o_p, sg_p, x_prompt, s0_p, GLA_TC)
    y_s, s_s = gla_layer(o_s, sg_s.reshape(Bs, Ls, D), x_sample, state_gla.astype(F32), Ls)

    return (y_p, y_s,
            k_p.reshape(B, L, H, hd), v_p.reshape(B, L, H, hd), lf_p, s_p.astype(state_gla.dtype),
            k_s.reshape(Bs, Ls, H, hd), v_s.reshape(Bs, Ls, H, hd), lf_s, s_s.astype(state_gla.dtype))
```

```python
import functools

import jax
import jax.numpy as jnp
from jax import lax
from jax.experimental import pallas as pl
from jax.experimental.pallas import tpu as pltpu

F32 = jnp.float32
BF16 = jnp.bfloat16

D_MODEL = 1024
EPS = 1e-6
NEG_INF = -1e30
LOG2E = 1.4426950408889634
NORM_SLACK = 1.01
SKIP_LOG2 = 160.0

FOX_HEADS = 16
FOX_HEAD_DIM = 64
FOX_WIDTH = FOX_HEADS * FOX_HEAD_DIM
FOX_PAIRS = FOX_HEADS // 2

GLA_HEADS = 4
GLA_KEY_WIDTH = 512
GLA_VAL_WIDTH = 1024
GLA_HEAD_K = 128
GLA_HEAD_V = 256
GLA_GATE_RANK = 16
GLA_GATE_TEMP = 16.0
GLA_CHUNK = 64

LANES = 128
VMEM_LIMIT = 56 * 1024 * 1024

PROJ_TM = 512
ATTN_BLOCK = 512
GLA_TC = 256


def _params(n_axes):
    return pltpu.CompilerParams(
        dimension_semantics=("arbitrary",) * n_axes,
        vmem_limit_bytes=VMEM_LIMIT,
    )


def _dot(a, b):
    return jnp.dot(a, b, preferred_element_type=F32)


def _dot_nt(a, b):
    return lax.dot_general(a, b, (((1,), (1,)), ((), ())), preferred_element_type=F32)


def _dot_tn(a, b):
    return lax.dot_general(a, b, (((0,), (0,)), ((), ())), preferred_element_type=F32)


def _log_sigmoid(z):
    return jnp.minimum(z, 0.0) - jnp.log1p(jnp.exp(-jnp.abs(z)))


def _rms(x, g):
    ms = jnp.mean(x * x, axis=-1, keepdims=True)
    return (x * lax.rsqrt(ms + EPS)) * g


def _split3(x):
    hi = x.astype(BF16)
    r1 = x - hi.astype(F32)
    mid = r1.astype(BF16)
    lo = (r1 - mid.astype(F32)).astype(BF16)
    return hi, mid, lo


def _tril_cumsum(tril_bf16, x):
    hi, mid, lo = _split3(x)
    return (_dot(tril_bf16, hi) + _dot(tril_bf16, mid)) + _dot(tril_bf16, lo)


def _tril(n):
    r = lax.broadcasted_iota(jnp.int32, (n, n), 0)
    c = lax.broadcasted_iota(jnp.int32, (n, n), 1)
    return jnp.where(r >= c, 1.0, 0.0).astype(BF16)


def _fox_proj_kernel(x_ref, g_ref, w_ref, wf_ref, bf_ref, *rest, want_vt):
    if want_vt:
        q_ref, k_ref, v_ref, kb_ref, sg_ref, lf_ref, vt_ref, nrm_ref = rest
    else:
        q_ref, k_ref, v_ref, kb_ref, sg_ref, lf_ref = rest
    hb = _rms(x_ref[...], g_ref[...]).astype(BF16)
    q = _dot(hb, w_ref[:, 0:FOX_WIDTH])
    qb = (q * (LOG2E * FOX_HEAD_DIM ** -0.5)).astype(BF16)
    q_ref[...] = qb
    k = _dot(hb, w_ref[:, FOX_WIDTH:2 * FOX_WIDTH])
    k_ref[...] = k
    kb = k.astype(BF16)
    kb_ref[...] = kb
    if want_vt:
        row = lax.broadcasted_iota(jnp.int32, (FOX_WIDTH, LANES), 0)
        col = lax.broadcasted_iota(jnp.int32, (FOX_WIDTH, LANES), 1)
        head_of = jnp.where((row >> 6) == col, 1.0, 0.0).astype(BF16)

        def bound(xb):
            xf = xb.astype(F32)
            sq = _dot((xf * xf).astype(BF16), head_of)
            return jnp.sqrt(jnp.max(sq, axis=0, keepdims=True)) * NORM_SLACK

        nrm_ref[...] = jnp.concatenate(
            [bound(kb), bound(qb), jnp.zeros((6, LANES), F32)], axis=0)
    v = _dot(hb, w_ref[:, 2 * FOX_WIDTH:3 * FOX_WIDTH])
    v_ref[...] = v
    if want_vt:
        vt_ref[...] = v.astype(BF16).T
    gate = _dot(hb, w_ref[:, 3 * FOX_WIDTH:4 * FOX_WIDTH])
    sg_ref[...] = gate * jax.nn.sigmoid(gate)
    fl = _dot(hb, wf_ref[...]) + bf_ref[...]
    lf_ref[...] = _log_sigmoid(fl)[:, :FOX_HEADS]


def _fox_proj(x, g, w_main, w_f, b_f, *, tm, want_vt):
    B, L, _ = x.shape
    n = L // tm
    tok = lambda w: pl.BlockSpec((None, tm, w), lambda b, i: (b, i, 0))
    full = lambda a: pl.BlockSpec(a.shape, lambda b, i: (0,) * a.ndim)
    in_specs = [tok(D_MODEL), full(g), full(w_main), full(w_f), full(b_f)]
    args = [x, g, w_main, w_f, b_f]
    out_specs = [tok(FOX_WIDTH)] * 5 + [tok(FOX_HEADS)]
    out_shape = [
        jax.ShapeDtypeStruct((B, L, FOX_WIDTH), BF16),
        jax.ShapeDtypeStruct((B, L, FOX_WIDTH), F32),
        jax.ShapeDtypeStruct((B, L, FOX_WIDTH), F32),
        jax.ShapeDtypeStruct((B, L, FOX_WIDTH), BF16),
        jax.ShapeDtypeStruct((B, L, FOX_WIDTH), F32),
        jax.ShapeDtypeStruct((B, L, FOX_HEADS), F32),
    ]
    if want_vt:
        out_specs.append(pl.BlockSpec((None, None, FOX_WIDTH, tm), lambda b, i: (b, i, 0, 0)))
        out_shape.append(jax.ShapeDtypeStruct((B, n, FOX_WIDTH, tm), BF16))
        out_specs.append(pl.BlockSpec((None, None, 8, LANES), lambda b, i: (b, i, 0, 0)))
        out_shape.append(jax.ShapeDtypeStruct((B, n, 8, LANES), F32))
    return pl.pallas_call(
        functools.partial(_fox_proj_kernel, want_vt=want_vt),
        grid=(B, n),
        in_specs=in_specs,
        out_specs=out_specs,
        out_shape=out_shape,
        compiler_params=_params(2),
        name="fox_proj_vt" if want_vt else "fox_proj",
    )(*args)


BIAS_SLOT = 8


def _bias_selectors():
    shape = (FOX_HEADS, LANES)
    head = lax.broadcasted_iota(jnp.int32, shape, 0)
    col = lax.broadcasted_iota(jnp.int32, shape, 1)
    slot = BIAS_SLOT * head
    sel_f = [jnp.where(col == slot + x, 1.0, 0.0).astype(BF16) for x in range(3)]
    sel_e = [jnp.where(col == slot + 3 + x, -1.0, 0.0).astype(BF16) for x in range(3)]
    lane = lax.broadcasted_iota(jnp.int32, (1, LANES), 1) & (BIAS_SLOT - 1)
    one_f = jnp.where((lane >= 3) & (lane < 6), 1.0, 0.0)
    one_e = jnp.where(lane < 3, 1.0, 0.0)
    return sel_f, sel_e, one_f, one_e


def _cumsum_kernel(lf_ref, c_ref, *rest, rows, n_chunks, want_bias):
    tril = _tril(rows)
    if want_bias:
        e_ref, f_ref, rng_ref, carry_ref = rest
        sel_f, sel_e, one_f, one_e = _bias_selectors()
    else:
        (carry_ref,) = rest

    @pl.when(pl.program_id(1) == 0)
    def _():
        carry_ref[...] = jnp.zeros_like(carry_ref)

    def body(i, carry):
        r0 = pl.multiple_of(i * rows, rows)
        cs = _tril_cumsum(tril, lf_ref[pl.ds(r0, rows), :]) + carry
        c2 = cs * LOG2E
        c_ref[pl.ds(r0, rows), :] = c2
        if want_bias:
            hi, mid, lo = _split3(c2)
            f_ref[pl.ds(r0, rows), :] = (
                (_dot(hi, sel_f[0]) + _dot(mid, sel_f[1])) + _dot(lo, sel_f[2]) + one_f).astype(BF16)
            e_ref[pl.ds(r0, rows), :] = (
                (_dot(hi, sel_e[0]) + _dot(mid, sel_e[1])) + _dot(lo, sel_e[2]) + one_e).astype(BF16)
        return cs[rows - 1:rows, :]

    carry_ref[...] = lax.fori_loop(0, n_chunks, body, carry_ref[...])
    if want_bias:
        c_all = c_ref[...]
        rng_ref[...] = jnp.concatenate(
            [jnp.min(c_all, axis=0, keepdims=True), jnp.max(c_all, axis=0, keepdims=True),
             jnp.zeros((6, FOX_HEADS), F32)], axis=0)


def _cumsum(lf, *, rows, tb, want_bias):
    B, L, H = lf.shape
    spec = pl.BlockSpec((None, tb, H), lambda b, i: (b, i, 0))
    wide = pl.BlockSpec((None, tb, LANES), lambda b, i: (b, i, 0))
    out_specs = [spec]
    out_shape = [jax.ShapeDtypeStruct((B, L, H), F32)]
    if want_bias:
        out_specs += [wide, wide, pl.BlockSpec((None, None, 8, H), lambda b, i: (b, i, 0, 0))]
        out_shape += [jax.ShapeDtypeStruct((B, L, LANES), BF16)] * 2
        out_shape += [jax.ShapeDtypeStruct((B, L // tb, 8, H), F32)]
    return pl.pallas_call(
        functools.partial(_cumsum_kernel, rows=rows, n_chunks=tb // rows, want_bias=want_bias),
        grid=(B, L // tb),
        in_specs=[spec],
        out_specs=out_specs,
        out_shape=out_shape,
        scratch_shapes=[pltpu.VMEM((1, FOX_HEADS), F32)],
        compiler_params=_params(2),
        name="fox_cumsum_bias" if want_bias else "fox_cumsum",
    )(lf)


def _skip_plan_kernel(kn_ref, cmin_ref, qn_ref, cmax_ref, knq_ref, first_ref, *, nb):
    j = lax.broadcasted_iota(jnp.int32, (nb, nb), 0).astype(F32)
    q = lax.broadcasted_iota(jnp.int32, (nb, nb), 1).astype(F32)
    rows = []
    for p in range(FOX_PAIRS):
        skippable = None
        for h in (2 * p, 2 * p + 1):
            qn = qn_ref[h:h + 1, :]
            bound = (kn_ref[:, h:h + 1] * qn + (cmax_ref[h:h + 1, :] - cmin_ref[:, h:h + 1])
                     + qn * knq_ref[h:h + 1, :])
            ok = bound < -SKIP_LOG2
            skippable = ok if skippable is None else (skippable & ok)
        must = (j < q) & jnp.logical_not(skippable)
        rows.append(jnp.min(jnp.where(must, j, q), axis=0, keepdims=True))
    first_ref[...] = jnp.concatenate(rows, axis=0).astype(jnp.int32)


def _skip_plan(kn, cmin, qn, cmax):
    B, nb, H = kn.shape
    by_block = pl.BlockSpec((None, nb, H), lambda b: (b, 0, 0))
    by_head = pl.BlockSpec((None, H, nb), lambda b: (b, 0, 0))
    t = lambda a: jnp.swapaxes(a, 1, 2)
    return pl.pallas_call(
        functools.partial(_skip_plan_kernel, nb=nb),
        grid=(B,),
        in_specs=[by_block, by_block, by_head, by_head, by_head],
        out_specs=pl.BlockSpec((None, FOX_PAIRS, nb), lambda b: (b, 0, 0)),
        out_shape=jax.ShapeDtypeStruct((B, FOX_PAIRS, nb), jnp.int32),
        compiler_params=_params(1),
        name="fox_skip_plan",
    )(kn, cmin, t(qn), t(cmax), t(kn))


def _stack_heads(q2):
    lane = lax.broadcasted_iota(jnp.int32, q2.shape, 1)
    zero = jnp.zeros_like(q2)
    return jnp.concatenate(
        [jnp.where(lane < FOX_HEAD_DIM, q2, zero), jnp.where(lane >= FOX_HEAD_DIM, q2, zero)], axis=0)


ATTN_TILE = 256


def _attn_prompt_kernel(first_ref, q_ref, f_ref, k_ref, e_ref, vt_ref, o_ref,
                        m_ref, l_ref, acc_ref, sa_ref, xa_ref, sb_ref, xb_ref, *, blk):
    qi = pl.program_id(2)
    j0 = first_ref[pl.program_id(0), pl.program_id(1), qi]
    hd = FOX_HEAD_DIM
    tw = ATTN_TILE
    per_head = blk // tw
    n_tiles = 2 * per_head

    q2 = q_ref[...]
    f2 = f_ref[...]
    lane = lax.broadcasted_iota(jnp.int32, q2.shape, 1)
    zero = jnp.zeros_like(q2)
    slot_a = 2 * BIAS_SLOT * pl.program_id(1)
    slot_b = slot_a + BIAS_SLOT
    q_heads = [
        jnp.concatenate([jnp.where(lane < hd, q2, zero),
                         jnp.where((lane >= slot_a) & (lane < slot_b), f2, zero)], axis=1),
        jnp.concatenate([jnp.where(lane >= hd, q2, zero),
                         jnp.where((lane >= slot_b) & (lane < slot_b + BIAS_SLOT), f2, zero)], axis=1),
    ]
    q_tiles = [q_heads[t // per_head][(t % per_head) * tw:(t % per_head + 1) * tw] for t in range(n_tiles)]

    m_ref[...] = jnp.full(m_ref.shape, NEG_INF, F32)
    l_ref[...] = jnp.zeros(l_ref.shape, F32)
    acc_ref[...] = jnp.zeros(acc_ref.shape, F32)

    def scores(j, buf, diagonal):
        s_ref, x_ref = buf
        k0 = pl.multiple_of(j * blk, blk)
        kaug = jnp.concatenate([k_ref[pl.ds(k0, blk), :], e_ref[pl.ds(k0, blk), :]], axis=1)
        for t in range(n_tiles):
            s = _dot_nt(kaug, q_tiles[t])
            if diagonal:
                r = lax.broadcasted_iota(jnp.int32, (blk, tw), 0)
                c = lax.broadcasted_iota(jnp.int32, (blk, tw), 1) + (t % per_head) * tw
                s = jnp.where(r <= c, s, NEG_INF)
            s_ref[t] = s
            x_ref[t] = jnp.max(s, axis=0, keepdims=True)

    def absorb(j, buf):
        s_ref, x_ref = buf
        vt = vt_ref[j]
        for t in range(n_tiles):
            head = t // per_head
            m_old = m_ref[t]
            m_new = jnp.maximum(m_old, x_ref[t])
            alpha = jnp.exp2(m_old - m_new)
            p = jnp.exp2(s_ref[t] - m_new)
            l_ref[t] = alpha * l_ref[t] + jnp.sum(p, axis=0, keepdims=True)
            m_ref[t] = m_new
            pv = _dot(vt[head * hd:(head + 1) * hd, :], p.astype(BF16))
            acc_ref[t] = acc_ref[t] * alpha + pv

    buf_a = (sa_ref, xa_ref)
    buf_b = (sb_ref, xb_ref)
    scores(qi, buf_a, True)
    behind = qi - j0

    def pair(n, carry):
        j = qi - 2 * n
        scores(j - 1, buf_b, False)
        absorb(j, buf_a)
        scores(j - 2, buf_a, False)
        absorb(j - 1, buf_b)
        return carry

    lax.fori_loop(0, behind // 2, pair, 0)

    @pl.when(behind % 2 == 1)
    def _():
        scores(j0, buf_b, False)
        absorb(j0 + 1, buf_a)
        absorb(j0, buf_b)

    @pl.when(behind % 2 == 0)
    def _():
        absorb(j0, buf_a)

    halves = [jnp.concatenate([acc_ref[t] / l_ref[t] for t in range(h * per_head, (h + 1) * per_head)], axis=1)
              for h in range(2)]
    o_ref[...] = jnp.concatenate(halves, axis=0).T


def _attn_prompt(first, q, f, kb, e, vt, *, blk):
    B, L, _ = q.shape
    nb = L // blk
    n_tiles = 2 * blk // ATTN_TILE
    qspec = pl.BlockSpec((None, blk, LANES), lambda b, p, i, first: (b, i, p))
    kspec = pl.BlockSpec((None, L, LANES), lambda b, p, i, first: (b, 0, p))
    return pl.pallas_call(
        functools.partial(_attn_prompt_kernel, blk=blk),
        grid_spec=pltpu.PrefetchScalarGridSpec(
            num_scalar_prefetch=1,
            grid=(B, FOX_PAIRS, nb),
            in_specs=[qspec,
                      pl.BlockSpec((None, blk, LANES), lambda b, p, i, first: (b, i, 0)),
                      kspec,
                      pl.BlockSpec((None, L, LANES), lambda b, p, i, first: (b, 0, 0)),
                      pl.BlockSpec((None, nb, LANES, blk), lambda b, p, i, first: (b, 0, p, 0))],
            out_specs=qspec,
            scratch_shapes=[
                pltpu.VMEM((n_tiles, 1, ATTN_TILE), F32),
                pltpu.VMEM((n_tiles, 1, ATTN_TILE), F32),
                pltpu.VMEM((n_tiles, FOX_HEAD_DIM, ATTN_TILE), F32),
                pltpu.VMEM((n_tiles, blk, ATTN_TILE), F32),
                pltpu.VMEM((n_tiles, 1, ATTN_TILE), F32),
                pltpu.VMEM((n_tiles, blk, ATTN_TILE), F32),
                pltpu.VMEM((n_tiles, 1, ATTN_TILE), F32),
            ],
        ),
        out_shape=jax.ShapeDtypeStruct((B, L, FOX_WIDTH), F32),
        compiler_params=_params(3),
        name="fox_attn_prompt",
    )(first, q, f, kb, e, vt)


def _attn_sample_kernel(q_ref, kn_ref, vn_ref, ck_ref, cv_ref, c_ref, ct_ref, o_ref, *, past, ls):
    p = pl.program_id(1)
    hd = FOX_HEAD_DIM
    qst = _stack_heads(q_ref[...])
    kc = ck_ref[...].astype(BF16)
    vc = cv_ref[...].astype(BF16)
    pad_k = jnp.zeros((LANES - ls, LANES), BF16)
    kn = jnp.concatenate([kn_ref[...], pad_k], axis=0)
    vn = jnp.concatenate([vn_ref[...].astype(BF16), pad_k], axis=0)

    cqb = c_ref[past:past + ls, :]
    lane = lax.broadcasted_iota(jnp.int32, cqb.shape, 1)
    cq = jnp.concatenate(
        [jnp.sum(jnp.where(lane == 2 * p + h2, cqb, 0.0), axis=1, keepdims=True) for h2 in range(2)],
        axis=0)
    cka = ct_ref[pl.ds(2 * p, 1), :]
    ckb = ct_ref[pl.ds(2 * p + 1, 1), :]

    def bias(lo, hi):
        return jnp.concatenate([cq[:ls] - cka[:, lo:hi], cq[ls:] - ckb[:, lo:hi]], axis=0)

    s_c = _dot_nt(qst, kc) + bias(0, past)
    s_n = _dot_nt(qst, kn) + bias(past, past + LANES)
    r = lax.broadcasted_iota(jnp.int32, s_n.shape, 0) & (ls - 1)
    c = lax.broadcasted_iota(jnp.int32, s_n.shape, 1)
    s_n = jnp.where(c <= r, s_n, NEG_INF)
    m = jnp.maximum(jnp.max(s_c, axis=1, keepdims=True), jnp.max(s_n, axis=1, keepdims=True))
    p_c = jnp.exp2(s_c - m)
    p_n = jnp.exp2(s_n - m)
    l = jnp.sum(p_c, axis=1, keepdims=True) + jnp.sum(p_n, axis=1, keepdims=True)
    o = (_dot(p_c.astype(BF16), vc) + _dot(p_n.astype(BF16), vn)) / l
    lane_o = lax.broadcasted_iota(jnp.int32, (ls, LANES), 1)
    o_ref[...] = jnp.where(lane_o < hd, o[:ls], o[ls:])


def _attn_sample(q, kb, v, cache_k, cache_v, c, ct, *, past):
    B, ls, _ = q.shape
    tot = c.shape[1]
    tile = lambda rows: pl.BlockSpec((None, rows, LANES), lambda b, p: (b, 0, p))
    return pl.pallas_call(
        functools.partial(_attn_sample_kernel, past=past, ls=ls),
        grid=(B, FOX_PAIRS),
        in_specs=[
            tile(ls), tile(ls), tile(ls), tile(past), tile(past),
            pl.BlockSpec((None, tot, FOX_HEADS), lambda b, p: (b, 0, 0)),
            pl.BlockSpec((None, FOX_HEADS, tot), lambda b, p: (b, 0, 0)),
        ],
        out_specs=tile(ls),
        out_shape=jax.ShapeDtypeStruct((B, ls, FOX_WIDTH), F32),
        compiler_params=_params(2),
        name="fox_attn_sample",
    )(q, kb, v, cache_k, cache_v, c, ct)


def _mid_kernel(o_ref, sg_ref, x_ref, wo_ref, g_ref, w_ref, wa1_ref, wa2_ref, ba_ref,
                y_ref, q_ref, k_ref, v_ref, sg2_ref, ga_ref):
    u = (o_ref[...] * sg_ref[...]).astype(BF16)
    y = x_ref[...] + _dot(u, wo_ref[...])
    y_ref[...] = y
    hb = _rms(y, g_ref[...]).astype(BF16)
    kw = GLA_KEY_WIDTH
    q_ref[...] = _dot(hb, w_ref[:, 0:kw]) * (GLA_HEAD_K ** -0.5)
    k_ref[...] = _dot(hb, w_ref[:, kw:2 * kw])
    v_ref[...] = _dot(hb, w_ref[:, 2 * kw:2 * kw + GLA_VAL_WIDTH]).astype(BF16)
    gate = _dot(hb, w_ref[:, 2 * kw + GLA_VAL_WIDTH:2 * kw + 2 * GLA_VAL_WIDTH])
    sg2_ref[...] = gate * jax.nn.sigmoid(gate)
    a1 = _dot(hb, wa1_ref[...]).astype(BF16)
    a = _dot(a1, wa2_ref[...]) + ba_ref[...]
    ga_ref[...] = _log_sigmoid(a) * (1.0 / GLA_GATE_TEMP)


def _mid(o, sg, x, w_out, g, w_main, w_a1, w_a2, b_a, *, tm):
    T = o.shape[0]
    tok = lambda w: pl.BlockSpec((tm, w), lambda i: (i, 0))
    full = lambda a: pl.BlockSpec(a.shape, lambda i: (0,) * a.ndim)
    return pl.pallas_call(
        _mid_kernel,
        grid=(T // tm,),
        in_specs=[tok(D_MODEL), tok(D_MODEL), tok(D_MODEL), full(w_out), full(g), full(w_main),
                  full(w_a1), full(w_a2), full(b_a)],
        out_specs=[tok(D_MODEL), tok(GLA_KEY_WIDTH), tok(GLA_KEY_WIDTH), tok(GLA_VAL_WIDTH),
                   tok(GLA_VAL_WIDTH), tok(GLA_KEY_WIDTH)],
        out_shape=[
            jax.ShapeDtypeStruct((T, D_MODEL), F32),
            jax.ShapeDtypeStruct((T, GLA_KEY_WIDTH), F32),
            jax.ShapeDtypeStruct((T, GLA_KEY_WIDTH), F32),
            jax.ShapeDtypeStruct((T, GLA_VAL_WIDTH), BF16),
            jax.ShapeDtypeStruct((T, GLA_VAL_WIDTH), F32),
            jax.ShapeDtypeStruct((T, GLA_KEY_WIDTH), F32),
        ],
        compiler_params=_params(1),
        name="fox_out_gla_proj",
    )(o, sg, x, w_out, g, w_main, w_a1, w_a2, b_a)


def _gla_kernel(q_ref, k_ref, v_ref, ga_ref, s0_ref, o_ref, s_ref, *, n_chunks):
    i = pl.program_id(1)
    ch = GLA_CHUNK
    dk, dv = GLA_HEAD_K, GLA_HEAD_V

    @pl.when(i == 0)
    def _():
        s_ref[...] = s0_ref[...]

    tril = _tril(ch)
    r = lax.broadcasted_iota(jnp.int32, (ch, ch), 0)
    c = lax.broadcasted_iota(jnp.int32, (ch, ch), 1)
    causal = r >= c
    eye = (lax.broadcasted_iota(jnp.int32, (dk, dk), 0) == lax.broadcasted_iota(jnp.int32, (dk, dk), 1))

    for ci in range(n_chunks):
        rows = slice(ci * ch, (ci + 1) * ch)
        b = _tril_cumsum(tril, ga_ref[rows, :])
        b_last = b[ch - 1:ch, :]
        eb = jnp.exp(b)
        qe = (q_ref[rows, :] * eb).astype(BF16)
        kk = k_ref[rows, :]
        ke = (kk * jnp.exp(-b)).astype(BF16)
        kd = (kk * jnp.exp(b_last - b)).astype(BF16)
        dec = jnp.exp(b_last)
        for h in range(GLA_HEADS):
            ks = slice(h * dk, (h + 1) * dk)
            vs = slice(h * dv, (h + 1) * dv)
            vh = v_ref[rows, vs]
            a = jnp.where(causal, _dot_nt(qe[:, ks], ke[:, ks]), 0.0)
            s_old = s_ref[h]
            o_ref[rows, vs] = _dot(a.astype(BF16), vh) + _dot(qe[:, ks], s_old.astype(BF16))
            dec_col = jnp.sum(jnp.where(eye, jnp.broadcast_to(dec[:, ks], (dk, dk)), 0.0),
                              axis=1, keepdims=True)
            s_ref[h] = dec_col * s_old + _dot_tn(kd[:, ks], vh)


def _gla(q, k, v, ga, s0, *, tc):
    B, L, _ = q.shape
    tok = lambda w: pl.BlockSpec((None, tc, w), lambda b, i: (b, i, 0))
    st = pl.BlockSpec((None, GLA_HEADS, GLA_HEAD_K, GLA_HEAD_V), lambda b, i: (b, 0, 0, 0))
    return pl.pallas_call(
        functools.partial(_gla_kernel, n_chunks=tc // GLA_CHUNK),
        grid=(B, L // tc),
        in_specs=[tok(GLA_KEY_WIDTH), tok(GLA_KEY_WIDTH), tok(GLA_VAL_WIDTH), tok(GLA_KEY_WIDTH), st],
        out_specs=[tok(GLA_VAL_WIDTH), st],
        out_shape=[
            jax.ShapeDtypeStruct((B, L, GLA_VAL_WIDTH), F32),
            jax.ShapeDtypeStruct((B, GLA_HEADS, GLA_HEAD_K, GLA_HEAD_V), F32),
        ],
        compiler_params=_params(2),
        name="gla_chunk",
    )(q, k, v, ga, s0)


def _gla_out_kernel(o_ref, sg_ref, y_ref, go_ref, wo_ref, gf_ref, out_ref, *, chunk_major):
    dv = GLA_HEAD_V
    if chunk_major:
        o = jnp.concatenate([o_ref[:, t, :] for t in range(o_ref.shape[1])], axis=0)
    else:
        o = o_ref[...]
    parts = []
    for h in range(GLA_HEADS):
        vs = slice(h * dv, (h + 1) * dv)
        parts.append((_rms(o[:, vs], go_ref[...]) * sg_ref[:, vs]).astype(BF16))
    u = jnp.concatenate(parts, axis=1)
    y = y_ref[...] + _dot(u, wo_ref[...])
    out_ref[...] = _rms(y, gf_ref[...])


def _gla_out(o, sg, y, g_o, w_out, g_f, *, tm):
    B, L, _ = o.shape
    n = L // GLA_CHUNK
    chunk_major = n > 1
    if chunk_major:
        tm = 8 * n
        o = o.reshape(B, n, GLA_CHUNK, D_MODEL)
        o_spec = pl.BlockSpec((None, n, 8, D_MODEL), lambda b, i: (b, 0, i, 0))
    else:
        flat = lambda a: a.reshape(1, B * L, D_MODEL)
        o, sg, y = flat(o), flat(sg), flat(y)
        o_spec = pl.BlockSpec((None, tm, D_MODEL), lambda b, i: (b, i, 0))
    nb, rows, _ = sg.shape
    tok = pl.BlockSpec((None, tm, D_MODEL), lambda b, i: (b, i, 0))
    full = lambda a: pl.BlockSpec(a.shape, lambda b, i: (0,) * a.ndim)
    return pl.pallas_call(
        functools.partial(_gla_out_kernel, chunk_major=chunk_major),
        grid=(nb, rows // tm),
        in_specs=[o_spec, tok, tok, full(g_o), full(w_out), full(g_f)],
        out_specs=tok,
        out_shape=jax.ShapeDtypeStruct((nb, rows, D_MODEL), F32),
        compiler_params=_params(2),
        name="gla_out",
    )(o, sg, y, g_o, w_out, g_f)


def _pad_cols(w, n):
    return jnp.pad(w, ((0, 0), (0, n - w.shape[1])))


def kernel(x_prompt, x_sample, cache_fox_k, cache_fox_v, cache_fox_logf, state_gla,
           g_norm_fox, w_in_fox, b_fox_f, w_out_fox,
           g_norm_gla, w_in_gla, w_gla_a2, b_gla_a, g_gla_o, w_out_gla, g_final):
    B, L, D = x_prompt.shape
    Bs, Ls, _ = x_sample.shape
    P = cache_fox_k.shape[1]
    H, hd = FOX_HEADS, FOX_HEAD_DIM

    row = lambda a: a.reshape(1, -1).astype(F32)
    w_fox = w_in_fox[:, :4 * FOX_WIDTH].astype(BF16)
    w_fox_f = _pad_cols(w_in_fox[:, 4 * FOX_WIDTH:], LANES).astype(BF16)
    b_f = _pad_cols(row(b_fox_f), LANES)
    w_o_fox = w_out_fox.astype(BF16)
    n_main = 2 * GLA_KEY_WIDTH + 2 * GLA_VAL_WIDTH
    w_gla = w_in_gla[:, :n_main].astype(BF16)
    w_a1 = _pad_cols(w_in_gla[:, n_main:], LANES).astype(BF16)
    w_a2 = jnp.pad(w_gla_a2, ((0, LANES - GLA_GATE_RANK), (0, 0))).astype(BF16)
    w_o_gla = w_out_gla.astype(BF16)

    q_p, k_p, v_p, kb_p, sg_p, lf_p, vt_p, nrm_p = _fox_proj(
        x_prompt, row(g_norm_fox), w_fox, w_fox_f, b_f, tm=ATTN_BLOCK, want_vt=True)
    _, e_p, f_p, rng_p = _cumsum(lf_p, rows=256, tb=ATTN_BLOCK, want_bias=True)
    first_p = _skip_plan(nrm_p[:, :, 0, :H], rng_p[:, :, 0, :], nrm_p[:, :, 1, :H], rng_p[:, :, 1, :])
    o_p = _attn_prompt(first_p, q_p, f_p, kb_p, e_p, vt_p, blk=ATTN_BLOCK)

    xs = x_sample.reshape(1, Bs * Ls, D)
    q_s, k_s, v_s, kb_s, sg_s, lf_s = _fox_proj(
        xs, row(g_norm_fox), w_fox, w_fox_f, b_f, tm=PROJ_TM, want_vt=False)
    sh = lambda a: a.reshape(Bs, Ls, a.shape[-1])
    q_s, k_s, v_s, kb_s, lf_s = sh(q_s), sh(k_s), sh(v_s), sh(kb_s), sh(lf_s)
    lf_all = jnp.concatenate(
        [cache_fox_logf.astype(F32), lf_s, jnp.zeros((Bs, LANES - Ls, H), F32)], axis=1)
    (c_s,) = _cumsum(lf_all, rows=LANES, tb=P + LANES, want_bias=False)
    o_s = _attn_sample(q_s, kb_s, v_s, cache_fox_k.reshape(Bs, P, FOX_WIDTH).astype(BF16),
                       cache_fox_v.reshape(Bs, P, FOX_WIDTH).astype(BF16),
                       c_s, jnp.swapaxes(c_s, 1, 2), past=P)

    def gla_layer(o, sg, x, s0, tc):
        b, l, _ = x.shape
        flat = lambda a: a.reshape(b * l, a.shape[-1])
        y1, q, k, v, sg2, ga = _mid(flat(o), flat(sg), flat(x), w_o_fox, row(g_norm_gla), w_gla,
                                    w_a1, w_a2, row(b_gla_a), tm=PROJ_TM)
        un = lambda a: a.reshape(b, l, a.shape[-1])
        og, s_out = _gla(un(q), un(k), un(v), un(ga), s0, tc=tc)
        y = _gla_out(og, un(sg2), un(y1), row(g_gla_o), w_o_gla, row(g_final), tm=PROJ_TM)
        return y.reshape(b, l, D), s_out

    s0_p = jnp.zeros((B, GLA_HEADS, GLA_HEAD_K, GLA_HEAD_V), F32)
    y_p, s_p = gla_layer(o_p, sg_p, x_prompt, s0_p, GLA_TC)
    y_s, s_s = gla_layer(o_s, sg_s.reshape(Bs, Ls, D), x_sample, state_gla.astype(F32), Ls)

    return (y_p, y_s,
            k_p.reshape(B, L, H, hd), v_p.reshape(B, L, H, hd), lf_p, s_p.astype(state_gla.dtype),
            k_s.reshape(Bs, Ls, H, hd), v_s.reshape(Bs, Ls, H, hd), lf_s, s_s.astype(state_gla.dtype))
```

```python
import functools

import jax
import jax.numpy as jnp
from jax import lax
from jax.experimental import pallas as pl
from jax.experimental.pallas import tpu as pltpu

F32 = jnp.float32
BF16 = jnp.bfloat16

D_MODEL = 1024
EPS = 1e-6
NEG_INF = -1e30
LOG2E = 1.4426950408889634
NORM_SLACK = 1.01
SKIP_LOG2 = 160.0

FOX_HEADS = 16
FOX_HEAD_DIM = 64
FOX_WIDTH = FOX_HEADS * FOX_HEAD_DIM
FOX_PAIRS = FOX_HEADS // 2

GLA_HEADS = 4
GLA_KEY_WIDTH = 512
GLA_VAL_WIDTH = 1024
GLA_HEAD_K = 128
GLA_HEAD_V = 256
GLA_GATE_RANK = 16
GLA_GATE_TEMP = 16.0
GLA_CHUNK = 64

LANES = 128
VMEM_LIMIT = 56 * 1024 * 1024

PROJ_TM = 512
ATTN_BLOCK = 512
GLA_TC = 256


def _params(n_axes):
    return pltpu.CompilerParams(
        dimension_semantics=("arbitrary",) * n_axes,
        vmem_limit_bytes=VMEM_LIMIT,
    )


def _dot(a, b):
    return jnp.dot(a, b, preferred_element_type=F32)


def _dot_nt(a, b):
    return lax.dot_general(a, b, (((1,), (1,)), ((), ())), preferred_element_type=F32)


def _dot_tn(a, b):
    return lax.dot_general(a, b, (((0,), (0,)), ((), ())), preferred_element_type=F32)


def _log_sigmoid(z):
    return jnp.minimum(z, 0.0) - jnp.log1p(jnp.exp(-jnp.abs(z)))


def _rms(x, g):
    ms = jnp.mean(x * x, axis=-1, keepdims=True)
    return (x * lax.rsqrt(ms + EPS)) * g


def _split3(x):
    hi = x.astype(BF16)
    r1 = x - hi.astype(F32)
    mid = r1.astype(BF16)
    lo = (r1 - mid.astype(F32)).astype(BF16)
    return hi, mid, lo


def _tril_cumsum(tril_bf16, x):
    hi, mid, lo = _split3(x)
    return (_dot(tril_bf16, hi) + _dot(tril_bf16, mid)) + _dot(tril_bf16, lo)


def _tril(n):
    r = lax.broadcasted_iota(jnp.int32, (n, n), 0)
    c = lax.broadcasted_iota(jnp.int32, (n, n), 1)
    return jnp.where(r >= c, 1.0, 0.0).astype(BF16)


def _fox_proj_kernel(x_ref, g_ref, w_ref, wf_ref, bf_ref, *rest, want_vt):
    if want_vt:
        q_ref, k_ref, v_ref, kb_ref, sg_ref, lf_ref, vt_ref, nrm_ref = rest
    else:
        q_ref, k_ref, v_ref, kb_ref, sg_ref, lf_ref = rest
    hb = _rms(x_ref[...], g_ref[...]).astype(BF16)
    q = _dot(hb, w_ref[:, 0:FOX_WIDTH])
    qb = (q * (LOG2E * FOX_HEAD_DIM ** -0.5)).astype(BF16)
    q_ref[...] = qb
    k = _dot(hb, w_ref[:, FOX_WIDTH:2 * FOX_WIDTH])
    k_ref[...] = k
    kb = k.astype(BF16)
    kb_ref[...] = kb
    if want_vt:
        row = lax.broadcasted_iota(jnp.int32, (FOX_WIDTH, LANES), 0)
        col = lax.broadcasted_iota(jnp.int32, (FOX_WIDTH, LANES), 1)
        head_of = jnp.where((row >> 6) == col, 1.0, 0.0).astype(BF16)

        def bound(xb):
            xf = xb.astype(F32)
            sq = _dot((xf * xf).astype(BF16), head_of)
            return jnp.sqrt(jnp.max(sq, axis=0, keepdims=True)) * NORM_SLACK

        nrm_ref[...] = jnp.concatenate(
            [bound(kb), bound(qb), jnp.zeros((6, LANES), F32)], axis=0)
    v = _dot(hb, w_ref[:, 2 * FOX_WIDTH:3 * FOX_WIDTH])
    v_ref[...] = v
    if want_vt:
        vt_ref[...] = v.astype(BF16).T
    gate = _dot(hb, w_ref[:, 3 * FOX_WIDTH:4 * FOX_WIDTH])
    sg_ref[...] = gate * jax.nn.sigmoid(gate)
    fl = _dot(hb, wf_ref[...]) + bf_ref[...]
    lf_ref[...] = _log_sigmoid(fl)[:, :FOX_HEADS]


def _fox_proj(x, g, w_main, w_f, b_f, *, tm, want_vt):
    B, L, _ = x.shape
    n = L // tm
    tok = lambda w: pl.BlockSpec((None, tm, w), lambda b, i: (b, i, 0))
    full = lambda a: pl.BlockSpec(a.shape, lambda b, i: (0,) * a.ndim)
    in_specs = [tok(D_MODEL), full(g), full(w_main), full(w_f), full(b_f)]
    args = [x, g, w_main, w_f, b_f]
    out_specs = [tok(FOX_WIDTH)] * 5 + [tok(FOX_HEADS)]
    out_shape = [
        jax.ShapeDtypeStruct((B, L, FOX_WIDTH), BF16),
        jax.ShapeDtypeStruct((B, L, FOX_WIDTH), F32),
        jax.ShapeDtypeStruct((B, L, FOX_WIDTH), F32),
        jax.ShapeDtypeStruct((B, L, FOX_WIDTH), BF16),
        jax.ShapeDtypeStruct((B, L, FOX_WIDTH), F32),
        jax.ShapeDtypeStruct((B, L, FOX_HEADS), F32),
    ]
    if want_vt:
        out_specs.append(pl.BlockSpec((None, None, FOX_WIDTH, tm), lambda b, i: (b, i, 0, 0)))
        out_shape.append(jax.ShapeDtypeStruct((B, n, FOX_WIDTH, tm), BF16))
        out_specs.append(pl.BlockSpec((None, None, 8, LANES), lambda b, i: (b, i, 0, 0)))
        out_shape.append(jax.ShapeDtypeStruct((B, n, 8, LANES), F32))
    return pl.pallas_call(
        functools.partial(_fox_proj_kernel, want_vt=want_vt),
        grid=(B, n),
        in_specs=in_specs,
        out_specs=out_specs,
        out_shape=out_shape,
        compiler_params=_params(2),
        name="fox_proj_vt" if want_vt else "fox_proj",
    )(*args)


BIAS_SLOT = 8


def _bias_selectors():
    shape = (FOX_HEADS, LANES)
    head = lax.broadcasted_iota(jnp.int32, shape, 0)
    col = lax.broadcasted_iota(jnp.int32, shape, 1)
    slot = BIAS_SLOT * head
    sel_f = [jnp.where(col == slot + x, 1.0, 0.0).astype(BF16) for x in range(3)]
    sel_e = [jnp.where(col == slot + 3 + x, -1.0, 0.0).astype(BF16) for x in range(3)]
    lane = lax.broadcasted_iota(jnp.int32, (1, LANES), 1) & (BIAS_SLOT - 1)
    one_f = jnp.where((lane >= 3) & (lane < 6), 1.0, 0.0)
    one_e = jnp.where(lane < 3, 1.0, 0.0)
    return sel_f, sel_e, one_f, one_e


def _cumsum_kernel(lf_ref, c_ref, *rest, rows, n_chunks, want_bias):
    tril = _tril(rows)
    if want_bias:
        e_ref, f_ref, rng_ref, carry_ref = rest
        sel_f, sel_e, one_f, one_e = _bias_selectors()
    else:
        (carry_ref,) = rest

    @pl.when(pl.program_id(1) == 0)
    def _():
        carry_ref[...] = jnp.zeros_like(carry_ref)

    def body(i, carry):
        r0 = pl.multiple_of(i * rows, rows)
        cs = _tril_cumsum(tril, lf_ref[pl.ds(r0, rows), :]) + carry
        c2 = cs * LOG2E
        c_ref[pl.ds(r0, rows), :] = c2
        if want_bias:
            hi, mid, lo = _split3(c2)
            f_ref[pl.ds(r0, rows), :] = (
                (_dot(hi, sel_f[0]) + _dot(mid, sel_f[1])) + _dot(lo, sel_f[2]) + one_f).astype(BF16)
            e_ref[pl.ds(r0, rows), :] = (
                (_dot(hi, sel_e[0]) + _dot(mid, sel_e[1])) + _dot(lo, sel_e[2]) + one_e).astype(BF16)
        return cs[rows - 1:rows, :]

    carry_ref[...] = lax.fori_loop(0, n_chunks, body, carry_ref[...])
    if want_bias:
        c_all = c_ref[...]
        rng_ref[...] = jnp.concatenate(
            [jnp.min(c_all, axis=0, keepdims=True), jnp.max(c_all, axis=0, keepdims=True),
             jnp.zeros((6, FOX_HEADS), F32)], axis=0)


def _cumsum(lf, *, rows, tb, want_bias):
    B, L, H = lf.shape
    spec = pl.BlockSpec((None, tb, H), lambda b, i: (b, i, 0))
    wide = pl.BlockSpec((None, tb, LANES), lambda b, i: (b, i, 0))
    out_specs = [spec]
    out_shape = [jax.ShapeDtypeStruct((B, L, H), F32)]
    if want_bias:
        out_specs += [wide, wide, pl.BlockSpec((None, None, 8, H), lambda b, i: (b, i, 0, 0))]
        out_shape += [jax.ShapeDtypeStruct((B, L, LANES), BF16)] * 2
        out_shape += [jax.ShapeDtypeStruct((B, L // tb, 8, H), F32)]
    return pl.pallas_call(
        functools.partial(_cumsum_kernel, rows=rows, n_chunks=tb // rows, want_bias=want_bias),
        grid=(B, L // tb),
        in_specs=[spec],
        out_specs=out_specs,
        out_shape=out_shape,
        scratch_shapes=[pltpu.VMEM((1, FOX_HEADS), F32)],
        compiler_params=_params(2),
        name="fox_cumsum_bias" if want_bias else "fox_cumsum",
    )(lf)


def _skip_plan_kernel(kn_ref, cmin_ref, qn_ref, cmax_ref, knq_ref, first_ref, *, nb):
    j = lax.broadcasted_iota(jnp.int32, (nb, nb), 0).astype(F32)
    q = lax.broadcasted_iota(jnp.int32, (nb, nb), 1).astype(F32)
    rows = []
    for p in range(FOX_PAIRS):
        skippable = None
        for h in (2 * p, 2 * p + 1):
            qn = qn_ref[h:h + 1, :]
            bound = (kn_ref[:, h:h + 1] * qn + (cmax_ref[h:h + 1, :] - cmin_ref[:, h:h + 1])
                     + qn * knq_ref[h:h + 1, :])
            ok = bound < -SKIP_LOG2
            skippable = ok if skippable is None else (skippable & ok)
        must = (j < q) & jnp.logical_not(skippable)
        rows.append(jnp.min(jnp.where(must, j, q), axis=0, keepdims=True))
    first_ref[...] = jnp.concatenate(rows, axis=0).astype(jnp.int32)


def _skip_plan(kn, cmin, qn, cmax):
    B, nb, H = kn.shape
    by_block = pl.BlockSpec((None, nb, H), lambda b: (b, 0, 0))
    by_head = pl.BlockSpec((None, H, nb), lambda b: (b, 0, 0))
    t = lambda a: jnp.swapaxes(a, 1, 2)
    return pl.pallas_call(
        functools.partial(_skip_plan_kernel, nb=nb),
        grid=(B,),
        in_specs=[by_block, by_block, by_head, by_head, by_head],
        out_specs=pl.BlockSpec((None, FOX_PAIRS, nb), lambda b: (b, 0, 0)),
        out_shape=jax.ShapeDtypeStruct((B, FOX_PAIRS, nb), jnp.int32),
        compiler_params=_params(1),
        name="fox_skip_plan",
    )(kn, cmin, t(qn), t(cmax), t(kn))


def _stack_heads(q2):
    lane = lax.broadcasted_iota(jnp.int32, q2.shape, 1)
    zero = jnp.zeros_like(q2)
    return jnp.concatenate(
        [jnp.where(lane < FOX_HEAD_DIM, q2, zero), jnp.where(lane >= FOX_HEAD_DIM, q2, zero)], axis=0)


ATTN_TILE = 256


def _attn_prompt_kernel(first_ref, q_ref, f_ref, k_ref, e_ref, vt_ref, o_ref,
                        m_ref, l_ref, acc_ref, sa_ref, xa_ref, sb_ref, xb_ref, *, blk):
    qi = pl.program_id(2)
    j0 = first_ref[pl.program_id(0), pl.program_id(1), qi]
    hd = FOX_HEAD_DIM
    tw = ATTN_TILE
    per_head = blk // tw
    n_tiles = 2 * per_head

    q2 = q_ref[...]
    f2 = f_ref[...]
    lane = lax.broadcasted_iota(jnp.int32, q2.shape, 1)
    zero = jnp.zeros_like(q2)
    slot_a = 2 * BIAS_SLOT * pl.program_id(1)
    slot_b = slot_a + BIAS_SLOT
    q_heads = [
        jnp.concatenate([jnp.where(lane < hd, q2, zero),
                         jnp.where((lane >= slot_a) & (lane < slot_b), f2, zero)], axis=1),
        jnp.concatenate([jnp.where(lane >= hd, q2, zero),
                         jnp.where((lane >= slot_b) & (lane < slot_b + BIAS_SLOT), f2, zero)], axis=1),
    ]
    q_tiles = [q_heads[t // per_head][(t % per_head) * tw:(t % per_head + 1) * tw] for t in range(n_tiles)]

    m_ref[...] = jnp.full(m_ref.shape, NEG_INF, F32)
    l_ref[...] = jnp.zeros(l_ref.shape, F32)
    acc_ref[...] = jnp.zeros(acc_ref.shape, F32)

    def scores(j, buf, diagonal):
        s_ref, x_ref = buf
        k0 = pl.multiple_of(j * blk, blk)
        kaug = jnp.concatenate([k_ref[pl.ds(k0, blk), :], e_ref[pl.ds(k0, blk), :]], axis=1)
        for t in range(n_tiles):
            s = _dot_nt(kaug, q_tiles[t])
            if diagonal:
                r = lax.broadcasted_iota(jnp.int32, (blk, tw), 0)
                c = lax.broadcasted_iota(jnp.int32, (blk, tw), 1) + (t % per_head) * tw
                s = jnp.where(r <= c, s, NEG_INF)
            s_ref[t] = s
            x_ref[t] = jnp.max(s, axis=0, keepdims=True)

    def absorb(j, buf):
        s_ref, x_ref = buf
        vt = vt_ref[j]
        for t in range(n_tiles):
            head = t // per_head
            m_old = m_ref[t]
            m_new = jnp.maximum(m_old, x_ref[t])
            alpha = jnp.exp2(m_old - m_new)
            p = jnp.exp2(s_ref[t] - m_new)
            l_ref[t] = alpha * l_ref[t] + jnp.sum(p, axis=0, keepdims=True)
            m_ref[t] = m_new
            pv = _dot(vt[head * hd:(head + 1) * hd, :], p.astype(BF16))
            acc_ref[t] = acc_ref[t] * alpha + pv

    buf_a = (sa_ref, xa_ref)
    buf_b = (sb_ref, xb_ref)
    scores(qi, buf_a, True)
    behind = qi - j0

    def pair(n, carry):
        j = qi - 2 * n
        scores(j - 1, buf_b, False)
        absorb(j, buf_a)
        scores(j - 2, buf_a, False)
        absorb(j - 1, buf_b)
        return carry

    lax.fori_loop(0, behind // 2, pair, 0)

    @pl.when(behind % 2 == 1)
    def _():
        scores(j0, buf_b, False)
        absorb(j0 + 1, buf_a)
        absorb(j0, buf_b)

    @pl.when(behind % 2 == 0)
    def _():
        absorb(j0, buf_a)

    halves = [jnp.concatenate([acc_ref[t] / l_ref[t] for t in range(h * per_head, (h + 1) * per_head)], axis=1)
              for h in range(2)]
    o_ref[...] = jnp.concatenate(halves, axis=0).T


def _attn_prompt(first, q, f, kb, e, vt, *, blk):
    B, L, _ = q.shape
    nb = L // blk
    n_tiles = 2 * blk // ATTN_TILE
    qspec = pl.BlockSpec((None, blk, LANES), lambda b, p, i, first: (b, i, p))
    kspec = pl.BlockSpec((None, L, LANES), lambda b, p, i, first: (b, 0, p))
    return pl.pallas_call(
        functools.partial(_attn_prompt_kernel, blk=blk),
        grid_spec=pltpu.PrefetchScalarGridSpec(
            num_scalar_prefetch=1,
            grid=(B, FOX_PAIRS, nb),
            in_specs=[qspec,
                      pl.BlockSpec((None, blk, LANES), lambda b, p, i, first: (b, i, 0)),
                      kspec,
                      pl.BlockSpec((None, L, LANES), lambda b, p, i, first: (b, 0, 0)),
                      pl.BlockSpec((None, nb, LANES, blk), lambda b, p, i, first: (b, 0, p, 0))],
            out_specs=qspec,
            scratch_shapes=[
                pltpu.VMEM((n_tiles, 1, ATTN_TILE), F32),
                pltpu.VMEM((n_tiles, 1, ATTN_TILE), F32),
                pltpu.VMEM((n_tiles, FOX_HEAD_DIM, ATTN_TILE), F32),
                pltpu.VMEM((n_tiles, blk, ATTN_TILE), F32),
                pltpu.VMEM((n_tiles, 1, ATTN_TILE), F32),
                pltpu.VMEM((n_tiles, blk, ATTN_TILE), F32),
                pltpu.VMEM((n_tiles, 1, ATTN_TILE), F32),
            ],
        ),
        out_shape=jax.ShapeDtypeStruct((B, L, FOX_WIDTH), F32),
        compiler_params=_params(3),
        name="fox_attn_prompt",
    )(first, q, f, kb, e, vt)


def _attn_sample_kernel(q_ref, kn_ref, vn_ref, ck_ref, cv_ref, c_ref, ct_ref, o_ref, *, past, ls):
    p = pl.program_id(1)
    hd = FOX_HEAD_DIM
    qst = _stack_heads(q_ref[...])
    kc = ck_ref[...].astype(BF16)
    vc = cv_ref[...].astype(BF16)
    pad_k = jnp.zeros((LANES - ls, LANES), BF16)
    kn = jnp.concatenate([kn_ref[...], pad_k], axis=0)
    vn = jnp.concatenate([vn_ref[...].astype(BF16), pad_k], axis=0)

    cqb = c_ref[past:past + ls, :]
    lane = lax.broadcasted_iota(jnp.int32, cqb.shape, 1)
    cq = jnp.concatenate(
        [jnp.sum(jnp.where(lane == 2 * p + h2, cqb, 0.0), axis=1, keepdims=True) for h2 in range(2)],
        axis=0)
    cka = ct_ref[pl.ds(2 * p, 1), :]
    ckb = ct_ref[pl.ds(2 * p + 1, 1), :]

    def bias(lo, hi):
        return jnp.concatenate([cq[:ls] - cka[:, lo:hi], cq[ls:] - ckb[:, lo:hi]], axis=0)

    s_c = _dot_nt(qst, kc) + bias(0, past)
    s_n = _dot_nt(qst, kn) + bias(past, past + LANES)
    r = lax.broadcasted_iota(jnp.int32, s_n.shape, 0) & (ls - 1)
    c = lax.broadcasted_iota(jnp.int32, s_n.shape, 1)
    s_n = jnp.where(c <= r, s_n, NEG_INF)
    m = jnp.maximum(jnp.max(s_c, axis=1, keepdims=True), jnp.max(s_n, axis=1, keepdims=True))
    p_c = jnp.exp2(s_c - m)
    p_n = jnp.exp2(s_n - m)
    l = jnp.sum(p_c, axis=1, keepdims=True) + jnp.sum(p_n, axis=1, keepdims=True)
    o = (_dot(p_c.astype(BF16), vc) + _dot(p_n.astype(BF16), vn)) / l
    lane_o = lax.broadcasted_iota(jnp.int32, (ls, LANES), 1)
    o_ref[...] = jnp.where(lane_o < hd, o[:ls], o[ls:])


def _attn_sample(q, kb, v, cache_k, cache_v, c, ct, *, past):
    B, ls, _ = q.shape
    tot = c.shape[1]
    tile = lambda rows: pl.BlockSpec((None, rows, LANES), lambda b, p: (b, 0, p))
    return pl.pallas_call(
        functools.partial(_attn_sample_kernel, past=past, ls=ls),
        grid=(B, FOX_PAIRS),
        in_specs=[
            tile(ls), tile(ls), tile(ls), tile(past), tile(past),
            pl.BlockSpec((None, tot, FOX_HEADS), lambda b, p: (b, 0, 0)),
            pl.BlockSpec((None, FOX_HEADS, tot), lambda b, p: (b, 0, 0)),
        ],
        out_specs=tile(ls),
        out_shape=jax.ShapeDtypeStruct((B, ls, FOX_WIDTH), F32),
        compiler_params=_params(2),
        name="fox_attn_sample",
    )(q, kb, v, cache_k, cache_v, c, ct)


def _mid_kernel(o_ref, sg_ref, x_ref, wo_ref, g_ref, w_ref, wa1_ref, wa2_ref, ba_ref,
                y_ref, q_ref, k_ref, v_ref, sg2_ref, ga_ref):
    u = (o_ref[...] * sg_ref[...]).astype(BF16)
    y = x_ref[...] + _dot(u, wo_ref[...])
    y_ref[...] = y
    hb = _rms(y, g_ref[...]).astype(BF16)
    kw = GLA_KEY_WIDTH
    q_ref[...] = _dot(hb, w_ref[:, 0:kw]) * (GLA_HEAD_K ** -0.5)
    k_ref[...] = _dot(hb, w_ref[:, kw:2 * kw])
    v_ref[...] = _dot(hb, w_ref[:, 2 * kw:2 * kw + GLA_VAL_WIDTH]).astype(BF16)
    gate = _dot(hb, w_ref[:, 2 * kw + GLA_VAL_WIDTH:2 * kw + 2 * GLA_VAL_WIDTH])
    sg2_ref[...] = gate * jax.nn.sigmoid(gate)
    a1 = _dot(hb, wa1_ref[...]).astype(BF16)
    a = _dot(a1, wa2_ref[...]) + ba_ref[...]
    ga_ref[...] = _log_sigmoid(a) * (1.0 / GLA_GATE_TEMP)


def _mid(o, sg, x, w_out, g, w_main, w_a1, w_a2, b_a, *, tm):
    T = o.shape[0]
    tok = lambda w: pl.BlockSpec((tm, w), lambda i: (i, 0))
    full = lambda a: pl.BlockSpec(a.shape, lambda i: (0,) * a.ndim)
    return pl.pallas_call(
        _mid_kernel,
        grid=(T // tm,),
        in_specs=[tok(D_MODEL), tok(D_MODEL), tok(D_MODEL), full(w_out), full(g), full(w_main),
                  full(w_a1), full(w_a2), full(b_a)],
        out_specs=[tok(D_MODEL), tok(GLA_KEY_WIDTH), tok(GLA_KEY_WIDTH), tok(GLA_VAL_WIDTH),
                   tok(GLA_VAL_WIDTH), tok(GLA_KEY_WIDTH)],
        out_shape=[
            jax.ShapeDtypeStruct((T, D_MODEL), F32),
            jax.ShapeDtypeStruct((T, GLA_KEY_WIDTH), F32),
            jax.ShapeDtypeStruct((T, GLA_KEY_WIDTH), F32),
            jax.ShapeDtypeStruct((T, GLA_VAL_WIDTH), BF16),
            jax.ShapeDtypeStruct((T, GLA_VAL_WIDTH), F32),
            jax.ShapeDtypeStruct((T, GLA_KEY_WIDTH), F32),
        ],
        compiler_params=_params(1),
        name="fox_out_gla_proj",
    )(o, sg, x, w_out, g, w_main, w_a1, w_a2, b_a)


def _gla_kernel(q_ref, k_ref, v_ref, ga_ref, s0_ref, o_ref, s_ref, *, n_chunks):
    i = pl.program_id(1)
    ch = GLA_CHUNK
    dk, dv = GLA_HEAD_K, GLA_HEAD_V

    @pl.when(i == 0)
    def _():
        s_ref[...] = s0_ref[...]

    t = n_chunks * ch
    r = lax.broadcasted_iota(jnp.int32, (t, t), 0)
    c = lax.broadcasted_iota(jnp.int32, (t, t), 1)
    causal = ((r >> 6) == (c >> 6)) & (r >= c)
    tril = jnp.where(causal, 1.0, 0.0).astype(BF16)
    eye = (lax.broadcasted_iota(jnp.int32, (dk, dk), 0) == lax.broadcasted_iota(jnp.int32, (dk, dk), 1))

    b = _tril_cumsum(tril, ga_ref[...])
    lasts = [b[(ci + 1) * ch - 1:(ci + 1) * ch, :] for ci in range(n_chunks)]
    b_last = jnp.concatenate([jnp.broadcast_to(x, (ch, GLA_KEY_WIDTH)) for x in lasts], axis=0)
    kk = k_ref[...]
    qe = (q_ref[...] * jnp.exp(b)).astype(BF16)
    ke = (kk * jnp.exp(-b)).astype(BF16)
    kd = (kk * jnp.exp(b_last - b)).astype(BF16)
    decs = [jnp.exp(x) for x in lasts]

    for h in range(GLA_HEADS):
        ks = slice(h * dk, (h + 1) * dk)
        vs = slice(h * dv, (h + 1) * dv)
        vh = v_ref[:, vs]
        a = jnp.where(causal, _dot_nt(qe[:, ks], ke[:, ks]), 0.0)
        o_intra = _dot(a.astype(BF16), vh)
        s = s_ref[h]
        for ci in range(n_chunks):
            rows = slice(ci * ch, (ci + 1) * ch)
            o_ref[rows, vs] = o_intra[rows] + _dot(qe[rows, ks], s.astype(BF16))
            dec_col = jnp.sum(jnp.where(eye, jnp.broadcast_to(decs[ci][:, ks], (dk, dk)), 0.0),
                              axis=1, keepdims=True)
            s = dec_col * s + _dot_tn(kd[rows, ks], vh[rows])
        s_ref[h] = s


def _gla(q, k, v, ga, s0, *, tc):
    B, L, _ = q.shape
    tok = lambda w: pl.BlockSpec((None, tc, w), lambda b, i: (b, i, 0))
    st = pl.BlockSpec((None, GLA_HEADS, GLA_HEAD_K, GLA_HEAD_V), lambda b, i: (b, 0, 0, 0))
    return pl.pallas_call(
        functools.partial(_gla_kernel, n_chunks=tc // GLA_CHUNK),
        grid=(B, L // tc),
        in_specs=[tok(GLA_KEY_WIDTH), tok(GLA_KEY_WIDTH), tok(GLA_VAL_WIDTH), tok(GLA_KEY_WIDTH), st],
        out_specs=[tok(GLA_VAL_WIDTH), st],
        out_shape=[
            jax.ShapeDtypeStruct((B, L, GLA_VAL_WIDTH), F32),
            jax.ShapeDtypeStruct((B, GLA_HEADS, GLA_HEAD_K, GLA_HEAD_V), F32),
        ],
        compiler_params=_params(2),
        name="gla_chunk",
    )(q, k, v, ga, s0)


def _gla_out_kernel(o_ref, sg_ref, y_ref, go_ref, wo_ref, gf_ref, out_ref, *, chunk_major):
    dv = GLA_HEAD_V
    if chunk_major:
        o = jnp.concatenate([o_ref[:, t, :] for t in range(o_ref.shape[1])], axis=0)
    else:
        o = o_ref[...]
    parts = []
    for h in range(GLA_HEADS):
        vs = slice(h * dv, (h + 1) * dv)
        parts.append((_rms(o[:, vs], go_ref[...]) * sg_ref[:, vs]).astype(BF16))
    u = jnp.concatenate(parts, axis=1)
    y = y_ref[...] + _dot(u, wo_ref[...])
    out_ref[...] = _rms(y, gf_ref[...])


def _gla_out(o, sg, y, g_o, w_out, g_f, *, tm):
    B, L, _ = o.shape
    n = L // GLA_CHUNK
    chunk_major = n > 1
    if chunk_major:
        tm = 8 * n
        o = o.reshape(B, n, GLA_CHUNK, D_MODEL)
        o_spec = pl.BlockSpec((None, n, 8, D_MODEL), lambda b, i: (b, 0, i, 0))
    else:
        flat = lambda a: a.reshape(1, B * L, D_MODEL)
        o, sg, y = flat(o), flat(sg), flat(y)
        o_spec = pl.BlockSpec((None, tm, D_MODEL), lambda b, i: (b, i, 0))
    nb, rows, _ = sg.shape
    tok = pl.BlockSpec((None, tm, D_MODEL), lambda b, i: (b, i, 0))
    full = lambda a: pl.BlockSpec(a.shape, lambda b, i: (0,) * a.ndim)
    return pl.pallas_call(
        functools.partial(_gla_out_kernel, chunk_major=chunk_major),
        grid=(nb, rows // tm),
        in_specs=[o_spec, tok, tok, full(g_o), full(w_out), full(g_f)],
        out_specs=tok,
        out_shape=jax.ShapeDtypeStruct((nb, rows, D_MODEL), F32),
        compiler_params=_params(2),
        name="gla_out",
    )(o, sg, y, g_o, w_out, g_f)


def _pad_cols(w, n):
    return jnp.pad(w, ((0, 0), (0, n - w.shape[1])))


def kernel(x_prompt, x_sample, cache_fox_k, cache_fox_v, cache_fox_logf, state_gla,
           g_norm_fox, w_in_fox, b_fox_f, w_out_fox,
           g_norm_gla, w_in_gla, w_gla_a2, b_gla_a, g_gla_o, w_out_gla, g_final):
    B, L, D = x_prompt.shape
    Bs, Ls, _ = x_sample.shape
    P = cache_fox_k.shape[1]
    H, hd = FOX_HEADS, FOX_HEAD_DIM

    row = lambda a: a.reshape(1, -1).astype(F32)
    w_fox = w_in_fox[:, :4 * FOX_WIDTH].astype(BF16)
    w_fox_f = _pad_cols(w_in_fox[:, 4 * FOX_WIDTH:], LANES).astype(BF16)
    b_f = _pad_cols(row(b_fox_f), LANES)
    w_o_fox = w_out_fox.astype(BF16)
    n_main = 2 * GLA_KEY_WIDTH + 2 * GLA_VAL_WIDTH
    w_gla = w_in_gla[:, :n_main].astype(BF16)
    w_a1 = _pad_cols(w_in_gla[:, n_main:], LANES).astype(BF16)
    w_a2 = jnp.pad(w_gla_a2, ((0, LANES - GLA_GATE_RANK), (0, 0))).astype(BF16)
    w_o_gla = w_out_gla.astype(BF16)

    q_p, k_p, v_p, kb_p, sg_p, lf_p, vt_p, nrm_p = _fox_proj(
        x_prompt, row(g_norm_fox), w_fox, w_fox_f, b_f, tm=ATTN_BLOCK, want_vt=True)
    _, e_p, f_p, rng_p = _cumsum(lf_p, rows=256, tb=ATTN_BLOCK, want_bias=True)
    first_p = _skip_plan(nrm_p[:, :, 0, :H], rng_p[:, :, 0, :], nrm_p[:, :, 1, :H], rng_p[:, :, 1, :])
    o_p = _attn_prompt(first_p, q_p, f_p, kb_p, e_p, vt_p, blk=ATTN_BLOCK)

    xs = x_sample.reshape(1, Bs * Ls, D)
    q_s, k_s, v_s, kb_s, sg_s, lf_s = _fox_proj(
        xs, row(g_norm_fox), w_fox, w_fox_f, b_f, tm=PROJ_TM, want_vt=False)
    sh = lambda a: a.reshape(Bs, Ls, a.shape[-1])
    q_s, k_s, v_s, kb_s, lf_s = sh(q_s), sh(k_s), sh(v_s), sh(kb_s), sh(lf_s)
    lf_all = jnp.concatenate(
        [cache_fox_logf.astype(F32), lf_s, jnp.zeros((Bs, LANES - Ls, H), F32)], axis=1)
    (c_s,) = _cumsum(lf_all, rows=LANES, tb=P + LANES, want_bias=False)
    o_s = _attn_sample(q_s, kb_s, v_s, cache_fox_k.reshape(Bs, P, FOX_WIDTH),
                       cache_fox_v.reshape(Bs, P, FOX_WIDTH), c_s, jnp.swapaxes(c_s, 1, 2), past=P)

    def gla_layer(o, sg, x, s0, tc):
        b, l, _ = x.shape
        flat = lambda a: a.reshape(b * l, a.shape[-1])
        y1, q, k, v, sg2, ga = _mid(flat(o), flat(sg), flat(x), w_o_fox, row(g_norm_gla), w_gla,
                                    w_a1, w_a2, row(b_gla_a), tm=PROJ_TM)
        un = lambda a: a.reshape(b, l, a.shape[-1])
        og, s_out = _gla(un(q), un(k), un(v), un(ga), s0, tc=tc)
        y = _gla_out(og, un(sg2), un(y1), row(g_gla_o), w_o_gla, row(g_final), tm=PROJ_TM)
        return y.reshape(b, l, D), s_out

    s0_p = jnp.zeros((B, GLA_HEADS, GLA_HEAD_K, GLA_HEAD_V), F32)
    y_p, s_p = gla_layer(o_p, sg_p, x_prompt, s0_p, GLA_TC)
    y_s, s_s = gla_layer(o_s, sg_s.reshape(Bs, Ls, D), x_sample, state_gla.astype(F32), Ls)

    return (y_p, y_s,
            k_p.reshape(B, L, H, hd), v_p.reshape(B, L, H, hd), lf_p, s_p.astype(state_gla.dtype),
            k_s.reshape(Bs, Ls, H, hd), v_s.reshape(Bs, Ls, H, hd), lf_s, s_s.astype(state_gla.dtype))
```

```python
import functools

import jax
import jax.numpy as jnp
from jax import lax
from jax.experimental import pallas as pl
from jax.experimental.pallas import tpu as pltpu

F32 = jnp.float32
BF16 = jnp.bfloat16

D_MODEL = 1024
EPS = 1e-6
NEG_INF = -1e30
LOG2E = 1.4426950408889634
NORM_SLACK = 1.01
SKIP_LOG2 = 160.0

FOX_HEADS = 16
FOX_HEAD_DIM = 64
FOX_WIDTH = FOX_HEADS * FOX_HEAD_DIM
FOX_PAIRS = FOX_HEADS // 2

GLA_HEADS = 4
GLA_KEY_WIDTH = 512
GLA_VAL_WIDTH = 1024
GLA_HEAD_K = 128
GLA_HEAD_V = 256
GLA_GATE_RANK = 16
GLA_GATE_TEMP = 16.0
GLA_CHUNK = 64

LANES = 128
VMEM_LIMIT = 56 * 1024 * 1024

PROJ_TM = 512
ATTN_BLOCK = 512
GLA_TC = 256


def _params(n_axes):
    return pltpu.CompilerParams(
        dimension_semantics=("arbitrary",) * n_axes,
        vmem_limit_bytes=VMEM_LIMIT,
    )


def _dot(a, b):
    return jnp.dot(a, b, preferred_element_type=F32)


def _dot_nt(a, b):
    return lax.dot_general(a, b, (((1,), (1,)), ((), ())), preferred_element_type=F32)


def _dot_tn(a, b):
    return lax.dot_general(a, b, (((0,), (0,)), ((), ())), preferred_element_type=F32)


def _log_sigmoid(z):
    return jnp.minimum(z, 0.0) - jnp.log1p(jnp.exp(-jnp.abs(z)))


def _rms(x, g):
    ms = jnp.mean(x * x, axis=-1, keepdims=True)
    return (x * lax.rsqrt(ms + EPS)) * g


def _split3(x):
    hi = x.astype(BF16)
    r1 = x - hi.astype(F32)
    mid = r1.astype(BF16)
    lo = (r1 - mid.astype(F32)).astype(BF16)
    return hi, mid, lo


def _tril_cumsum(tril_bf16, x):
    hi, mid, lo = _split3(x)
    return (_dot(tril_bf16, hi) + _dot(tril_bf16, mid)) + _dot(tril_bf16, lo)


def _tril(n):
    r = lax.broadcasted_iota(jnp.int32, (n, n), 0)
    c = lax.broadcasted_iota(jnp.int32, (n, n), 1)
    return jnp.where(r >= c, 1.0, 0.0).astype(BF16)


def _fox_proj_kernel(x_ref, g_ref, w_ref, wf_ref, bf_ref, *rest, want_vt):
    if want_vt:
        q_ref, k_ref, v_ref, kb_ref, sg_ref, lf_ref, vt_ref, nrm_ref = rest
    else:
        q_ref, k_ref, v_ref, kb_ref, sg_ref, lf_ref = rest
    hb = _rms(x_ref[...], g_ref[...]).astype(BF16)
    q = _dot(hb, w_ref[:, 0:FOX_WIDTH])
    qb = (q * (LOG2E * FOX_HEAD_DIM ** -0.5)).astype(BF16)
    q_ref[...] = qb
    k = _dot(hb, w_ref[:, FOX_WIDTH:2 * FOX_WIDTH])
    k_ref[...] = k
    kb = k.astype(BF16)
    kb_ref[...] = kb
    if want_vt:
        row = lax.broadcasted_iota(jnp.int32, (FOX_WIDTH, LANES), 0)
        col = lax.broadcasted_iota(jnp.int32, (FOX_WIDTH, LANES), 1)
        head_of = jnp.where((row >> 6) == col, 1.0, 0.0).astype(BF16)

        def bound(xb):
            xf = xb.astype(F32)
            sq = _dot((xf * xf).astype(BF16), head_of)
            return jnp.sqrt(jnp.max(sq, axis=0, keepdims=True)) * NORM_SLACK

        nrm_ref[...] = jnp.concatenate(
            [bound(kb), bound(qb), jnp.zeros((6, LANES), F32)], axis=0)
    v = _dot(hb, w_ref[:, 2 * FOX_WIDTH:3 * FOX_WIDTH])
    v_ref[...] = v
    if want_vt:
        vt_ref[...] = v.astype(BF16).T
    gate = _dot(hb, w_ref[:, 3 * FOX_WIDTH:4 * FOX_WIDTH])
    sg_ref[...] = gate * jax.nn.sigmoid(gate)
    fl = _dot(hb, wf_ref[...]) + bf_ref[...]
    lf_ref[...] = _log_sigmoid(fl)[:, :FOX_HEADS]


def _fox_proj(x, g, w_main, w_f, b_f, *, tm, want_vt):
    B, L, _ = x.shape
    n = L // tm
    tok = lambda w: pl.BlockSpec((None, tm, w), lambda b, i: (b, i, 0))
    full = lambda a: pl.BlockSpec(a.shape, lambda b, i: (0,) * a.ndim)
    in_specs = [tok(D_MODEL), full(g), full(w_main), full(w_f), full(b_f)]
    args = [x, g, w_main, w_f, b_f]
    out_specs = [tok(FOX_WIDTH)] * 5 + [tok(FOX_HEADS)]
    out_shape = [
        jax.ShapeDtypeStruct((B, L, FOX_WIDTH), BF16),
        jax.ShapeDtypeStruct((B, L, FOX_WIDTH), F32),
        jax.ShapeDtypeStruct((B, L, FOX_WIDTH), F32),
        jax.ShapeDtypeStruct((B, L, FOX_WIDTH), BF16),
        jax.ShapeDtypeStruct((B, L, FOX_WIDTH), F32),
        jax.ShapeDtypeStruct((B, L, FOX_HEADS), F32),
    ]
    if want_vt:
        out_specs.append(pl.BlockSpec((None, None, FOX_WIDTH, tm), lambda b, i: (b, i, 0, 0)))
        out_shape.append(jax.ShapeDtypeStruct((B, n, FOX_WIDTH, tm), BF16))
        out_specs.append(pl.BlockSpec((None, None, 8, LANES), lambda b, i: (b, i, 0, 0)))
        out_shape.append(jax.ShapeDtypeStruct((B, n, 8, LANES), F32))
    return pl.pallas_call(
        functools.partial(_fox_proj_kernel, want_vt=want_vt),
        grid=(B, n),
        in_specs=in_specs,
        out_specs=out_specs,
        out_shape=out_shape,
        compiler_params=_params(2),
        name="fox_proj_vt" if want_vt else "fox_proj",
    )(*args)


BIAS_SLOT = 8


def _bias_selectors():
    shape = (FOX_HEADS, LANES)
    head = lax.broadcasted_iota(jnp.int32, shape, 0)
    col = lax.broadcasted_iota(jnp.int32, shape, 1)
    slot = BIAS_SLOT * head
    sel_f = [jnp.where(col == slot + x, 1.0, 0.0).astype(BF16) for x in range(3)]
    sel_e = [jnp.where(col == slot + 3 + x, -1.0, 0.0).astype(BF16) for x in range(3)]
    lane = lax.broadcasted_iota(jnp.int32, (1, LANES), 1) & (BIAS_SLOT - 1)
    one_f = jnp.where((lane >= 3) & (lane < 6), 1.0, 0.0)
    one_e = jnp.where(lane < 3, 1.0, 0.0)
    return sel_f, sel_e, one_f, one_e


def _cumsum_kernel(lf_ref, c_ref, *rest, rows, n_chunks, want_bias):
    tril = _tril(rows)
    if want_bias:
        e_ref, f_ref, rng_ref, carry_ref = rest
        sel_f, sel_e, one_f, one_e = _bias_selectors()
    else:
        (carry_ref,) = rest

    @pl.when(pl.program_id(1) == 0)
    def _():
        carry_ref[...] = jnp.zeros_like(carry_ref)

    def body(i, carry):
        r0 = pl.multiple_of(i * rows, rows)
        cs = _tril_cumsum(tril, lf_ref[pl.ds(r0, rows), :]) + carry
        c2 = cs * LOG2E
        c_ref[pl.ds(r0, rows), :] = c2
        if want_bias:
            hi, mid, lo = _split3(c2)
            f_ref[pl.ds(r0, rows), :] = (
                (_dot(hi, sel_f[0]) + _dot(mid, sel_f[1])) + _dot(lo, sel_f[2]) + one_f).astype(BF16)
            e_ref[pl.ds(r0, rows), :] = (
                (_dot(hi, sel_e[0]) + _dot(mid, sel_e[1])) + _dot(lo, sel_e[2]) + one_e).astype(BF16)
        return cs[rows - 1:rows, :]

    carry_ref[...] = lax.fori_loop(0, n_chunks, body, carry_ref[...])
    if want_bias:
        c_all = c_ref[...]
        rng_ref[...] = jnp.concatenate(
            [jnp.min(c_all, axis=0, keepdims=True), jnp.max(c_all, axis=0, keepdims=True),
             jnp.zeros((6, FOX_HEADS), F32)], axis=0)


def _cumsum(lf, *, rows, tb, want_bias):
    B, L, H = lf.shape
    spec = pl.BlockSpec((None, tb, H), lambda b, i: (b, i, 0))
    wide = pl.BlockSpec((None, tb, LANES), lambda b, i: (b, i, 0))
    out_specs = [spec]
    out_shape = [jax.ShapeDtypeStruct((B, L, H), F32)]
    if want_bias:
        out_specs += [wide, wide, pl.BlockSpec((None, None, 8, H), lambda b, i: (b, i, 0, 0))]
        out_shape += [jax.ShapeDtypeStruct((B, L, LANES), BF16)] * 2
        out_shape += [jax.ShapeDtypeStruct((B, L // tb, 8, H), F32)]
    return pl.pallas_call(
        functools.partial(_cumsum_kernel, rows=rows, n_chunks=tb // rows, want_bias=want_bias),
        grid=(B, L // tb),
        in_specs=[spec],
        out_specs=out_specs,
        out_shape=out_shape,
        scratch_shapes=[pltpu.VMEM((1, FOX_HEADS), F32)],
        compiler_params=_params(2),
        name="fox_cumsum_bias" if want_bias else "fox_cumsum",
    )(lf)


def _skip_plan_kernel(kn_ref, cmin_ref, qn_ref, cmax_ref, knq_ref, first_ref, *, nb):
    j = lax.broadcasted_iota(jnp.int32, (nb, nb), 0).astype(F32)
    q = lax.broadcasted_iota(jnp.int32, (nb, nb), 1).astype(F32)
    rows = []
    for p in range(FOX_PAIRS):
        skippable = None
        for h in (2 * p, 2 * p + 1):
            qn = qn_ref[h:h + 1, :]
            bound = (kn_ref[:, h:h + 1] * qn + (cmax_ref[h:h + 1, :] - cmin_ref[:, h:h + 1])
                     + qn * knq_ref[h:h + 1, :])
            ok = bound < -SKIP_LOG2
            skippable = ok if skippable is None else (skippable & ok)
        must = (j < q) & jnp.logical_not(skippable)
        rows.append(jnp.min(jnp.where(must, j, q), axis=0, keepdims=True))
    first_ref[...] = jnp.concatenate(rows, axis=0).astype(jnp.int32)


def _skip_plan(kn, cmin, qn, cmax):
    B, nb, H = kn.shape
    by_block = pl.BlockSpec((None, nb, H), lambda b: (b, 0, 0))
    by_head = pl.BlockSpec((None, H, nb), lambda b: (b, 0, 0))
    t = lambda a: jnp.swapaxes(a, 1, 2)
    return pl.pallas_call(
        functools.partial(_skip_plan_kernel, nb=nb),
        grid=(B,),
        in_specs=[by_block, by_block, by_head, by_head, by_head],
        out_specs=pl.BlockSpec((None, FOX_PAIRS, nb), lambda b: (b, 0, 0)),
        out_shape=jax.ShapeDtypeStruct((B, FOX_PAIRS, nb), jnp.int32),
        compiler_params=_params(1),
        name="fox_skip_plan",
    )(kn, cmin, t(qn), t(cmax), t(kn))


def _stack_heads(q2):
    lane = lax.broadcasted_iota(jnp.int32, q2.shape, 1)
    zero = jnp.zeros_like(q2)
    return jnp.concatenate(
        [jnp.where(lane < FOX_HEAD_DIM, q2, zero), jnp.where(lane >= FOX_HEAD_DIM, q2, zero)], axis=0)


ATTN_TILE = 256
SUM_ROWS = 16


def _attn_prompt_kernel(n_ref, vq_ref, vj_ref, q_ref, f_ref, k_ref, e_ref, vt_ref, o_ref,
                        m_ref, acc_ref, sa_ref, xa_ref, sb_ref, xb_ref, *, blk, nb):
    b_id = pl.program_id(0)
    p_id = pl.program_id(1)
    hd = FOX_HEAD_DIM
    tw = ATTN_TILE
    per_head = blk // tw
    n_tiles = 2 * per_head

    lane = lax.broadcasted_iota(jnp.int32, (blk, LANES), 1)
    zero = jnp.zeros((blk, LANES), BF16)
    slot_a = 2 * BIAS_SLOT * p_id
    slot_b = slot_a + BIAS_SLOT
    ones_rows = jnp.ones((SUM_ROWS, blk), BF16)

    m_ref[...] = jnp.full(m_ref.shape, NEG_INF, F32)
    acc_ref[...] = jnp.zeros(acc_ref.shape, F32)

    def query_tiles(qi):
        r0 = pl.multiple_of(qi * blk, blk)
        q2 = q_ref[pl.ds(r0, blk), :]
        f2 = f_ref[pl.ds(r0, blk), :]
        heads = [
            jnp.concatenate([jnp.where(lane < hd, q2, zero),
                             jnp.where((lane >= slot_a) & (lane < slot_b), f2, zero)], axis=1),
            jnp.concatenate([jnp.where(lane >= hd, q2, zero),
                             jnp.where((lane >= slot_b) & (lane < slot_b + BIAS_SLOT), f2, zero)], axis=1),
        ]
        return [heads[t // per_head][(t % per_head) * tw:(t % per_head + 1) * tw] for t in range(n_tiles)]

    def scores(qi, j, buf, diagonal):
        s_ref, x_ref = buf
        q_tiles = query_tiles(qi)
        k0 = pl.multiple_of(j * blk, blk)
        kaug = jnp.concatenate([k_ref[pl.ds(k0, blk), :], e_ref[pl.ds(k0, blk), :]], axis=1)
        for t in range(n_tiles):
            s = _dot_nt(kaug, q_tiles[t])
            if diagonal:
                r = lax.broadcasted_iota(jnp.int32, (blk, tw), 0)
                c = lax.broadcasted_iota(jnp.int32, (blk, tw), 1) + (t % per_head) * tw
                s = jnp.where(r <= c, s, NEG_INF)
            s_ref[t] = s
            x_ref[t] = jnp.max(s, axis=0, keepdims=True)

    def absorb(qi, j, buf):
        s_ref, x_ref = buf
        vt = vt_ref[j]
        v_heads = [jnp.concatenate([vt[h * hd:(h + 1) * hd, :], ones_rows], axis=0) for h in range(2)]
        for t in range(n_tiles):
            idx = qi * n_tiles + t
            m_old = m_ref[idx]
            m_new = jnp.maximum(m_old, x_ref[t])
            alpha = jnp.exp2(m_old - m_new)
            p = jnp.exp2(s_ref[t] - m_new)
            m_ref[idx] = m_new
            pv = _dot(v_heads[t // per_head], p.astype(BF16))
            acc_ref[idx] = acc_ref[idx] * alpha + pv

    buf_a = (sa_ref, xa_ref)
    buf_b = (sb_ref, xb_ref)

    scores(0, 0, buf_a, True)

    def diag_pair(n, carry):
        i0 = 2 * n
        i2 = jnp.minimum(i0 + 2, nb - 1)
        scores(i0 + 1, i0 + 1, buf_b, True)
        absorb(i0, i0, buf_a)
        scores(i2, i2, buf_a, True)
        absorb(i0 + 1, i0 + 1, buf_b)
        return carry

    lax.fori_loop(0, nb // 2, diag_pair, 0)

    n_off = n_ref[b_id, p_id]
    visit = lambda v: (vq_ref[b_id, p_id, v], vj_ref[b_id, p_id, v])

    @pl.when(n_off > 0)
    def _():
        scores(*visit(0), buf_a, False)

    def off_pair(n, carry):
        v0 = 2 * n
        scores(*visit(v0 + 1), buf_b, False)
        absorb(*visit(v0), buf_a)
        scores(*visit(v0 + 2), buf_a, False)
        absorb(*visit(v0 + 1), buf_b)
        return carry

    lax.fori_loop(0, jnp.maximum(n_off - 1, 0) // 2, off_pair, 0)

    @pl.when((n_off > 0) & (n_off % 2 == 0))
    def _():
        scores(*visit(n_off - 1), buf_b, False)
        absorb(*visit(n_off - 2), buf_a)
        absorb(*visit(n_off - 1), buf_b)

    @pl.when(n_off % 2 == 1)
    def _():
        absorb(*visit(n_off - 1), buf_a)

    def finish(qi, carry):
        halves = [jnp.concatenate([acc_ref[qi * n_tiles + t, 0:hd, :] / acc_ref[qi * n_tiles + t, hd:hd + 1, :]
                                   for t in range(h * per_head, (h + 1) * per_head)], axis=1)
                  for h in range(2)]
        o_ref[pl.ds(pl.multiple_of(qi * blk, blk), blk), :] = jnp.concatenate(halves, axis=0).T
        return carry

    lax.fori_loop(0, nb, finish, 0)


def _visit_lists(first):
    B, P, nb = first.shape
    vmax = nb * (nb - 1) // 2 + 8
    q_idx = jnp.arange(nb, dtype=jnp.int32)
    cnt = q_idx - first
    incl = jnp.cumsum(cnt, axis=-1)
    excl = incl - cnt
    v = jnp.arange(vmax, dtype=jnp.int32)
    vq = jnp.minimum(jnp.sum(v[None, None, :, None] >= incl[:, :, None, :], axis=-1), nb - 1).astype(jnp.int32)
    take = lambda a: jnp.take_along_axis(a, vq, axis=-1)
    vj = jnp.clip(take(first) + (v - take(excl)), 0, nb - 1).astype(jnp.int32)
    return incl[..., -1].astype(jnp.int32), vq, vj


def _attn_prompt(first, q, f, kb, e, vt, *, blk):
    B, L, _ = q.shape
    nb = L // blk
    n_tiles = 2 * blk // ATTN_TILE
    n_off, vq, vj = _visit_lists(first)
    by_pair = pl.BlockSpec((None, L, LANES), lambda b, p, *_: (b, 0, p))
    shared = pl.BlockSpec((None, L, LANES), lambda b, p, *_: (b, 0, 0))
    return pl.pallas_call(
        functools.partial(_attn_prompt_kernel, blk=blk, nb=nb),
        grid_spec=pltpu.PrefetchScalarGridSpec(
            num_scalar_prefetch=3,
            grid=(B, FOX_PAIRS),
            in_specs=[by_pair, shared, by_pair, shared,
                      pl.BlockSpec((None, nb, LANES, blk), lambda b, p, *_: (b, 0, p, 0))],
            out_specs=by_pair,
            scratch_shapes=[
                pltpu.VMEM((nb * n_tiles, 1, ATTN_TILE), F32),
                pltpu.VMEM((nb * n_tiles, FOX_HEAD_DIM + SUM_ROWS, ATTN_TILE), F32),
                pltpu.VMEM((n_tiles, blk, ATTN_TILE), F32),
                pltpu.VMEM((n_tiles, 1, ATTN_TILE), F32),
                pltpu.VMEM((n_tiles, blk, ATTN_TILE), F32),
                pltpu.VMEM((n_tiles, 1, ATTN_TILE), F32),
            ],
        ),
        out_shape=jax.ShapeDtypeStruct((B, L, FOX_WIDTH), F32),
        compiler_params=_params(2),
        name="fox_attn_prompt",
    )(n_off, vq, vj, q, f, kb, e, vt)


def _attn_sample_kernel(q_ref, kn_ref, vn_ref, ck_ref, cv_ref, c_ref, ct_ref, o_ref, *, past, ls):
    p = pl.program_id(1)
    hd = FOX_HEAD_DIM
    qst = _stack_heads(q_ref[...])
    kc = ck_ref[...].astype(BF16)
    vc = cv_ref[...].astype(BF16)
    pad_k = jnp.zeros((LANES - ls, LANES), BF16)
    kn = jnp.concatenate([kn_ref[...], pad_k], axis=0)
    vn = jnp.concatenate([vn_ref[...].astype(BF16), pad_k], axis=0)

    cqb = c_ref[past:past + ls, :]
    lane = lax.broadcasted_iota(jnp.int32, cqb.shape, 1)
    cq = jnp.concatenate(
        [jnp.sum(jnp.where(lane == 2 * p + h2, cqb, 0.0), axis=1, keepdims=True) for h2 in range(2)],
        axis=0)
    cka = ct_ref[pl.ds(2 * p, 1), :]
    ckb = ct_ref[pl.ds(2 * p + 1, 1), :]

    def bias(lo, hi):
        return jnp.concatenate([cq[:ls] - cka[:, lo:hi], cq[ls:] - ckb[:, lo:hi]], axis=0)

    s_c = _dot_nt(qst, kc) + bias(0, past)
    s_n = _dot_nt(qst, kn) + bias(past, past + LANES)
    r = lax.broadcasted_iota(jnp.int32, s_n.shape, 0) & (ls - 1)
    c = lax.broadcasted_iota(jnp.int32, s_n.shape, 1)
    s_n = jnp.where(c <= r, s_n, NEG_INF)
    m = jnp.maximum(jnp.max(s_c, axis=1, keepdims=True), jnp.max(s_n, axis=1, keepdims=True))
    p_c = jnp.exp2(s_c - m)
    p_n = jnp.exp2(s_n - m)
    l = jnp.sum(p_c, axis=1, keepdims=True) + jnp.sum(p_n, axis=1, keepdims=True)
    o = (_dot(p_c.astype(BF16), vc) + _dot(p_n.astype(BF16), vn)) / l
    lane_o = lax.broadcasted_iota(jnp.int32, (ls, LANES), 1)
    o_ref[...] = jnp.where(lane_o < hd, o[:ls], o[ls:])


def _attn_sample(q, kb, v, cache_k, cache_v, c, ct, *, past):
    B, ls, _ = q.shape
    tot = c.shape[1]
    tile = lambda rows: pl.BlockSpec((None, rows, LANES), lambda b, p: (b, 0, p))
    return pl.pallas_call(
        functools.partial(_attn_sample_kernel, past=past, ls=ls),
        grid=(B, FOX_PAIRS),
        in_specs=[
            tile(ls), tile(ls), tile(ls), tile(past), tile(past),
            pl.BlockSpec((None, tot, FOX_HEADS), lambda b, p: (b, 0, 0)),
            pl.BlockSpec((None, FOX_HEADS, tot), lambda b, p: (b, 0, 0)),
        ],
        out_specs=tile(ls),
        out_shape=jax.ShapeDtypeStruct((B, ls, FOX_WIDTH), F32),
        compiler_params=_params(2),
        name="fox_attn_sample",
    )(q, kb, v, cache_k, cache_v, c, ct)


def _mid_kernel(o_ref, sg_ref, x_ref, wo_ref, g_ref, w_ref, wa1_ref, wa2_ref, ba_ref,
                y_ref, q_ref, k_ref, v_ref, sg2_ref, ga_ref):
    u = (o_ref[...] * sg_ref[...]).astype(BF16)
    y = x_ref[...] + _dot(u, wo_ref[...])
    y_ref[...] = y
    hb = _rms(y, g_ref[...]).astype(BF16)
    kw = GLA_KEY_WIDTH
    q_ref[...] = _dot(hb, w_ref[:, 0:kw]) * (GLA_HEAD_K ** -0.5)
    k_ref[...] = _dot(hb, w_ref[:, kw:2 * kw])
    v_ref[...] = _dot(hb, w_ref[:, 2 * kw:2 * kw + GLA_VAL_WIDTH]).astype(BF16)
    gate = _dot(hb, w_ref[:, 2 * kw + GLA_VAL_WIDTH:2 * kw + 2 * GLA_VAL_WIDTH])
    sg2_ref[...] = gate * jax.nn.sigmoid(gate)
    a1 = _dot(hb, wa1_ref[...]).astype(BF16)
    a = _dot(a1, wa2_ref[...]) + ba_ref[...]
    ga_ref[...] = _log_sigmoid(a) * (1.0 / GLA_GATE_TEMP)


def _mid(o, sg, x, w_out, g, w_main, w_a1, w_a2, b_a, *, tm):
    T = o.shape[0]
    tok = lambda w: pl.BlockSpec((tm, w), lambda i: (i, 0))
    full = lambda a: pl.BlockSpec(a.shape, lambda i: (0,) * a.ndim)
    return pl.pallas_call(
        _mid_kernel,
        grid=(T // tm,),
        in_specs=[tok(D_MODEL), tok(D_MODEL), tok(D_MODEL), full(w_out), full(g), full(w_main),
                  full(w_a1), full(w_a2), full(b_a)],
        out_specs=[tok(D_MODEL), tok(GLA_KEY_WIDTH), tok(GLA_KEY_WIDTH), tok(GLA_VAL_WIDTH),
                   tok(GLA_VAL_WIDTH), tok(GLA_KEY_WIDTH)],
        out_shape=[
            jax.ShapeDtypeStruct((T, D_MODEL), F32),
            jax.ShapeDtypeStruct((T, GLA_KEY_WIDTH), F32),
            jax.ShapeDtypeStruct((T, GLA_KEY_WIDTH), F32),
            jax.ShapeDtypeStruct((T, GLA_VAL_WIDTH), BF16),
            jax.ShapeDtypeStruct((T, GLA_VAL_WIDTH), F32),
            jax.ShapeDtypeStruct((T, GLA_KEY_WIDTH), F32),
        ],
        compiler_params=_params(1),
        name="fox_out_gla_proj",
    )(o, sg, x, w_out, g, w_main, w_a1, w_a2, b_a)


def _gla_kernel(q_ref, k_ref, v_ref, ga_ref, s0_ref, o_ref, s_ref, *, n_chunks):
    i = pl.program_id(1)
    ch = GLA_CHUNK
    dk, dv = GLA_HEAD_K, GLA_HEAD_V

    @pl.when(i == 0)
    def _():
        s_ref[...] = s0_ref[...]

    t = n_chunks * ch
    r = lax.broadcasted_iota(jnp.int32, (t, t), 0)
    c = lax.broadcasted_iota(jnp.int32, (t, t), 1)
    causal = ((r >> 6) == (c >> 6)) & (r >= c)
    tril = jnp.where(causal, 1.0, 0.0).astype(BF16)
    eye = (lax.broadcasted_iota(jnp.int32, (dk, dk), 0) == lax.broadcasted_iota(jnp.int32, (dk, dk), 1))

    b = _tril_cumsum(tril, ga_ref[...])
    lasts = [b[(ci + 1) * ch - 1:(ci + 1) * ch, :] for ci in range(n_chunks)]
    b_last = jnp.concatenate([jnp.broadcast_to(x, (ch, GLA_KEY_WIDTH)) for x in lasts], axis=0)
    kk = k_ref[...]
    qe = (q_ref[...] * jnp.exp(b)).astype(BF16)
    ke = (kk * jnp.exp(-b)).astype(BF16)
    kd = (kk * jnp.exp(b_last - b)).astype(BF16)
    decs = [jnp.exp(x) for x in lasts]

    for h in range(GLA_HEADS):
        ks = slice(h * dk, (h + 1) * dk)
        vs = slice(h * dv, (h + 1) * dv)
        vh = v_ref[:, vs]
        a = jnp.where(causal, _dot_nt(qe[:, ks], ke[:, ks]), 0.0)
        o_intra = _dot(a.astype(BF16), vh)
        s = s_ref[h]
        for ci in range(n_chunks):
            rows = slice(ci * ch, (ci + 1) * ch)
            o_ref[rows, vs] = o_intra[rows] + _dot(qe[rows, ks], s.astype(BF16))
            dec_col = jnp.sum(jnp.where(eye, jnp.broadcast_to(decs[ci][:, ks], (dk, dk)), 0.0),
                              axis=1, keepdims=True)
            s = dec_col * s + _dot_tn(kd[rows, ks], vh[rows])
        s_ref[h] = s


def _gla(q, k, v, ga, s0, *, tc):
    B, L, _ = q.shape
    tok = lambda w: pl.BlockSpec((None, tc, w), lambda b, i: (b, i, 0))
    st = pl.BlockSpec((None, GLA_HEADS, GLA_HEAD_K, GLA_HEAD_V), lambda b, i: (b, 0, 0, 0))
    return pl.pallas_call(
        functools.partial(_gla_kernel, n_chunks=tc // GLA_CHUNK),
        grid=(B, L // tc),
        in_specs=[tok(GLA_KEY_WIDTH), tok(GLA_KEY_WIDTH), tok(GLA_VAL_WIDTH), tok(GLA_KEY_WIDTH), st],
        out_specs=[tok(GLA_VAL_WIDTH), st],
        out_shape=[
            jax.ShapeDtypeStruct((B, L, GLA_VAL_WIDTH), F32),
            jax.ShapeDtypeStruct((B, GLA_HEADS, GLA_HEAD_K, GLA_HEAD_V), F32),
        ],
        compiler_params=_params(2),
        name="gla_chunk",
    )(q, k, v, ga, s0)


def _gla_out_kernel(o_ref, sg_ref, y_ref, go_ref, wo_ref, gf_ref, out_ref, *, chunk_major):
    dv = GLA_HEAD_V
    if chunk_major:
        o = jnp.concatenate([o_ref[:, t, :] for t in range(o_ref.shape[1])], axis=0)
    else:
        o = o_ref[...]
    parts = []
    for h in range(GLA_HEADS):
        vs = slice(h * dv, (h + 1) * dv)
        parts.append((_rms(o[:, vs], go_ref[...]) * sg_ref[:, vs]).astype(BF16))
    u = jnp.concatenate(parts, axis=1)
    y = y_ref[...] + _dot(u, wo_ref[...])
    out_ref[...] = _rms(y, gf_ref[...])


def _gla_out(o, sg, y, g_o, w_out, g_f, *, tm):
    B, L, _ = o.shape
    n = L // GLA_CHUNK
    chunk_major = n > 1
    if chunk_major:
        tm = 8 * n
        o = o.reshape(B, n, GLA_CHUNK, D_MODEL)
        o_spec = pl.BlockSpec((None, n, 8, D_MODEL), lambda b, i: (b, 0, i, 0))
    else:
        flat = lambda a: a.reshape(1, B * L, D_MODEL)
        o, sg, y = flat(o), flat(sg), flat(y)
        o_spec = pl.BlockSpec((None, tm, D_MODEL), lambda b, i: (b, i, 0))
    nb, rows, _ = sg.shape
    tok = pl.BlockSpec((None, tm, D_MODEL), lambda b, i: (b, i, 0))
    full = lambda a: pl.BlockSpec(a.shape, lambda b, i: (0,) * a.ndim)
    return pl.pallas_call(
        functools.partial(_gla_out_kernel, chunk_major=chunk_major),
        grid=(nb, rows // tm),
        in_specs=[o_spec, tok, tok, full(g_o), full(w_out), full(g_f)],
        out_specs=tok,
        out_shape=jax.ShapeDtypeStruct((nb, rows, D_MODEL), F32),
        compiler_params=_params(2),
        name="gla_out",
    )(o, sg, y, g_o, w_out, g_f)


def _pad_cols(w, n):
    return jnp.pad(w, ((0, 0), (0, n - w.shape[1])))


def kernel(x_prompt, x_sample, cache_fox_k, cache_fox_v, cache_fox_logf, state_gla,
           g_norm_fox, w_in_fox, b_fox_f, w_out_fox,
           g_norm_gla, w_in_gla, w_gla_a2, b_gla_a, g_gla_o, w_out_gla, g_final):
    B, L, D = x_prompt.shape
    Bs, Ls, _ = x_sample.shape
    P = cache_fox_k.shape[1]
    H, hd = FOX_HEADS, FOX_HEAD_DIM

    row = lambda a: a.reshape(1, -1).astype(F32)
    w_fox = w_in_fox[:, :4 * FOX_WIDTH].astype(BF16)
    w_fox_f = _pad_cols(w_in_fox[:, 4 * FOX_WIDTH:], LANES).astype(BF16)
    b_f = _pad_cols(row(b_fox_f), LANES)
    w_o_fox = w_out_fox.astype(BF16)
    n_main = 2 * GLA_KEY_WIDTH + 2 * GLA_VAL_WIDTH
    w_gla = w_in_gla[:, :n_main].astype(BF16)
    w_a1 = _pad_cols(w_in_gla[:, n_main:], LANES).astype(BF16)
    w_a2 = jnp.pad(w_gla_a2, ((0, LANES - GLA_GATE_RANK), (0, 0))).astype(BF16)
    w_o_gla = w_out_gla.astype(BF16)

    q_p, k_p, v_p, kb_p, sg_p, lf_p, vt_p, nrm_p = _fox_proj(
        x_prompt, row(g_norm_fox), w_fox, w_fox_f, b_f, tm=ATTN_BLOCK, want_vt=True)
    _, e_p, f_p, rng_p = _cumsum(lf_p, rows=256, tb=ATTN_BLOCK, want_bias=True)
    first_p = _skip_plan(nrm_p[:, :, 0, :H], rng_p[:, :, 0, :], nrm_p[:, :, 1, :H], rng_p[:, :, 1, :])
    o_p = _attn_prompt(first_p, q_p, f_p, kb_p, e_p, vt_p, blk=ATTN_BLOCK)

    xs = x_sample.reshape(1, Bs * Ls, D)
    q_s, k_s, v_s, kb_s, sg_s, lf_s = _fox_proj(
        xs, row(g_norm_fox), w_fox, w_fox_f, b_f, tm=PROJ_TM, want_vt=False)
    sh = lambda a: a.reshape(Bs, Ls, a.shape[-1])
    q_s, k_s, v_s, kb_s, lf_s = sh(q_s), sh(k_s), sh(v_s), sh(kb_s), sh(lf_s)
    lf_all = jnp.concatenate(
        [cache_fox_logf.astype(F32), lf_s, jnp.zeros((Bs, LANES - Ls, H), F32)], axis=1)
    (c_s,) = _cumsum(lf_all, rows=LANES, tb=P + LANES, want_bias=False)
    o_s = _attn_sample(q_s, kb_s, v_s, cache_fox_k.reshape(Bs, P, FOX_WIDTH),
                       cache_fox_v.reshape(Bs, P, FOX_WIDTH), c_s, jnp.swapaxes(c_s, 1, 2), past=P)

    def gla_layer(o, sg, x, s0, tc):
        b, l, _ = x.shape
        flat = lambda a: a.reshape(b * l, a.shape[-1])
        y1, q, k, v, sg2, ga = _mid(flat(o), flat(sg), flat(x), w_o_fox, row(g_norm_gla), w_gla,
                                    w_a1, w_a2, row(b_gla_a), tm=PROJ_TM)
        un = lambda a: a.reshape(b, l, a.shape[-1])
        og, s_out = _gla(un(q), un(k), un(v), un(ga), s0, tc=tc)
        y = _gla_out(og, un(sg2), un(y1), row(g_gla_o), w_o_gla, row(g_final), tm=PROJ_TM)
        return y.reshape(b, l, D), s_out

    s0_p = jnp.zeros((B, GLA_HEADS, GLA_HEAD_K, GLA_HEAD_V), F32)
    y_p, s_p = gla_layer(o_p, sg_p, x_prompt, s0_p, GLA_TC)
    y_s, s_s = gla_layer(o_s, sg_s.reshape(Bs, Ls, D), x_sample, state_gla.astype(F32), Ls)

    return (y_p, y_s,
            k_p.reshape(B, L, H, hd), v_p.reshape(B, L, H, hd), lf_p, s_p.astype(state_gla.dtype),
            k_s.reshape(Bs, Ls, H, hd), v_s.reshape(Bs, Ls, H, hd), lf_s, s_s.astype(state_gla.dtype))
```

```python
import functools

import jax
import jax.numpy as jnp
from jax import lax
from jax.experimental import pallas as pl
from jax.experimental.pallas import tpu as pltpu

F32 = jnp.float32
BF16 = jnp.bfloat16

D_MODEL = 1024
EPS = 1e-6
NEG_INF = -1e30
LOG2E = 1.4426950408889634
NORM_SLACK = 1.01
SKIP_LOG2 = 160.0

FOX_HEADS = 16
FOX_HEAD_DIM = 64
FOX_WIDTH = FOX_HEADS * FOX_HEAD_DIM
FOX_PAIRS = FOX_HEADS // 2

GLA_HEADS = 4
GLA_KEY_WIDTH = 512
GLA_VAL_WIDTH = 1024
GLA_HEAD_K = 128
GLA_HEAD_V = 256
GLA_GATE_RANK = 16
GLA_GATE_TEMP = 16.0
GLA_CHUNK = 64

LANES = 128
VMEM_LIMIT = 56 * 1024 * 1024

PROJ_TM = 512
ATTN_BLOCK = 512
GLA_TC = 256


def _params(n_axes):
    return pltpu.CompilerParams(
        dimension_semantics=("arbitrary",) * n_axes,
        vmem_limit_bytes=VMEM_LIMIT,
    )


def _dot(a, b):
    return jnp.dot(a, b, preferred_element_type=F32)


def _dot_nt(a, b):
    return lax.dot_general(a, b, (((1,), (1,)), ((), ())), preferred_element_type=F32)


def _dot_tn(a, b):
    return lax.dot_general(a, b, (((0,), (0,)), ((), ())), preferred_element_type=F32)


def _log_sigmoid(z):
    return jnp.minimum(z, 0.0) - jnp.log1p(jnp.exp(-jnp.abs(z)))


def _rms(x, g):
    ms = jnp.mean(x * x, axis=-1, keepdims=True)
    return (x * lax.rsqrt(ms + EPS)) * g


def _split3(x):
    hi = x.astype(BF16)
    r1 = x - hi.astype(F32)
    mid = r1.astype(BF16)
    lo = (r1 - mid.astype(F32)).astype(BF16)
    return hi, mid, lo


def _tril_cumsum(tril_bf16, x):
    hi, mid, lo = _split3(x)
    return (_dot(tril_bf16, hi) + _dot(tril_bf16, mid)) + _dot(tril_bf16, lo)


def _tril(n):
    r = lax.broadcasted_iota(jnp.int32, (n, n), 0)
    c = lax.broadcasted_iota(jnp.int32, (n, n), 1)
    return jnp.where(r >= c, 1.0, 0.0).astype(BF16)


def _fox_proj_kernel(x_ref, g_ref, w_ref, wf_ref, bf_ref, *rest, want_vt):
    if want_vt:
        q_ref, k_ref, v_ref, kb_ref, sg_ref, lf_ref, vt_ref, nrm_ref = rest
    else:
        q_ref, k_ref, v_ref, kb_ref, sg_ref, lf_ref = rest
    hb = _rms(x_ref[...], g_ref[...]).astype(BF16)
    q = _dot(hb, w_ref[:, 0:FOX_WIDTH])
    qb = (q * (LOG2E * FOX_HEAD_DIM ** -0.5)).astype(BF16)
    q_ref[...] = qb
    k = _dot(hb, w_ref[:, FOX_WIDTH:2 * FOX_WIDTH])
    k_ref[...] = k
    kb = k.astype(BF16)
    kb_ref[...] = kb
    if want_vt:
        row = lax.broadcasted_iota(jnp.int32, (FOX_WIDTH, LANES), 0)
        col = lax.broadcasted_iota(jnp.int32, (FOX_WIDTH, LANES), 1)
        head_of = jnp.where((row >> 6) == col, 1.0, 0.0).astype(BF16)

        def bound(xb):
            xf = xb.astype(F32)
            sq = _dot((xf * xf).astype(BF16), head_of)
            return jnp.sqrt(jnp.max(sq, axis=0, keepdims=True)) * NORM_SLACK

        nrm_ref[...] = jnp.concatenate(
            [bound(kb), bound(qb), jnp.zeros((6, LANES), F32)], axis=0)
    v = _dot(hb, w_ref[:, 2 * FOX_WIDTH:3 * FOX_WIDTH])
    v_ref[...] = v
    if want_vt:
        vt_ref[...] = v.astype(BF16).T
    gate = _dot(hb, w_ref[:, 3 * FOX_WIDTH:4 * FOX_WIDTH])
    sg_ref[...] = gate * jax.nn.sigmoid(gate)
    fl = _dot(hb, wf_ref[...]) + bf_ref[...]
    lf_ref[...] = _log_sigmoid(fl)[:, :FOX_HEADS]


def _fox_proj(x, g, w_main, w_f, b_f, *, tm, want_vt):
    B, L, _ = x.shape
    n = L // tm
    tok = lambda w: pl.BlockSpec((None, tm, w), lambda b, i: (b, i, 0))
    full = lambda a: pl.BlockSpec(a.shape, lambda b, i: (0,) * a.ndim)
    in_specs = [tok(D_MODEL), full(g), full(w_main), full(w_f), full(b_f)]
    args = [x, g, w_main, w_f, b_f]
    out_specs = [tok(FOX_WIDTH)] * 5 + [tok(FOX_HEADS)]
    out_shape = [
        jax.ShapeDtypeStruct((B, L, FOX_WIDTH), BF16),
        jax.ShapeDtypeStruct((B, L, FOX_WIDTH), F32),
        jax.ShapeDtypeStruct((B, L, FOX_WIDTH), F32),
        jax.ShapeDtypeStruct((B, L, FOX_WIDTH), BF16),
        jax.ShapeDtypeStruct((B, L, FOX_WIDTH), F32),
        jax.ShapeDtypeStruct((B, L, FOX_HEADS), F32),
    ]
    if want_vt:
        out_specs.append(pl.BlockSpec((None, None, FOX_WIDTH, tm), lambda b, i: (b, i, 0, 0)))
        out_shape.append(jax.ShapeDtypeStruct((B, n, FOX_WIDTH, tm), BF16))
        out_specs.append(pl.BlockSpec((None, None, 8, LANES), lambda b, i: (b, i, 0, 0)))
        out_shape.append(jax.ShapeDtypeStruct((B, n, 8, LANES), F32))
    return pl.pallas_call(
        functools.partial(_fox_proj_kernel, want_vt=want_vt),
        grid=(B, n),
        in_specs=in_specs,
        out_specs=out_specs,
        out_shape=out_shape,
        compiler_params=_params(2),
        name="fox_proj_vt" if want_vt else "fox_proj",
    )(*args)


BIAS_SLOT = 8


def _bias_selectors():
    shape = (FOX_HEADS, LANES)
    head = lax.broadcasted_iota(jnp.int32, shape, 0)
    col = lax.broadcasted_iota(jnp.int32, shape, 1)
    slot = BIAS_SLOT * head
    sel_f = [jnp.where(col == slot + x, 1.0, 0.0).astype(BF16) for x in range(3)]
    sel_e = [jnp.where(col == slot + 3 + x, -1.0, 0.0).astype(BF16) for x in range(3)]
    lane = lax.broadcasted_iota(jnp.int32, (1, LANES), 1) & (BIAS_SLOT - 1)
    one_f = jnp.where((lane >= 3) & (lane < 6), 1.0, 0.0)
    one_e = jnp.where(lane < 3, 1.0, 0.0)
    return sel_f, sel_e, one_f, one_e


def _cumsum_kernel(lf_ref, c_ref, *rest, rows, n_chunks, want_bias):
    tril = _tril(rows)
    if want_bias:
        e_ref, f_ref, rng_ref, carry_ref = rest
        sel_f, sel_e, one_f, one_e = _bias_selectors()
    else:
        (carry_ref,) = rest

    @pl.when(pl.program_id(1) == 0)
    def _():
        carry_ref[...] = jnp.zeros_like(carry_ref)

    def body(i, carry):
        r0 = pl.multiple_of(i * rows, rows)
        cs = _tril_cumsum(tril, lf_ref[pl.ds(r0, rows), :]) + carry
        c2 = cs * LOG2E
        c_ref[pl.ds(r0, rows), :] = c2
        if want_bias:
            hi, mid, lo = _split3(c2)
            f_ref[pl.ds(r0, rows), :] = (
                (_dot(hi, sel_f[0]) + _dot(mid, sel_f[1])) + _dot(lo, sel_f[2]) + one_f).astype(BF16)
            e_ref[pl.ds(r0, rows), :] = (
                (_dot(hi, sel_e[0]) + _dot(mid, sel_e[1])) + _dot(lo, sel_e[2]) + one_e).astype(BF16)
        return cs[rows - 1:rows, :]

    carry_ref[...] = lax.fori_loop(0, n_chunks, body, carry_ref[...])
    if want_bias:
        c_all = c_ref[...]
        rng_ref[...] = jnp.concatenate(
            [jnp.min(c_all, axis=0, keepdims=True), jnp.max(c_all, axis=0, keepdims=True),
             jnp.zeros((6, FOX_HEADS), F32)], axis=0)


def _cumsum(lf, *, rows, tb, want_bias):
    B, L, H = lf.shape
    spec = pl.BlockSpec((None, tb, H), lambda b, i: (b, i, 0))
    wide = pl.BlockSpec((None, tb, LANES), lambda b, i: (b, i, 0))
    out_specs = [spec]
    out_shape = [jax.ShapeDtypeStruct((B, L, H), F32)]
    if want_bias:
        out_specs += [wide, wide, pl.BlockSpec((None, None, 8, H), lambda b, i: (b, i, 0, 0))]
        out_shape += [jax.ShapeDtypeStruct((B, L, LANES), BF16)] * 2
        out_shape += [jax.ShapeDtypeStruct((B, L // tb, 8, H), F32)]
    return pl.pallas_call(
        functools.partial(_cumsum_kernel, rows=rows, n_chunks=tb // rows, want_bias=want_bias),
        grid=(B, L // tb),
        in_specs=[spec],
        out_specs=out_specs,
        out_shape=out_shape,
        scratch_shapes=[pltpu.VMEM((1, FOX_HEADS), F32)],
        compiler_params=_params(2),
        name="fox_cumsum_bias" if want_bias else "fox_cumsum",
    )(lf)


def _skip_plan_kernel(kn_ref, cmin_ref, qn_ref, cmax_ref, knq_ref, first_ref, *, nb):
    j = lax.broadcasted_iota(jnp.int32, (nb, nb), 0).astype(F32)
    q = lax.broadcasted_iota(jnp.int32, (nb, nb), 1).astype(F32)
    rows = []
    for p in range(FOX_PAIRS):
        skippable = None
        for h in (2 * p, 2 * p + 1):
            qn = qn_ref[h:h + 1, :]
            bound = (kn_ref[:, h:h + 1] * qn + (cmax_ref[h:h + 1, :] - cmin_ref[:, h:h + 1])
                     + qn * knq_ref[h:h + 1, :])
            ok = bound < -SKIP_LOG2
            skippable = ok if skippable is None else (skippable & ok)
        must = (j < q) & jnp.logical_not(skippable)
        rows.append(jnp.min(jnp.where(must, j, q), axis=0, keepdims=True))
    first_ref[...] = jnp.concatenate(rows, axis=0).astype(jnp.int32)


def _skip_plan(kn, cmin, qn, cmax):
    B, nb, H = kn.shape
    by_block = pl.BlockSpec((None, nb, H), lambda b: (b, 0, 0))
    by_head = pl.BlockSpec((None, H, nb), lambda b: (b, 0, 0))
    t = lambda a: jnp.swapaxes(a, 1, 2)
    return pl.pallas_call(
        functools.partial(_skip_plan_kernel, nb=nb),
        grid=(B,),
        in_specs=[by_block, by_block, by_head, by_head, by_head],
        out_specs=pl.BlockSpec((None, FOX_PAIRS, nb), lambda b: (b, 0, 0)),
        out_shape=jax.ShapeDtypeStruct((B, FOX_PAIRS, nb), jnp.int32),
        compiler_params=_params(1),
        name="fox_skip_plan",
    )(kn, cmin, t(qn), t(cmax), t(kn))


def _stack_heads(q2):
    lane = lax.broadcasted_iota(jnp.int32, q2.shape, 1)
    zero = jnp.zeros_like(q2)
    return jnp.concatenate(
        [jnp.where(lane < FOX_HEAD_DIM, q2, zero), jnp.where(lane >= FOX_HEAD_DIM, q2, zero)], axis=0)


ATTN_TILE = 256
SUM_ROWS = 16


def _attn_prompt_kernel(n_ref, vq_ref, vj_ref, q_ref, f_ref, k_ref, e_ref, vt_ref, o_ref,
                        m_ref, acc_ref, sa_ref, xa_ref, sb_ref, xb_ref, *, blk, nb):
    b_id = pl.program_id(0)
    p_id = pl.program_id(1)
    hd = FOX_HEAD_DIM
    tw = ATTN_TILE
    per_head = blk // tw
    n_tiles = 2 * per_head

    lane = lax.broadcasted_iota(jnp.int32, (blk, LANES), 1)
    zero = jnp.zeros((blk, LANES), BF16)
    slot_a = 2 * BIAS_SLOT * p_id
    slot_b = slot_a + BIAS_SLOT
    ones_rows = jnp.ones((SUM_ROWS, blk), BF16)

    m_ref[...] = jnp.full(m_ref.shape, NEG_INF, F32)
    acc_ref[...] = jnp.zeros(acc_ref.shape, F32)

    def query_tiles(qi):
        r0 = pl.multiple_of(qi * blk, blk)
        q2 = q_ref[pl.ds(r0, blk), :]
        f2 = f_ref[pl.ds(r0, blk), :]
        heads = [
            jnp.concatenate([jnp.where(lane < hd, q2, zero),
                             jnp.where((lane >= slot_a) & (lane < slot_b), f2, zero)], axis=1),
            jnp.concatenate([jnp.where(lane >= hd, q2, zero),
                             jnp.where((lane >= slot_b) & (lane < slot_b + BIAS_SLOT), f2, zero)], axis=1),
        ]
        return [heads[t // per_head][(t % per_head) * tw:(t % per_head + 1) * tw] for t in range(n_tiles)]

    def scores(qi, j, buf, diagonal):
        s_ref, x_ref = buf
        q_tiles = query_tiles(qi)
        k0 = pl.multiple_of(j * blk, blk)
        kaug = jnp.concatenate([k_ref[pl.ds(k0, blk), :], e_ref[pl.ds(k0, blk), :]], axis=1)
        for t in range(n_tiles):
            if diagonal:
                c0 = (t % per_head) * tw
                live = c0 + tw
                s = _dot_nt(kaug[:live], q_tiles[t])
                r = lax.broadcasted_iota(jnp.int32, (live, tw), 0)
                c = lax.broadcasted_iota(jnp.int32, (live, tw), 1) + c0
                s = jnp.where(r <= c, s, NEG_INF)
                x_ref[t] = jnp.max(s, axis=0, keepdims=True)
                if live < blk:
                    s = jnp.concatenate([s, jnp.full((blk - live, tw), NEG_INF, F32)], axis=0)
                s_ref[t] = s
            else:
                s = _dot_nt(kaug, q_tiles[t])
                s_ref[t] = s
                x_ref[t] = jnp.max(s, axis=0, keepdims=True)

    def absorb(qi, j, buf):
        s_ref, x_ref = buf
        vt = vt_ref[j]
        v_heads = [jnp.concatenate([vt[h * hd:(h + 1) * hd, :], ones_rows], axis=0) for h in range(2)]
        for t in range(n_tiles):
            idx = qi * n_tiles + t
            m_old = m_ref[idx]
            m_new = jnp.maximum(m_old, x_ref[t])
            alpha = jnp.exp2(m_old - m_new)
            p = jnp.exp2(s_ref[t] - m_new)
            m_ref[idx] = m_new
            pv = _dot(v_heads[t // per_head], p.astype(BF16))
            acc_ref[idx] = acc_ref[idx] * alpha + pv

    buf_a = (sa_ref, xa_ref)
    buf_b = (sb_ref, xb_ref)

    scores(0, 0, buf_a, True)

    def diag_pair(n, carry):
        i0 = 2 * n
        i2 = jnp.minimum(i0 + 2, nb - 1)
        scores(i0 + 1, i0 + 1, buf_b, True)
        absorb(i0, i0, buf_a)
        scores(i2, i2, buf_a, True)
        absorb(i0 + 1, i0 + 1, buf_b)
        return carry

    lax.fori_loop(0, nb // 2, diag_pair, 0)

    n_off = n_ref[b_id, p_id]
    visit = lambda v: (vq_ref[b_id, p_id, v], vj_ref[b_id, p_id, v])

    @pl.when(n_off > 0)
    def _():
        scores(*visit(0), buf_a, False)

    def off_pair(n, carry):
        v0 = 2 * n
        scores(*visit(v0 + 1), buf_b, False)
        absorb(*visit(v0), buf_a)
        scores(*visit(v0 + 2), buf_a, False)
        absorb(*visit(v0 + 1), buf_b)
        return carry

    lax.fori_loop(0, jnp.maximum(n_off - 1, 0) // 2, off_pair, 0)

    @pl.when((n_off > 0) & (n_off % 2 == 0))
    def _():
        scores(*visit(n_off - 1), buf_b, False)
        absorb(*visit(n_off - 2), buf_a)
        absorb(*visit(n_off - 1), buf_b)

    @pl.when(n_off % 2 == 1)
    def _():
        absorb(*visit(n_off - 1), buf_a)

    def finish(qi, carry):
        halves = [jnp.concatenate([acc_ref[qi * n_tiles + t, 0:hd, :] / acc_ref[qi * n_tiles + t, hd:hd + 1, :]
                                   for t in range(h * per_head, (h + 1) * per_head)], axis=1)
                  for h in range(2)]
        o_ref[pl.ds(pl.multiple_of(qi * blk, blk), blk), :] = jnp.concatenate(halves, axis=0).T
        return carry

    lax.fori_loop(0, nb, finish, 0)


def _visit_lists(first):
    B, P, nb = first.shape
    vmax = nb * (nb - 1) // 2 + 8
    q_idx = jnp.arange(nb, dtype=jnp.int32)
    cnt = q_idx - first
    incl = jnp.cumsum(cnt, axis=-1)
    excl = incl - cnt
    v = jnp.arange(vmax, dtype=jnp.int32)
    vq = jnp.minimum(jnp.sum(v[None, None, :, None] >= incl[:, :, None, :], axis=-1), nb - 1).astype(jnp.int32)
    take = lambda a: jnp.take_along_axis(a, vq, axis=-1)
    vj = jnp.clip(take(first) + (v - take(excl)), 0, nb - 1).astype(jnp.int32)
    return incl[..., -1].astype(jnp.int32), vq, vj


def _attn_prompt(first, q, f, kb, e, vt, *, blk):
    B, L, _ = q.shape
    nb = L // blk
    n_tiles = 2 * blk // ATTN_TILE
    n_off, vq, vj = _visit_lists(first)
    by_pair = pl.BlockSpec((None, L, LANES), lambda b, p, *_: (b, 0, p))
    shared = pl.BlockSpec((None, L, LANES), lambda b, p, *_: (b, 0, 0))
    return pl.pallas_call(
        functools.partial(_attn_prompt_kernel, blk=blk, nb=nb),
        grid_spec=pltpu.PrefetchScalarGridSpec(
            num_scalar_prefetch=3,
            grid=(B, FOX_PAIRS),
            in_specs=[by_pair, shared, by_pair, shared,
                      pl.BlockSpec((None, nb, LANES, blk), lambda b, p, *_: (b, 0, p, 0))],
            out_specs=by_pair,
            scratch_shapes=[
                pltpu.VMEM((nb * n_tiles, 1, ATTN_TILE), F32),
                pltpu.VMEM((nb * n_tiles, FOX_HEAD_DIM + SUM_ROWS, ATTN_TILE), F32),
                pltpu.VMEM((n_tiles, blk, ATTN_TILE), F32),
                pltpu.VMEM((n_tiles, 1, ATTN_TILE), F32),
                pltpu.VMEM((n_tiles, blk, ATTN_TILE), F32),
                pltpu.VMEM((n_tiles, 1, ATTN_TILE), F32),
            ],
        ),
        out_shape=jax.ShapeDtypeStruct((B, L, FOX_WIDTH), F32),
        compiler_params=_params(2),
        name="fox_attn_prompt",
    )(n_off, vq, vj, q, f, kb, e, vt)


def _attn_sample_kernel(q_ref, kn_ref, vn_ref, ck_ref, cv_ref, c_ref, ct_ref, o_ref, *, past, ls):
    hd = FOX_HEAD_DIM
    pad_k = jnp.zeros((LANES - ls, LANES), BF16)
    cqb = c_ref[past:past + ls, :]
    lane = lax.broadcasted_iota(jnp.int32, cqb.shape, 1)
    r = lax.broadcasted_iota(jnp.int32, (2 * ls, LANES), 0) & (ls - 1)
    c = lax.broadcasted_iota(jnp.int32, (2 * ls, LANES), 1)
    lane_o = lax.broadcasted_iota(jnp.int32, (ls, LANES), 1)

    for p in range(FOX_PAIRS):
        cols = slice(p * LANES, (p + 1) * LANES)
        qst = _stack_heads(q_ref[:, cols])
        kc = ck_ref[:, cols].astype(BF16)
        vc = cv_ref[:, cols].astype(BF16)
        kn = jnp.concatenate([kn_ref[:, cols], pad_k], axis=0)
        vn = jnp.concatenate([vn_ref[:, cols].astype(BF16), pad_k], axis=0)
        cq = jnp.concatenate(
            [jnp.sum(jnp.where(lane == 2 * p + h2, cqb, 0.0), axis=1, keepdims=True) for h2 in range(2)],
            axis=0)
        cka = ct_ref[2 * p:2 * p + 1, :]
        ckb = ct_ref[2 * p + 1:2 * p + 2, :]

        def bias(lo, hi):
            return jnp.concatenate([cq[:ls] - cka[:, lo:hi], cq[ls:] - ckb[:, lo:hi]], axis=0)

        s_c = _dot_nt(qst, kc) + bias(0, past)
        s_n = _dot_nt(qst, kn) + bias(past, past + LANES)
        s_n = jnp.where(c <= r, s_n, NEG_INF)
        m = jnp.maximum(jnp.max(s_c, axis=1, keepdims=True), jnp.max(s_n, axis=1, keepdims=True))
        p_c = jnp.exp2(s_c - m)
        p_n = jnp.exp2(s_n - m)
        l = jnp.sum(p_c, axis=1, keepdims=True) + jnp.sum(p_n, axis=1, keepdims=True)
        o = (_dot(p_c.astype(BF16), vc) + _dot(p_n.astype(BF16), vn)) / l
        o_ref[:, cols] = jnp.where(lane_o < hd, o[:ls], o[ls:])


def _attn_sample(q, kb, v, cache_k, cache_v, c, ct, *, past):
    B, ls, _ = q.shape
    tot = c.shape[1]
    rows = lambda n: pl.BlockSpec((None, n, FOX_WIDTH), lambda b: (b, 0, 0))
    return pl.pallas_call(
        functools.partial(_attn_sample_kernel, past=past, ls=ls),
        grid=(B,),
        in_specs=[
            rows(ls), rows(ls), rows(ls), rows(past), rows(past),
            pl.BlockSpec((None, tot, FOX_HEADS), lambda b: (b, 0, 0)),
            pl.BlockSpec((None, FOX_HEADS, tot), lambda b: (b, 0, 0)),
        ],
        out_specs=rows(ls),
        out_shape=jax.ShapeDtypeStruct((B, ls, FOX_WIDTH), F32),
        compiler_params=_params(1),
        name="fox_attn_sample",
    )(q, kb, v, cache_k, cache_v, c, ct)


def _mid_kernel(o_ref, sg_ref, x_ref, wo_ref, g_ref, w_ref, wa1_ref, wa2_ref, ba_ref,
                y_ref, q_ref, k_ref, v_ref, sg2_ref, ga_ref):
    u = (o_ref[...] * sg_ref[...]).astype(BF16)
    y = x_ref[...] + _dot(u, wo_ref[...])
    y_ref[...] = y
    hb = _rms(y, g_ref[...]).astype(BF16)
    kw = GLA_KEY_WIDTH
    q_ref[...] = _dot(hb, w_ref[:, 0:kw]) * (GLA_HEAD_K ** -0.5)
    k_ref[...] = _dot(hb, w_ref[:, kw:2 * kw])
    v_ref[...] = _dot(hb, w_ref[:, 2 * kw:2 * kw + GLA_VAL_WIDTH]).astype(BF16)
    gate = _dot(hb, w_ref[:, 2 * kw + GLA_VAL_WIDTH:2 * kw + 2 * GLA_VAL_WIDTH])
    sg2_ref[...] = gate * jax.nn.sigmoid(gate)
    a1 = _dot(hb, wa1_ref[...]).astype(BF16)
    a = _dot(a1, wa2_ref[...]) + ba_ref[...]
    ga_ref[...] = _log_sigmoid(a) * (1.0 / GLA_GATE_TEMP)


def _mid(o, sg, x, w_out, g, w_main, w_a1, w_a2, b_a, *, tm):
    T = o.shape[0]
    tok = lambda w: pl.BlockSpec((tm, w), lambda i: (i, 0))
    full = lambda a: pl.BlockSpec(a.shape, lambda i: (0,) * a.ndim)
    return pl.pallas_call(
        _mid_kernel,
        grid=(T // tm,),
        in_specs=[tok(D_MODEL), tok(D_MODEL), tok(D_MODEL), full(w_out), full(g), full(w_main),
                  full(w_a1), full(w_a2), full(b_a)],
        out_specs=[tok(D_MODEL), tok(GLA_KEY_WIDTH), tok(GLA_KEY_WIDTH), tok(GLA_VAL_WIDTH),
                   tok(GLA_VAL_WIDTH), tok(GLA_KEY_WIDTH)],
        out_shape=[
            jax.ShapeDtypeStruct((T, D_MODEL), F32),
            jax.ShapeDtypeStruct((T, GLA_KEY_WIDTH), F32),
            jax.ShapeDtypeStruct((T, GLA_KEY_WIDTH), F32),
            jax.ShapeDtypeStruct((T, GLA_VAL_WIDTH), BF16),
            jax.ShapeDtypeStruct((T, GLA_VAL_WIDTH), F32),
            jax.ShapeDtypeStruct((T, GLA_KEY_WIDTH), F32),
        ],
        compiler_params=_params(1),
        name="fox_out_gla_proj",
    )(o, sg, x, w_out, g, w_main, w_a1, w_a2, b_a)


def _gla_kernel(q_ref, k_ref, v_ref, ga_ref, s0_ref, o_ref, s_ref, *, n_chunks):
    i = pl.program_id(1)
    ch = GLA_CHUNK
    dk, dv = GLA_HEAD_K, GLA_HEAD_V

    @pl.when(i == 0)
    def _():
        s_ref[...] = s0_ref[...]

    t = n_chunks * ch
    r = lax.broadcasted_iota(jnp.int32, (t, t), 0)
    c = lax.broadcasted_iota(jnp.int32, (t, t), 1)
    causal = ((r >> 6) == (c >> 6)) & (r >= c)
    tril = jnp.where(causal, 1.0, 0.0).astype(BF16)
    eye = (lax.broadcasted_iota(jnp.int32, (dk, dk), 0) == lax.broadcasted_iota(jnp.int32, (dk, dk), 1))

    b = _tril_cumsum(tril, ga_ref[...])
    lasts = [b[(ci + 1) * ch - 1:(ci + 1) * ch, :] for ci in range(n_chunks)]
    b_last = jnp.concatenate([jnp.broadcast_to(x, (ch, GLA_KEY_WIDTH)) for x in lasts], axis=0)
    kk = k_ref[...]
    qe = (q_ref[...] * jnp.exp(b)).astype(BF16)
    ke = (kk * jnp.exp(-b)).astype(BF16)
    kd = (kk * jnp.exp(b_last - b)).astype(BF16)
    decs = [jnp.exp(x) for x in lasts]

    for h in range(GLA_HEADS):
        ks = slice(h * dk, (h + 1) * dk)
        vs = slice(h * dv, (h + 1) * dv)
        vh = v_ref[:, vs]
        a = jnp.where(causal, _dot_nt(qe[:, ks], ke[:, ks]), 0.0)
        o_intra = _dot(a.astype(BF16), vh)
        s = s_ref[h]
        for ci in range(n_chunks):
            rows = slice(ci * ch, (ci + 1) * ch)
            o_ref[rows, vs] = o_intra[rows] + _dot(qe[rows, ks], s.astype(BF16))
            dec_col = jnp.sum(jnp.where(eye, jnp.broadcast_to(decs[ci][:, ks], (dk, dk)), 0.0),
                              axis=1, keepdims=True)
            s = dec_col * s + _dot_tn(kd[rows, ks], vh[rows])
        s_ref[h] = s


def _gla(q, k, v, ga, s0, *, tc):
    B, L, _ = q.shape
    tok = lambda w: pl.BlockSpec((None, tc, w), lambda b, i: (b, i, 0))
    st = pl.BlockSpec((None, GLA_HEADS, GLA_HEAD_K, GLA_HEAD_V), lambda b, i: (b, 0, 0, 0))
    return pl.pallas_call(
        functools.partial(_gla_kernel, n_chunks=tc // GLA_CHUNK),
        grid=(B, L // tc),
        in_specs=[tok(GLA_KEY_WIDTH), tok(GLA_KEY_WIDTH), tok(GLA_VAL_WIDTH), tok(GLA_KEY_WIDTH), st],
        out_specs=[tok(GLA_VAL_WIDTH), st],
        out_shape=[
            jax.ShapeDtypeStruct((B, L, GLA_VAL_WIDTH), F32),
            jax.ShapeDtypeStruct((B, GLA_HEADS, GLA_HEAD_K, GLA_HEAD_V), F32),
        ],
        compiler_params=_params(2),
        name="gla_chunk",
    )(q, k, v, ga, s0)


def _gla_out_kernel(o_ref, sg_ref, y_ref, go_ref, wo_ref, gf_ref, out_ref, *, chunk_major):
    dv = GLA_HEAD_V
    if chunk_major:
        o = jnp.concatenate([o_ref[:, t, :] for t in range(o_ref.shape[1])], axis=0)
    else:
        o = o_ref[...]
    parts = []
    for h in range(GLA_HEADS):
        vs = slice(h * dv, (h + 1) * dv)
        parts.append((_rms(o[:, vs], go_ref[...]) * sg_ref[:, vs]).astype(BF16))
    u = jnp.concatenate(parts, axis=1)
    y = y_ref[...] + _dot(u, wo_ref[...])
    out_ref[...] = _rms(y, gf_ref[...])


def _gla_out(o, sg, y, g_o, w_out, g_f, *, tm):
    B, L, _ = o.shape
    n = L // GLA_CHUNK
    chunk_major = n > 1
    if chunk_major:
        tm = 8 * n
        o = o.reshape(B, n, GLA_CHUNK, D_MODEL)
        o_spec = pl.BlockSpec((None, n, 8, D_MODEL), lambda b, i: (b, 0, i, 0))
    else:
        flat = lambda a: a.reshape(1, B * L, D_MODEL)
        o, sg, y = flat(o), flat(sg), flat(y)
        o_spec = pl.BlockSpec((None, tm, D_MODEL), lambda b, i: (b, i, 0))
    nb, rows, _ = sg.shape
    tok = pl.BlockSpec((None, tm, D_MODEL), lambda b, i: (b, i, 0))
    full = lambda a: pl.BlockSpec(a.shape, lambda b, i: (0,) * a.ndim)
    return pl.pallas_call(
        functools.partial(_gla_out_kernel, chunk_major=chunk_major),
        grid=(nb, rows // tm),
        in_specs=[o_spec, tok, tok, full(g_o), full(w_out), full(g_f)],
        out_specs=tok,
        out_shape=jax.ShapeDtypeStruct((nb, rows, D_MODEL), F32),
        compiler_params=_params(2),
        name="gla_out",
    )(o, sg, y, g_o, w_out, g_f)


def _pad_cols(w, n):
    return jnp.pad(w, ((0, 0), (0, n - w.shape[1])))


def kernel(x_prompt, x_sample, cache_fox_k, cache_fox_v, cache_fox_logf, state_gla,
           g_norm_fox, w_in_fox, b_fox_f, w_out_fox,
           g_norm_gla, w_in_gla, w_gla_a2, b_gla_a, g_gla_o, w_out_gla, g_final):
    B, L, D = x_prompt.shape
    Bs, Ls, _ = x_sample.shape
    P = cache_fox_k.shape[1]
    H, hd = FOX_HEADS, FOX_HEAD_DIM

    row = lambda a: a.reshape(1, -1).astype(F32)
    w_fox = w_in_fox[:, :4 * FOX_WIDTH].astype(BF16)
    w_fox_f = _pad_cols(w_in_fox[:, 4 * FOX_WIDTH:], LANES).astype(BF16)
    b_f = _pad_cols(row(b_fox_f), LANES)
    w_o_fox = w_out_fox.astype(BF16)
    n_main = 2 * GLA_KEY_WIDTH + 2 * GLA_VAL_WIDTH
    w_gla = w_in_gla[:, :n_main].astype(BF16)
    w_a1 = _pad_cols(w_in_gla[:, n_main:], LANES).astype(BF16)
    w_a2 = jnp.pad(w_gla_a2, ((0, LANES - GLA_GATE_RANK), (0, 0))).astype(BF16)
    w_o_gla = w_out_gla.astype(BF16)

    q_p, k_p, v_p, kb_p, sg_p, lf_p, vt_p, nrm_p = _fox_proj(
        x_prompt, row(g_norm_fox), w_fox, w_fox_f, b_f, tm=ATTN_BLOCK, want_vt=True)
    _, e_p, f_p, rng_p = _cumsum(lf_p, rows=ATTN_BLOCK, tb=ATTN_BLOCK, want_bias=True)
    first_p = _skip_plan(nrm_p[:, :, 0, :H], rng_p[:, :, 0, :], nrm_p[:, :, 1, :H], rng_p[:, :, 1, :])
    o_p = _attn_prompt(first_p, q_p, f_p, kb_p, e_p, vt_p, blk=ATTN_BLOCK)

    xs = x_sample.reshape(1, Bs * Ls, D)
    q_s, k_s, v_s, kb_s, sg_s, lf_s = _fox_proj(
        xs, row(g_norm_fox), w_fox, w_fox_f, b_f, tm=PROJ_TM, want_vt=False)
    sh = lambda a: a.reshape(Bs, Ls, a.shape[-1])
    q_s, k_s, v_s, kb_s, lf_s = sh(q_s), sh(k_s), sh(v_s), sh(kb_s), sh(lf_s)
    lf_all = jnp.concatenate(
        [cache_fox_logf.astype(F32), lf_s, jnp.zeros((Bs, LANES - Ls, H), F32)], axis=1)
    (c_s,) = _cumsum(lf_all, rows=(P + LANES) // 3, tb=P + LANES, want_bias=False)
    o_s = _attn_sample(q_s, kb_s, v_s, cache_fox_k.reshape(Bs, P, FOX_WIDTH),
                       cache_fox_v.reshape(Bs, P, FOX_WIDTH), c_s, jnp.swapaxes(c_s, 1, 2), past=P)

    def gla_layer(o, sg, x, s0, tc):
        b, l, _ = x.shape
        flat = lambda a: a.reshape(b * l, a.shape[-1])
        y1, q, k, v, sg2, ga = _mid(flat(o), flat(sg), flat(x), w_o_fox, row(g_norm_gla), w_gla,
                                    w_a1, w_a2, row(b_gla_a), tm=PROJ_TM)
        un = lambda a: a.reshape(b, l, a.shape[-1])
        og, s_out = _gla(un(q), un(k), un(v), un(ga), s0, tc=tc)
        y = _gla_out(og, un(sg2), un(y1), row(g_gla_o), w_o_gla, row(g_final), tm=PROJ_TM)
        return y.reshape(b, l, D), s_out

    s0_p = jnp.zeros((B, GLA_HEADS, GLA_HEAD_K, GLA_HEAD_V), F32)
    y_p, s_p = gla_layer(o_p, sg_p, x_prompt, s0_p, GLA_TC)
    y_s, s_s = gla_layer(o_s, sg_s.reshape(Bs, Ls, D), x_sample, state_gla.astype(F32), Ls)

    return (y_p, y_s,
            k_p.reshape(B, L, H, hd), v_p.reshape(B, L, H, hd), lf_p, s_p.astype(state_gla.dtype),
            k_s.reshape(Bs, Ls, H, hd), v_s.reshape(Bs, Ls, H, hd), lf_s, s_s.astype(state_gla.dtype))
```

```python
import functools

import jax
import jax.numpy as jnp
from jax import lax
from jax.experimental import pallas as pl
from jax.experimental.pallas import tpu as pltpu

F32 = jnp.float32
BF16 = jnp.bfloat16

D_MODEL = 1024
EPS = 1e-6
NEG_INF = -1e30
LOG2E = 1.4426950408889634
NORM_SLACK = 1.01
SKIP_LOG2 = 160.0

FOX_HEADS = 16
FOX_HEAD_DIM = 64
FOX_WIDTH = FOX_HEADS * FOX_HEAD_DIM
FOX_PAIRS = FOX_HEADS // 2

GLA_HEADS = 4
GLA_KEY_WIDTH = 512
GLA_VAL_WIDTH = 1024
GLA_HEAD_K = 128
GLA_HEAD_V = 256
GLA_GATE_RANK = 16
GLA_GATE_TEMP = 16.0
GLA_CHUNK = 64

LANES = 128
VMEM_LIMIT = 56 * 1024 * 1024

PROJ_TM = 512
ATTN_BLOCK = 512
GLA_TC = 256


def _params(n_axes):
    return pltpu.CompilerParams(
        dimension_semantics=("arbitrary",) * n_axes,
        vmem_limit_bytes=VMEM_LIMIT,
    )


def _dot(a, b):
    return jnp.dot(a, b, preferred_element_type=F32)


def _dot_nt(a, b):
    return lax.dot_general(a, b, (((1,), (1,)), ((), ())), preferred_element_type=F32)


def _dot_tn(a, b):
    return lax.dot_general(a, b, (((0,), (0,)), ((), ())), preferred_element_type=F32)


def _log_sigmoid(z):
    return jnp.minimum(z, 0.0) - jnp.log1p(jnp.exp(-jnp.abs(z)))


def _rms(x, g):
    ms = jnp.mean(x * x, axis=-1, keepdims=True)
    return (x * lax.rsqrt(ms + EPS)) * g


def _split3(x):
    hi = x.astype(BF16)
    r1 = x - hi.astype(F32)
    mid = r1.astype(BF16)
    lo = (r1 - mid.astype(F32)).astype(BF16)
    return hi, mid, lo


def _tril_cumsum(tril_bf16, x):
    hi, mid, lo = _split3(x)
    return (_dot(tril_bf16, hi) + _dot(tril_bf16, mid)) + _dot(tril_bf16, lo)


def _tril(n):
    r = lax.broadcasted_iota(jnp.int32, (n, n), 0)
    c = lax.broadcasted_iota(jnp.int32, (n, n), 1)
    return jnp.where(r >= c, 1.0, 0.0).astype(BF16)


def _fox_proj_kernel(x_ref, g_ref, w_ref, wf_ref, bf_ref, *rest, want_vt):
    if want_vt:
        q_ref, k_ref, v_ref, kb_ref, sg_ref, lf_ref, vt_ref, nrm_ref = rest
    else:
        q_ref, k_ref, v_ref, kb_ref, sg_ref, lf_ref = rest
    hb = _rms(x_ref[...], g_ref[...]).astype(BF16)
    q = _dot(hb, w_ref[:, 0:FOX_WIDTH])
    qb = (q * (LOG2E * FOX_HEAD_DIM ** -0.5)).astype(BF16)
    q_ref[...] = qb
    k = _dot(hb, w_ref[:, FOX_WIDTH:2 * FOX_WIDTH])
    k_ref[...] = k
    kb = k.astype(BF16)
    kb_ref[...] = kb
    if want_vt:
        row = lax.broadcasted_iota(jnp.int32, (FOX_WIDTH, LANES), 0)
        col = lax.broadcasted_iota(jnp.int32, (FOX_WIDTH, LANES), 1)
        head_of = jnp.where((row >> 6) == col, 1.0, 0.0).astype(BF16)

        def bound(xb):
            xf = xb.astype(F32)
            sq = _dot((xf * xf).astype(BF16), head_of)
            return jnp.sqrt(jnp.max(sq, axis=0, keepdims=True)) * NORM_SLACK

        nrm_ref[...] = jnp.concatenate(
            [bound(kb), bound(qb), jnp.zeros((6, LANES), F32)], axis=0)
    v = _dot(hb, w_ref[:, 2 * FOX_WIDTH:3 * FOX_WIDTH])
    v_ref[...] = v
    if want_vt:
        vt_ref[...] = v.astype(BF16).T
    gate = _dot(hb, w_ref[:, 3 * FOX_WIDTH:4 * FOX_WIDTH])
    sg_ref[...] = gate * jax.nn.sigmoid(gate)
    fl = _dot(hb, wf_ref[...]) + bf_ref[...]
    lf_ref[...] = _log_sigmoid(fl)[:, :FOX_HEADS]


def _fox_proj(x, g, w_main, w_f, b_f, *, tm, want_vt):
    B, L, _ = x.shape
    n = L // tm
    tok = lambda w: pl.BlockSpec((None, tm, w), lambda b, i: (b, i, 0))
    full = lambda a: pl.BlockSpec(a.shape, lambda b, i: (0,) * a.ndim)
    in_specs = [tok(D_MODEL), full(g), full(w_main), full(w_f), full(b_f)]
    args = [x, g, w_main, w_f, b_f]
    out_specs = [tok(FOX_WIDTH)] * 5 + [tok(FOX_HEADS)]
    out_shape = [
        jax.ShapeDtypeStruct((B, L, FOX_WIDTH), BF16),
        jax.ShapeDtypeStruct((B, L, FOX_WIDTH), F32),
        jax.ShapeDtypeStruct((B, L, FOX_WIDTH), F32),
        jax.ShapeDtypeStruct((B, L, FOX_WIDTH), BF16),
        jax.ShapeDtypeStruct((B, L, FOX_WIDTH), F32),
        jax.ShapeDtypeStruct((B, L, FOX_HEADS), F32),
    ]
    if want_vt:
        out_specs.append(pl.BlockSpec((None, None, FOX_WIDTH, tm), lambda b, i: (b, i, 0, 0)))
        out_shape.append(jax.ShapeDtypeStruct((B, n, FOX_WIDTH, tm), BF16))
        out_specs.append(pl.BlockSpec((None, None, 8, LANES), lambda b, i: (b, i, 0, 0)))
        out_shape.append(jax.ShapeDtypeStruct((B, n, 8, LANES), F32))
    return pl.pallas_call(
        functools.partial(_fox_proj_kernel, want_vt=want_vt),
        grid=(B, n),
        in_specs=in_specs,
        out_specs=out_specs,
        out_shape=out_shape,
        compiler_params=_params(2),
        name="fox_proj_vt" if want_vt else "fox_proj",
    )(*args)


BIAS_SLOT = 8


def _bias_selectors():
    shape = (FOX_HEADS, LANES)
    head = lax.broadcasted_iota(jnp.int32, shape, 0)
    col = lax.broadcasted_iota(jnp.int32, shape, 1)
    slot = BIAS_SLOT * head
    sel_f = [jnp.where(col == slot + x, 1.0, 0.0).astype(BF16) for x in range(3)]
    sel_e = [jnp.where(col == slot + 3 + x, -1.0, 0.0).astype(BF16) for x in range(3)]
    lane = lax.broadcasted_iota(jnp.int32, (1, LANES), 1) & (BIAS_SLOT - 1)
    one_f = jnp.where((lane >= 3) & (lane < 6), 1.0, 0.0)
    one_e = jnp.where(lane < 3, 1.0, 0.0)
    return sel_f, sel_e, one_f, one_e


def _cumsum_kernel(lf_ref, c_ref, *rest, rows, n_chunks, want_bias):
    tril = _tril(rows)
    if want_bias:
        e_ref, f_ref, rng_ref, carry_ref = rest
        sel_f, sel_e, one_f, one_e = _bias_selectors()
    else:
        (carry_ref,) = rest

    @pl.when(pl.program_id(1) == 0)
    def _():
        carry_ref[...] = jnp.zeros_like(carry_ref)

    def body(i, carry):
        r0 = pl.multiple_of(i * rows, rows)
        cs = _tril_cumsum(tril, lf_ref[pl.ds(r0, rows), :]) + carry
        c2 = cs * LOG2E
        c_ref[pl.ds(r0, rows), :] = c2
        if want_bias:
            hi, mid, lo = _split3(c2)
            f_ref[pl.ds(r0, rows), :] = (
                (_dot(hi, sel_f[0]) + _dot(mid, sel_f[1])) + _dot(lo, sel_f[2]) + one_f).astype(BF16)
            e_ref[pl.ds(r0, rows), :] = (
                (_dot(hi, sel_e[0]) + _dot(mid, sel_e[1])) + _dot(lo, sel_e[2]) + one_e).astype(BF16)
        return cs[rows - 1:rows, :]

    carry_ref[...] = lax.fori_loop(0, n_chunks, body, carry_ref[...])
    if want_bias:
        c_all = c_ref[...]
        rng_ref[...] = jnp.concatenate(
            [jnp.min(c_all, axis=0, keepdims=True), jnp.max(c_all, axis=0, keepdims=True),
             jnp.zeros((6, FOX_HEADS), F32)], axis=0)


def _cumsum(lf, *, rows, tb, want_bias):
    B, L, H = lf.shape
    spec = pl.BlockSpec((None, tb, H), lambda b, i: (b, i, 0))
    wide = pl.BlockSpec((None, tb, LANES), lambda b, i: (b, i, 0))
    out_specs = [spec]
    out_shape = [jax.ShapeDtypeStruct((B, L, H), F32)]
    if want_bias:
        out_specs += [wide, wide, pl.BlockSpec((None, None, 8, H), lambda b, i: (b, i, 0, 0))]
        out_shape += [jax.ShapeDtypeStruct((B, L, LANES), BF16)] * 2
        out_shape += [jax.ShapeDtypeStruct((B, L // tb, 8, H), F32)]
    return pl.pallas_call(
        functools.partial(_cumsum_kernel, rows=rows, n_chunks=tb // rows, want_bias=want_bias),
        grid=(B, L // tb),
        in_specs=[spec],
        out_specs=out_specs,
        out_shape=out_shape,
        scratch_shapes=[pltpu.VMEM((1, FOX_HEADS), F32)],
        compiler_params=_params(2),
        name="fox_cumsum_bias" if want_bias else "fox_cumsum",
    )(lf)


def _skip_plan_kernel(kn_ref, cmin_ref, qn_ref, cmax_ref, knq_ref, first_ref, *, nb):
    j = lax.broadcasted_iota(jnp.int32, (nb, nb), 0).astype(F32)
    q = lax.broadcasted_iota(jnp.int32, (nb, nb), 1).astype(F32)
    rows = []
    for p in range(FOX_PAIRS):
        skippable = None
        for h in (2 * p, 2 * p + 1):
            qn = qn_ref[h:h + 1, :]
            bound = (kn_ref[:, h:h + 1] * qn + (cmax_ref[h:h + 1, :] - cmin_ref[:, h:h + 1])
                     + qn * knq_ref[h:h + 1, :])
            ok = bound < -SKIP_LOG2
            skippable = ok if skippable is None else (skippable & ok)
        must = (j < q) & jnp.logical_not(skippable)
        rows.append(jnp.min(jnp.where(must, j, q), axis=0, keepdims=True))
    first_ref[...] = jnp.concatenate(rows, axis=0).astype(jnp.int32)


def _skip_plan(kn, cmin, qn, cmax):
    B, nb, H = kn.shape
    by_block = pl.BlockSpec((None, nb, H), lambda b: (b, 0, 0))
    by_head = pl.BlockSpec((None, H, nb), lambda b: (b, 0, 0))
    t = lambda a: jnp.swapaxes(a, 1, 2)
    return pl.pallas_call(
        functools.partial(_skip_plan_kernel, nb=nb),
        grid=(B,),
        in_specs=[by_block, by_block, by_head, by_head, by_head],
        out_specs=pl.BlockSpec((None, FOX_PAIRS, nb), lambda b: (b, 0, 0)),
        out_shape=jax.ShapeDtypeStruct((B, FOX_PAIRS, nb), jnp.int32),
        compiler_params=_params(1),
        name="fox_skip_plan",
    )(kn, cmin, t(qn), t(cmax), t(kn))


def _stack_heads(q2):
    lane = lax.broadcasted_iota(jnp.int32, q2.shape, 1)
    zero = jnp.zeros_like(q2)
    return jnp.concatenate(
        [jnp.where(lane < FOX_HEAD_DIM, q2, zero), jnp.where(lane >= FOX_HEAD_DIM, q2, zero)], axis=0)


ATTN_TILE = 256
SUM_ROWS = 16


def _attn_prompt_kernel(n_ref, vq_ref, vj_ref, q_ref, f_ref, k_ref, e_ref, vt_ref, o_ref,
                        m_ref, acc_ref, sa_ref, xa_ref, sb_ref, xb_ref, *, blk, nb):
    b_id = pl.program_id(0)
    p_id = pl.program_id(1)
    hd = FOX_HEAD_DIM
    tw = ATTN_TILE
    per_head = blk // tw
    n_tiles = 2 * per_head

    lane = lax.broadcasted_iota(jnp.int32, (blk, LANES), 1)
    zero = jnp.zeros((blk, LANES), BF16)
    slot_a = 2 * BIAS_SLOT * p_id
    slot_b = slot_a + BIAS_SLOT
    ones_rows = jnp.ones((SUM_ROWS, blk), BF16)

    m_ref[...] = jnp.full(m_ref.shape, NEG_INF, F32)
    acc_ref[...] = jnp.zeros(acc_ref.shape, F32)

    def query_tiles(qi):
        r0 = pl.multiple_of(qi * blk, blk)
        q2 = q_ref[pl.ds(r0, blk), :]
        f2 = f_ref[pl.ds(r0, blk), :]
        heads = [
            jnp.concatenate([jnp.where(lane < hd, q2, zero),
                             jnp.where((lane >= slot_a) & (lane < slot_b), f2, zero)], axis=1),
            jnp.concatenate([jnp.where(lane >= hd, q2, zero),
                             jnp.where((lane >= slot_b) & (lane < slot_b + BIAS_SLOT), f2, zero)], axis=1),
        ]
        return [heads[t // per_head][(t % per_head) * tw:(t % per_head + 1) * tw] for t in range(n_tiles)]

    def scores(qi, j, buf, diagonal):
        s_ref, x_ref = buf
        q_tiles = query_tiles(qi)
        k0 = pl.multiple_of(j * blk, blk)
        kaug = jnp.concatenate([k_ref[pl.ds(k0, blk), :], e_ref[pl.ds(k0, blk), :]], axis=1)
        for t in range(n_tiles):
            if diagonal:
                c0 = (t % per_head) * tw
                live = c0 + tw
                s = _dot_nt(kaug[:live], q_tiles[t])
                r = lax.broadcasted_iota(jnp.int32, (live, tw), 0)
                c = lax.broadcasted_iota(jnp.int32, (live, tw), 1) + c0
                s = jnp.where(r <= c, s, NEG_INF)
                x_ref[t] = jnp.max(s, axis=0, keepdims=True)
                if live < blk:
                    s = jnp.concatenate([s, jnp.full((blk - live, tw), NEG_INF, F32)], axis=0)
                s_ref[t] = s
            else:
                s = _dot_nt(kaug, q_tiles[t])
                s_ref[t] = s
                x_ref[t] = jnp.max(s, axis=0, keepdims=True)

    def absorb(qi, j, buf):
        s_ref, x_ref = buf
        vt = vt_ref[j]
        v_heads = [jnp.concatenate([vt[h * hd:(h + 1) * hd, :], ones_rows], axis=0) for h in range(2)]
        for t in range(n_tiles):
            idx = qi * n_tiles + t
            m_old = m_ref[idx]
            m_new = jnp.maximum(m_old, x_ref[t])
            alpha = jnp.exp2(m_old - m_new)
            p = jnp.exp2(s_ref[t] - m_new)
            m_ref[idx] = m_new
            pv = _dot(v_heads[t // per_head], p.astype(BF16))
            acc_ref[idx] = acc_ref[idx] * alpha + pv

    buf_a = (sa_ref, xa_ref)
    buf_b = (sb_ref, xb_ref)

    scores(0, 0, buf_a, True)

    def diag_pair(n, carry):
        i0 = 2 * n
        i2 = jnp.minimum(i0 + 2, nb - 1)
        scores(i0 + 1, i0 + 1, buf_b, True)
        absorb(i0, i0, buf_a)
        scores(i2, i2, buf_a, True)
        absorb(i0 + 1, i0 + 1, buf_b)
        return carry

    lax.fori_loop(0, nb // 2, diag_pair, 0)

    n_off = n_ref[b_id, p_id]
    visit = lambda v: (vq_ref[b_id, p_id, v], vj_ref[b_id, p_id, v])

    @pl.when(n_off > 0)
    def _():
        scores(*visit(0), buf_a, False)

    def off_pair(n, carry):
        v0 = 2 * n
        scores(*visit(v0 + 1), buf_b, False)
        absorb(*visit(v0), buf_a)
        scores(*visit(v0 + 2), buf_a, False)
        absorb(*visit(v0 + 1), buf_b)
        return carry

    lax.fori_loop(0, jnp.maximum(n_off - 1, 0) // 2, off_pair, 0)

    @pl.when((n_off > 0) & (n_off % 2 == 0))
    def _():
        scores(*visit(n_off - 1), buf_b, False)
        absorb(*visit(n_off - 2), buf_a)
        absorb(*visit(n_off - 1), buf_b)

    @pl.when(n_off % 2 == 1)
    def _():
        absorb(*visit(n_off - 1), buf_a)

    def finish(qi, carry):
        halves = [jnp.concatenate([acc_ref[qi * n_tiles + t, 0:hd, :] / acc_ref[qi * n_tiles + t, hd:hd + 1, :]
                                   for t in range(h * per_head, (h + 1) * per_head)], axis=1)
                  for h in range(2)]
        o_ref[pl.ds(pl.multiple_of(qi * blk, blk), blk), :] = jnp.concatenate(halves, axis=0).T
        return carry

    lax.fori_loop(0, nb, finish, 0)


def _visit_lists(first):
    B, P, nb = first.shape
    vmax = nb * (nb - 1) // 2 + 8
    q_idx = jnp.arange(nb, dtype=jnp.int32)
    cnt = q_idx - first
    incl = jnp.cumsum(cnt, axis=-1)
    excl = incl - cnt
    v = jnp.arange(vmax, dtype=jnp.int32)
    vq = jnp.minimum(jnp.sum(v[None, None, :, None] >= incl[:, :, None, :], axis=-1), nb - 1).astype(jnp.int32)
    pick = vq[..., None] == q_idx
    take = lambda a: jnp.sum(jnp.where(pick, a[:, :, None, :], 0), axis=-1)
    vj = jnp.clip(take(first) + (v - take(excl)), 0, nb - 1).astype(jnp.int32)
    return incl[..., -1].astype(jnp.int32), vq, vj


def _attn_prompt(first, q, f, kb, e, vt, *, blk):
    B, L, _ = q.shape
    nb = L // blk
    n_tiles = 2 * blk // ATTN_TILE
    n_off, vq, vj = _visit_lists(first)
    by_pair = pl.BlockSpec((None, L, LANES), lambda b, p, *_: (b, 0, p))
    shared = pl.BlockSpec((None, L, LANES), lambda b, p, *_: (b, 0, 0))
    return pl.pallas_call(
        functools.partial(_attn_prompt_kernel, blk=blk, nb=nb),
        grid_spec=pltpu.PrefetchScalarGridSpec(
            num_scalar_prefetch=3,
            grid=(B, FOX_PAIRS),
            in_specs=[by_pair, shared, by_pair, shared,
                      pl.BlockSpec((None, nb, LANES, blk), lambda b, p, *_: (b, 0, p, 0))],
            out_specs=by_pair,
            scratch_shapes=[
                pltpu.VMEM((nb * n_tiles, 1, ATTN_TILE), F32),
                pltpu.VMEM((nb * n_tiles, FOX_HEAD_DIM + SUM_ROWS, ATTN_TILE), F32),
                pltpu.VMEM((n_tiles, blk, ATTN_TILE), F32),
                pltpu.VMEM((n_tiles, 1, ATTN_TILE), F32),
                pltpu.VMEM((n_tiles, blk, ATTN_TILE), F32),
                pltpu.VMEM((n_tiles, 1, ATTN_TILE), F32),
            ],
        ),
        out_shape=jax.ShapeDtypeStruct((B, L, FOX_WIDTH), F32),
        compiler_params=_params(2),
        name="fox_attn_prompt",
    )(n_off, vq, vj, q, f, kb, e, vt)


def _attn_sample_kernel(q_ref, kn_ref, vn_ref, ck_ref, cv_ref, c_ref, ct_ref, o_ref, *, past, ls):
    hd = FOX_HEAD_DIM
    pad_k = jnp.zeros((LANES - ls, LANES), BF16)
    cqb = c_ref[past:past + ls, :]
    lane = lax.broadcasted_iota(jnp.int32, cqb.shape, 1)
    r = lax.broadcasted_iota(jnp.int32, (2 * ls, LANES), 0) & (ls - 1)
    c = lax.broadcasted_iota(jnp.int32, (2 * ls, LANES), 1)
    lane_o = lax.broadcasted_iota(jnp.int32, (ls, LANES), 1)

    for p in range(FOX_PAIRS):
        cols = slice(p * LANES, (p + 1) * LANES)
        qst = _stack_heads(q_ref[:, cols])
        kc_t = ck_ref[cols, :].astype(BF16)
        vc_t = cv_ref[cols, :].astype(BF16)
        kn = jnp.concatenate([kn_ref[:, cols], pad_k], axis=0)
        vn = jnp.concatenate([vn_ref[:, cols].astype(BF16), pad_k], axis=0)
        cq = jnp.concatenate(
            [jnp.sum(jnp.where(lane == 2 * p + h2, cqb, 0.0), axis=1, keepdims=True) for h2 in range(2)],
            axis=0)
        cka = ct_ref[2 * p:2 * p + 1, :]
        ckb = ct_ref[2 * p + 1:2 * p + 2, :]

        def bias(lo, hi):
            return jnp.concatenate([cq[:ls] - cka[:, lo:hi], cq[ls:] - ckb[:, lo:hi]], axis=0)

        s_c = _dot(qst, kc_t) + bias(0, past)
        s_n = _dot_nt(qst, kn) + bias(past, past + LANES)
        s_n = jnp.where(c <= r, s_n, NEG_INF)
        m = jnp.maximum(jnp.max(s_c, axis=1, keepdims=True), jnp.max(s_n, axis=1, keepdims=True))
        p_c = jnp.exp2(s_c - m)
        p_n = jnp.exp2(s_n - m)
        l = jnp.sum(p_c, axis=1, keepdims=True) + jnp.sum(p_n, axis=1, keepdims=True)
        o = (_dot_nt(p_c.astype(BF16), vc_t) + _dot(p_n.astype(BF16), vn)) / l
        o_ref[:, cols] = jnp.where(lane_o < hd, o[:ls], o[ls:])


def _attn_sample(q, kb, v, cache_k, cache_v, c, ct, *, past):
    B, ls, _ = q.shape
    tot = c.shape[1]
    rows = lambda n: pl.BlockSpec((None, n, FOX_WIDTH), lambda b: (b, 0, 0))
    cache_t = pl.BlockSpec((None, FOX_WIDTH, past), lambda b: (b, 0, 0))
    return pl.pallas_call(
        functools.partial(_attn_sample_kernel, past=past, ls=ls),
        grid=(B,),
        in_specs=[
            rows(ls), rows(ls), rows(ls), cache_t, cache_t,
            pl.BlockSpec((None, tot, FOX_HEADS), lambda b: (b, 0, 0)),
            pl.BlockSpec((None, FOX_HEADS, tot), lambda b: (b, 0, 0)),
        ],
        out_specs=rows(ls),
        out_shape=jax.ShapeDtypeStruct((B, ls, FOX_WIDTH), F32),
        compiler_params=_params(1),
        name="fox_attn_sample",
    )(q, kb, v, cache_k, cache_v, c, ct)


def _mid_kernel(o_ref, sg_ref, x_ref, wo_ref, g_ref, w_ref, wa1_ref, wa2_ref, ba_ref,
                y_ref, q_ref, k_ref, v_ref, sg2_ref, ga_ref):
    u = (o_ref[...] * sg_ref[...]).astype(BF16)
    y = x_ref[...] + _dot(u, wo_ref[...])
    y_ref[...] = y
    hb = _rms(y, g_ref[...]).astype(BF16)
    kw = GLA_KEY_WIDTH
    q_ref[...] = _dot(hb, w_ref[:, 0:kw]) * (GLA_HEAD_K ** -0.5)
    k_ref[...] = _dot(hb, w_ref[:, kw:2 * kw])
    v_ref[...] = _dot(hb, w_ref[:, 2 * kw:2 * kw + GLA_VAL_WIDTH]).astype(BF16)
    gate = _dot(hb, w_ref[:, 2 * kw + GLA_VAL_WIDTH:2 * kw + 2 * GLA_VAL_WIDTH])
    sg2_ref[...] = gate * jax.nn.sigmoid(gate)
    a1 = _dot(hb, wa1_ref[...]).astype(BF16)
    a = _dot(a1, wa2_ref[...]) + ba_ref[...]
    ga_ref[...] = _log_sigmoid(a) * (1.0 / GLA_GATE_TEMP)


def _mid(o, sg, x, w_out, g, w_main, w_a1, w_a2, b_a, *, tm):
    T = o.shape[0]
    tok = lambda w: pl.BlockSpec((tm, w), lambda i: (i, 0))
    full = lambda a: pl.BlockSpec(a.shape, lambda i: (0,) * a.ndim)
    return pl.pallas_call(
        _mid_kernel,
        grid=(T // tm,),
        in_specs=[tok(D_MODEL), tok(D_MODEL), tok(D_MODEL), full(w_out), full(g), full(w_main),
                  full(w_a1), full(w_a2), full(b_a)],
        out_specs=[tok(D_MODEL), tok(GLA_KEY_WIDTH), tok(GLA_KEY_WIDTH), tok(GLA_VAL_WIDTH),
                   tok(GLA_VAL_WIDTH), tok(GLA_KEY_WIDTH)],
        out_shape=[
            jax.ShapeDtypeStruct((T, D_MODEL), F32),
            jax.ShapeDtypeStruct((T, GLA_KEY_WIDTH), F32),
            jax.ShapeDtypeStruct((T, GLA_KEY_WIDTH), F32),
            jax.ShapeDtypeStruct((T, GLA_VAL_WIDTH), BF16),
            jax.ShapeDtypeStruct((T, GLA_VAL_WIDTH), F32),
            jax.ShapeDtypeStruct((T, GLA_KEY_WIDTH), F32),
        ],
        compiler_params=_params(1),
        name="fox_out_gla_proj",
    )(o, sg, x, w_out, g, w_main, w_a1, w_a2, b_a)


def _gla_kernel(q_ref, k_ref, v_ref, ga_ref, s0_ref, o_ref, s_ref, *, n_chunks):
    i = pl.program_id(1)
    ch = GLA_CHUNK
    dk, dv = GLA_HEAD_K, GLA_HEAD_V

    @pl.when(i == 0)
    def _():
        s_ref[...] = s0_ref[...]

    t = n_chunks * ch
    r = lax.broadcasted_iota(jnp.int32, (t, t), 0)
    c = lax.broadcasted_iota(jnp.int32, (t, t), 1)
    causal = ((r >> 6) == (c >> 6)) & (r >= c)
    tril = jnp.where(causal, 1.0, 0.0).astype(BF16)
    eye = (lax.broadcasted_iota(jnp.int32, (dk, dk), 0) == lax.broadcasted_iota(jnp.int32, (dk, dk), 1))

    b = _tril_cumsum(tril, ga_ref[...])
    lasts = [b[(ci + 1) * ch - 1:(ci + 1) * ch, :] for ci in range(n_chunks)]
    b_last = jnp.concatenate([jnp.broadcast_to(x, (ch, GLA_KEY_WIDTH)) for x in lasts], axis=0)
    kk = k_ref[...]
    qe = (q_ref[...] * jnp.exp(b)).astype(BF16)
    ke = (kk * jnp.exp(-b)).astype(BF16)
    kd = (kk * jnp.exp(b_last - b)).astype(BF16)
    decs = [jnp.exp(x) for x in lasts]

    for h in range(GLA_HEADS):
        ks = slice(h * dk, (h + 1) * dk)
        vs = slice(h * dv, (h + 1) * dv)
        vh = v_ref[:, vs]
        a = jnp.where(causal, _dot_nt(qe[:, ks], ke[:, ks]), 0.0)
        o_intra = _dot(a.astype(BF16), vh)
        s = s_ref[h]
        for ci in range(n_chunks):
            rows = slice(ci * ch, (ci + 1) * ch)
            o_ref[rows, vs] = o_intra[rows] + _dot(qe[rows, ks], s.astype(BF16))
            dec_col = jnp.sum(jnp.where(eye, jnp.broadcast_to(decs[ci][:, ks], (dk, dk)), 0.0),
                              axis=1, keepdims=True)
            s = dec_col * s + _dot_tn(kd[rows, ks], vh[rows])
        s_ref[h] = s


def _gla(q, k, v, ga, s0, *, tc):
    B, L, _ = q.shape
    tok = lambda w: pl.BlockSpec((None, tc, w), lambda b, i: (b, i, 0))
    st = pl.BlockSpec((None, GLA_HEADS, GLA_HEAD_K, GLA_HEAD_V), lambda b, i: (b, 0, 0, 0))
    return pl.pallas_call(
        functools.partial(_gla_kernel, n_chunks=tc // GLA_CHUNK),
        grid=(B, L // tc),
        in_specs=[tok(GLA_KEY_WIDTH), tok(GLA_KEY_WIDTH), tok(GLA_VAL_WIDTH), tok(GLA_KEY_WIDTH), st],
        out_specs=[tok(GLA_VAL_WIDTH), st],
        out_shape=[
            jax.ShapeDtypeStruct((B, L, GLA_VAL_WIDTH), F32),
            jax.ShapeDtypeStruct((B, GLA_HEADS, GLA_HEAD_K, GLA_HEAD_V), F32),
        ],
        compiler_params=_params(2),
        name="gla_chunk",
    )(q, k, v, ga, s0)


def _gla_out_kernel(o_ref, sg_ref, y_ref, go_ref, wo_ref, gf_ref, out_ref, *, chunk_major):
    dv = GLA_HEAD_V
    if chunk_major:
        o = jnp.concatenate([o_ref[:, t, :] for t in range(o_ref.shape[1])], axis=0)
    else:
        o = o_ref[...]
    parts = []
    for h in range(GLA_HEADS):
        vs = slice(h * dv, (h + 1) * dv)
        parts.append((_rms(o[:, vs], go_ref[...]) * sg_ref[:, vs]).astype(BF16))
    u = jnp.concatenate(parts, axis=1)
    y = y_ref[...] + _dot(u, wo_ref[...])
    out_ref[...] = _rms(y, gf_ref[...])


def _gla_out(o, sg, y, g_o, w_out, g_f, *, tm):
    B, L, _ = o.shape
    n = L // GLA_CHUNK
    chunk_major = n > 1
    if chunk_major:
        tm = 8 * n
        o = o.reshape(B, n, GLA_CHUNK, D_MODEL)
        o_spec = pl.BlockSpec((None, n, 8, D_MODEL), lambda b, i: (b, 0, i, 0))
    else:
        flat = lambda a: a.reshape(1, B * L, D_MODEL)
        o, sg, y = flat(o), flat(sg), flat(y)
        o_spec = pl.BlockSpec((None, tm, D_MODEL), lambda b, i: (b, i, 0))
    nb, rows, _ = sg.shape
    tok = pl.BlockSpec((None, tm, D_MODEL), lambda b, i: (b, i, 0))
    full = lambda a: pl.BlockSpec(a.shape, lambda b, i: (0,) * a.ndim)
    return pl.pallas_call(
        functools.partial(_gla_out_kernel, chunk_major=chunk_major),
        grid=(nb, rows // tm),
        in_specs=[o_spec, tok, tok, full(g_o), full(w_out), full(g_f)],
        out_specs=tok,
        out_shape=jax.ShapeDtypeStruct((nb, rows, D_MODEL), F32),
        compiler_params=_params(2),
        name="gla_out",
    )(o, sg, y, g_o, w_out, g_f)


def _pad_cols(w, n):
    return jnp.pad(w, ((0, 0), (0, n - w.shape[1])))


def kernel(x_prompt, x_sample, cache_fox_k, cache_fox_v, cache_fox_logf, state_gla,
           g_norm_fox, w_in_fox, b_fox_f, w_out_fox,
           g_norm_gla, w_in_gla, w_gla_a2, b_gla_a, g_gla_o, w_out_gla, g_final):
    B, L, D = x_prompt.shape
    Bs, Ls, _ = x_sample.shape
    P = cache_fox_k.shape[1]
    H, hd = FOX_HEADS, FOX_HEAD_DIM

    row = lambda a: a.reshape(1, -1).astype(F32)
    w_fox = w_in_fox[:, :4 * FOX_WIDTH].astype(BF16)
    w_fox_f = _pad_cols(w_in_fox[:, 4 * FOX_WIDTH:], LANES).astype(BF16)
    b_f = _pad_cols(row(b_fox_f), LANES)
    w_o_fox = w_out_fox.astype(BF16)
    n_main = 2 * GLA_KEY_WIDTH + 2 * GLA_VAL_WIDTH
    w_gla = w_in_gla[:, :n_main].astype(BF16)
    w_a1 = _pad_cols(w_in_gla[:, n_main:], LANES).astype(BF16)
    w_a2 = jnp.pad(w_gla_a2, ((0, LANES - GLA_GATE_RANK), (0, 0))).astype(BF16)
    w_o_gla = w_out_gla.astype(BF16)

    q_p, k_p, v_p, kb_p, sg_p, lf_p, vt_p, nrm_p = _fox_proj(
        x_prompt, row(g_norm_fox), w_fox, w_fox_f, b_f, tm=ATTN_BLOCK, want_vt=True)
    _, e_p, f_p, rng_p = _cumsum(lf_p, rows=ATTN_BLOCK, tb=ATTN_BLOCK, want_bias=True)
    first_p = _skip_plan(nrm_p[:, :, 0, :H], rng_p[:, :, 0, :], nrm_p[:, :, 1, :H], rng_p[:, :, 1, :])
    o_p = _attn_prompt(first_p, q_p, f_p, kb_p, e_p, vt_p, blk=ATTN_BLOCK)

    xs = x_sample.reshape(1, Bs * Ls, D)
    q_s, k_s, v_s, kb_s, sg_s, lf_s = _fox_proj(
        xs, row(g_norm_fox), w_fox, w_fox_f, b_f, tm=PROJ_TM, want_vt=False)
    sh = lambda a: a.reshape(Bs, Ls, a.shape[-1])
    q_s, k_s, v_s, kb_s, lf_s = sh(q_s), sh(k_s), sh(v_s), sh(kb_s), sh(lf_s)
    lf_all = jnp.concatenate(
        [cache_fox_logf.astype(F32), lf_s, jnp.zeros((Bs, LANES - Ls, H), F32)], axis=1)
    (c_s,) = _cumsum(lf_all, rows=(P + LANES) // 3, tb=P + LANES, want_bias=False)
    cache_t = lambda a: jnp.transpose(a, (0, 2, 3, 1)).reshape(Bs, FOX_WIDTH, P)
    o_s = _attn_sample(q_s, kb_s, v_s, cache_t(cache_fox_k), cache_t(cache_fox_v),
                       c_s, jnp.swapaxes(c_s, 1, 2), past=P)

    def gla_layer(o, sg, x, s0, tc):
        b, l, _ = x.shape
        flat = lambda a: a.reshape(b * l, a.shape[-1])
        y1, q, k, v, sg2, ga = _mid(flat(o), flat(sg), flat(x), w_o_fox, row(g_norm_gla), w_gla,
                                    w_a1, w_a2, row(b_gla_a), tm=PROJ_TM)
        un = lambda a: a.reshape(b, l, a.shape[-1])
        og, s_out = _gla(un(q), un(k), un(v), un(ga), s0, tc=tc)
        y = _gla_out(og, un(sg2), un(y1), row(g_gla_o), w_o_gla, row(g_final), tm=PROJ_TM)
        return y.reshape(b, l, D), s_out

    s0_p = jnp.zeros((B, GLA_HEADS, GLA_HEAD_K, GLA_HEAD_V), F32)
    y_p, s_p = gla_layer(o_p, sg_p, x_prompt, s0_p, GLA_TC)
    y_s, s_s = gla_layer(o_s, sg_s.reshape(Bs, Ls, D), x_sample, state_gla.astype(F32), Ls)

    return (y_p, y_s,
            k_p.reshape(B, L, H, hd), v_p.reshape(B, L, H, hd), lf_p, s_p.astype(state_gla.dtype),
            k_s.reshape(Bs, Ls, H, hd), v_s.reshape(Bs, Ls, H, hd), lf_s, s_s.astype(state_gla.dtype))
```

```python
import functools

import jax
import jax.numpy as jnp
from jax import lax
from jax.experimental import pallas as pl
from jax.experimental.pallas import tpu as pltpu

F32 = jnp.float32
BF16 = jnp.bfloat16

D_MODEL = 1024
EPS = 1e-6
NEG_INF = -1e30
LOG2E = 1.4426950408889634
NORM_SLACK = 1.01
SKIP_LOG2 = 152.0

FOX_HEADS = 16
FOX_HEAD_DIM = 64
FOX_WIDTH = FOX_HEADS * FOX_HEAD_DIM
FOX_PAIRS = FOX_HEADS // 2

GLA_HEADS = 4
GLA_KEY_WIDTH = 512
GLA_VAL_WIDTH = 1024
GLA_HEAD_K = 128
GLA_HEAD_V = 256
GLA_GATE_RANK = 16
GLA_GATE_TEMP = 16.0
GLA_CHUNK = 64

LANES = 128
VMEM_LIMIT = 56 * 1024 * 1024

PROJ_TM = 512
ATTN_BLOCK = 512
GLA_TC = 256


def _params(n_axes, flags=None):
    return pltpu.CompilerParams(
        dimension_semantics=("arbitrary",) * n_axes,
        vmem_limit_bytes=VMEM_LIMIT,
        flags=flags,
    )


def _dot(a, b):
    return jnp.dot(a, b, preferred_element_type=F32)


def _dot_nt(a, b):
    return lax.dot_general(a, b, (((1,), (1,)), ((), ())), preferred_element_type=F32)


def _dot_tn(a, b):
    return lax.dot_general(a, b, (((0,), (0,)), ((), ())), preferred_element_type=F32)


def _log_sigmoid(z):
    return jnp.minimum(z, 0.0) - jnp.log1p(jnp.exp(-jnp.abs(z)))


def _rms(x, g):
    ms = jnp.mean(x * x, axis=-1, keepdims=True)
    return (x * lax.rsqrt(ms + EPS)) * g


def _split3(x):
    hi = x.astype(BF16)
    r1 = x - hi.astype(F32)
    mid = r1.astype(BF16)
    lo = (r1 - mid.astype(F32)).astype(BF16)
    return hi, mid, lo


def _tril_cumsum(tril_bf16, x):
    hi, mid, lo = _split3(x)
    return (_dot(tril_bf16, hi) + _dot(tril_bf16, mid)) + _dot(tril_bf16, lo)


def _tril(n):
    r = lax.broadcasted_iota(jnp.int32, (n, n), 0)
    c = lax.broadcasted_iota(jnp.int32, (n, n), 1)
    return jnp.where(r >= c, 1.0, 0.0).astype(BF16)


def _fox_proj_kernel(x_ref, g_ref, w_ref, wf_ref, bf_ref, *rest, want_vt):
    if want_vt:
        q_ref, k_ref, v_ref, kb_ref, sg_ref, lf_ref, vt_ref, nrm_ref = rest
    else:
        q_ref, k_ref, v_ref, kb_ref, sg_ref, lf_ref = rest
    hb = _rms(x_ref[...], g_ref[...]).astype(BF16)
    q = _dot(hb, w_ref[:, 0:FOX_WIDTH])
    qb = (q * (LOG2E * FOX_HEAD_DIM ** -0.5)).astype(BF16)
    q_ref[...] = qb
    k = _dot(hb, w_ref[:, FOX_WIDTH:2 * FOX_WIDTH])
    k_ref[...] = k
    kb = k.astype(BF16)
    kb_ref[...] = kb
    if want_vt:
        row = lax.broadcasted_iota(jnp.int32, (FOX_WIDTH, LANES), 0)
        col = lax.broadcasted_iota(jnp.int32, (FOX_WIDTH, LANES), 1)
        head_of = jnp.where((row >> 6) == col, 1.0, 0.0).astype(BF16)

        def bound(xb):
            xf = xb.astype(F32)
            sq = _dot((xf * xf).astype(BF16), head_of)
            return jnp.sqrt(jnp.max(sq, axis=0, keepdims=True)) * NORM_SLACK

        nrm_ref[...] = jnp.concatenate(
            [bound(kb), bound(qb), jnp.zeros((6, LANES), F32)], axis=0)
    v = _dot(hb, w_ref[:, 2 * FOX_WIDTH:3 * FOX_WIDTH])
    v_ref[...] = v
    if want_vt:
        vt_ref[...] = v.astype(BF16).T
    gate = _dot(hb, w_ref[:, 3 * FOX_WIDTH:4 * FOX_WIDTH])
    sg_ref[...] = gate * jax.nn.sigmoid(gate)
    fl = _dot(hb, wf_ref[...]) + bf_ref[...]
    lf_ref[...] = _log_sigmoid(fl)[:, :FOX_HEADS]


def _fox_proj(x, g, w_main, w_f, b_f, *, tm, want_vt):
    B, L, _ = x.shape
    n = L // tm
    tok = lambda w: pl.BlockSpec((None, tm, w), lambda b, i: (b, i, 0))
    full = lambda a: pl.BlockSpec(a.shape, lambda b, i: (0,) * a.ndim)
    in_specs = [tok(D_MODEL), full(g), full(w_main), full(w_f), full(b_f)]
    args = [x, g, w_main, w_f, b_f]
    out_specs = [tok(FOX_WIDTH)] * 5 + [tok(FOX_HEADS)]
    out_shape = [
        jax.ShapeDtypeStruct((B, L, FOX_WIDTH), BF16),
        jax.ShapeDtypeStruct((B, L, FOX_WIDTH), F32),
        jax.ShapeDtypeStruct((B, L, FOX_WIDTH), F32),
        jax.ShapeDtypeStruct((B, L, FOX_WIDTH), BF16),
        jax.ShapeDtypeStruct((B, L, FOX_WIDTH), F32),
        jax.ShapeDtypeStruct((B, L, FOX_HEADS), F32),
    ]
    if want_vt:
        out_specs.append(pl.BlockSpec((None, None, FOX_WIDTH, tm), lambda b, i: (b, i, 0, 0)))
        out_shape.append(jax.ShapeDtypeStruct((B, n, FOX_WIDTH, tm), BF16))
        out_specs.append(pl.BlockSpec((None, None, 8, LANES), lambda b, i: (b, i, 0, 0)))
        out_shape.append(jax.ShapeDtypeStruct((B, n, 8, LANES), F32))
    return pl.pallas_call(
        functools.partial(_fox_proj_kernel, want_vt=want_vt),
        grid=(B, n),
        in_specs=in_specs,
        out_specs=out_specs,
        out_shape=out_shape,
        compiler_params=_params(2),
        name="fox_proj_vt" if want_vt else "fox_proj",
    )(*args)


BIAS_SLOT = 8


def _bias_selectors():
    shape = (FOX_HEADS, LANES)
    head = lax.broadcasted_iota(jnp.int32, shape, 0)
    col = lax.broadcasted_iota(jnp.int32, shape, 1)
    slot = BIAS_SLOT * head
    sel_f = [jnp.where(col == slot + x, 1.0, 0.0).astype(BF16) for x in range(3)]
    sel_e = [jnp.where(col == slot + 3 + x, -1.0, 0.0).astype(BF16) for x in range(3)]
    lane = lax.broadcasted_iota(jnp.int32, (1, LANES), 1) & (BIAS_SLOT - 1)
    one_f = jnp.where((lane >= 3) & (lane < 6), 1.0, 0.0)
    one_e = jnp.where(lane < 3, 1.0, 0.0)
    return sel_f, sel_e, one_f, one_e


def _bias_kernel(lf_ref, e_ref, f_ref, rng_ref, carry_ref, *, rows):
    tril = _tril(rows)
    sel_f, sel_e, one_f, one_e = _bias_selectors()

    @pl.when(pl.program_id(1) == 0)
    def _():
        carry_ref[...] = jnp.zeros_like(carry_ref)

    cs = _tril_cumsum(tril, lf_ref[...]) + carry_ref[...]
    carry_ref[...] = cs[rows - 1:rows, :]
    c2 = cs * LOG2E
    hi, mid, lo = _split3(c2)
    f_ref[...] = ((_dot(hi, sel_f[0]) + _dot(mid, sel_f[1])) + _dot(lo, sel_f[2]) + one_f).astype(BF16)
    e_ref[...] = ((_dot(hi, sel_e[0]) + _dot(mid, sel_e[1])) + _dot(lo, sel_e[2]) + one_e).astype(BF16)
    rng_ref[...] = jnp.concatenate(
        [jnp.min(c2, axis=0, keepdims=True), jnp.max(c2, axis=0, keepdims=True),
         jnp.zeros((6, FOX_HEADS), F32)], axis=0)


def _bias_tiles(lf, *, rows):
    B, L, H = lf.shape
    spec = pl.BlockSpec((None, rows, H), lambda b, i: (b, i, 0))
    wide = pl.BlockSpec((None, rows, LANES), lambda b, i: (b, i, 0))
    return pl.pallas_call(
        functools.partial(_bias_kernel, rows=rows),
        grid=(B, L // rows),
        in_specs=[spec],
        out_specs=[wide, wide, pl.BlockSpec((None, None, 8, H), lambda b, i: (b, i, 0, 0))],
        out_shape=[jax.ShapeDtypeStruct((B, L, LANES), BF16)] * 2
        + [jax.ShapeDtypeStruct((B, L // rows, 8, H), F32)],
        scratch_shapes=[pltpu.VMEM((1, FOX_HEADS), F32)],
        compiler_params=_params(2),
        name="fox_cumsum_bias",
    )(lf)


def _cumsum_short_kernel(lf_ref, c_ref, *, rows):
    tril = _tril(rows)
    nbatch, length, _ = lf_ref.shape
    for bi in range(nbatch):
        carry = jnp.zeros((1, FOX_HEADS), F32)
        for r0 in range(0, length, rows):
            cs = _tril_cumsum(tril, lf_ref[bi, r0:r0 + rows, :]) + carry
            c_ref[bi, r0:r0 + rows, :] = cs * LOG2E
            carry = cs[rows - 1:rows, :]


def _cumsum_short(lf, *, rows, per_step):
    B, L, H = lf.shape
    spec = pl.BlockSpec((per_step, L, H), lambda b: (b, 0, 0))
    return pl.pallas_call(
        functools.partial(_cumsum_short_kernel, rows=rows),
        grid=(B // per_step,),
        in_specs=[spec],
        out_specs=spec,
        out_shape=jax.ShapeDtypeStruct((B, L, H), F32),
        compiler_params=_params(1),
        name="fox_cumsum",
    )(lf)


def _skip_plan_kernel(kn_ref, cmin_ref, qn_ref, cmax_ref, knq_ref, first_ref, *, nb):
    j = lax.broadcasted_iota(jnp.int32, (nb, nb), 0).astype(F32)
    q = lax.broadcasted_iota(jnp.int32, (nb, nb), 1).astype(F32)
    rows = []
    for p in range(FOX_PAIRS):
        skippable = None
        for h in (2 * p, 2 * p + 1):
            qn = qn_ref[h:h + 1, :]
            bound = (kn_ref[:, h:h + 1] * qn + (cmax_ref[h:h + 1, :] - cmin_ref[:, h:h + 1])
                     + qn * knq_ref[h:h + 1, :])
            ok = bound < -SKIP_LOG2
            skippable = ok if skippable is None else (skippable & ok)
        must = (j < q) & jnp.logical_not(skippable)
        rows.append(jnp.min(jnp.where(must, j, q), axis=0, keepdims=True))
    first_ref[...] = jnp.concatenate(rows, axis=0).astype(jnp.int32)


def _skip_plan(kn, cmin, qn, cmax):
    B, nb, H = kn.shape
    by_block = pl.BlockSpec((None, nb, H), lambda b: (b, 0, 0))
    by_head = pl.BlockSpec((None, H, nb), lambda b: (b, 0, 0))
    t = lambda a: jnp.swapaxes(a, 1, 2)
    return pl.pallas_call(
        functools.partial(_skip_plan_kernel, nb=nb),
        grid=(B,),
        in_specs=[by_block, by_block, by_head, by_head, by_head],
        out_specs=pl.BlockSpec((None, FOX_PAIRS, nb), lambda b: (b, 0, 0)),
        out_shape=jax.ShapeDtypeStruct((B, FOX_PAIRS, nb), jnp.int32),
        compiler_params=_params(1),
        name="fox_skip_plan",
    )(kn, cmin, t(qn), t(cmax), t(kn))


def _stack_heads(q2):
    lane = lax.broadcasted_iota(jnp.int32, q2.shape, 1)
    zero = jnp.zeros_like(q2)
    return jnp.concatenate(
        [jnp.where(lane < FOX_HEAD_DIM, q2, zero), jnp.where(lane >= FOX_HEAD_DIM, q2, zero)], axis=0)


ATTN_TILE = 256
SUM_ROWS = 16


def _attn_prompt_kernel(n_ref, vq_ref, vj_ref, q_ref, f_ref, k_ref, e_ref, vt_ref, o_ref,
                        m_ref, acc_ref, sa_ref, xa_ref, sb_ref, xb_ref, *, blk, nb):
    b_id = pl.program_id(0)
    p_id = pl.program_id(1)
    hd = FOX_HEAD_DIM
    tw = ATTN_TILE
    per_head = blk // tw
    n_tiles = 2 * per_head

    lane = lax.broadcasted_iota(jnp.int32, (blk, LANES), 1)
    zero = jnp.zeros((blk, LANES), BF16)
    slot_a = 2 * BIAS_SLOT * p_id
    slot_b = slot_a + BIAS_SLOT
    ones_rows = jnp.ones((SUM_ROWS, blk), BF16)

    m_ref[...] = jnp.full(m_ref.shape, NEG_INF, F32)
    acc_ref[...] = jnp.zeros(acc_ref.shape, F32)

    def query_tiles(qi):
        r0 = pl.multiple_of(qi * blk, blk)
        q2 = q_ref[pl.ds(r0, blk), :]
        f2 = f_ref[pl.ds(r0, blk), :]
        heads = [
            jnp.concatenate([jnp.where(lane < hd, q2, zero),
                             jnp.where((lane >= slot_a) & (lane < slot_b), f2, zero)], axis=1),
            jnp.concatenate([jnp.where(lane >= hd, q2, zero),
                             jnp.where((lane >= slot_b) & (lane < slot_b + BIAS_SLOT), f2, zero)], axis=1),
        ]
        return [heads[t // per_head][(t % per_head) * tw:(t % per_head + 1) * tw] for t in range(n_tiles)]

    def scores(qi, j, buf, diagonal):
        s_ref, x_ref = buf
        q_tiles = query_tiles(qi)
        k0 = pl.multiple_of(j * blk, blk)
        kaug = jnp.concatenate([k_ref[pl.ds(k0, blk), :], e_ref[pl.ds(k0, blk), :]], axis=1)
        for t in range(n_tiles):
            if diagonal:
                c0 = (t % per_head) * tw
                live = c0 + tw
                s = _dot_nt(kaug[:live], q_tiles[t])
                r = lax.broadcasted_iota(jnp.int32, (live, tw), 0)
                c = lax.broadcasted_iota(jnp.int32, (live, tw), 1) + c0
                s = jnp.where(r <= c, s, NEG_INF)
                x_ref[t] = jnp.max(s, axis=0, keepdims=True)
                if live < blk:
                    s = jnp.concatenate([s, jnp.full((blk - live, tw), NEG_INF, F32)], axis=0)
                s_ref[t] = s
            else:
                s = _dot_nt(kaug, q_tiles[t])
                s_ref[t] = s
                x_ref[t] = jnp.max(s, axis=0, keepdims=True)

    def absorb(qi, j, buf):
        s_ref, x_ref = buf
        vt = vt_ref[j]
        v_heads = [jnp.concatenate([vt[h * hd:(h + 1) * hd, :], ones_rows], axis=0) for h in range(2)]
        for t in range(n_tiles):
            idx = qi * n_tiles + t
            m_old = m_ref[idx]
            m_new = jnp.maximum(m_old, x_ref[t])
            alpha = jnp.exp2(m_old - m_new)
            p = jnp.exp2(s_ref[t] - m_new)
            m_ref[idx] = m_new
            pv = _dot(v_heads[t // per_head], p.astype(BF16))
            acc_ref[idx] = acc_ref[idx] * alpha + pv

    buf_a = (sa_ref, xa_ref)
    buf_b = (sb_ref, xb_ref)

    scores(0, 0, buf_a, True)

    def diag_pair(n, carry):
        i0 = 2 * n
        i2 = jnp.minimum(i0 + 2, nb - 1)
        scores(i0 + 1, i0 + 1, buf_b, True)
        absorb(i0, i0, buf_a)
        scores(i2, i2, buf_a, True)
        absorb(i0 + 1, i0 + 1, buf_b)
        return carry

    lax.fori_loop(0, nb // 2, diag_pair, 0)

    n_off = n_ref[b_id, p_id]
    visit = lambda v: (vq_ref[b_id, p_id, v], vj_ref[b_id, p_id, v])

    @pl.when(n_off > 0)
    def _():
        scores(*visit(0), buf_a, False)

    def off_pair(n, carry):
        v0 = 2 * n
        scores(*visit(v0 + 1), buf_b, False)
        absorb(*visit(v0), buf_a)
        scores(*visit(v0 + 2), buf_a, False)
        absorb(*visit(v0 + 1), buf_b)
        return carry

    lax.fori_loop(0, jnp.maximum(n_off - 1, 0) // 2, off_pair, 0)

    @pl.when((n_off > 0) & (n_off % 2 == 0))
    def _():
        scores(*visit(n_off - 1), buf_b, False)
        absorb(*visit(n_off - 2), buf_a)
        absorb(*visit(n_off - 1), buf_b)

    @pl.when(n_off % 2 == 1)
    def _():
        absorb(*visit(n_off - 1), buf_a)

    def finish(qi, carry):
        halves = [jnp.concatenate([acc_ref[qi * n_tiles + t, 0:hd, :] / acc_ref[qi * n_tiles + t, hd:hd + 1, :]
                                   for t in range(h * per_head, (h + 1) * per_head)], axis=1)
                  for h in range(2)]
        o_ref[pl.ds(pl.multiple_of(qi * blk, blk), blk), :] = jnp.concatenate(halves, axis=0).T
        return carry

    lax.fori_loop(0, nb, finish, 0)


def _visit_lists(first):
    B, P, nb = first.shape
    vmax = nb * (nb - 1) // 2 + 8
    q_idx = jnp.arange(nb, dtype=jnp.int32)
    cnt = q_idx - first
    incl = jnp.cumsum(cnt, axis=-1)
    excl = incl - cnt
    v = jnp.arange(vmax, dtype=jnp.int32)
    vq = jnp.minimum(jnp.sum(v[None, None, :, None] >= incl[:, :, None, :], axis=-1), nb - 1).astype(jnp.int32)
    pick = vq[..., None] == q_idx
    take = lambda a: jnp.sum(jnp.where(pick, a[:, :, None, :], 0), axis=-1)
    vj = jnp.clip(take(first) + (v - take(excl)), 0, nb - 1).astype(jnp.int32)
    return incl[..., -1].astype(jnp.int32), vq, vj


def _attn_prompt(first, q, f, kb, e, vt, *, blk):
    B, L, _ = q.shape
    nb = L // blk
    n_tiles = 2 * blk // ATTN_TILE
    n_off, vq, vj = _visit_lists(first)
    by_pair = pl.BlockSpec((None, L, LANES), lambda b, p, *_: (b, 0, p))
    shared = pl.BlockSpec((None, L, LANES), lambda b, p, *_: (b, 0, 0))
    return pl.pallas_call(
        functools.partial(_attn_prompt_kernel, blk=blk, nb=nb),
        grid_spec=pltpu.PrefetchScalarGridSpec(
            num_scalar_prefetch=3,
            grid=(B, FOX_PAIRS),
            in_specs=[by_pair, shared, by_pair, shared,
                      pl.BlockSpec((None, nb, LANES, blk), lambda b, p, *_: (b, 0, p, 0))],
            out_specs=by_pair,
            scratch_shapes=[
                pltpu.VMEM((nb * n_tiles, 1, ATTN_TILE), F32),
                pltpu.VMEM((nb * n_tiles, FOX_HEAD_DIM + SUM_ROWS, ATTN_TILE), F32),
                pltpu.VMEM((n_tiles, blk, ATTN_TILE), F32),
                pltpu.VMEM((n_tiles, 1, ATTN_TILE), F32),
                pltpu.VMEM((n_tiles, blk, ATTN_TILE), F32),
                pltpu.VMEM((n_tiles, 1, ATTN_TILE), F32),
            ],
        ),
        out_shape=jax.ShapeDtypeStruct((B, L, FOX_WIDTH), F32),
        compiler_params=_params(2),
        name="fox_attn_prompt",
    )(n_off, vq, vj, q, f, kb, e, vt)


def _attn_sample_kernel(q_ref, kn_ref, vn_ref, ck_ref, cv_ref, c_ref, ct_ref, o_ref, *, past, ls):
    hd = FOX_HEAD_DIM
    pad_k = jnp.zeros((LANES - ls, LANES), BF16)
    cqb = c_ref[past:past + ls, :]
    lane = lax.broadcasted_iota(jnp.int32, cqb.shape, 1)
    r = lax.broadcasted_iota(jnp.int32, (2 * ls, LANES), 0) & (ls - 1)
    c = lax.broadcasted_iota(jnp.int32, (2 * ls, LANES), 1)
    lane_o = lax.broadcasted_iota(jnp.int32, (ls, LANES), 1)

    for p in range(FOX_PAIRS):
        cols = slice(p * LANES, (p + 1) * LANES)
        qst = _stack_heads(q_ref[:, cols])
        kc_t = ck_ref[cols, :].astype(BF16)
        vc_t = cv_ref[cols, :].astype(BF16)
        kn = jnp.concatenate([kn_ref[:, cols], pad_k], axis=0)
        vn = jnp.concatenate([vn_ref[:, cols].astype(BF16), pad_k], axis=0)
        cq = jnp.concatenate(
            [jnp.sum(jnp.where(lane == 2 * p + h2, cqb, 0.0), axis=1, keepdims=True) for h2 in range(2)],
            axis=0)
        cka = ct_ref[2 * p:2 * p + 1, :]
        ckb = ct_ref[2 * p + 1:2 * p + 2, :]

        def bias(lo, hi):
            return jnp.concatenate([cq[:ls] - cka[:, lo:hi], cq[ls:] - ckb[:, lo:hi]], axis=0)

        s_c = _dot(qst, kc_t) + bias(0, past)
        s_n = _dot_nt(qst, kn) + bias(past, past + LANES)
        s_n = jnp.where(c <= r, s_n, NEG_INF)
        m = jnp.maximum(jnp.max(s_c, axis=1, keepdims=True), jnp.max(s_n, axis=1, keepdims=True))
        p_c = jnp.exp2(s_c - m)
        p_n = jnp.exp2(s_n - m)
        l = jnp.sum(p_c, axis=1, keepdims=True) + jnp.sum(p_n, axis=1, keepdims=True)
        o = (_dot_nt(p_c.astype(BF16), vc_t) + _dot(p_n.astype(BF16), vn)) / l
        o_ref[:, cols] = jnp.where(lane_o < hd, o[:ls], o[ls:])


def _attn_sample(q, kb, v, cache_k, cache_v, c, ct, *, past):
    B, ls, _ = q.shape
    tot = c.shape[1]
    rows = lambda n: pl.BlockSpec((None, n, FOX_WIDTH), lambda b: (b, 0, 0))
    cache_t = pl.BlockSpec((None, FOX_WIDTH, past), lambda b: (b, 0, 0))
    return pl.pallas_call(
        functools.partial(_attn_sample_kernel, past=past, ls=ls),
        grid=(B,),
        in_specs=[
            rows(ls), rows(ls), rows(ls), cache_t, cache_t,
            pl.BlockSpec((None, tot, FOX_HEADS), lambda b: (b, 0, 0)),
            pl.BlockSpec((None, FOX_HEADS, tot), lambda b: (b, 0, 0)),
        ],
        out_specs=rows(ls),
        out_shape=jax.ShapeDtypeStruct((B, ls, FOX_WIDTH), F32),
        compiler_params=_params(1),
        name="fox_attn_sample",
    )(q, kb, v, cache_k, cache_v, c, ct)


def _mid_kernel(o_ref, sg_ref, x_ref, wo_ref, g_ref, w_ref, wa1_ref, wa2_ref, ba_ref,
                y_ref, q_ref, k_ref, v_ref, sg2_ref, ga_ref):
    u = (o_ref[...] * sg_ref[...]).astype(BF16)
    y = x_ref[...] + _dot(u, wo_ref[...])
    y_ref[...] = y
    hb = _rms(y, g_ref[...]).astype(BF16)
    kw = GLA_KEY_WIDTH
    q_ref[...] = _dot(hb, w_ref[:, 0:kw]) * (GLA_HEAD_K ** -0.5)
    k_ref[...] = _dot(hb, w_ref[:, kw:2 * kw])
    v_ref[...] = _dot(hb, w_ref[:, 2 * kw:2 * kw + GLA_VAL_WIDTH]).astype(BF16)
    gate = _dot(hb, w_ref[:, 2 * kw + GLA_VAL_WIDTH:2 * kw + 2 * GLA_VAL_WIDTH])
    sg2_ref[...] = gate * jax.nn.sigmoid(gate)
    a1 = _dot(hb, wa1_ref[...]).astype(BF16)
    a = _dot(a1, wa2_ref[...]) + ba_ref[...]
    ga_ref[...] = _log_sigmoid(a) * (1.0 / GLA_GATE_TEMP)


def _mid(o, sg, x, w_out, g, w_main, w_a1, w_a2, b_a, *, tm):
    T = o.shape[0]
    tok = lambda w: pl.BlockSpec((tm, w), lambda i: (i, 0))
    full = lambda a: pl.BlockSpec(a.shape, lambda i: (0,) * a.ndim)
    return pl.pallas_call(
        _mid_kernel,
        grid=(T // tm,),
        in_specs=[tok(D_MODEL), tok(D_MODEL), tok(D_MODEL), full(w_out), full(g), full(w_main),
                  full(w_a1), full(w_a2), full(b_a)],
        out_specs=[tok(D_MODEL), tok(GLA_KEY_WIDTH), tok(GLA_KEY_WIDTH), tok(GLA_VAL_WIDTH),
                   tok(GLA_VAL_WIDTH), tok(GLA_KEY_WIDTH)],
        out_shape=[
            jax.ShapeDtypeStruct((T, D_MODEL), F32),
            jax.ShapeDtypeStruct((T, GLA_KEY_WIDTH), F32),
            jax.ShapeDtypeStruct((T, GLA_KEY_WIDTH), F32),
            jax.ShapeDtypeStruct((T, GLA_VAL_WIDTH), BF16),
            jax.ShapeDtypeStruct((T, GLA_VAL_WIDTH), F32),
            jax.ShapeDtypeStruct((T, GLA_KEY_WIDTH), F32),
        ],
        compiler_params=_params(1),
        name="fox_out_gla_proj",
    )(o, sg, x, w_out, g, w_main, w_a1, w_a2, b_a)


def _gla_kernel(q_ref, k_ref, v_ref, ga_ref, s0_ref, o_ref, s_ref, *, n_chunks, chained):
    i = pl.program_id(1)
    ch = GLA_CHUNK
    dk, dv = GLA_HEAD_K, GLA_HEAD_V

    @pl.when(i == 0)
    def _():
        s_ref[...] = s0_ref[...]

    t = n_chunks * ch
    r = lax.broadcasted_iota(jnp.int32, (t, t), 0)
    c = lax.broadcasted_iota(jnp.int32, (t, t), 1)
    causal = ((r >> 6) == (c >> 6)) & (r >= c)
    tril = jnp.where(causal, 1.0, 0.0).astype(BF16)
    eye = (lax.broadcasted_iota(jnp.int32, (dk, dk), 0) == lax.broadcasted_iota(jnp.int32, (dk, dk), 1))

    b = _tril_cumsum(tril, ga_ref[...])
    lasts = [b[(ci + 1) * ch - 1:(ci + 1) * ch, :] for ci in range(n_chunks)]
    b_last = jnp.concatenate([jnp.broadcast_to(x, (ch, GLA_KEY_WIDTH)) for x in lasts], axis=0)
    kk = k_ref[...]
    qe = (q_ref[...] * jnp.exp(b)).astype(BF16)
    ke = (kk * jnp.exp(-b)).astype(BF16)
    kd = (kk * jnp.exp(b_last - b)).astype(BF16)
    decs = [jnp.exp(x) for x in lasts]

    for h in range(GLA_HEADS):
        ks = slice(h * dk, (h + 1) * dk)
        vs = slice(h * dv, (h + 1) * dv)
        vh = v_ref[:, vs]
        a = jnp.where(causal, _dot_nt(qe[:, ks], ke[:, ks]), 0.0)
        o_intra = _dot(a.astype(BF16), vh)
        s = s_ref[0, h]
        for ci in range(n_chunks):
            rows = slice(ci * ch, (ci + 1) * ch)
            if not chained:
                s = s_ref[ci, h]
            o_ref[rows, vs] = o_intra[rows] + _dot(qe[rows, ks], s.astype(BF16))
            dec_col = jnp.sum(jnp.where(eye, jnp.broadcast_to(decs[ci][:, ks], (dk, dk)), 0.0),
                              axis=1, keepdims=True)
            s = dec_col * s + _dot_tn(kd[rows, ks], vh[rows])
            if not chained:
                s_ref[ci, h] = s
        if chained:
            s_ref[0, h] = s


def _gla(q, k, v, ga, s0, *, tc):
    B, L, _ = q.shape
    n_chunks = tc // GLA_CHUNK
    chained = L > GLA_CHUNK
    if chained:
        n_states = 1
    else:
        n_states = n_chunks
        fold = lambda a: a.reshape(B // n_chunks, tc, a.shape[-1])
        q, k, v, ga = fold(q), fold(k), fold(v), fold(ga)
    G, T, _ = q.shape
    tok = lambda w: pl.BlockSpec((None, tc, w), lambda b, i: (b, i, 0))
    st = pl.BlockSpec((n_states, GLA_HEADS, GLA_HEAD_K, GLA_HEAD_V), lambda b, i: (b, 0, 0, 0))
    o, s = pl.pallas_call(
        functools.partial(_gla_kernel, n_chunks=n_chunks, chained=chained),
        grid=(G, T // tc),
        in_specs=[tok(GLA_KEY_WIDTH), tok(GLA_KEY_WIDTH), tok(GLA_VAL_WIDTH), tok(GLA_KEY_WIDTH), st],
        out_specs=[tok(GLA_VAL_WIDTH), st],
        out_shape=[
            jax.ShapeDtypeStruct((G, T, GLA_VAL_WIDTH), F32),
            jax.ShapeDtypeStruct((B, GLA_HEADS, GLA_HEAD_K, GLA_HEAD_V), F32),
        ],
        compiler_params=_params(2),
        name="gla_chunk",
    )(q, k, v, ga, s0)
    return o.reshape(B, L, GLA_VAL_WIDTH), s


def _gla_out_kernel(o_ref, sg_ref, y_ref, go_ref, wo_ref, gf_ref, out_ref, *, chunk_major):
    dv = GLA_HEAD_V
    if chunk_major:
        o = jnp.concatenate([o_ref[:, t, :] for t in range(o_ref.shape[1])], axis=0)
    else:
        o = o_ref[...]
    parts = []
    for h in range(GLA_HEADS):
        vs = slice(h * dv, (h + 1) * dv)
        parts.append((_rms(o[:, vs], go_ref[...]) * sg_ref[:, vs]).astype(BF16))
    u = jnp.concatenate(parts, axis=1)
    y = y_ref[...] + _dot(u, wo_ref[...])
    out_ref[...] = _rms(y, gf_ref[...])


def _gla_out(o, sg, y, g_o, w_out, g_f, *, tm):
    B, L, _ = o.shape
    n = L // GLA_CHUNK
    chunk_major = n > 1
    if chunk_major:
        tm = 8 * n
        o = o.reshape(B, n, GLA_CHUNK, D_MODEL)
        o_spec = pl.BlockSpec((None, n, 8, D_MODEL), lambda b, i: (b, 0, i, 0))
    else:
        flat = lambda a: a.reshape(1, B * L, D_MODEL)
        o, sg, y = flat(o), flat(sg), flat(y)
        o_spec = pl.BlockSpec((None, tm, D_MODEL), lambda b, i: (b, i, 0))
    nb, rows, _ = sg.shape
    tok = pl.BlockSpec((None, tm, D_MODEL), lambda b, i: (b, i, 0))
    full = lambda a: pl.BlockSpec(a.shape, lambda b, i: (0,) * a.ndim)
    return pl.pallas_call(
        functools.partial(_gla_out_kernel, chunk_major=chunk_major),
        grid=(nb, rows // tm),
        in_specs=[o_spec, tok, tok, full(g_o), full(w_out), full(g_f)],
        out_specs=tok,
        out_shape=jax.ShapeDtypeStruct((nb, rows, D_MODEL), F32),
        compiler_params=_params(2),
        name="gla_out",
    )(o, sg, y, g_o, w_out, g_f)


def _pad_cols(w, n):
    return jnp.pad(w, ((0, 0), (0, n - w.shape[1])))


def kernel(x_prompt, x_sample, cache_fox_k, cache_fox_v, cache_fox_logf, state_gla,
           g_norm_fox, w_in_fox, b_fox_f, w_out_fox,
           g_norm_gla, w_in_gla, w_gla_a2, b_gla_a, g_gla_o, w_out_gla, g_final):
    B, L, D = x_prompt.shape
    Bs, Ls, _ = x_sample.shape
    P = cache_fox_k.shape[1]
    H, hd = FOX_HEADS, FOX_HEAD_DIM

    row = lambda a: a.reshape(1, -1).astype(F32)
    w_fox = w_in_fox[:, :4 * FOX_WIDTH].astype(BF16)
    w_fox_f = _pad_cols(w_in_fox[:, 4 * FOX_WIDTH:], LANES).astype(BF16)
    b_f = _pad_cols(row(b_fox_f), LANES)
    w_o_fox = w_out_fox.astype(BF16)
    n_main = 2 * GLA_KEY_WIDTH + 2 * GLA_VAL_WIDTH
    w_gla = w_in_gla[:, :n_main].astype(BF16)
    w_a1 = _pad_cols(w_in_gla[:, n_main:], LANES).astype(BF16)
    w_a2 = jnp.pad(w_gla_a2, ((0, LANES - GLA_GATE_RANK), (0, 0))).astype(BF16)
    w_o_gla = w_out_gla.astype(BF16)

    q_p, k_p, v_p, kb_p, sg_p, lf_p, vt_p, nrm_p = _fox_proj(
        x_prompt, row(g_norm_fox), w_fox, w_fox_f, b_f, tm=ATTN_BLOCK, want_vt=True)
    e_p, f_p, rng_p = _bias_tiles(lf_p, rows=ATTN_BLOCK)
    first_p = _skip_plan(nrm_p[:, :, 0, :H], rng_p[:, :, 0, :], nrm_p[:, :, 1, :H], rng_p[:, :, 1, :])
    o_p = _attn_prompt(first_p, q_p, f_p, kb_p, e_p, vt_p, blk=ATTN_BLOCK)

    xs = x_sample.reshape(1, Bs * Ls, D)
    q_s, k_s, v_s, kb_s, sg_s, lf_s = _fox_proj(
        xs, row(g_norm_fox), w_fox, w_fox_f, b_f, tm=PROJ_TM, want_vt=False)
    sh = lambda a: a.reshape(Bs, Ls, a.shape[-1])
    q_s, k_s, v_s, kb_s, lf_s = sh(q_s), sh(k_s), sh(v_s), sh(kb_s), sh(lf_s)
    lf_all = jnp.concatenate(
        [cache_fox_logf.astype(F32), lf_s, jnp.zeros((Bs, LANES - Ls, H), F32)], axis=1)
    c_s = _cumsum_short(lf_all, rows=(P + LANES) // 3, per_step=4)
    cache_t = lambda a: jnp.transpose(a, (0, 2, 3, 1)).reshape(Bs, FOX_WIDTH, P)
    o_s = _attn_sample(q_s, kb_s, v_s, cache_t(cache_fox_k), cache_t(cache_fox_v),
                       c_s, jnp.swapaxes(c_s, 1, 2), past=P)

    def gla_layer(o, sg, x, s0, tc):
        b, l, _ = x.shape
        flat = lambda a: a.reshape(b * l, a.shape[-1])
        y1, q, k, v, sg2, ga = _mid(flat(o), flat(sg), flat(x), w_o_fox, row(g_norm_gla), w_gla,
                                    w_a1, w_a2, row(b_gla_a), tm=PROJ_TM)
        un = lambda a: a.reshape(b, l, a.shape[-1])
        og, s_out = _gla(un(q), un(k), un(v), un(ga), s0, tc=tc)
        y = _gla_out(og, un(sg2), un(y1), row(g_gla_o), w_o_gla, row(g_final), tm=PROJ_TM)
        return y.reshape(b, l, D), s_out

    s0_p = jnp.zeros((B, GLA_HEADS, GLA_HEAD_K, GLA_HEAD_V), F32)
    y_p, s_p = gla_layer(o_p, sg_p, x_prompt, s0_p, GLA_TC)
    y_s, s_s = gla_layer(o_s, sg_s.reshape(Bs, Ls, D), x_sample, state_gla.astype(F32), GLA_TC)

    return (y_p, y_s,
            k_p.reshape(B, L, H, hd), v_p.reshape(B, L, H, hd), lf_p, s_p.astype(state_gla.dtype),
            k_s.reshape(Bs, Ls, H, hd), v_s.reshape(Bs, Ls, H, hd), lf_s, s_s.astype(state_gla.dtype))
```

```python
import functools

import jax
import jax.numpy as jnp
from jax import lax
from jax.experimental import pallas as pl
from jax.experimental.pallas import tpu as pltpu

F32 = jnp.float32
BF16 = jnp.bfloat16

D_MODEL = 1024
EPS = 1e-6
NEG_INF = -1e30
LOG2E = 1.4426950408889634
NORM_SLACK = 1.01
SKIP_LOG2 = 152.0

FOX_HEADS = 16
FOX_HEAD_DIM = 64
FOX_WIDTH = FOX_HEADS * FOX_HEAD_DIM
FOX_PAIRS = FOX_HEADS // 2

GLA_HEADS = 4
GLA_KEY_WIDTH = 512
GLA_VAL_WIDTH = 1024
GLA_HEAD_K = 128
GLA_HEAD_V = 256
GLA_GATE_RANK = 16
GLA_GATE_TEMP = 16.0
GLA_CHUNK = 64

LANES = 128
VMEM_LIMIT = 56 * 1024 * 1024

PROJ_TM = 512
ATTN_BLOCK = 512
GLA_TC = 256


def _params(n_axes, flags=None):
    return pltpu.CompilerParams(
        dimension_semantics=("arbitrary",) * n_axes,
        vmem_limit_bytes=VMEM_LIMIT,
        flags=flags,
    )


def _dot(a, b):
    return jnp.dot(a, b, preferred_element_type=F32)


def _dot_nt(a, b):
    return lax.dot_general(a, b, (((1,), (1,)), ((), ())), preferred_element_type=F32)


def _dot_tn(a, b):
    return lax.dot_general(a, b, (((0,), (0,)), ((), ())), preferred_element_type=F32)


def _log_sigmoid(z):
    return jnp.minimum(z, 0.0) - jnp.log1p(jnp.exp(-jnp.abs(z)))


def _rms(x, g):
    ms = jnp.mean(x * x, axis=-1, keepdims=True)
    return (x * lax.rsqrt(ms + EPS)) * g


def _split3(x):
    hi = x.astype(BF16)
    r1 = x - hi.astype(F32)
    mid = r1.astype(BF16)
    lo = (r1 - mid.astype(F32)).astype(BF16)
    return hi, mid, lo


def _tril_cumsum(tril_bf16, x):
    hi, mid, lo = _split3(x)
    return (_dot(tril_bf16, hi) + _dot(tril_bf16, mid)) + _dot(tril_bf16, lo)


def _tril(n):
    r = lax.broadcasted_iota(jnp.int32, (n, n), 0)
    c = lax.broadcasted_iota(jnp.int32, (n, n), 1)
    return jnp.where(r >= c, 1.0, 0.0).astype(BF16)


def _fox_proj_kernel(x_ref, g_ref, w_ref, wf_ref, bf_ref, *rest, want_vt):
    if want_vt:
        q_ref, k_ref, v_ref, kb_ref, sg_ref, lf_ref, vt_ref, nrm_ref = rest
    else:
        q_ref, k_ref, v_ref, kb_ref, sg_ref, lf_ref = rest
    hb = _rms(x_ref[...], g_ref[...]).astype(BF16)
    q = _dot(hb, w_ref[:, 0:FOX_WIDTH])
    qb = (q * (LOG2E * FOX_HEAD_DIM ** -0.5)).astype(BF16)
    q_ref[...] = qb
    k = _dot(hb, w_ref[:, FOX_WIDTH:2 * FOX_WIDTH])
    k_ref[...] = k
    kb = k.astype(BF16)
    kb_ref[...] = kb
    if want_vt:
        row = lax.broadcasted_iota(jnp.int32, (FOX_WIDTH, LANES), 0)
        col = lax.broadcasted_iota(jnp.int32, (FOX_WIDTH, LANES), 1)
        head_of = jnp.where((row >> 6) == col, 1.0, 0.0).astype(BF16)

        def bound(xb):
            xf = xb.astype(F32)
            sq = _dot((xf * xf).astype(BF16), head_of)
            return jnp.sqrt(jnp.max(sq, axis=0, keepdims=True)) * NORM_SLACK

        nrm_ref[...] = jnp.concatenate(
            [bound(kb), bound(qb), jnp.zeros((6, LANES), F32)], axis=0)
    v = _dot(hb, w_ref[:, 2 * FOX_WIDTH:3 * FOX_WIDTH])
    v_ref[...] = v
    if want_vt:
        vt_ref[...] = v.astype(BF16).T
    gate = _dot(hb, w_ref[:, 3 * FOX_WIDTH:4 * FOX_WIDTH])
    sg_ref[...] = gate * jax.nn.sigmoid(gate)
    fl = _dot(hb, wf_ref[...]) + bf_ref[...]
    lf_ref[...] = _log_sigmoid(fl)[:, :FOX_HEADS]


def _fox_proj(x, g, w_main, w_f, b_f, *, tm, want_vt):
    B, L, _ = x.shape
    n = L // tm
    tok = lambda w: pl.BlockSpec((None, tm, w), lambda b, i: (b, i, 0))
    full = lambda a: pl.BlockSpec(a.shape, lambda b, i: (0,) * a.ndim)
    in_specs = [tok(D_MODEL), full(g), full(w_main), full(w_f), full(b_f)]
    args = [x, g, w_main, w_f, b_f]
    out_specs = [tok(FOX_WIDTH)] * 5 + [tok(FOX_HEADS)]
    out_shape = [
        jax.ShapeDtypeStruct((B, L, FOX_WIDTH), BF16),
        jax.ShapeDtypeStruct((B, L, FOX_WIDTH), F32),
        jax.ShapeDtypeStruct((B, L, FOX_WIDTH), F32),
        jax.ShapeDtypeStruct((B, L, FOX_WIDTH), BF16),
        jax.ShapeDtypeStruct((B, L, FOX_WIDTH), F32),
        jax.ShapeDtypeStruct((B, L, FOX_HEADS), F32),
    ]
    if want_vt:
        out_specs.append(pl.BlockSpec((None, None, FOX_WIDTH, tm), lambda b, i: (b, i, 0, 0)))
        out_shape.append(jax.ShapeDtypeStruct((B, n, FOX_WIDTH, tm), BF16))
        out_specs.append(pl.BlockSpec((None, None, 8, LANES), lambda b, i: (b, i, 0, 0)))
        out_shape.append(jax.ShapeDtypeStruct((B, n, 8, LANES), F32))
    return pl.pallas_call(
        functools.partial(_fox_proj_kernel, want_vt=want_vt),
        grid=(B, n),
        in_specs=in_specs,
        out_specs=out_specs,
        out_shape=out_shape,
        compiler_params=_params(2),
        name="fox_proj_vt" if want_vt else "fox_proj",
    )(*args)


BIAS_SLOT = 8


def _bias_selectors():
    shape = (FOX_HEADS, LANES)
    head = lax.broadcasted_iota(jnp.int32, shape, 0)
    col = lax.broadcasted_iota(jnp.int32, shape, 1)
    slot = BIAS_SLOT * head
    sel_f = [jnp.where(col == slot + x, 1.0, 0.0).astype(BF16) for x in range(3)]
    sel_e = [jnp.where(col == slot + 3 + x, -1.0, 0.0).astype(BF16) for x in range(3)]
    lane = lax.broadcasted_iota(jnp.int32, (1, LANES), 1) & (BIAS_SLOT - 1)
    one_f = jnp.where((lane >= 3) & (lane < 6), 1.0, 0.0)
    one_e = jnp.where(lane < 3, 1.0, 0.0)
    return sel_f, sel_e, one_f, one_e


def _bias_kernel(lf_ref, e_ref, f_ref, rng_ref, carry_ref, *, rows):
    tril = _tril(rows)
    sel_f, sel_e, one_f, one_e = _bias_selectors()

    @pl.when(pl.program_id(1) == 0)
    def _():
        carry_ref[...] = jnp.zeros_like(carry_ref)

    cs = _tril_cumsum(tril, lf_ref[...]) + carry_ref[...]
    carry_ref[...] = cs[rows - 1:rows, :]
    c2 = cs * LOG2E
    hi, mid, lo = _split3(c2)
    f_ref[...] = ((_dot(hi, sel_f[0]) + _dot(mid, sel_f[1])) + _dot(lo, sel_f[2]) + one_f).astype(BF16)
    e_ref[...] = ((_dot(hi, sel_e[0]) + _dot(mid, sel_e[1])) + _dot(lo, sel_e[2]) + one_e).astype(BF16)
    rng_ref[...] = jnp.concatenate(
        [jnp.min(c2, axis=0, keepdims=True), jnp.max(c2, axis=0, keepdims=True),
         jnp.zeros((6, FOX_HEADS), F32)], axis=0)


def _bias_tiles(lf, *, rows):
    B, L, H = lf.shape
    spec = pl.BlockSpec((None, rows, H), lambda b, i: (b, i, 0))
    wide = pl.BlockSpec((None, rows, LANES), lambda b, i: (b, i, 0))
    return pl.pallas_call(
        functools.partial(_bias_kernel, rows=rows),
        grid=(B, L // rows),
        in_specs=[spec],
        out_specs=[wide, wide, pl.BlockSpec((None, None, 8, H), lambda b, i: (b, i, 0, 0))],
        out_shape=[jax.ShapeDtypeStruct((B, L, LANES), BF16)] * 2
        + [jax.ShapeDtypeStruct((B, L // rows, 8, H), F32)],
        scratch_shapes=[pltpu.VMEM((1, FOX_HEADS), F32)],
        compiler_params=_params(2),
        name="fox_cumsum_bias",
    )(lf)


def _cumsum_short_kernel(lf_ref, c_ref, *, rows):
    tril = _tril(rows)
    nbatch, length, _ = lf_ref.shape
    for bi in range(nbatch):
        carry = jnp.zeros((1, FOX_HEADS), F32)
        for r0 in range(0, length, rows):
            cs = _tril_cumsum(tril, lf_ref[bi, r0:r0 + rows, :]) + carry
            c_ref[bi, r0:r0 + rows, :] = cs * LOG2E
            carry = cs[rows - 1:rows, :]


def _cumsum_short(lf, *, rows, per_step):
    B, L, H = lf.shape
    spec = pl.BlockSpec((per_step, L, H), lambda b: (b, 0, 0))
    return pl.pallas_call(
        functools.partial(_cumsum_short_kernel, rows=rows),
        grid=(B // per_step,),
        in_specs=[spec],
        out_specs=spec,
        out_shape=jax.ShapeDtypeStruct((B, L, H), F32),
        compiler_params=_params(1),
        name="fox_cumsum",
    )(lf)


def _skip_plan_kernel(kn_ref, cmin_ref, qn_ref, cmax_ref, knq_ref, first_ref, *, nb):
    j = lax.broadcasted_iota(jnp.int32, (nb, nb), 0).astype(F32)
    q = lax.broadcasted_iota(jnp.int32, (nb, nb), 1).astype(F32)
    rows = []
    for p in range(FOX_PAIRS):
        skippable = None
        for h in (2 * p, 2 * p + 1):
            qn = qn_ref[h:h + 1, :]
            bound = (kn_ref[:, h:h + 1] * qn + (cmax_ref[h:h + 1, :] - cmin_ref[:, h:h + 1])
                     + qn * knq_ref[h:h + 1, :])
            ok = bound < -SKIP_LOG2
            skippable = ok if skippable is None else (skippable & ok)
        must = (j < q) & jnp.logical_not(skippable)
        rows.append(jnp.min(jnp.where(must, j, q), axis=0, keepdims=True))
    first_ref[...] = jnp.concatenate(rows, axis=0).astype(jnp.int32)


def _skip_plan(kn, cmin, qn, cmax):
    B, nb, H = kn.shape
    by_block = pl.BlockSpec((None, nb, H), lambda b: (b, 0, 0))
    by_head = pl.BlockSpec((None, H, nb), lambda b: (b, 0, 0))
    t = lambda a: jnp.swapaxes(a, 1, 2)
    return pl.pallas_call(
        functools.partial(_skip_plan_kernel, nb=nb),
        grid=(B,),
        in_specs=[by_block, by_block, by_head, by_head, by_head],
        out_specs=pl.BlockSpec((None, FOX_PAIRS, nb), lambda b: (b, 0, 0)),
        out_shape=jax.ShapeDtypeStruct((B, FOX_PAIRS, nb), jnp.int32),
        compiler_params=_params(1),
        name="fox_skip_plan",
    )(kn, cmin, t(qn), t(cmax), t(kn))


def _stack_heads(q2):
    lane = lax.broadcasted_iota(jnp.int32, q2.shape, 1)
    zero = jnp.zeros_like(q2)
    return jnp.concatenate(
        [jnp.where(lane < FOX_HEAD_DIM, q2, zero), jnp.where(lane >= FOX_HEAD_DIM, q2, zero)], axis=0)


ATTN_TILE = 256
SUM_ROWS = 16


def _attn_prompt_kernel(n_ref, vq_ref, vj_ref, q_ref, f_ref, k_ref, e_ref, vt_ref, o_ref,
                        m_ref, acc_ref, sa_ref, xa_ref, sb_ref, xb_ref, *, blk, nb):
    b_id = pl.program_id(0)
    p_id = pl.program_id(1)
    hd = FOX_HEAD_DIM
    tw = ATTN_TILE
    per_head = blk // tw
    n_tiles = 2 * per_head

    lane = lax.broadcasted_iota(jnp.int32, (blk, LANES), 1)
    zero = jnp.zeros((blk, LANES), BF16)
    slot_a = 2 * BIAS_SLOT * p_id
    slot_b = slot_a + BIAS_SLOT
    ones_rows = jnp.ones((SUM_ROWS, blk), BF16)

    m_ref[...] = jnp.full(m_ref.shape, NEG_INF, F32)
    acc_ref[...] = jnp.zeros(acc_ref.shape, F32)

    def query_tiles(qi):
        r0 = pl.multiple_of(qi * blk, blk)
        q2 = q_ref[pl.ds(r0, blk), :]
        f2 = f_ref[pl.ds(r0, blk), :]
        heads = [
            jnp.concatenate([jnp.where(lane < hd, q2, zero),
                             jnp.where((lane >= slot_a) & (lane < slot_b), f2, zero)], axis=1),
            jnp.concatenate([jnp.where(lane >= hd, q2, zero),
                             jnp.where((lane >= slot_b) & (lane < slot_b + BIAS_SLOT), f2, zero)], axis=1),
        ]
        return [heads[t // per_head][(t % per_head) * tw:(t % per_head + 1) * tw] for t in range(n_tiles)]

    def stage(nxt, cur, diagonal):
        if nxt is not None:
            qi_n, j_n, (sn_ref, xn_ref) = nxt
            q_tiles = query_tiles(qi_n)
            k0 = pl.multiple_of(j_n * blk, blk)
            kaug = jnp.concatenate([k_ref[pl.ds(k0, blk), :], e_ref[pl.ds(k0, blk), :]], axis=1)
        if cur is not None:
            qi_c, j_c, (sc_ref, xc_ref) = cur
            vt = vt_ref[j_c]
            v_heads = [jnp.concatenate([vt[h * hd:(h + 1) * hd, :], ones_rows], axis=0) for h in range(2)]
        for t in range(n_tiles):
            c0 = (t % per_head) * tw
            live = c0 + tw if diagonal else blk
            if nxt is not None:
                s = _dot_nt(kaug[:live], q_tiles[t])
                if diagonal:
                    r = lax.broadcasted_iota(jnp.int32, (live, tw), 0)
                    c = lax.broadcasted_iota(jnp.int32, (live, tw), 1) + c0
                    s = jnp.where(r <= c, s, NEG_INF)
                sn_ref[t, 0:live, :] = s
                xn_ref[t] = jnp.max(s, axis=0, keepdims=True)
            if cur is not None:
                idx = qi_c * n_tiles + t
                m_old = m_ref[idx]
                m_new = jnp.maximum(m_old, xc_ref[t])
                alpha = jnp.exp2(m_old - m_new)
                p = jnp.exp2(sc_ref[t, 0:live, :] - m_new)
                m_ref[idx] = m_new
                pv = _dot(v_heads[t // per_head][:, 0:live], p.astype(BF16))
                acc_ref[idx] = acc_ref[idx] * alpha + pv

    buf_a = (sa_ref, xa_ref)
    buf_b = (sb_ref, xb_ref)

    stage((0, 0, buf_a), None, True)

    def diag_pair(n, carry):
        i0 = 2 * n
        i2 = jnp.minimum(i0 + 2, nb - 1)
        stage((i0 + 1, i0 + 1, buf_b), (i0, i0, buf_a), True)
        stage((i2, i2, buf_a), (i0 + 1, i0 + 1, buf_b), True)
        return carry

    lax.fori_loop(0, nb // 2, diag_pair, 0)

    n_off = n_ref[b_id, p_id]
    visit = lambda v, buf: (vq_ref[b_id, p_id, v], vj_ref[b_id, p_id, v], buf)

    @pl.when(n_off > 0)
    def _():
        stage(visit(0, buf_a), None, False)

    def off_pair(n, carry):
        v0 = 2 * n
        stage(visit(v0 + 1, buf_b), visit(v0, buf_a), False)
        stage(visit(v0 + 2, buf_a), visit(v0 + 1, buf_b), False)
        return carry

    lax.fori_loop(0, jnp.maximum(n_off - 1, 0) // 2, off_pair, 0)

    @pl.when((n_off > 0) & (n_off % 2 == 0))
    def _():
        stage(visit(n_off - 1, buf_b), visit(n_off - 2, buf_a), False)
        stage(None, visit(n_off - 1, buf_b), False)

    @pl.when(n_off % 2 == 1)
    def _():
        stage(None, visit(n_off - 1, buf_a), False)

    def finish(qi, carry):
        halves = [jnp.concatenate([acc_ref[qi * n_tiles + t, 0:hd, :] / acc_ref[qi * n_tiles + t, hd:hd + 1, :]
                                   for t in range(h * per_head, (h + 1) * per_head)], axis=1)
                  for h in range(2)]
        o_ref[pl.ds(pl.multiple_of(qi * blk, blk), blk), :] = jnp.concatenate(halves, axis=0).T
        return carry

    lax.fori_loop(0, nb, finish, 0)


def _visit_lists(first):
    B, P, nb = first.shape
    vmax = nb * (nb - 1) // 2 + 8
    q_idx = jnp.arange(nb, dtype=jnp.int32)
    cnt = q_idx - first
    incl = jnp.cumsum(cnt, axis=-1)
    excl = incl - cnt
    v = jnp.arange(vmax, dtype=jnp.int32)
    vq = jnp.minimum(jnp.sum(v[None, None, :, None] >= incl[:, :, None, :], axis=-1), nb - 1).astype(jnp.int32)
    pick = vq[..., None] == q_idx
    take = lambda a: jnp.sum(jnp.where(pick, a[:, :, None, :], 0), axis=-1)
    vj = jnp.clip(take(first) + (v - take(excl)), 0, nb - 1).astype(jnp.int32)
    return incl[..., -1].astype(jnp.int32), vq, vj


def _attn_prompt(first, q, f, kb, e, vt, *, blk):
    B, L, _ = q.shape
    nb = L // blk
    n_tiles = 2 * blk // ATTN_TILE
    n_off, vq, vj = _visit_lists(first)
    by_pair = pl.BlockSpec((None, L, LANES), lambda b, p, *_: (b, 0, p))
    shared = pl.BlockSpec((None, L, LANES), lambda b, p, *_: (b, 0, 0))
    return pl.pallas_call(
        functools.partial(_attn_prompt_kernel, blk=blk, nb=nb),
        grid_spec=pltpu.PrefetchScalarGridSpec(
            num_scalar_prefetch=3,
            grid=(B, FOX_PAIRS),
            in_specs=[by_pair, shared, by_pair, shared,
                      pl.BlockSpec((None, nb, LANES, blk), lambda b, p, *_: (b, 0, p, 0))],
            out_specs=by_pair,
            scratch_shapes=[
                pltpu.VMEM((nb * n_tiles, 1, ATTN_TILE), F32),
                pltpu.VMEM((nb * n_tiles, FOX_HEAD_DIM + SUM_ROWS, ATTN_TILE), F32),
                pltpu.VMEM((n_tiles, blk, ATTN_TILE), F32),
                pltpu.VMEM((n_tiles, 1, ATTN_TILE), F32),
                pltpu.VMEM((n_tiles, blk, ATTN_TILE), F32),
                pltpu.VMEM((n_tiles, 1, ATTN_TILE), F32),
            ],
        ),
        out_shape=jax.ShapeDtypeStruct((B, L, FOX_WIDTH), F32),
        compiler_params=_params(2),
        name="fox_attn_prompt",
    )(n_off, vq, vj, q, f, kb, e, vt)


def _attn_sample_kernel(q_ref, kn_ref, vn_ref, ck_ref, cv_ref, c_ref, ct_ref, o_ref, *, past, ls):
    hd = FOX_HEAD_DIM
    pad_k = jnp.zeros((LANES - ls, LANES), BF16)
    cqb = c_ref[past:past + ls, :]
    lane = lax.broadcasted_iota(jnp.int32, cqb.shape, 1)
    r = lax.broadcasted_iota(jnp.int32, (2 * ls, LANES), 0) & (ls - 1)
    c = lax.broadcasted_iota(jnp.int32, (2 * ls, LANES), 1)
    lane_o = lax.broadcasted_iota(jnp.int32, (ls, LANES), 1)

    for p in range(FOX_PAIRS):
        cols = slice(p * LANES, (p + 1) * LANES)
        qst = _stack_heads(q_ref[:, cols])
        kc_t = ck_ref[cols, :].astype(BF16)
        vc_t = cv_ref[cols, :].astype(BF16)
        kn = jnp.concatenate([kn_ref[:, cols], pad_k], axis=0)
        vn = jnp.concatenate([vn_ref[:, cols].astype(BF16), pad_k], axis=0)
        cq = jnp.concatenate(
            [jnp.sum(jnp.where(lane == 2 * p + h2, cqb, 0.0), axis=1, keepdims=True) for h2 in range(2)],
            axis=0)
        cka = ct_ref[2 * p:2 * p + 1, :]
        ckb = ct_ref[2 * p + 1:2 * p + 2, :]

        def bias(lo, hi):
            return jnp.concatenate([cq[:ls] - cka[:, lo:hi], cq[ls:] - ckb[:, lo:hi]], axis=0)

        s_c = _dot(qst, kc_t) + bias(0, past)
        s_n = _dot_nt(qst, kn) + bias(past, past + LANES)
        s_n = jnp.where(c <= r, s_n, NEG_INF)
        m = jnp.maximum(jnp.max(s_c, axis=1, keepdims=True), jnp.max(s_n, axis=1, keepdims=True))
        p_c = jnp.exp2(s_c - m)
        p_n = jnp.exp2(s_n - m)
        l = jnp.sum(p_c, axis=1, keepdims=True) + jnp.sum(p_n, axis=1, keepdims=True)
        o = (_dot_nt(p_c.astype(BF16), vc_t) + _dot(p_n.astype(BF16), vn)) / l
        o_ref[:, cols] = jnp.where(lane_o < hd, o[:ls], o[ls:])


def _attn_sample(q, kb, v, cache_k, cache_v, c, ct, *, past):
    B, ls, _ = q.shape
    tot = c.shape[1]
    rows = lambda n: pl.BlockSpec((None, n, FOX_WIDTH), lambda b: (b, 0, 0))
    cache_t = pl.BlockSpec((None, FOX_WIDTH, past), lambda b: (b, 0, 0))
    return pl.pallas_call(
        functools.partial(_attn_sample_kernel, past=past, ls=ls),
        grid=(B,),
        in_specs=[
            rows(ls), rows(ls), rows(ls), cache_t, cache_t,
            pl.BlockSpec((None, tot, FOX_HEADS), lambda b: (b, 0, 0)),
            pl.BlockSpec((None, FOX_HEADS, tot), lambda b: (b, 0, 0)),
        ],
        out_specs=rows(ls),
        out_shape=jax.ShapeDtypeStruct((B, ls, FOX_WIDTH), F32),
        compiler_params=_params(1),
        name="fox_attn_sample",
    )(q, kb, v, cache_k, cache_v, c, ct)


def _mid_kernel(o_ref, sg_ref, x_ref, wo_ref, g_ref, w_ref, wa1_ref, wa2_ref, ba_ref,
                y_ref, q_ref, k_ref, v_ref, sg2_ref, ga_ref):
    kw = GLA_KEY_WIDTH
    tm = o_ref.shape[0]
    for rows in (slice(0, tm // 2), slice(tm // 2, tm)):
        u = (o_ref[rows, :] * sg_ref[rows, :]).astype(BF16)
        y = x_ref[rows, :] + _dot(u, wo_ref[...])
        y_ref[rows, :] = y
        hb = _rms(y, g_ref[...]).astype(BF16)
        q_ref[rows, :] = _dot(hb, w_ref[:, 0:kw]) * (GLA_HEAD_K ** -0.5)
        k_ref[rows, :] = _dot(hb, w_ref[:, kw:2 * kw])
        v_ref[rows, :] = _dot(hb, w_ref[:, 2 * kw:2 * kw + GLA_VAL_WIDTH]).astype(BF16)
        gate = _dot(hb, w_ref[:, 2 * kw + GLA_VAL_WIDTH:2 * kw + 2 * GLA_VAL_WIDTH])
        sg2_ref[rows, :] = gate * jax.nn.sigmoid(gate)
        a1 = _dot(hb, wa1_ref[...]).astype(BF16)
        a = _dot(a1, wa2_ref[...]) + ba_ref[...]
        ga_ref[rows, :] = _log_sigmoid(a) * (1.0 / GLA_GATE_TEMP)


def _mid(o, sg, x, w_out, g, w_main, w_a1, w_a2, b_a, *, tm):
    T = o.shape[0]
    tok = lambda w: pl.BlockSpec((tm, w), lambda i: (i, 0))
    full = lambda a: pl.BlockSpec(a.shape, lambda i: (0,) * a.ndim)
    return pl.pallas_call(
        _mid_kernel,
        grid=(T // tm,),
        in_specs=[tok(D_MODEL), tok(D_MODEL), tok(D_MODEL), full(w_out), full(g), full(w_main),
                  full(w_a1), full(w_a2), full(b_a)],
        out_specs=[tok(D_MODEL), tok(GLA_KEY_WIDTH), tok(GLA_KEY_WIDTH), tok(GLA_VAL_WIDTH),
                   tok(GLA_VAL_WIDTH), tok(GLA_KEY_WIDTH)],
        out_shape=[
            jax.ShapeDtypeStruct((T, D_MODEL), F32),
            jax.ShapeDtypeStruct((T, GLA_KEY_WIDTH), F32),
            jax.ShapeDtypeStruct((T, GLA_KEY_WIDTH), F32),
            jax.ShapeDtypeStruct((T, GLA_VAL_WIDTH), BF16),
            jax.ShapeDtypeStruct((T, GLA_VAL_WIDTH), F32),
            jax.ShapeDtypeStruct((T, GLA_KEY_WIDTH), F32),
        ],
        compiler_params=_params(1),
        name="fox_out_gla_proj",
    )(o, sg, x, w_out, g, w_main, w_a1, w_a2, b_a)


def _gla_kernel(q_ref, k_ref, v_ref, ga_ref, s0_ref, o_ref, s_ref, *, n_chunks, chained):
    i = pl.program_id(1)
    ch = GLA_CHUNK
    dk, dv = GLA_HEAD_K, GLA_HEAD_V

    @pl.when(i == 0)
    def _():
        s_ref[...] = s0_ref[...]

    t = n_chunks * ch
    r = lax.broadcasted_iota(jnp.int32, (t, t), 0)
    c = lax.broadcasted_iota(jnp.int32, (t, t), 1)
    causal = ((r >> 6) == (c >> 6)) & (r >= c)
    tril = jnp.where(causal, 1.0, 0.0).astype(BF16)
    eye = (lax.broadcasted_iota(jnp.int32, (dk, dk), 0) == lax.broadcasted_iota(jnp.int32, (dk, dk), 1))

    b = _tril_cumsum(tril, ga_ref[...])
    lasts = [b[(ci + 1) * ch - 1:(ci + 1) * ch, :] for ci in range(n_chunks)]
    b_last = jnp.concatenate([jnp.broadcast_to(x, (ch, GLA_KEY_WIDTH)) for x in lasts], axis=0)
    kk = k_ref[...]
    qe = (q_ref[...] * jnp.exp(b)).astype(BF16)
    ke = (kk * jnp.exp(-b)).astype(BF16)
    kd = (kk * jnp.exp(b_last - b)).astype(BF16)
    decs = [jnp.exp(x) for x in lasts]

    for h in range(GLA_HEADS):
        ks = slice(h * dk, (h + 1) * dk)
        vs = slice(h * dv, (h + 1) * dv)
        vh = v_ref[:, vs]
        a = jnp.where(causal, _dot_nt(qe[:, ks], ke[:, ks]), 0.0)
        o_intra = _dot(a.astype(BF16), vh)
        s = s_ref[0, h]
        for ci in range(n_chunks):
            rows = slice(ci * ch, (ci + 1) * ch)
            if not chained:
                s = s_ref[ci, h]
            o_ref[rows, vs] = o_intra[rows] + _dot(qe[rows, ks], s.astype(BF16))
            dec_col = jnp.sum(jnp.where(eye, jnp.broadcast_to(decs[ci][:, ks], (dk, dk)), 0.0),
                              axis=1, keepdims=True)
            s = dec_col * s + _dot_tn(kd[rows, ks], vh[rows])
            if not chained:
                s_ref[ci, h] = s
        if chained:
            s_ref[0, h] = s


def _gla(q, k, v, ga, s0, *, tc):
    B, L, _ = q.shape
    n_chunks = tc // GLA_CHUNK
    chained = L > GLA_CHUNK
    if chained:
        n_states = 1
    else:
        n_states = n_chunks
        fold = lambda a: a.reshape(B // n_chunks, tc, a.shape[-1])
        q, k, v, ga = fold(q), fold(k), fold(v), fold(ga)
    G, T, _ = q.shape
    tok = lambda w: pl.BlockSpec((None, tc, w), lambda b, i: (b, i, 0))
    st = pl.BlockSpec((n_states, GLA_HEADS, GLA_HEAD_K, GLA_HEAD_V), lambda b, i: (b, 0, 0, 0))
    o, s = pl.pallas_call(
        functools.partial(_gla_kernel, n_chunks=n_chunks, chained=chained),
        grid=(G, T // tc),
        in_specs=[tok(GLA_KEY_WIDTH), tok(GLA_KEY_WIDTH), tok(GLA_VAL_WIDTH), tok(GLA_KEY_WIDTH), st],
        out_specs=[tok(GLA_VAL_WIDTH), st],
        out_shape=[
            jax.ShapeDtypeStruct((G, T, GLA_VAL_WIDTH), F32),
            jax.ShapeDtypeStruct((B, GLA_HEADS, GLA_HEAD_K, GLA_HEAD_V), F32),
        ],
        compiler_params=_params(2),
        name="gla_chunk",
    )(q, k, v, ga, s0)
    return o.reshape(B, L, GLA_VAL_WIDTH), s


def _gla_out_kernel(o_ref, sg_ref, y_ref, go_ref, wo_ref, gf_ref, out_ref, *, chunk_major):
    dv = GLA_HEAD_V
    if chunk_major:
        o = jnp.concatenate([o_ref[:, t, :] for t in range(o_ref.shape[1])], axis=0)
    else:
        o = o_ref[...]
    parts = []
    for h in range(GLA_HEADS):
        vs = slice(h * dv, (h + 1) * dv)
        parts.append((_rms(o[:, vs], go_ref[...]) * sg_ref[:, vs]).astype(BF16))
    u = jnp.concatenate(parts, axis=1)
    y = y_ref[...] + _dot(u, wo_ref[...])
    out_ref[...] = _rms(y, gf_ref[...])


def _gla_out(o, sg, y, g_o, w_out, g_f, *, tm):
    B, L, _ = o.shape
    n = L // GLA_CHUNK
    chunk_major = n > 1
    if chunk_major:
        tm = 8 * n
        o = o.reshape(B, n, GLA_CHUNK, D_MODEL)
        o_spec = pl.BlockSpec((None, n, 8, D_MODEL), lambda b, i: (b, 0, i, 0))
    else:
        flat = lambda a: a.reshape(1, B * L, D_MODEL)
        o, sg, y = flat(o), flat(sg), flat(y)
        o_spec = pl.BlockSpec((None, tm, D_MODEL), lambda b, i: (b, i, 0))
    nb, rows, _ = sg.shape
    tok = pl.BlockSpec((None, tm, D_MODEL), lambda b, i: (b, i, 0))
    full = lambda a: pl.BlockSpec(a.shape, lambda b, i: (0,) * a.ndim)
    return pl.pallas_call(
        functools.partial(_gla_out_kernel, chunk_major=chunk_major),
        grid=(nb, rows // tm),
        in_specs=[o_spec, tok, tok, full(g_o), full(w_out), full(g_f)],
        out_specs=tok,
        out_shape=jax.ShapeDtypeStruct((nb, rows, D_MODEL), F32),
        compiler_params=_params(2),
        name="gla_out",
    )(o, sg, y, g_o, w_out, g_f)


def _pad_cols(w, n):
    return jnp.pad(w, ((0, 0), (0, n - w.shape[1])))


def kernel(x_prompt, x_sample, cache_fox_k, cache_fox_v, cache_fox_logf, state_gla,
           g_norm_fox, w_in_fox, b_fox_f, w_out_fox,
           g_norm_gla, w_in_gla, w_gla_a2, b_gla_a, g_gla_o, w_out_gla, g_final):
    B, L, D = x_prompt.shape
    Bs, Ls, _ = x_sample.shape
    P = cache_fox_k.shape[1]
    H, hd = FOX_HEADS, FOX_HEAD_DIM

    row = lambda a: a.reshape(1, -1).astype(F32)
    w_fox = w_in_fox[:, :4 * FOX_WIDTH].astype(BF16)
    w_fox_f = _pad_cols(w_in_fox[:, 4 * FOX_WIDTH:], LANES).astype(BF16)
    b_f = _pad_cols(row(b_fox_f), LANES)
    w_o_fox = w_out_fox.astype(BF16)
    n_main = 2 * GLA_KEY_WIDTH + 2 * GLA_VAL_WIDTH
    w_gla = w_in_gla[:, :n_main].astype(BF16)
    w_a1 = _pad_cols(w_in_gla[:, n_main:], LANES).astype(BF16)
    w_a2 = jnp.pad(w_gla_a2, ((0, LANES - GLA_GATE_RANK), (0, 0))).astype(BF16)
    w_o_gla = w_out_gla.astype(BF16)

    q_p, k_p, v_p, kb_p, sg_p, lf_p, vt_p, nrm_p = _fox_proj(
        x_prompt, row(g_norm_fox), w_fox, w_fox_f, b_f, tm=ATTN_BLOCK, want_vt=True)
    e_p, f_p, rng_p = _bias_tiles(lf_p, rows=ATTN_BLOCK)
    first_p = _skip_plan(nrm_p[:, :, 0, :H], rng_p[:, :, 0, :], nrm_p[:, :, 1, :H], rng_p[:, :, 1, :])
    o_p = _attn_prompt(first_p, q_p, f_p, kb_p, e_p, vt_p, blk=ATTN_BLOCK)

    xs = x_sample.reshape(1, Bs * Ls, D)
    q_s, k_s, v_s, kb_s, sg_s, lf_s = _fox_proj(
        xs, row(g_norm_fox), w_fox, w_fox_f, b_f, tm=PROJ_TM, want_vt=False)
    sh = lambda a: a.reshape(Bs, Ls, a.shape[-1])
    q_s, k_s, v_s, kb_s, lf_s = sh(q_s), sh(k_s), sh(v_s), sh(kb_s), sh(lf_s)
    lf_all = jnp.concatenate(
        [cache_fox_logf.astype(F32), lf_s, jnp.zeros((Bs, LANES - Ls, H), F32)], axis=1)
    c_s = _cumsum_short(lf_all, rows=(P + LANES) // 3, per_step=4)
    cache_t = lambda a: jnp.transpose(a, (0, 2, 3, 1)).reshape(Bs, FOX_WIDTH, P)
    o_s = _attn_sample(q_s, kb_s, v_s, cache_t(cache_fox_k), cache_t(cache_fox_v),
                       c_s, jnp.swapaxes(c_s, 1, 2), past=P)

    def gla_layer(o, sg, x, s0, tc):
        b, l, _ = x.shape
        flat = lambda a: a.reshape(b * l, a.shape[-1])
        y1, q, k, v, sg2, ga = _mid(flat(o), flat(sg), flat(x), w_o_fox, row(g_norm_gla), w_gla,
                                    w_a1, w_a2, row(b_gla_a), tm=PROJ_TM)
        un = lambda a: a.reshape(b, l, a.shape[-1])
        og, s_out = _gla(un(q), un(k), un(v), un(ga), s0, tc=tc)
        y = _gla_out(og, un(sg2), un(y1), row(g_gla_o), w_o_gla, row(g_final), tm=PROJ_TM)
        return y.reshape(b, l, D), s_out

    s0_p = jnp.zeros((B, GLA_HEADS, GLA_HEAD_K, GLA_HEAD_V), F32)
    y_p, s_p = gla_layer(o_p, sg_p, x_prompt, s0_p, GLA_TC)
    y_s, s_s = gla_layer(o_s, sg_s.reshape(Bs, Ls, D), x_sample, state_gla.astype(F32), GLA_TC)

    return (y_p, y_s,
            k_p.reshape(B, L, H, hd), v_p.reshape(B, L, H, hd), lf_p, s_p.astype(state_gla.dtype),
            k_s.reshape(Bs, Ls, H, hd), v_s.reshape(Bs, Ls, H, hd), lf_s, s_s.astype(state_gla.dtype))
```

```python
import functools

import jax
import jax.numpy as jnp
from jax import lax
from jax.experimental import pallas as pl
from jax.experimental.pallas import tpu as pltpu

F32 = jnp.float32
BF16 = jnp.bfloat16

D_MODEL = 1024
EPS = 1e-6
NEG_INF = -1e30
LOG2E = 1.4426950408889634
NORM_SLACK = 1.01
SKIP_LOG2 = 152.0

FOX_HEADS = 16
FOX_HEAD_DIM = 64
FOX_WIDTH = FOX_HEADS * FOX_HEAD_DIM
FOX_PAIRS = FOX_HEADS // 2

GLA_HEADS = 4
GLA_KEY_WIDTH = 512
GLA_VAL_WIDTH = 1024
GLA_HEAD_K = 128
GLA_HEAD_V = 256
GLA_GATE_RANK = 16
GLA_GATE_TEMP = 16.0
GLA_CHUNK = 64

LANES = 128
VMEM_LIMIT = 56 * 1024 * 1024

PROJ_TM = 512
ATTN_BLOCK = 512
GLA_TC = 256


def _params(n_axes, flags=None):
    return pltpu.CompilerParams(
        dimension_semantics=("arbitrary",) * n_axes,
        vmem_limit_bytes=VMEM_LIMIT,
        flags=flags,
    )


def _dot(a, b):
    return jnp.dot(a, b, preferred_element_type=F32)


def _dot_nt(a, b):
    return lax.dot_general(a, b, (((1,), (1,)), ((), ())), preferred_element_type=F32)


def _dot_tn(a, b):
    return lax.dot_general(a, b, (((0,), (0,)), ((), ())), preferred_element_type=F32)


def _log_sigmoid(z):
    return jnp.minimum(z, 0.0) - jnp.log1p(jnp.exp(-jnp.abs(z)))


def _rms(x, g):
    ms = jnp.mean(x * x, axis=-1, keepdims=True)
    return (x * lax.rsqrt(ms + EPS)) * g


def _split3(x):
    hi = x.astype(BF16)
    r1 = x - hi.astype(F32)
    mid = r1.astype(BF16)
    lo = (r1 - mid.astype(F32)).astype(BF16)
    return hi, mid, lo


def _tril_cumsum(tril_bf16, x):
    hi, mid, lo = _split3(x)
    return (_dot(tril_bf16, hi) + _dot(tril_bf16, mid)) + _dot(tril_bf16, lo)


def _tril(n):
    r = lax.broadcasted_iota(jnp.int32, (n, n), 0)
    c = lax.broadcasted_iota(jnp.int32, (n, n), 1)
    return jnp.where(r >= c, 1.0, 0.0).astype(BF16)


def _fox_proj_kernel(x_ref, g_ref, w_ref, wf_ref, bf_ref, *rest, want_vt):
    if want_vt:
        q_ref, k_ref, v_ref, kb_ref, sg_ref, lf_ref, vt_ref, nrm_ref, lft_ref = rest
    else:
        q_ref, k_ref, v_ref, kb_ref, sg_ref, lf_ref = rest
    hb = _rms(x_ref[...], g_ref[...]).astype(BF16)
    q = _dot(hb, w_ref[:, 0:FOX_WIDTH])
    qb = (q * (LOG2E * FOX_HEAD_DIM ** -0.5)).astype(BF16)
    q_ref[...] = qb
    k = _dot(hb, w_ref[:, FOX_WIDTH:2 * FOX_WIDTH])
    k_ref[...] = k
    kb = k.astype(BF16)
    kb_ref[...] = kb
    if want_vt:
        row = lax.broadcasted_iota(jnp.int32, (FOX_WIDTH, LANES), 0)
        col = lax.broadcasted_iota(jnp.int32, (FOX_WIDTH, LANES), 1)
        head_of = jnp.where((row >> 6) == col, 1.0, 0.0).astype(BF16)

        def bound(xb):
            xf = xb.astype(F32)
            sq = _dot((xf * xf).astype(BF16), head_of)
            return jnp.sqrt(jnp.max(sq, axis=0, keepdims=True)) * NORM_SLACK

        nrm_ref[...] = jnp.concatenate(
            [bound(kb), bound(qb), jnp.zeros((6, LANES), F32)], axis=0)
    v = _dot(hb, w_ref[:, 2 * FOX_WIDTH:3 * FOX_WIDTH])
    v_ref[...] = v
    if want_vt:
        vt_ref[...] = v.astype(BF16).T
    gate = _dot(hb, w_ref[:, 3 * FOX_WIDTH:4 * FOX_WIDTH])
    sg_ref[...] = gate * jax.nn.sigmoid(gate)
    fl = _dot(hb, wf_ref[...]) + bf_ref[...]
    lf = _log_sigmoid(fl)
    lf_ref[...] = lf[:, :FOX_HEADS]
    if want_vt:
        lft_ref[...] = lf.T[:FOX_HEADS, :]


def _fox_proj(x, g, w_main, w_f, b_f, *, tm, want_vt):
    B, L, _ = x.shape
    n = L // tm
    tok = lambda w: pl.BlockSpec((None, tm, w), lambda b, i: (b, i, 0))
    full = lambda a: pl.BlockSpec(a.shape, lambda b, i: (0,) * a.ndim)
    in_specs = [tok(D_MODEL), full(g), full(w_main), full(w_f), full(b_f)]
    args = [x, g, w_main, w_f, b_f]
    out_specs = [tok(FOX_WIDTH)] * 5 + [tok(FOX_HEADS)]
    out_shape = [
        jax.ShapeDtypeStruct((B, L, FOX_WIDTH), BF16),
        jax.ShapeDtypeStruct((B, L, FOX_WIDTH), F32),
        jax.ShapeDtypeStruct((B, L, FOX_WIDTH), F32),
        jax.ShapeDtypeStruct((B, L, FOX_WIDTH), BF16),
        jax.ShapeDtypeStruct((B, L, FOX_WIDTH), F32),
        jax.ShapeDtypeStruct((B, L, FOX_HEADS), F32),
    ]
    if want_vt:
        out_specs.append(pl.BlockSpec((None, None, FOX_WIDTH, tm), lambda b, i: (b, i, 0, 0)))
        out_shape.append(jax.ShapeDtypeStruct((B, n, FOX_WIDTH, tm), BF16))
        out_specs.append(pl.BlockSpec((None, None, 8, LANES), lambda b, i: (b, i, 0, 0)))
        out_shape.append(jax.ShapeDtypeStruct((B, n, 8, LANES), F32))
        out_specs.append(pl.BlockSpec((None, FOX_HEADS, tm), lambda b, i: (b, 0, i)))
        out_shape.append(jax.ShapeDtypeStruct((B, FOX_HEADS, L), F32))
    return pl.pallas_call(
        functools.partial(_fox_proj_kernel, want_vt=want_vt),
        grid=(B, n),
        in_specs=in_specs,
        out_specs=out_specs,
        out_shape=out_shape,
        compiler_params=_params(2),
        name="fox_proj_vt" if want_vt else "fox_proj",
    )(*args)


BIAS_SLOT = 8


def _bias_selectors():
    shape = (LANES, 3 * FOX_HEADS)
    lane = lax.broadcasted_iota(jnp.int32, shape, 0)
    col = lax.broadcasted_iota(jnp.int32, shape, 1)
    slot = BIAS_SLOT * (col & (FOX_HEADS - 1))
    piece = col >> 4
    sel_f = jnp.where(lane == slot + piece, 1.0, 0.0).astype(BF16)
    sel_e = jnp.where(lane == slot + 3 + piece, -1.0, 0.0).astype(BF16)
    in_slot = lax.broadcasted_iota(jnp.int32, (LANES, 1), 0) & (BIAS_SLOT - 1)
    one_f = jnp.where((in_slot >= 3) & (in_slot < 6), 1.0, 0.0)
    one_e = jnp.where(in_slot < 3, 1.0, 0.0)
    return sel_f, sel_e, one_f, one_e


def _stack3(x):
    return jnp.concatenate(_split3(x), axis=0)


def _bias_kernel(lft_ref, e_ref, f_ref, rng_ref, carry_ref, *, rows):
    r = lax.broadcasted_iota(jnp.int32, (rows, rows), 0)
    c = lax.broadcasted_iota(jnp.int32, (rows, rows), 1)
    upper = jnp.where(r <= c, 1.0, 0.0).astype(BF16)
    sel_f, sel_e, one_f, one_e = _bias_selectors()
    nh = FOX_HEADS

    @pl.when(pl.program_id(1) == 0)
    def _():
        carry_ref[...] = jnp.zeros_like(carry_ref)

    carry = carry_ref[:, 0:1]
    for sb in range(rng_ref.shape[0]):
        cols = slice(sb * rows, (sb + 1) * rows)
        st = _dot(_stack3(lft_ref[:, cols]), upper)
        cs = ((st[0:nh] + st[nh:2 * nh]) + st[2 * nh:3 * nh]) + carry
        carry = cs[:, rows - 1:rows]
        c2 = cs * LOG2E
        pieces = _stack3(c2)
        f_ref[cols, :] = (_dot(sel_f, pieces) + one_f).T.astype(BF16)
        e_ref[cols, :] = (_dot(sel_e, pieces) + one_e).T.astype(BF16)
        rng_ref[sb] = jnp.concatenate(
            [jnp.broadcast_to(jnp.min(c2, axis=1, keepdims=True), (nh, LANES)),
             jnp.broadcast_to(jnp.max(c2, axis=1, keepdims=True), (nh, LANES))], axis=0)
    carry_ref[...] = jnp.broadcast_to(carry, carry_ref.shape)


def _bias_tiles(lft, *, rows):
    B, H, L = lft.shape
    per_step = 4
    span = per_step * rows
    wide = pl.BlockSpec((None, span, LANES), lambda b, i: (b, i, 0))
    e, f, rng = pl.pallas_call(
        functools.partial(_bias_kernel, rows=rows),
        grid=(B, L // span),
        in_specs=[pl.BlockSpec((None, H, span), lambda b, i: (b, 0, i))],
        out_specs=[wide, wide, pl.BlockSpec((None, per_step, 2 * H, LANES), lambda b, i: (b, i, 0, 0))],
        out_shape=[jax.ShapeDtypeStruct((B, L, LANES), BF16)] * 2
        + [jax.ShapeDtypeStruct((B, L // rows, 2 * H, LANES), F32)],
        scratch_shapes=[pltpu.VMEM((H, LANES), F32)],
        compiler_params=_params(2),
        name="fox_cumsum_bias",
    )(lft)
    return e, f, rng[:, :, :H, 0], rng[:, :, H:, 0]


def _cumsum_short_kernel(lf_ref, c_ref, *, rows):
    tril = _tril(rows)
    nbatch, length, _ = lf_ref.shape
    for bi in range(nbatch):
        carry = jnp.zeros((1, FOX_HEADS), F32)
        for r0 in range(0, length, rows):
            cs = _tril_cumsum(tril, lf_ref[bi, r0:r0 + rows, :]) + carry
            c_ref[bi, r0:r0 + rows, :] = cs * LOG2E
            carry = cs[rows - 1:rows, :]


def _cumsum_short(lf, *, rows, per_step):
    B, L, H = lf.shape
    spec = pl.BlockSpec((per_step, L, H), lambda b: (b, 0, 0))
    return pl.pallas_call(
        functools.partial(_cumsum_short_kernel, rows=rows),
        grid=(B // per_step,),
        in_specs=[spec],
        out_specs=spec,
        out_shape=jax.ShapeDtypeStruct((B, L, H), F32),
        compiler_params=_params(1),
        name="fox_cumsum",
    )(lf)


def _skip_plan_kernel(kn_ref, cmin_ref, qn_ref, cmax_ref, knq_ref, first_ref, *, nb):
    j = lax.broadcasted_iota(jnp.int32, (nb, nb), 0).astype(F32)
    q = lax.broadcasted_iota(jnp.int32, (nb, nb), 1).astype(F32)
    rows = []
    for p in range(FOX_PAIRS):
        skippable = None
        for h in (2 * p, 2 * p + 1):
            qn = qn_ref[h:h + 1, :]
            bound = (kn_ref[:, h:h + 1] * qn + (cmax_ref[h:h + 1, :] - cmin_ref[:, h:h + 1])
                     + qn * knq_ref[h:h + 1, :])
            ok = bound < -SKIP_LOG2
            skippable = ok if skippable is None else (skippable & ok)
        must = (j < q) & jnp.logical_not(skippable)
        rows.append(jnp.min(jnp.where(must, j, q), axis=0, keepdims=True))
    first_ref[...] = jnp.concatenate(rows, axis=0).astype(jnp.int32)


def _skip_plan(kn, cmin, qn, cmax):
    B, nb, H = kn.shape
    by_block = pl.BlockSpec((None, nb, H), lambda b: (b, 0, 0))
    by_head = pl.BlockSpec((None, H, nb), lambda b: (b, 0, 0))
    t = lambda a: jnp.swapaxes(a, 1, 2)
    return pl.pallas_call(
        functools.partial(_skip_plan_kernel, nb=nb),
        grid=(B,),
        in_specs=[by_block, by_block, by_head, by_head, by_head],
        out_specs=pl.BlockSpec((None, FOX_PAIRS, nb), lambda b: (b, 0, 0)),
        out_shape=jax.ShapeDtypeStruct((B, FOX_PAIRS, nb), jnp.int32),
        compiler_params=_params(1),
        name="fox_skip_plan",
    )(kn, cmin, t(qn), t(cmax), t(kn))


def _stack_heads(q2):
    lane = lax.broadcasted_iota(jnp.int32, q2.shape, 1)
    zero = jnp.zeros_like(q2)
    return jnp.concatenate(
        [jnp.where(lane < FOX_HEAD_DIM, q2, zero), jnp.where(lane >= FOX_HEAD_DIM, q2, zero)], axis=0)


ATTN_TILE = 256
SUM_ROWS = 16


def _attn_prompt_kernel(n_ref, vq_ref, vj_ref, q_ref, f_ref, k_ref, e_ref, vt_ref, o_ref,
                        m_ref, acc_ref, sa_ref, xa_ref, sb_ref, xb_ref, *, blk, nb):
    b_id = pl.program_id(0)
    p_id = pl.program_id(1)
    hd = FOX_HEAD_DIM
    tw = ATTN_TILE
    per_head = blk // tw
    n_tiles = 2 * per_head

    lane = lax.broadcasted_iota(jnp.int32, (blk, LANES), 1)
    zero = jnp.zeros((blk, LANES), BF16)
    slot_a = 2 * BIAS_SLOT * p_id
    slot_b = slot_a + BIAS_SLOT
    ones_rows = jnp.ones((SUM_ROWS, blk), BF16)

    m_ref[...] = jnp.full(m_ref.shape, NEG_INF, F32)
    acc_ref[...] = jnp.zeros(acc_ref.shape, F32)

    def query_tiles(qi):
        r0 = pl.multiple_of(qi * blk, blk)
        q2 = q_ref[pl.ds(r0, blk), :]
        f2 = f_ref[pl.ds(r0, blk), :]
        heads = [
            jnp.concatenate([jnp.where(lane < hd, q2, zero),
                             jnp.where((lane >= slot_a) & (lane < slot_b), f2, zero)], axis=1),
            jnp.concatenate([jnp.where(lane >= hd, q2, zero),
                             jnp.where((lane >= slot_b) & (lane < slot_b + BIAS_SLOT), f2, zero)], axis=1),
        ]
        return [heads[t // per_head][(t % per_head) * tw:(t % per_head + 1) * tw] for t in range(n_tiles)]

    def stage(nxt, cur, diagonal):
        if nxt is not None:
            qi_n, j_n, (sn_ref, xn_ref) = nxt
            q_tiles = query_tiles(qi_n)
            k0 = pl.multiple_of(j_n * blk, blk)
            kaug = jnp.concatenate([k_ref[pl.ds(k0, blk), :], e_ref[pl.ds(k0, blk), :]], axis=1)
        if cur is not None:
            qi_c, j_c, (sc_ref, xc_ref) = cur
            vt = vt_ref[j_c]
            v_heads = [jnp.concatenate([vt[h * hd:(h + 1) * hd, :], ones_rows], axis=0) for h in range(2)]
        for t in range(n_tiles):
            c0 = (t % per_head) * tw
            live = c0 + tw if diagonal else blk
            if nxt is not None:
                s = _dot_nt(kaug[:live], q_tiles[t])
                if diagonal:
                    r = lax.broadcasted_iota(jnp.int32, (live, tw), 0)
                    c = lax.broadcasted_iota(jnp.int32, (live, tw), 1) + c0
                    s = jnp.where(r <= c, s, NEG_INF)
                sn_ref[t, 0:live, :] = s
                xn_ref[t] = jnp.max(s, axis=0, keepdims=True)
            if cur is not None:
                idx = qi_c * n_tiles + t
                m_old = m_ref[idx]
                m_new = jnp.maximum(m_old, xc_ref[t])
                alpha = jnp.exp2(m_old - m_new)
                p = jnp.exp2(sc_ref[t, 0:live, :] - m_new)
                m_ref[idx] = m_new
                pv = _dot(v_heads[t // per_head][:, 0:live], p.astype(BF16))
                acc_ref[idx] = acc_ref[idx] * alpha + pv

    buf_a = (sa_ref, xa_ref)
    buf_b = (sb_ref, xb_ref)

    stage((0, 0, buf_a), None, True)

    def diag_pair(n, carry):
        i0 = 2 * n
        i2 = jnp.minimum(i0 + 2, nb - 1)
        stage((i0 + 1, i0 + 1, buf_b), (i0, i0, buf_a), True)
        stage((i2, i2, buf_a), (i0 + 1, i0 + 1, buf_b), True)
        return carry

    lax.fori_loop(0, nb // 2, diag_pair, 0)

    n_off = n_ref[b_id, p_id]
    visit = lambda v, buf: (vq_ref[b_id, p_id, v], vj_ref[b_id, p_id, v], buf)

    @pl.when(n_off > 0)
    def _():
        stage(visit(0, buf_a), None, False)

    def off_pair(n, carry):
        v0 = 2 * n
        stage(visit(v0 + 1, buf_b), visit(v0, buf_a), False)
        stage(visit(v0 + 2, buf_a), visit(v0 + 1, buf_b), False)
        return carry

    lax.fori_loop(0, jnp.maximum(n_off - 1, 0) // 2, off_pair, 0)

    @pl.when((n_off > 0) & (n_off % 2 == 0))
    def _():
        stage(visit(n_off - 1, buf_b), visit(n_off - 2, buf_a), False)
        stage(None, visit(n_off - 1, buf_b), False)

    @pl.when(n_off % 2 == 1)
    def _():
        stage(None, visit(n_off - 1, buf_a), False)

    def finish(qi, carry):
        halves = [jnp.concatenate([acc_ref[qi * n_tiles + t, 0:hd, :] / acc_ref[qi * n_tiles + t, hd:hd + 1, :]
                                   for t in range(h * per_head, (h + 1) * per_head)], axis=1)
                  for h in range(2)]
        o_ref[pl.ds(pl.multiple_of(qi * blk, blk), blk), :] = jnp.concatenate(halves, axis=0).T
        return carry

    lax.fori_loop(0, nb, finish, 0)


def _visit_lists(first):
    B, P, nb = first.shape
    vmax = nb * (nb - 1) // 2 + 8
    q_idx = jnp.arange(nb, dtype=jnp.int32)
    cnt = q_idx - first
    incl = jnp.cumsum(cnt, axis=-1)
    excl = incl - cnt
    v = jnp.arange(vmax, dtype=jnp.int32)
    vq = jnp.minimum(jnp.sum(v[None, None, :, None] >= incl[:, :, None, :], axis=-1), nb - 1).astype(jnp.int32)
    pick = vq[..., None] == q_idx
    take = lambda a: jnp.sum(jnp.where(pick, a[:, :, None, :], 0), axis=-1)
    vj = jnp.clip(take(first) + (v - take(excl)), 0, nb - 1).astype(jnp.int32)
    return incl[..., -1].astype(jnp.int32), vq, vj


def _attn_prompt(first, q, f, kb, e, vt, *, blk):
    B, L, _ = q.shape
    nb = L // blk
    n_tiles = 2 * blk // ATTN_TILE
    n_off, vq, vj = _visit_lists(first)
    by_pair = pl.BlockSpec((None, L, LANES), lambda b, p, *_: (b, 0, p))
    shared = pl.BlockSpec((None, L, LANES), lambda b, p, *_: (b, 0, 0))
    return pl.pallas_call(
        functools.partial(_attn_prompt_kernel, blk=blk, nb=nb),
        grid_spec=pltpu.PrefetchScalarGridSpec(
            num_scalar_prefetch=3,
            grid=(B, FOX_PAIRS),
            in_specs=[by_pair, shared, by_pair, shared,
                      pl.BlockSpec((None, nb, LANES, blk), lambda b, p, *_: (b, 0, p, 0))],
            out_specs=by_pair,
            scratch_shapes=[
                pltpu.VMEM((nb * n_tiles, 1, ATTN_TILE), F32),
                pltpu.VMEM((nb * n_tiles, FOX_HEAD_DIM + SUM_ROWS, ATTN_TILE), F32),
                pltpu.VMEM((n_tiles, blk, ATTN_TILE), F32),
                pltpu.VMEM((n_tiles, 1, ATTN_TILE), F32),
                pltpu.VMEM((n_tiles, blk, ATTN_TILE), F32),
                pltpu.VMEM((n_tiles, 1, ATTN_TILE), F32),
            ],
        ),
        out_shape=jax.ShapeDtypeStruct((B, L, FOX_WIDTH), F32),
        compiler_params=_params(2),
        name="fox_attn_prompt",
    )(n_off, vq, vj, q, f, kb, e, vt)


def _attn_sample_kernel(q_ref, kn_ref, vn_ref, ck_ref, cv_ref, c_ref, ct_ref, o_ref, *, past, ls):
    hd = FOX_HEAD_DIM
    pad_k = jnp.zeros((LANES - ls, LANES), BF16)
    cqb = c_ref[past:past + ls, :]
    lane = lax.broadcasted_iota(jnp.int32, cqb.shape, 1)
    r = lax.broadcasted_iota(jnp.int32, (2 * ls, LANES), 0) & (ls - 1)
    c = lax.broadcasted_iota(jnp.int32, (2 * ls, LANES), 1)
    lane_o = lax.broadcasted_iota(jnp.int32, (ls, LANES), 1)

    for p in range(FOX_PAIRS):
        cols = slice(p * LANES, (p + 1) * LANES)
        qst = _stack_heads(q_ref[:, cols])
        kc_t = ck_ref[cols, :].astype(BF16)
        vc_t = cv_ref[cols, :].astype(BF16)
        kn = jnp.concatenate([kn_ref[:, cols], pad_k], axis=0)
        vn = jnp.concatenate([vn_ref[:, cols].astype(BF16), pad_k], axis=0)
        cq = jnp.concatenate(
            [jnp.sum(jnp.where(lane == 2 * p + h2, cqb, 0.0), axis=1, keepdims=True) for h2 in range(2)],
            axis=0)
        cka = ct_ref[2 * p:2 * p + 1, :]
        ckb = ct_ref[2 * p + 1:2 * p + 2, :]

        def bias(lo, hi):
            return jnp.concatenate([cq[:ls] - cka[:, lo:hi], cq[ls:] - ckb[:, lo:hi]], axis=0)

        s_c = _dot(qst, kc_t) + bias(0, past)
        s_n = _dot_nt(qst, kn) + bias(past, past + LANES)
        s_n = jnp.where(c <= r, s_n, NEG_INF)
        m = jnp.maximum(jnp.max(s_c, axis=1, keepdims=True), jnp.max(s_n, axis=1, keepdims=True))
        p_c = jnp.exp2(s_c - m)
        p_n = jnp.exp2(s_n - m)
        l = jnp.sum(p_c, axis=1, keepdims=True) + jnp.sum(p_n, axis=1, keepdims=True)
        o = (_dot_nt(p_c.astype(BF16), vc_t) + _dot(p_n.astype(BF16), vn)) / l
        o_ref[:, cols] = jnp.where(lane_o < hd, o[:ls], o[ls:])


def _attn_sample(q, kb, v, cache_k, cache_v, c, ct, *, past):
    B, ls, _ = q.shape
    tot = c.shape[1]
    rows = lambda n: pl.BlockSpec((None, n, FOX_WIDTH), lambda b: (b, 0, 0))
    cache_t = pl.BlockSpec((None, FOX_WIDTH, past), lambda b: (b, 0, 0))
    return pl.pallas_call(
        functools.partial(_attn_sample_kernel, past=past, ls=ls),
        grid=(B,),
        in_specs=[
            rows(ls), rows(ls), rows(ls), cache_t, cache_t,
            pl.BlockSpec((None, tot, FOX_HEADS), lambda b: (b, 0, 0)),
            pl.BlockSpec((None, FOX_HEADS, tot), lambda b: (b, 0, 0)),
        ],
        out_specs=rows(ls),
        out_shape=jax.ShapeDtypeStruct((B, ls, FOX_WIDTH), F32),
        compiler_params=_params(1),
        name="fox_attn_sample",
    )(q, kb, v, cache_k, cache_v, c, ct)


def _mid_kernel(o_ref, sg_ref, x_ref, wo_ref, g_ref, w_ref, wa1_ref, wa2_ref, ba_ref,
                y_ref, q_ref, k_ref, v_ref, sg2_ref, ga_ref):
    kw = GLA_KEY_WIDTH
    tm = o_ref.shape[0]
    for rows in (slice(0, tm // 2), slice(tm // 2, tm)):
        u = (o_ref[rows, :] * sg_ref[rows, :]).astype(BF16)
        y = x_ref[rows, :] + _dot(u, wo_ref[...])
        y_ref[rows, :] = y
        hb = _rms(y, g_ref[...]).astype(BF16)
        q_ref[rows, :] = _dot(hb, w_ref[:, 0:kw]) * (GLA_HEAD_K ** -0.5)
        k_ref[rows, :] = _dot(hb, w_ref[:, kw:2 * kw])
        v_ref[rows, :] = _dot(hb, w_ref[:, 2 * kw:2 * kw + GLA_VAL_WIDTH]).astype(BF16)
        gate = _dot(hb, w_ref[:, 2 * kw + GLA_VAL_WIDTH:2 * kw + 2 * GLA_VAL_WIDTH])
        sg2_ref[rows, :] = gate * jax.nn.sigmoid(gate)
        a1 = _dot(hb, wa1_ref[...]).astype(BF16)
        a = _dot(a1, wa2_ref[...]) + ba_ref[...]
        ga_ref[rows, :] = _log_sigmoid(a) * (1.0 / GLA_GATE_TEMP)


def _mid(o, sg, x, w_out, g, w_main, w_a1, w_a2, b_a, *, tm):
    T = o.shape[0]
    tok = lambda w: pl.BlockSpec((tm, w), lambda i: (i, 0))
    full = lambda a: pl.BlockSpec(a.shape, lambda i: (0,) * a.ndim)
    return pl.pallas_call(
        _mid_kernel,
        grid=(T // tm,),
        in_specs=[tok(D_MODEL), tok(D_MODEL), tok(D_MODEL), full(w_out), full(g), full(w_main),
                  full(w_a1), full(w_a2), full(b_a)],
        out_specs=[tok(D_MODEL), tok(GLA_KEY_WIDTH), tok(GLA_KEY_WIDTH), tok(GLA_VAL_WIDTH),
                   tok(GLA_VAL_WIDTH), tok(GLA_KEY_WIDTH)],
        out_shape=[
            jax.ShapeDtypeStruct((T, D_MODEL), F32),
            jax.ShapeDtypeStruct((T, GLA_KEY_WIDTH), F32),
            jax.ShapeDtypeStruct((T, GLA_KEY_WIDTH), F32),
            jax.ShapeDtypeStruct((T, GLA_VAL_WIDTH), BF16),
            jax.ShapeDtypeStruct((T, GLA_VAL_WIDTH), F32),
            jax.ShapeDtypeStruct((T, GLA_KEY_WIDTH), F32),
        ],
        compiler_params=_params(1),
        name="fox_out_gla_proj",
    )(o, sg, x, w_out, g, w_main, w_a1, w_a2, b_a)


def _gla_kernel(q_ref, k_ref, v_ref, ga_ref, s0_ref, o_ref, s_ref, *, n_chunks, chained):
    i = pl.program_id(1)
    ch = GLA_CHUNK
    dk, dv = GLA_HEAD_K, GLA_HEAD_V

    @pl.when(i == 0)
    def _():
        s_ref[...] = s0_ref[...]

    t = n_chunks * ch
    r = lax.broadcasted_iota(jnp.int32, (t, t), 0)
    c = lax.broadcasted_iota(jnp.int32, (t, t), 1)
    causal = ((r >> 6) == (c >> 6)) & (r >= c)
    tril = jnp.where(causal, 1.0, 0.0).astype(BF16)
    eye = (lax.broadcasted_iota(jnp.int32, (dk, dk), 0) == lax.broadcasted_iota(jnp.int32, (dk, dk), 1))

    b = _tril_cumsum(tril, ga_ref[...])
    lasts = [b[(ci + 1) * ch - 1:(ci + 1) * ch, :] for ci in range(n_chunks)]
    b_last = jnp.concatenate([jnp.broadcast_to(x, (ch, GLA_KEY_WIDTH)) for x in lasts], axis=0)
    kk = k_ref[...]
    qe = (q_ref[...] * jnp.exp(b)).astype(BF16)
    ke = (kk * jnp.exp(-b)).astype(BF16)
    kd = (kk * jnp.exp(b_last - b)).astype(BF16)
    decs = [jnp.exp(x) for x in lasts]

    for h in range(GLA_HEADS):
        ks = slice(h * dk, (h + 1) * dk)
        vs = slice(h * dv, (h + 1) * dv)
        vh = v_ref[:, vs]
        a = jnp.where(causal, _dot_nt(qe[:, ks], ke[:, ks]), 0.0)
        o_intra = _dot(a.astype(BF16), vh)
        s = s_ref[0, h]
        for ci in range(n_chunks):
            rows = slice(ci * ch, (ci + 1) * ch)
            if not chained:
                s = s_ref[ci, h]
            o_ref[rows, vs] = o_intra[rows] + _dot(qe[rows, ks], s.astype(BF16))
            dec_col = jnp.sum(jnp.where(eye, jnp.broadcast_to(decs[ci][:, ks], (dk, dk)), 0.0),
                              axis=1, keepdims=True)
            s = dec_col * s + _dot_tn(kd[rows, ks], vh[rows])
            if not chained:
                s_ref[ci, h] = s
        if chained:
            s_ref[0, h] = s


def _gla(q, k, v, ga, s0, *, tc):
    B, L, _ = q.shape
    n_chunks = tc // GLA_CHUNK
    chained = L > GLA_CHUNK
    if chained:
        n_states = 1
    else:
        n_states = n_chunks
        fold = lambda a: a.reshape(B // n_chunks, tc, a.shape[-1])
        q, k, v, ga = fold(q), fold(k), fold(v), fold(ga)
    G, T, _ = q.shape
    tok = lambda w: pl.BlockSpec((None, tc, w), lambda b, i: (b, i, 0))
    st = pl.BlockSpec((n_states, GLA_HEADS, GLA_HEAD_K, GLA_HEAD_V), lambda b, i: (b, 0, 0, 0))
    o, s = pl.pallas_call(
        functools.partial(_gla_kernel, n_chunks=n_chunks, chained=chained),
        grid=(G, T // tc),
        in_specs=[tok(GLA_KEY_WIDTH), tok(GLA_KEY_WIDTH), tok(GLA_VAL_WIDTH), tok(GLA_KEY_WIDTH), st],
        out_specs=[tok(GLA_VAL_WIDTH), st],
        out_shape=[
            jax.ShapeDtypeStruct((G, T, GLA_VAL_WIDTH), F32),
            jax.ShapeDtypeStruct((B, GLA_HEADS, GLA_HEAD_K, GLA_HEAD_V), F32),
        ],
        compiler_params=_params(2),
        name="gla_chunk",
    )(q, k, v, ga, s0)
    return o.reshape(B, L, GLA_VAL_WIDTH), s


def _gla_out_kernel(o_ref, sg_ref, y_ref, go_ref, wo_ref, gf_ref, out_ref, *, chunk_major):
    dv = GLA_HEAD_V
    if chunk_major:
        o = jnp.concatenate([o_ref[:, t, :] for t in range(o_ref.shape[1])], axis=0)
    else:
        o = o_ref[...]
    parts = []
    for h in range(GLA_HEADS):
        vs = slice(h * dv, (h + 1) * dv)
        parts.append((_rms(o[:, vs], go_ref[...]) * sg_ref[:, vs]).astype(BF16))
    u = jnp.concatenate(parts, axis=1)
    y = y_ref[...] + _dot(u, wo_ref[...])
    out_ref[...] = _rms(y, gf_ref[...])


def _gla_out(o, sg, y, g_o, w_out, g_f, *, tm):
    B, L, _ = o.shape
    n = L // GLA_CHUNK
    chunk_major = n > 1
    if chunk_major:
        tm = 8 * n
        o = o.reshape(B, n, GLA_CHUNK, D_MODEL)
        o_spec = pl.BlockSpec((None, n, 8, D_MODEL), lambda b, i: (b, 0, i, 0))
    else:
        flat = lambda a: a.reshape(1, B * L, D_MODEL)
        o, sg, y = flat(o), flat(sg), flat(y)
        o_spec = pl.BlockSpec((None, tm, D_MODEL), lambda b, i: (b, i, 0))
    nb, rows, _ = sg.shape
    tok = pl.BlockSpec((None, tm, D_MODEL), lambda b, i: (b, i, 0))
    full = lambda a: pl.BlockSpec(a.shape, lambda b, i: (0,) * a.ndim)
    return pl.pallas_call(
        functools.partial(_gla_out_kernel, chunk_major=chunk_major),
        grid=(nb, rows // tm),
        in_specs=[o_spec, tok, tok, full(g_o), full(w_out), full(g_f)],
        out_specs=tok,
        out_shape=jax.ShapeDtypeStruct((nb, rows, D_MODEL), F32),
        compiler_params=_params(2),
        name="gla_out",
    )(o, sg, y, g_o, w_out, g_f)


def _pad_cols(w, n):
    return jnp.pad(w, ((0, 0), (0, n - w.shape[1])))


def kernel(x_prompt, x_sample, cache_fox_k, cache_fox_v, cache_fox_logf, state_gla,
           g_norm_fox, w_in_fox, b_fox_f, w_out_fox,
           g_norm_gla, w_in_gla, w_gla_a2, b_gla_a, g_gla_o, w_out_gla, g_final):
    B, L, D = x_prompt.shape
    Bs, Ls, _ = x_sample.shape
    P = cache_fox_k.shape[1]
    H, hd = FOX_HEADS, FOX_HEAD_DIM

    row = lambda a: a.reshape(1, -1).astype(F32)
    w_fox = w_in_fox[:, :4 * FOX_WIDTH].astype(BF16)
    w_fox_f = _pad_cols(w_in_fox[:, 4 * FOX_WIDTH:], LANES).astype(BF16)
    b_f = _pad_cols(row(b_fox_f), LANES)
    w_o_fox = w_out_fox.astype(BF16)
    n_main = 2 * GLA_KEY_WIDTH + 2 * GLA_VAL_WIDTH
    w_gla = w_in_gla[:, :n_main].astype(BF16)
    w_a1 = _pad_cols(w_in_gla[:, n_main:], LANES).astype(BF16)
    w_a2 = jnp.pad(w_gla_a2, ((0, LANES - GLA_GATE_RANK), (0, 0))).astype(BF16)
    w_o_gla = w_out_gla.astype(BF16)

    q_p, k_p, v_p, kb_p, sg_p, lf_p, vt_p, nrm_p, lft_p = _fox_proj(
        x_prompt, row(g_norm_fox), w_fox, w_fox_f, b_f, tm=ATTN_BLOCK, want_vt=True)
    e_p, f_p, cmin_p, cmax_p = _bias_tiles(lft_p, rows=ATTN_BLOCK)
    first_p = _skip_plan(nrm_p[:, :, 0, :H], cmin_p, nrm_p[:, :, 1, :H], cmax_p)
    o_p = _attn_prompt(first_p, q_p, f_p, kb_p, e_p, vt_p, blk=ATTN_BLOCK)

    xs = x_sample.reshape(1, Bs * Ls, D)
    q_s, k_s, v_s, kb_s, sg_s, lf_s = _fox_proj(
        xs, row(g_norm_fox), w_fox, w_fox_f, b_f, tm=PROJ_TM, want_vt=False)
    sh = lambda a: a.reshape(Bs, Ls, a.shape[-1])
    q_s, k_s, v_s, kb_s, lf_s = sh(q_s), sh(k_s), sh(v_s), sh(kb_s), sh(lf_s)
    lf_all = jnp.concatenate(
        [cache_fox_logf.astype(F32), lf_s, jnp.zeros((Bs, LANES - Ls, H), F32)], axis=1)
    c_s = _cumsum_short(lf_all, rows=(P + LANES) // 3, per_step=4)
    cache_t = lambda a: jnp.transpose(a, (0, 2, 3, 1)).reshape(Bs, FOX_WIDTH, P)
    o_s = _attn_sample(q_s, kb_s, v_s, cache_t(cache_fox_k), cache_t(cache_fox_v),
                       c_s, jnp.swapaxes(c_s, 1, 2), past=P)

    def gla_layer(o, sg, x, s0, tc):
        b, l, _ = x.shape
        flat = lambda a: a.reshape(b * l, a.shape[-1])
        y1, q, k, v, sg2, ga = _mid(flat(o), flat(sg), flat(x), w_o_fox, row(g_norm_gla), w_gla,
                                    w_a1, w_a2, row(b_gla_a), tm=PROJ_TM)
        un = lambda a: a.reshape(b, l, a.shape[-1])
        og, s_out = _gla(un(q), un(k), un(v), un(ga), s0, tc=tc)
        y = _gla_out(og, un(sg2), un(y1), row(g_gla_o), w_o_gla, row(g_final), tm=PROJ_TM)
        return y.reshape(b, l, D), s_out

    s0_p = jnp.zeros((B, GLA_HEADS, GLA_HEAD_K, GLA_HEAD_V), F32)
    y_p, s_p = gla_layer(o_p, sg_p, x_prompt, s0_p, GLA_TC)
    y_s, s_s = gla_layer(o_s, sg_s.reshape(Bs, Ls, D), x_sample, state_gla.astype(F32), GLA_TC)

    return (y_p, y_s,
            k_p.reshape(B, L, H, hd), v_p.reshape(B, L, H, hd), lf_p, s_p.astype(state_gla.dtype),
            k_s.reshape(Bs, Ls, H, hd), v_s.reshape(Bs, Ls, H, hd), lf_s, s_s.astype(state_gla.dtype))
```

```python
import functools

import jax
import jax.numpy as jnp
from jax import lax
from jax.experimental import pallas as pl
from jax.experimental.pallas import tpu as pltpu

F32 = jnp.float32
BF16 = jnp.bfloat16

D_MODEL = 1024
EPS = 1e-6
NEG_INF = -1e30
LOG2E = 1.4426950408889634
NORM_SLACK = 1.01
SKIP_LOG2 = 152.0

FOX_HEADS = 16
FOX_HEAD_DIM = 64
FOX_WIDTH = FOX_HEADS * FOX_HEAD_DIM
FOX_PAIRS = FOX_HEADS // 2

GLA_HEADS = 4
GLA_KEY_WIDTH = 512
GLA_VAL_WIDTH = 1024
GLA_HEAD_K = 128
GLA_HEAD_V = 256
GLA_GATE_RANK = 16
GLA_GATE_TEMP = 16.0
GLA_CHUNK = 64

LANES = 128
VMEM_LIMIT = 56 * 1024 * 1024

PROJ_TM = 512
ATTN_BLOCK = 512
GLA_TC = 256


def _params(n_axes, flags=None):
    return pltpu.CompilerParams(
        dimension_semantics=("arbitrary",) * n_axes,
        vmem_limit_bytes=VMEM_LIMIT,
        flags=flags,
    )


def _dot(a, b):
    return jnp.dot(a, b, preferred_element_type=F32)


def _dot_nt(a, b):
    return lax.dot_general(a, b, (((1,), (1,)), ((), ())), preferred_element_type=F32)


def _dot_tn(a, b):
    return lax.dot_general(a, b, (((0,), (0,)), ((), ())), preferred_element_type=F32)


def _log_sigmoid(z):
    return jnp.minimum(z, 0.0) - jnp.log1p(jnp.exp(-jnp.abs(z)))


def _rms(x, g):
    ms = jnp.mean(x * x, axis=-1, keepdims=True)
    return (x * lax.rsqrt(ms + EPS)) * g


def _split3(x):
    hi = x.astype(BF16)
    r1 = x - hi.astype(F32)
    mid = r1.astype(BF16)
    lo = (r1 - mid.astype(F32)).astype(BF16)
    return hi, mid, lo


def _tril_cumsum(tril_bf16, x):
    hi, mid, lo = _split3(x)
    return (_dot(tril_bf16, hi) + _dot(tril_bf16, mid)) + _dot(tril_bf16, lo)


def _fox_proj_kernel(x_ref, g_ref, w_ref, wf_ref, bf_ref, *rest, want_vt):
    if want_vt:
        q_ref, k_ref, v_ref, kb_ref, sg_ref, lft_ref, vt_ref, nrm_ref = rest
    else:
        q_ref, k_ref, v_ref, kb_ref, sg_ref, lft_ref = rest
    fw = FOX_WIDTH
    hb = _rms(x_ref[...], g_ref[...]).astype(BF16)
    q = _dot_nt(hb, w_ref[0:fw, :])
    qb = (q * (LOG2E * FOX_HEAD_DIM ** -0.5)).astype(BF16)
    q_ref[...] = qb
    k = _dot_nt(hb, w_ref[fw:2 * fw, :])
    k_ref[...] = k
    kb = k.astype(BF16)
    kb_ref[...] = kb
    if want_vt:
        row = lax.broadcasted_iota(jnp.int32, (FOX_WIDTH, LANES), 0)
        col = lax.broadcasted_iota(jnp.int32, (FOX_WIDTH, LANES), 1)
        head_of = jnp.where((row >> 6) == col, 1.0, 0.0).astype(BF16)

        def bound(xb):
            xf = xb.astype(F32)
            sq = _dot((xf * xf).astype(BF16), head_of)
            return jnp.sqrt(jnp.max(sq, axis=0, keepdims=True)) * NORM_SLACK

        nrm_ref[...] = jnp.concatenate(
            [bound(kb), bound(qb), jnp.zeros((6, LANES), F32)], axis=0)
    v = _dot_nt(hb, w_ref[2 * fw:3 * fw, :])
    v_ref[...] = v
    if want_vt:
        vt_ref[...] = v.astype(BF16).T
    gate = _dot_nt(hb, w_ref[3 * fw:4 * fw, :])
    sg_ref[...] = gate * jax.nn.sigmoid(gate)
    fl = _dot_nt(hb, wf_ref[...]) + bf_ref[...]
    lft_ref[...] = _log_sigmoid(fl).T[:FOX_HEADS, :]


def _fox_proj(x, g, w_main, w_f, b_f, *, tm, want_vt):
    B, L, _ = x.shape
    n = L // tm
    tok = lambda w: pl.BlockSpec((None, tm, w), lambda b, i: (b, i, 0))
    full = lambda a: pl.BlockSpec(a.shape, lambda b, i: (0,) * a.ndim)
    in_specs = [tok(D_MODEL), full(g), full(w_main), full(w_f), full(b_f)]
    args = [x, g, w_main, w_f, b_f]
    out_specs = [tok(FOX_WIDTH)] * 5 + [pl.BlockSpec((None, FOX_HEADS, tm), lambda b, i: (b, 0, i))]
    out_shape = [
        jax.ShapeDtypeStruct((B, L, FOX_WIDTH), BF16),
        jax.ShapeDtypeStruct((B, L, FOX_WIDTH), F32),
        jax.ShapeDtypeStruct((B, L, FOX_WIDTH), F32),
        jax.ShapeDtypeStruct((B, L, FOX_WIDTH), BF16),
        jax.ShapeDtypeStruct((B, L, FOX_WIDTH), F32),
        jax.ShapeDtypeStruct((B, FOX_HEADS, L), F32),
    ]
    if want_vt:
        out_specs.append(pl.BlockSpec((None, None, FOX_WIDTH, tm), lambda b, i: (b, i, 0, 0)))
        out_shape.append(jax.ShapeDtypeStruct((B, n, FOX_WIDTH, tm), BF16))
        out_specs.append(pl.BlockSpec((None, None, 8, LANES), lambda b, i: (b, i, 0, 0)))
        out_shape.append(jax.ShapeDtypeStruct((B, n, 8, LANES), F32))
    return pl.pallas_call(
        functools.partial(_fox_proj_kernel, want_vt=want_vt),
        grid=(B, n),
        in_specs=in_specs,
        out_specs=out_specs,
        out_shape=out_shape,
        compiler_params=_params(2),
        name="fox_proj_vt" if want_vt else "fox_proj",
    )(*args)


BIAS_SLOT = 8


def _bias_selectors():
    shape = (LANES, 3 * FOX_HEADS)
    lane = lax.broadcasted_iota(jnp.int32, shape, 0)
    col = lax.broadcasted_iota(jnp.int32, shape, 1)
    slot = BIAS_SLOT * (col & (FOX_HEADS - 1))
    piece = col >> 4
    sel_f = jnp.where(lane == slot + piece, 1.0, 0.0).astype(BF16)
    sel_e = jnp.where(lane == slot + 3 + piece, -1.0, 0.0).astype(BF16)
    in_slot = lax.broadcasted_iota(jnp.int32, (LANES, 1), 0) & (BIAS_SLOT - 1)
    one_f = jnp.where((in_slot >= 3) & (in_slot < 6), 1.0, 0.0)
    one_e = jnp.where(in_slot < 3, 1.0, 0.0)
    return sel_f, sel_e, one_f, one_e


def _stack3(x):
    return jnp.concatenate(_split3(x), axis=0)


def _bias_kernel(lft_ref, e_ref, f_ref, rng_ref, carry_ref, *, rows):
    r = lax.broadcasted_iota(jnp.int32, (rows, rows), 0)
    c = lax.broadcasted_iota(jnp.int32, (rows, rows), 1)
    upper = jnp.where(r <= c, 1.0, 0.0).astype(BF16)
    sel_f, sel_e, one_f, one_e = _bias_selectors()
    nh = FOX_HEADS

    @pl.when(pl.program_id(1) == 0)
    def _():
        carry_ref[...] = jnp.zeros_like(carry_ref)

    carry = carry_ref[:, 0:1]
    for sb in range(rng_ref.shape[0]):
        cols = slice(sb * rows, (sb + 1) * rows)
        st = _dot(_stack3(lft_ref[:, cols]), upper)
        cs = ((st[0:nh] + st[nh:2 * nh]) + st[2 * nh:3 * nh]) + carry
        carry = cs[:, rows - 1:rows]
        c2 = cs * LOG2E
        pieces = _stack3(c2)
        f_ref[cols, :] = (_dot(sel_f, pieces) + one_f).T.astype(BF16)
        e_ref[cols, :] = (_dot(sel_e, pieces) + one_e).T.astype(BF16)
        rng_ref[sb] = jnp.concatenate(
            [jnp.broadcast_to(jnp.min(c2, axis=1, keepdims=True), (nh, LANES)),
             jnp.broadcast_to(jnp.max(c2, axis=1, keepdims=True), (nh, LANES))], axis=0)
    carry_ref[...] = jnp.broadcast_to(carry, carry_ref.shape)


def _bias_tiles(lft, *, rows):
    B, H, L = lft.shape
    per_step = 4
    span = per_step * rows
    wide = pl.BlockSpec((None, span, LANES), lambda b, i: (b, i, 0))
    e, f, rng = pl.pallas_call(
        functools.partial(_bias_kernel, rows=rows),
        grid=(B, L // span),
        in_specs=[pl.BlockSpec((None, H, span), lambda b, i: (b, 0, i))],
        out_specs=[wide, wide, pl.BlockSpec((None, per_step, 2 * H, LANES), lambda b, i: (b, i, 0, 0))],
        out_shape=[jax.ShapeDtypeStruct((B, L, LANES), BF16)] * 2
        + [jax.ShapeDtypeStruct((B, L // rows, 2 * H, LANES), F32)],
        scratch_shapes=[pltpu.VMEM((H, LANES), F32)],
        compiler_params=_params(2),
        name="fox_cumsum_bias",
    )(lft)
    return e, f, rng[:, :, :H, 0], rng[:, :, H:, 0]


def _cumsum_short_kernel(lft_ref, ct_ref, *, cols):
    nbatch, nh, length = lft_ref.shape
    r = lax.broadcasted_iota(jnp.int32, (cols, cols), 0)
    c = lax.broadcasted_iota(jnp.int32, (cols, cols), 1)
    upper = jnp.where(r <= c, 1.0, 0.0).astype(BF16)
    carry = [jnp.zeros((nh, 1), F32)] * nbatch
    for c0 in range(0, length, cols):
        span = slice(c0, c0 + cols)
        st = _dot(jnp.concatenate([_stack3(lft_ref[bi, :, span]) for bi in range(nbatch)], axis=0), upper)
        for bi in range(nbatch):
            base = 3 * nh * bi
            cs = ((st[base:base + nh] + st[base + nh:base + 2 * nh]) + st[base + 2 * nh:base + 3 * nh]) + carry[bi]
            ct_ref[bi, :, span] = cs * LOG2E
            carry[bi] = cs[:, cols - 1:cols]


def _cumsum_short(lft, *, cols, per_step):
    B, H, L = lft.shape
    spec = pl.BlockSpec((per_step, H, L), lambda b: (b, 0, 0))
    return pl.pallas_call(
        functools.partial(_cumsum_short_kernel, cols=cols),
        grid=(B // per_step,),
        in_specs=[spec],
        out_specs=spec,
        out_shape=jax.ShapeDtypeStruct((B, H, L), F32),
        compiler_params=_params(1),
        name="fox_cumsum",
    )(lft)


def _skip_plan_kernel(kn_ref, cmin_ref, qn_ref, cmax_ref, knq_ref, first_ref, *, nb):
    j = lax.broadcasted_iota(jnp.int32, (nb, nb), 0).astype(F32)
    q = lax.broadcasted_iota(jnp.int32, (nb, nb), 1).astype(F32)
    rows = []
    for p in range(FOX_PAIRS):
        skippable = None
        for h in (2 * p, 2 * p + 1):
            qn = qn_ref[h:h + 1, :]
            bound = (kn_ref[:, h:h + 1] * qn + (cmax_ref[h:h + 1, :] - cmin_ref[:, h:h + 1])
                     + qn * knq_ref[h:h + 1, :])
            ok = bound < -SKIP_LOG2
            skippable = ok if skippable is None else (skippable & ok)
        must = (j < q) & jnp.logical_not(skippable)
        rows.append(jnp.min(jnp.where(must, j, q), axis=0, keepdims=True))
    first_ref[...] = jnp.concatenate(rows, axis=0).astype(jnp.int32)


def _skip_plan(kn, cmin, qn, cmax):
    B, nb, H = kn.shape
    by_block = pl.BlockSpec((None, nb, H), lambda b: (b, 0, 0))
    by_head = pl.BlockSpec((None, H, nb), lambda b: (b, 0, 0))
    t = lambda a: jnp.swapaxes(a, 1, 2)
    return pl.pallas_call(
        functools.partial(_skip_plan_kernel, nb=nb),
        grid=(B,),
        in_specs=[by_block, by_block, by_head, by_head, by_head],
        out_specs=pl.BlockSpec((None, FOX_PAIRS, nb), lambda b: (b, 0, 0)),
        out_shape=jax.ShapeDtypeStruct((B, FOX_PAIRS, nb), jnp.int32),
        compiler_params=_params(1),
        name="fox_skip_plan",
    )(kn, cmin, t(qn), t(cmax), t(kn))


def _stack_heads(q2):
    lane = lax.broadcasted_iota(jnp.int32, q2.shape, 1)
    zero = jnp.zeros_like(q2)
    return jnp.concatenate(
        [jnp.where(lane < FOX_HEAD_DIM, q2, zero), jnp.where(lane >= FOX_HEAD_DIM, q2, zero)], axis=0)


ATTN_TILE = 256
SUM_ROWS = 16


def _attn_prompt_kernel(n_ref, vq_ref, vj_ref, q_ref, f_ref, k_ref, e_ref, vt_ref, o_ref,
                        m_ref, acc_ref, sa_ref, xa_ref, sb_ref, xb_ref, *, blk, nb):
    b_id = pl.program_id(0)
    p_id = pl.program_id(1)
    hd = FOX_HEAD_DIM
    tw = ATTN_TILE
    per_head = blk // tw
    n_tiles = 2 * per_head

    lane = lax.broadcasted_iota(jnp.int32, (blk, LANES), 1)
    zero = jnp.zeros((blk, LANES), BF16)
    slot_a = 2 * BIAS_SLOT * p_id
    slot_b = slot_a + BIAS_SLOT
    ones_rows = jnp.ones((SUM_ROWS, blk), BF16)

    m_ref[...] = jnp.full(m_ref.shape, NEG_INF, F32)
    acc_ref[...] = jnp.zeros(acc_ref.shape, F32)

    def query_tiles(qi):
        r0 = pl.multiple_of(qi * blk, blk)
        q2 = q_ref[pl.ds(r0, blk), :]
        f2 = f_ref[pl.ds(r0, blk), :]
        heads = [
            jnp.concatenate([jnp.where(lane < hd, q2, zero),
                             jnp.where((lane >= slot_a) & (lane < slot_b), f2, zero)], axis=1),
            jnp.concatenate([jnp.where(lane >= hd, q2, zero),
                             jnp.where((lane >= slot_b) & (lane < slot_b + BIAS_SLOT), f2, zero)], axis=1),
        ]
        return [heads[t // per_head][(t % per_head) * tw:(t % per_head + 1) * tw] for t in range(n_tiles)]

    def stage(nxt, cur, diagonal):
        if nxt is not None:
            qi_n, j_n, (sn_ref, xn_ref) = nxt
            q_tiles = query_tiles(qi_n)
            k0 = pl.multiple_of(j_n * blk, blk)
            kaug = jnp.concatenate([k_ref[pl.ds(k0, blk), :], e_ref[pl.ds(k0, blk), :]], axis=1)
        if cur is not None:
            qi_c, j_c, (sc_ref, xc_ref) = cur
            vt = vt_ref[j_c]
            v_heads = [jnp.concatenate([vt[h * hd:(h + 1) * hd, :], ones_rows], axis=0) for h in range(2)]
        for t in range(n_tiles):
            c0 = (t % per_head) * tw
            live = c0 + tw if diagonal else blk
            if nxt is not None:
                s = _dot_nt(kaug[:live], q_tiles[t])
                if diagonal:
                    r = lax.broadcasted_iota(jnp.int32, (live, tw), 0)
                    c = lax.broadcasted_iota(jnp.int32, (live, tw), 1) + c0
                    s = jnp.where(r <= c, s, NEG_INF)
                sn_ref[t, 0:live, :] = s
                xn_ref[t] = jnp.max(s, axis=0, keepdims=True)
            if cur is not None:
                idx = qi_c * n_tiles + t
                m_old = m_ref[idx]
                m_new = jnp.maximum(m_old, xc_ref[t])
                alpha = jnp.exp2(m_old - m_new)
                p = jnp.exp2(sc_ref[t, 0:live, :] - m_new)
                m_ref[idx] = m_new
                pv = _dot(v_heads[t // per_head][:, 0:live], p.astype(BF16))
                acc_ref[idx] = acc_ref[idx] * alpha + pv

    buf_a = (sa_ref, xa_ref)
    buf_b = (sb_ref, xb_ref)

    stage((0, 0, buf_a), None, True)

    def diag_pair(n, carry):
        i0 = 2 * n
        i2 = jnp.minimum(i0 + 2, nb - 1)
        stage((i0 + 1, i0 + 1, buf_b), (i0, i0, buf_a), True)
        stage((i2, i2, buf_a), (i0 + 1, i0 + 1, buf_b), True)
        return carry

    lax.fori_loop(0, nb // 2, diag_pair, 0)

    n_off = n_ref[b_id, p_id]
    visit = lambda v, buf: (vq_ref[b_id, p_id, v], vj_ref[b_id, p_id, v], buf)

    @pl.when(n_off > 0)
    def _():
        stage(visit(0, buf_a), None, False)

    def off_pair(n, carry):
        v0 = 2 * n
        stage(visit(v0 + 1, buf_b), visit(v0, buf_a), False)
        stage(visit(v0 + 2, buf_a), visit(v0 + 1, buf_b), False)
        return carry

    lax.fori_loop(0, jnp.maximum(n_off - 1, 0) // 2, off_pair, 0)

    @pl.when((n_off > 0) & (n_off % 2 == 0))
    def _():
        stage(visit(n_off - 1, buf_b), visit(n_off - 2, buf_a), False)
        stage(None, visit(n_off - 1, buf_b), False)

    @pl.when(n_off % 2 == 1)
    def _():
        stage(None, visit(n_off - 1, buf_a), False)

    def finish(qi, carry):
        halves = [jnp.concatenate([acc_ref[qi * n_tiles + t, 0:hd, :] / acc_ref[qi * n_tiles + t, hd:hd + 1, :]
                                   for t in range(h * per_head, (h + 1) * per_head)], axis=1)
                  for h in range(2)]
        o_ref[pl.ds(pl.multiple_of(qi * blk, blk), blk), :] = jnp.concatenate(halves, axis=0).T
        return carry

    lax.fori_loop(0, nb, finish, 0)


def _visit_lists(first):
    B, P, nb = first.shape
    vmax = nb * (nb - 1) // 2 + 8
    q_idx = jnp.arange(nb, dtype=jnp.int32)
    cnt = q_idx - first
    incl = jnp.cumsum(cnt, axis=-1)
    excl = incl - cnt
    v = jnp.arange(vmax, dtype=jnp.int32)
    vq = jnp.minimum(jnp.sum(v[None, None, :, None] >= incl[:, :, None, :], axis=-1), nb - 1).astype(jnp.int32)
    pick = vq[..., None] == q_idx
    take = lambda a: jnp.sum(jnp.where(pick, a[:, :, None, :], 0), axis=-1)
    vj = jnp.clip(take(first) + (v - take(excl)), 0, nb - 1).astype(jnp.int32)
    return incl[..., -1].astype(jnp.int32), vq, vj


def _attn_prompt(first, q, f, kb, e, vt, *, blk):
    B, L, _ = q.shape
    nb = L // blk
    n_tiles = 2 * blk // ATTN_TILE
    n_off, vq, vj = _visit_lists(first)
    by_pair = pl.BlockSpec((None, L, LANES), lambda b, p, *_: (b, 0, p))
    shared = pl.BlockSpec((None, L, LANES), lambda b, p, *_: (b, 0, 0))
    return pl.pallas_call(
        functools.partial(_attn_prompt_kernel, blk=blk, nb=nb),
        grid_spec=pltpu.PrefetchScalarGridSpec(
            num_scalar_prefetch=3,
            grid=(B, FOX_PAIRS),
            in_specs=[by_pair, shared, by_pair, shared,
                      pl.BlockSpec((None, nb, LANES, blk), lambda b, p, *_: (b, 0, p, 0))],
            out_specs=by_pair,
            scratch_shapes=[
                pltpu.VMEM((nb * n_tiles, 1, ATTN_TILE), F32),
                pltpu.VMEM((nb * n_tiles, FOX_HEAD_DIM + SUM_ROWS, ATTN_TILE), F32),
                pltpu.VMEM((n_tiles, blk, ATTN_TILE), F32),
                pltpu.VMEM((n_tiles, 1, ATTN_TILE), F32),
                pltpu.VMEM((n_tiles, blk, ATTN_TILE), F32),
                pltpu.VMEM((n_tiles, 1, ATTN_TILE), F32),
            ],
        ),
        out_shape=jax.ShapeDtypeStruct((B, L, FOX_WIDTH), F32),
        compiler_params=_params(2),
        name="fox_attn_prompt",
    )(n_off, vq, vj, q, f, kb, e, vt)


def _attn_sample_kernel(q_ref, kn_ref, vn_ref, ck_ref, cv_ref, cq_ref, ct_ref, o_ref, *, past, ls):
    hd = FOX_HEAD_DIM
    pad_k = jnp.zeros((LANES - ls, LANES), BF16)
    cqb = cq_ref[...]
    lane = lax.broadcasted_iota(jnp.int32, cqb.shape, 1)
    r = lax.broadcasted_iota(jnp.int32, (2 * ls, LANES), 0) & (ls - 1)
    c = lax.broadcasted_iota(jnp.int32, (2 * ls, LANES), 1)
    lane_o = lax.broadcasted_iota(jnp.int32, (ls, LANES), 1)

    for p in range(FOX_PAIRS):
        cols = slice(p * LANES, (p + 1) * LANES)
        qst = _stack_heads(q_ref[:, cols])
        kc_t = ck_ref[cols, :].astype(BF16)
        vc_t = cv_ref[cols, :].astype(BF16)
        kn = jnp.concatenate([kn_ref[:, cols], pad_k], axis=0)
        vn = jnp.concatenate([vn_ref[:, cols].astype(BF16), pad_k], axis=0)
        cq = jnp.concatenate(
            [jnp.sum(jnp.where(lane == 2 * p + h2, cqb, 0.0), axis=1, keepdims=True) for h2 in range(2)],
            axis=0)
        cka = ct_ref[2 * p:2 * p + 1, :]
        ckb = ct_ref[2 * p + 1:2 * p + 2, :]

        def bias(lo, hi):
            return jnp.concatenate([cq[:ls] - cka[:, lo:hi], cq[ls:] - ckb[:, lo:hi]], axis=0)

        s_c = _dot(qst, kc_t) + bias(0, past)
        s_n = _dot_nt(qst, kn) + bias(past, past + LANES)
        s_n = jnp.where(c <= r, s_n, NEG_INF)
        m = jnp.maximum(jnp.max(s_c, axis=1, keepdims=True), jnp.max(s_n, axis=1, keepdims=True))
        p_c = jnp.exp2(s_c - m)
        p_n = jnp.exp2(s_n - m)
        l = jnp.sum(p_c, axis=1, keepdims=True) + jnp.sum(p_n, axis=1, keepdims=True)
        o = (_dot_nt(p_c.astype(BF16), vc_t) + _dot(p_n.astype(BF16), vn)) / l
        o_ref[:, cols] = jnp.where(lane_o < hd, o[:ls], o[ls:])


def _attn_sample(q, kb, v, cache_k, cache_v, cq, ct, *, past):
    B, ls, _ = q.shape
    tot = ct.shape[2]
    rows = lambda n: pl.BlockSpec((None, n, FOX_WIDTH), lambda b: (b, 0, 0))
    cache_t = pl.BlockSpec((None, FOX_WIDTH, past), lambda b: (b, 0, 0))
    return pl.pallas_call(
        functools.partial(_attn_sample_kernel, past=past, ls=ls),
        grid=(B,),
        in_specs=[
            rows(ls), rows(ls), rows(ls), cache_t, cache_t,
            pl.BlockSpec((None, ls, FOX_HEADS), lambda b: (b, 0, 0)),
            pl.BlockSpec((None, FOX_HEADS, tot), lambda b: (b, 0, 0)),
        ],
        out_specs=rows(ls),
        out_shape=jax.ShapeDtypeStruct((B, ls, FOX_WIDTH), F32),
        compiler_params=_params(1),
        name="fox_attn_sample",
    )(q, kb, v, cache_k, cache_v, cq, ct)


def _mid_kernel(o_ref, sg_ref, x_ref, wo_ref, g_ref, w_ref, wa1_ref, wa2_ref, ba_ref,
                y_ref, q_ref, k_ref, v_ref, sg2_ref, ga_ref):
    kw = GLA_KEY_WIDTH
    u = (o_ref[...] * sg_ref[...]).astype(BF16)
    y = x_ref[...] + _dot(u, wo_ref[...])
    y_ref[...] = y
    hb = _rms(y, g_ref[...]).astype(BF16)
    q_ref[...] = _dot_nt(hb, w_ref[0:kw, :]) * (GLA_HEAD_K ** -0.5)
    k_ref[...] = _dot_nt(hb, w_ref[kw:2 * kw, :])
    v_ref[...] = _dot_nt(hb, w_ref[2 * kw:2 * kw + GLA_VAL_WIDTH, :]).astype(BF16)
    gate = _dot_nt(hb, w_ref[2 * kw + GLA_VAL_WIDTH:2 * kw + 2 * GLA_VAL_WIDTH, :])
    sg2_ref[...] = gate * jax.nn.sigmoid(gate)
    a1 = _dot_nt(hb, wa1_ref[...]).astype(BF16)
    a = _dot(a1, wa2_ref[...]) + ba_ref[...]
    ga_ref[...] = _log_sigmoid(a) * (1.0 / GLA_GATE_TEMP)


def _mid(o, sg, x, w_out, g, w_main, w_a1, w_a2, b_a, *, tm):
    T = o.shape[0]
    tok = lambda w: pl.BlockSpec((tm, w), lambda i: (i, 0))
    full = lambda a: pl.BlockSpec(a.shape, lambda i: (0,) * a.ndim)
    return pl.pallas_call(
        _mid_kernel,
        grid=(T // tm,),
        in_specs=[tok(D_MODEL), tok(D_MODEL), tok(D_MODEL), full(w_out), full(g), full(w_main),
                  full(w_a1), full(w_a2), full(b_a)],
        out_specs=[tok(D_MODEL), tok(GLA_KEY_WIDTH), tok(GLA_KEY_WIDTH), tok(GLA_VAL_WIDTH),
                   tok(GLA_VAL_WIDTH), tok(GLA_KEY_WIDTH)],
        out_shape=[
            jax.ShapeDtypeStruct((T, D_MODEL), F32),
            jax.ShapeDtypeStruct((T, GLA_KEY_WIDTH), F32),
            jax.ShapeDtypeStruct((T, GLA_KEY_WIDTH), F32),
            jax.ShapeDtypeStruct((T, GLA_VAL_WIDTH), BF16),
            jax.ShapeDtypeStruct((T, GLA_VAL_WIDTH), F32),
            jax.ShapeDtypeStruct((T, GLA_KEY_WIDTH), F32),
        ],
        compiler_params=_params(1),
        name="fox_out_gla_proj",
    )(o, sg, x, w_out, g, w_main, w_a1, w_a2, b_a)


def _gla_kernel(q_ref, k_ref, v_ref, ga_ref, s0_ref, o_ref, s_ref, *, n_chunks, chained):
    i = pl.program_id(1)
    ch = GLA_CHUNK
    dk, dv = GLA_HEAD_K, GLA_HEAD_V

    @pl.when(i == 0)
    def _():
        s_ref[...] = s0_ref[...]

    t = n_chunks * ch
    r = lax.broadcasted_iota(jnp.int32, (t, t), 0)
    c = lax.broadcasted_iota(jnp.int32, (t, t), 1)
    causal = ((r >> 6) == (c >> 6)) & (r >= c)
    tril = jnp.where(causal, 1.0, 0.0).astype(BF16)
    eye = (lax.broadcasted_iota(jnp.int32, (dk, dk), 0) == lax.broadcasted_iota(jnp.int32, (dk, dk), 1))

    b = _tril_cumsum(tril, ga_ref[...])
    lasts = [b[(ci + 1) * ch - 1:(ci + 1) * ch, :] for ci in range(n_chunks)]
    b_last = jnp.concatenate([jnp.broadcast_to(x, (ch, GLA_KEY_WIDTH)) for x in lasts], axis=0)
    kk = k_ref[...]
    qe = (q_ref[...] * jnp.exp(b)).astype(BF16)
    ke = (kk * jnp.exp(-b)).astype(BF16)
    kd = (kk * jnp.exp(b_last - b)).astype(BF16)
    decs = [jnp.exp(x) for x in lasts]

    for h in range(GLA_HEADS):
        ks = slice(h * dk, (h + 1) * dk)
        vs = slice(h * dv, (h + 1) * dv)
        vh = v_ref[:, vs]
        a = jnp.where(causal, _dot_nt(qe[:, ks], ke[:, ks]), 0.0)
        o_intra = _dot(a.astype(BF16), vh)
        s = s_ref[0, h]
        for ci in range(n_chunks):
            rows = slice(ci * ch, (ci + 1) * ch)
            if not chained:
                s = s_ref[ci, h]
            o_ref[rows, vs] = o_intra[rows] + _dot(qe[rows, ks], s.astype(BF16))
            dec_col = jnp.sum(jnp.where(eye, jnp.broadcast_to(decs[ci][:, ks], (dk, dk)), 0.0),
                              axis=1, keepdims=True)
            s = dec_col * s + _dot_tn(kd[rows, ks], vh[rows])
            if not chained:
                s_ref[ci, h] = s
        if chained:
            s_ref[0, h] = s


def _gla(q, k, v, ga, s0, *, tc):
    B, L, _ = q.shape
    n_chunks = tc // GLA_CHUNK
    chained = L > GLA_CHUNK
    if chained:
        n_states = 1
    else:
        n_states = n_chunks
        fold = lambda a: a.reshape(B // n_chunks, tc, a.shape[-1])
        q, k, v, ga = fold(q), fold(k), fold(v), fold(ga)
    G, T, _ = q.shape
    tok = lambda w: pl.BlockSpec((None, tc, w), lambda b, i: (b, i, 0))
    st = pl.BlockSpec((n_states, GLA_HEADS, GLA_HEAD_K, GLA_HEAD_V), lambda b, i: (b, 0, 0, 0))
    o, s = pl.pallas_call(
        functools.partial(_gla_kernel, n_chunks=n_chunks, chained=chained),
        grid=(G, T // tc),
        in_specs=[tok(GLA_KEY_WIDTH), tok(GLA_KEY_WIDTH), tok(GLA_VAL_WIDTH), tok(GLA_KEY_WIDTH), st],
        out_specs=[tok(GLA_VAL_WIDTH), st],
        out_shape=[
            jax.ShapeDtypeStruct((G, T, GLA_VAL_WIDTH), F32),
            jax.ShapeDtypeStruct((B, GLA_HEADS, GLA_HEAD_K, GLA_HEAD_V), F32),
        ],
        compiler_params=_params(2),
        name="gla_chunk",
    )(q, k, v, ga, s0)
    return o.reshape(B, L, GLA_VAL_WIDTH), s


def _gla_out_kernel(o_ref, sg_ref, y_ref, go_ref, wo_ref, gf_ref, out_ref, *, chunk_major):
    dv = GLA_HEAD_V
    if chunk_major:
        o = jnp.concatenate([o_ref[:, t, :] for t in range(o_ref.shape[1])], axis=0)
    else:
        o = o_ref[...]
    parts = []
    for h in range(GLA_HEADS):
        vs = slice(h * dv, (h + 1) * dv)
        parts.append((_rms(o[:, vs], go_ref[...]) * sg_ref[:, vs]).astype(BF16))
    u = jnp.concatenate(parts, axis=1)
    y = y_ref[...] + _dot(u, wo_ref[...])
    out_ref[...] = _rms(y, gf_ref[...])


def _gla_out(o, sg, y, g_o, w_out, g_f, *, tm):
    B, L, _ = o.shape
    n = L // GLA_CHUNK
    chunk_major = n > 1
    if chunk_major:
        tm = 8 * n
        o = o.reshape(B, n, GLA_CHUNK, D_MODEL)
        o_spec = pl.BlockSpec((None, n, 8, D_MODEL), lambda b, i: (b, 0, i, 0))
    else:
        flat = lambda a: a.reshape(1, B * L, D_MODEL)
        o, sg, y = flat(o), flat(sg), flat(y)
        o_spec = pl.BlockSpec((None, tm, D_MODEL), lambda b, i: (b, i, 0))
    nb, rows, _ = sg.shape
    tok = pl.BlockSpec((None, tm, D_MODEL), lambda b, i: (b, i, 0))
    full = lambda a: pl.BlockSpec(a.shape, lambda b, i: (0,) * a.ndim)
    return pl.pallas_call(
        functools.partial(_gla_out_kernel, chunk_major=chunk_major),
        grid=(nb, rows // tm),
        in_specs=[o_spec, tok, tok, full(g_o), full(w_out), full(g_f)],
        out_specs=tok,
        out_shape=jax.ShapeDtypeStruct((nb, rows, D_MODEL), F32),
        compiler_params=_params(2),
        name="gla_out",
    )(o, sg, y, g_o, w_out, g_f)


def _pad_cols(w, n):
    return jnp.pad(w, ((0, 0), (0, n - w.shape[1])))


def kernel(x_prompt, x_sample, cache_fox_k, cache_fox_v, cache_fox_logf, state_gla,
           g_norm_fox, w_in_fox, b_fox_f, w_out_fox,
           g_norm_gla, w_in_gla, w_gla_a2, b_gla_a, g_gla_o, w_out_gla, g_final):
    B, L, D = x_prompt.shape
    Bs, Ls, _ = x_sample.shape
    P = cache_fox_k.shape[1]
    H, hd = FOX_HEADS, FOX_HEAD_DIM

    row = lambda a: a.reshape(1, -1).astype(F32)
    pad_rows = lambda w: jnp.pad(w, ((0, LANES - w.shape[0]), (0, 0)))
    w_fox_t = w_in_fox.T
    w_fox = w_fox_t[:4 * FOX_WIDTH].astype(BF16)
    w_fox_f = pad_rows(w_fox_t[4 * FOX_WIDTH:]).astype(BF16)
    b_f = _pad_cols(row(b_fox_f), LANES)
    w_o_fox = w_out_fox.astype(BF16)
    n_main = 2 * GLA_KEY_WIDTH + 2 * GLA_VAL_WIDTH
    w_gla_t = w_in_gla.T
    w_gla = w_gla_t[:n_main].astype(BF16)
    w_a1 = pad_rows(w_gla_t[n_main:]).astype(BF16)
    w_a2 = pad_rows(w_gla_a2).astype(BF16)
    w_o_gla = w_out_gla.astype(BF16)

    q_p, k_p, v_p, kb_p, sg_p, lft_p, vt_p, nrm_p = _fox_proj(
        x_prompt, row(g_norm_fox), w_fox, w_fox_f, b_f, tm=ATTN_BLOCK, want_vt=True)
    e_p, f_p, cmin_p, cmax_p = _bias_tiles(lft_p, rows=ATTN_BLOCK)
    first_p = _skip_plan(nrm_p[:, :, 0, :H], cmin_p, nrm_p[:, :, 1, :H], cmax_p)
    o_p = _attn_prompt(first_p, q_p, f_p, kb_p, e_p, vt_p, blk=ATTN_BLOCK)
    lf_p = jnp.swapaxes(lft_p, 1, 2)

    xs = x_sample.reshape(1, Bs * Ls, D)
    q_s, k_s, v_s, kb_s, sg_s, lft_s = _fox_proj(
        xs, row(g_norm_fox), w_fox, w_fox_f, b_f, tm=PROJ_TM, want_vt=False)
    sh = lambda a: a.reshape(Bs, Ls, a.shape[-1])
    q_s, k_s, v_s, kb_s = sh(q_s), sh(k_s), sh(v_s), sh(kb_s)
    lft_s = jnp.swapaxes(lft_s.reshape(H, Bs, Ls), 0, 1)
    lf_s = jnp.swapaxes(lft_s, 1, 2)
    lft_all = jnp.concatenate(
        [jnp.swapaxes(cache_fox_logf.astype(F32), 1, 2), lft_s, jnp.zeros((Bs, H, LANES - Ls), F32)], axis=2)
    ct_s = _cumsum_short(lft_all, cols=(P + LANES) // 3, per_step=4)
    cache_t = lambda a: jnp.transpose(a, (0, 2, 3, 1)).reshape(Bs, FOX_WIDTH, P)
    o_s = _attn_sample(q_s, kb_s, v_s, cache_t(cache_fox_k), cache_t(cache_fox_v),
                       jnp.swapaxes(ct_s[:, :, P:P + Ls], 1, 2), ct_s, past=P)

    def gla_layer(o, sg, x, s0, tc):
        b, l, _ = x.shape
        flat = lambda a: a.reshape(b * l, a.shape[-1])
        y1, q, k, v, sg2, ga = _mid(flat(o), flat(sg), flat(x), w_o_fox, row(g_norm_gla), w_gla,
                                    w_a1, w_a2, row(b_gla_a), tm=PROJ_TM)
        un = lambda a: a.reshape(b, l, a.shape[-1])
        og, s_out = _gla(un(q), un(k), un(v), un(ga), s0, tc=tc)
        y = _gla_out(og, un(sg2), un(y1), row(g_gla_o), w_o_gla, row(g_final), tm=PROJ_TM)
        return y.reshape(b, l, D), s_out

    s0_p = jnp.zeros((B, GLA_HEADS, GLA_HEAD_K, GLA_HEAD_V), F32)
    y_p, s_p = gla_layer(o_p, sg_p, x_prompt, s0_p, GLA_TC)
    y_s, s_s = gla_layer(o_s, sg_s.reshape(Bs, Ls, D), x_sample, state_gla.astype(F32), GLA_TC)

    return (y_p, y_s,
            k_p.reshape(B, L, H, hd), v_p.reshape(B, L, H, hd), lf_p, s_p.astype(state_gla.dtype),
            k_s.reshape(Bs, Ls, H, hd), v_s.reshape(Bs, Ls, H, hd), lf_s, s_s.astype(state_gla.dtype))
```

```python
import functools

import jax
import jax.numpy as jnp
from jax import lax
from jax.experimental import pallas as pl
from jax.experimental.pallas import tpu as pltpu

F32 = jnp.float32
BF16 = jnp.bfloat16

D_MODEL = 1024
EPS = 1e-6
NEG_INF = -1e30
LOG2E = 1.4426950408889634
NORM_SLACK = 1.01
SKIP_LOG2 = 152.0

FOX_HEADS = 16
FOX_HEAD_DIM = 64
FOX_WIDTH = FOX_HEADS * FOX_HEAD_DIM
FOX_PAIRS = FOX_HEADS // 2

GLA_HEADS = 4
GLA_KEY_WIDTH = 512
GLA_VAL_WIDTH = 1024
GLA_HEAD_K = 128
GLA_HEAD_V = 256
GLA_GATE_RANK = 16
GLA_GATE_TEMP = 16.0
GLA_CHUNK = 64

LANES = 128
VMEM_LIMIT = 56 * 1024 * 1024

PROJ_TM = 512
ATTN_BLOCK = 512
GLA_GROUP = 4
GLA_TC = 512


def _params(n_axes, flags=None):
    return pltpu.CompilerParams(
        dimension_semantics=("arbitrary",) * n_axes,
        vmem_limit_bytes=VMEM_LIMIT,
        flags=flags,
    )


def _dot(a, b):
    return jnp.dot(a, b, preferred_element_type=F32)


def _dot_nt(a, b):
    return lax.dot_general(a, b, (((1,), (1,)), ((), ())), preferred_element_type=F32)


def _dot_tn(a, b):
    return lax.dot_general(a, b, (((0,), (0,)), ((), ())), preferred_element_type=F32)


def _log_sigmoid(z):
    return jnp.minimum(z, 0.0) - jnp.log1p(jnp.exp(-jnp.abs(z)))


def _rms(x, g):
    ms = jnp.mean(x * x, axis=-1, keepdims=True)
    return (x * lax.rsqrt(ms + EPS)) * g


def _split3(x):
    hi = x.astype(BF16)
    r1 = x - hi.astype(F32)
    mid = r1.astype(BF16)
    lo = (r1 - mid.astype(F32)).astype(BF16)
    return hi, mid, lo


def _tril_cumsum(tril_bf16, x):
    hi, mid, lo = _split3(x)
    return (_dot(tril_bf16, hi) + _dot(tril_bf16, mid)) + _dot(tril_bf16, lo)


def _fox_proj_kernel(x_ref, g_ref, w_ref, wf_ref, bf_ref, *rest, want_vt):
    if want_vt:
        q_ref, k_ref, v_ref, kb_ref, sg_ref, lft_ref, vt_ref, nrm_ref = rest
    else:
        q_ref, k_ref, v_ref, kb_ref, sg_ref, lft_ref = rest
    fw = FOX_WIDTH
    hb = _rms(x_ref[...], g_ref[...]).astype(BF16)
    q = _dot_nt(hb, w_ref[0:fw, :])
    qb = (q * (LOG2E * FOX_HEAD_DIM ** -0.5)).astype(BF16)
    q_ref[...] = qb
    k = _dot_nt(hb, w_ref[fw:2 * fw, :])
    k_ref[...] = k
    kb = k.astype(BF16)
    kb_ref[...] = kb
    if want_vt:
        row = lax.broadcasted_iota(jnp.int32, (FOX_WIDTH, LANES), 0)
        col = lax.broadcasted_iota(jnp.int32, (FOX_WIDTH, LANES), 1)
        head_of = jnp.where((row >> 6) == col, 1.0, 0.0).astype(BF16)

        def bound(xb):
            xf = xb.astype(F32)
            sq = _dot((xf * xf).astype(BF16), head_of)
            return jnp.sqrt(jnp.max(sq, axis=0, keepdims=True)) * NORM_SLACK

        nrm_ref[...] = jnp.concatenate(
            [bound(kb), bound(qb), jnp.zeros((6, LANES), F32)], axis=0)
    v = _dot_nt(hb, w_ref[2 * fw:3 * fw, :])
    v_ref[...] = v
    if want_vt:
        vt_ref[...] = v.astype(BF16).T
    gate = _dot_nt(hb, w_ref[3 * fw:4 * fw, :])
    sg_ref[...] = gate * jax.nn.sigmoid(gate)
    fl = _dot_nt(hb, wf_ref[...]) + bf_ref[...]
    lft_ref[...] = _log_sigmoid(fl).T[:FOX_HEADS, :]


def _fox_proj(x, g, w_main, w_f, b_f, *, tm, want_vt):
    B, L, _ = x.shape
    n = L // tm
    tok = lambda w: pl.BlockSpec((None, tm, w), lambda b, i: (b, i, 0))
    full = lambda a: pl.BlockSpec(a.shape, lambda b, i: (0,) * a.ndim)
    in_specs = [tok(D_MODEL), full(g), full(w_main), full(w_f), full(b_f)]
    args = [x, g, w_main, w_f, b_f]
    out_specs = [tok(FOX_WIDTH)] * 5 + [pl.BlockSpec((None, FOX_HEADS, tm), lambda b, i: (b, 0, i))]
    out_shape = [
        jax.ShapeDtypeStruct((B, L, FOX_WIDTH), BF16),
        jax.ShapeDtypeStruct((B, L, FOX_WIDTH), F32),
        jax.ShapeDtypeStruct((B, L, FOX_WIDTH), F32),
        jax.ShapeDtypeStruct((B, L, FOX_WIDTH), BF16),
        jax.ShapeDtypeStruct((B, L, FOX_WIDTH), F32),
        jax.ShapeDtypeStruct((B, FOX_HEADS, L), F32),
    ]
    if want_vt:
        out_specs.append(pl.BlockSpec((None, None, FOX_WIDTH, tm), lambda b, i: (b, i, 0, 0)))
        out_shape.append(jax.ShapeDtypeStruct((B, n, FOX_WIDTH, tm), BF16))
        out_specs.append(pl.BlockSpec((None, None, 8, LANES), lambda b, i: (b, i, 0, 0)))
        out_shape.append(jax.ShapeDtypeStruct((B, n, 8, LANES), F32))
    return pl.pallas_call(
        functools.partial(_fox_proj_kernel, want_vt=want_vt),
        grid=(B, n),
        in_specs=in_specs,
        out_specs=out_specs,
        out_shape=out_shape,
        compiler_params=_params(2),
        name="fox_proj_vt" if want_vt else "fox_proj",
    )(*args)


BIAS_SLOT = 8


def _bias_selectors():
    shape = (LANES, 3 * FOX_HEADS)
    lane = lax.broadcasted_iota(jnp.int32, shape, 0)
    col = lax.broadcasted_iota(jnp.int32, shape, 1)
    slot = BIAS_SLOT * (col & (FOX_HEADS - 1))
    piece = col >> 4
    sel_f = jnp.where(lane == slot + piece, 1.0, 0.0).astype(BF16)
    sel_e = jnp.where(lane == slot + 3 + piece, -1.0, 0.0).astype(BF16)
    in_slot = lax.broadcasted_iota(jnp.int32, (LANES, 1), 0) & (BIAS_SLOT - 1)
    one_f = jnp.where((in_slot >= 3) & (in_slot < 6), 1.0, 0.0)
    one_e = jnp.where(in_slot < 3, 1.0, 0.0)
    return sel_f, sel_e, one_f, one_e


def _stack3(x):
    return jnp.concatenate(_split3(x), axis=0)


def _bias_kernel(lft_ref, e_ref, f_ref, rng_ref, carry_ref, *, rows):
    r = lax.broadcasted_iota(jnp.int32, (rows, rows), 0)
    c = lax.broadcasted_iota(jnp.int32, (rows, rows), 1)
    upper = jnp.where(r <= c, 1.0, 0.0).astype(BF16)
    sel_f, sel_e, one_f, one_e = _bias_selectors()
    nh = FOX_HEADS

    @pl.when(pl.program_id(1) == 0)
    def _():
        carry_ref[...] = jnp.zeros_like(carry_ref)

    carry = carry_ref[:, 0:1]
    for sb in range(rng_ref.shape[0]):
        cols = slice(sb * rows, (sb + 1) * rows)
        st = _dot(_stack3(lft_ref[:, cols]), upper)
        cs = ((st[0:nh] + st[nh:2 * nh]) + st[2 * nh:3 * nh]) + carry
        carry = cs[:, rows - 1:rows]
        c2 = cs * LOG2E
        pieces = _stack3(c2)
        f_ref[cols, :] = (_dot(sel_f, pieces) + one_f).T.astype(BF16)
        e_ref[cols, :] = (_dot(sel_e, pieces) + one_e).T.astype(BF16)
        rng_ref[sb] = jnp.concatenate(
            [jnp.broadcast_to(jnp.min(c2, axis=1, keepdims=True), (nh, LANES)),
             jnp.broadcast_to(jnp.max(c2, axis=1, keepdims=True), (nh, LANES))], axis=0)
    carry_ref[...] = jnp.broadcast_to(carry, carry_ref.shape)


def _bias_tiles(lft, *, rows):
    B, H, L = lft.shape
    per_step = 4
    span = per_step * rows
    wide = pl.BlockSpec((None, span, LANES), lambda b, i: (b, i, 0))
    e, f, rng = pl.pallas_call(
        functools.partial(_bias_kernel, rows=rows),
        grid=(B, L // span),
        in_specs=[pl.BlockSpec((None, H, span), lambda b, i: (b, 0, i))],
        out_specs=[wide, wide, pl.BlockSpec((None, per_step, 2 * H, LANES), lambda b, i: (b, i, 0, 0))],
        out_shape=[jax.ShapeDtypeStruct((B, L, LANES), BF16)] * 2
        + [jax.ShapeDtypeStruct((B, L // rows, 2 * H, LANES), F32)],
        scratch_shapes=[pltpu.VMEM((H, LANES), F32)],
        compiler_params=_params(2),
        name="fox_cumsum_bias",
    )(lft)
    return e, f, rng[:, :, :H, 0], rng[:, :, H:, 0]


def _cumsum_short_kernel(lft_ref, ct_ref, *, cols):
    nbatch, nh, length = lft_ref.shape
    r = lax.broadcasted_iota(jnp.int32, (cols, cols), 0)
    c = lax.broadcasted_iota(jnp.int32, (cols, cols), 1)
    upper = jnp.where(r <= c, 1.0, 0.0).astype(BF16)
    carry = [jnp.zeros((nh, 1), F32)] * nbatch
    for c0 in range(0, length, cols):
        span = slice(c0, c0 + cols)
        st = _dot(jnp.concatenate([_stack3(lft_ref[bi, :, span]) for bi in range(nbatch)], axis=0), upper)
        for bi in range(nbatch):
            base = 3 * nh * bi
            cs = ((st[base:base + nh] + st[base + nh:base + 2 * nh]) + st[base + 2 * nh:base + 3 * nh]) + carry[bi]
            ct_ref[bi, :, span] = cs * LOG2E
            carry[bi] = cs[:, cols - 1:cols]


def _cumsum_short(lft, *, cols, per_step):
    B, H, L = lft.shape
    spec = pl.BlockSpec((per_step, H, L), lambda b: (b, 0, 0))
    return pl.pallas_call(
        functools.partial(_cumsum_short_kernel, cols=cols),
        grid=(B // per_step,),
        in_specs=[spec],
        out_specs=spec,
        out_shape=jax.ShapeDtypeStruct((B, H, L), F32),
        compiler_params=_params(1),
        name="fox_cumsum",
    )(lft)


def _skip_plan_kernel(kn_ref, cmin_ref, qn_ref, cmax_ref, knq_ref, first_ref, *, nb):
    j = lax.broadcasted_iota(jnp.int32, (nb, nb), 0).astype(F32)
    q = lax.broadcasted_iota(jnp.int32, (nb, nb), 1).astype(F32)
    rows = []
    for p in range(FOX_PAIRS):
        skippable = None
        for h in (2 * p, 2 * p + 1):
            qn = qn_ref[h:h + 1, :]
            bound = (kn_ref[:, h:h + 1] * qn + (cmax_ref[h:h + 1, :] - cmin_ref[:, h:h + 1])
                     + qn * knq_ref[h:h + 1, :])
            ok = bound < -SKIP_LOG2
            skippable = ok if skippable is None else (skippable & ok)
        must = (j < q) & jnp.logical_not(skippable)
        rows.append(jnp.min(jnp.where(must, j, q), axis=0, keepdims=True))
    first_ref[...] = jnp.concatenate(rows, axis=0).astype(jnp.int32)


def _skip_plan(kn, cmin, qn, cmax):
    B, nb, H = kn.shape
    by_block = pl.BlockSpec((None, nb, H), lambda b: (b, 0, 0))
    by_head = pl.BlockSpec((None, H, nb), lambda b: (b, 0, 0))
    t = lambda a: jnp.swapaxes(a, 1, 2)
    return pl.pallas_call(
        functools.partial(_skip_plan_kernel, nb=nb),
        grid=(B,),
        in_specs=[by_block, by_block, by_head, by_head, by_head],
        out_specs=pl.BlockSpec((None, FOX_PAIRS, nb), lambda b: (b, 0, 0)),
        out_shape=jax.ShapeDtypeStruct((B, FOX_PAIRS, nb), jnp.int32),
        compiler_params=_params(1),
        name="fox_skip_plan",
    )(kn, cmin, t(qn), t(cmax), t(kn))


def _stack_heads(q2):
    lane = lax.broadcasted_iota(jnp.int32, q2.shape, 1)
    zero = jnp.zeros_like(q2)
    return jnp.concatenate(
        [jnp.where(lane < FOX_HEAD_DIM, q2, zero), jnp.where(lane >= FOX_HEAD_DIM, q2, zero)], axis=0)


ATTN_TILE = 256
SUM_ROWS = 16


def _attn_prompt_kernel(n_ref, vq_ref, vj_ref, q_ref, f_ref, k_ref, e_ref, vt_ref, o_ref,
                        m_ref, acc_ref, sa_ref, xa_ref, sb_ref, xb_ref, *, blk, nb):
    b_id = pl.program_id(0)
    p_id = pl.program_id(1)
    hd = FOX_HEAD_DIM
    tw = ATTN_TILE
    per_head = blk // tw
    n_tiles = 2 * per_head

    lane = lax.broadcasted_iota(jnp.int32, (blk, LANES), 1)
    zero = jnp.zeros((blk, LANES), BF16)
    slot_a = 2 * BIAS_SLOT * p_id
    slot_b = slot_a + BIAS_SLOT
    ones_rows = jnp.ones((SUM_ROWS, blk), BF16)

    m_ref[...] = jnp.full(m_ref.shape, NEG_INF, F32)
    acc_ref[...] = jnp.zeros(acc_ref.shape, F32)

    def query_tiles(qi):
        r0 = pl.multiple_of(qi * blk, blk)
        q2 = q_ref[pl.ds(r0, blk), :]
        f2 = f_ref[pl.ds(r0, blk), :]
        heads = [
            jnp.concatenate([jnp.where(lane < hd, q2, zero),
                             jnp.where((lane >= slot_a) & (lane < slot_b), f2, zero)], axis=1),
            jnp.concatenate([jnp.where(lane >= hd, q2, zero),
                             jnp.where((lane >= slot_b) & (lane < slot_b + BIAS_SLOT), f2, zero)], axis=1),
        ]
        return [heads[t // per_head][(t % per_head) * tw:(t % per_head + 1) * tw] for t in range(n_tiles)]

    def stage(nxt, cur, diagonal):
        if nxt is not None:
            qi_n, j_n, (sn_ref, xn_ref) = nxt
            q_tiles = query_tiles(qi_n)
            k0 = pl.multiple_of(j_n * blk, blk)
            kaug = jnp.concatenate([k_ref[pl.ds(k0, blk), :], e_ref[pl.ds(k0, blk), :]], axis=1)
        if cur is not None:
            qi_c, j_c, (sc_ref, xc_ref) = cur
            vt = vt_ref[j_c]
            v_heads = [jnp.concatenate([vt[h * hd:(h + 1) * hd, :], ones_rows], axis=0) for h in range(2)]
        for t in range(n_tiles):
            c0 = (t % per_head) * tw
            live = c0 + tw if diagonal else blk
            if nxt is not None:
                s = _dot_nt(kaug[:live], q_tiles[t])
                if diagonal:
                    r = lax.broadcasted_iota(jnp.int32, (live, tw), 0)
                    c = lax.broadcasted_iota(jnp.int32, (live, tw), 1) + c0
                    s = jnp.where(r <= c, s, NEG_INF)
                sn_ref[t, 0:live, :] = s
                xn_ref[t] = jnp.max(s, axis=0, keepdims=True)
            if cur is not None:
                idx = qi_c * n_tiles + t
                m_old = m_ref[idx]
                m_new = jnp.maximum(m_old, xc_ref[t])
                alpha = jnp.exp2(m_old - m_new)
                p = jnp.exp2(sc_ref[t, 0:live, :] - m_new)
                m_ref[idx] = m_new
                pv = _dot(v_heads[t // per_head][:, 0:live], p.astype(BF16))
                acc_ref[idx] = acc_ref[idx] * alpha + pv

    buf_a = (sa_ref, xa_ref)
    buf_b = (sb_ref, xb_ref)

    stage((0, 0, buf_a), None, True)

    def diag_pair(n, carry):
        i0 = 2 * n
        i2 = jnp.minimum(i0 + 2, nb - 1)
        stage((i0 + 1, i0 + 1, buf_b), (i0, i0, buf_a), True)
        stage((i2, i2, buf_a), (i0 + 1, i0 + 1, buf_b), True)
        return carry

    lax.fori_loop(0, nb // 2, diag_pair, 0)

    n_off = n_ref[b_id, p_id]
    visit = lambda v, buf: (vq_ref[b_id, p_id, v], vj_ref[b_id, p_id, v], buf)

    @pl.when(n_off > 0)
    def _():
        stage(visit(0, buf_a), None, False)

    def off_pair(n, carry):
        v0 = 2 * n
        stage(visit(v0 + 1, buf_b), visit(v0, buf_a), False)
        stage(visit(v0 + 2, buf_a), visit(v0 + 1, buf_b), False)
        return carry

    lax.fori_loop(0, jnp.maximum(n_off - 1, 0) // 2, off_pair, 0)

    @pl.when((n_off > 0) & (n_off % 2 == 0))
    def _():
        stage(visit(n_off - 1, buf_b), visit(n_off - 2, buf_a), False)
        stage(None, visit(n_off - 1, buf_b), False)

    @pl.when(n_off % 2 == 1)
    def _():
        stage(None, visit(n_off - 1, buf_a), False)

    def finish(qi, carry):
        halves = [jnp.concatenate([acc_ref[qi * n_tiles + t, 0:hd, :] / acc_ref[qi * n_tiles + t, hd:hd + 1, :]
                                   for t in range(h * per_head, (h + 1) * per_head)], axis=1)
                  for h in range(2)]
        o_ref[pl.ds(pl.multiple_of(qi * blk, blk), blk), :] = jnp.concatenate(halves, axis=0).T
        return carry

    lax.fori_loop(0, nb, finish, 0)


def _visit_lists(first):
    B, P, nb = first.shape
    vmax = nb * (nb - 1) // 2 + 8
    q_idx = jnp.arange(nb, dtype=jnp.int32)
    cnt = q_idx - first
    incl = jnp.cumsum(cnt, axis=-1)
    excl = incl - cnt
    v = jnp.arange(vmax, dtype=jnp.int32)
    vq = jnp.minimum(jnp.sum(v[None, None, :, None] >= incl[:, :, None, :], axis=-1), nb - 1).astype(jnp.int32)
    pick = vq[..., None] == q_idx
    take = lambda a: jnp.sum(jnp.where(pick, a[:, :, None, :], 0), axis=-1)
    vj = jnp.clip(take(first) + (v - take(excl)), 0, nb - 1).astype(jnp.int32)
    return incl[..., -1].astype(jnp.int32), vq, vj


def _attn_prompt(first, q, f, kb, e, vt, *, blk):
    B, L, _ = q.shape
    nb = L // blk
    n_tiles = 2 * blk // ATTN_TILE
    n_off, vq, vj = _visit_lists(first)
    by_pair = pl.BlockSpec((None, L, LANES), lambda b, p, *_: (b, 0, p))
    shared = pl.BlockSpec((None, L, LANES), lambda b, p, *_: (b, 0, 0))
    return pl.pallas_call(
        functools.partial(_attn_prompt_kernel, blk=blk, nb=nb),
        grid_spec=pltpu.PrefetchScalarGridSpec(
            num_scalar_prefetch=3,
            grid=(B, FOX_PAIRS),
            in_specs=[by_pair, shared, by_pair, shared,
                      pl.BlockSpec((None, nb, LANES, blk), lambda b, p, *_: (b, 0, p, 0))],
            out_specs=by_pair,
            scratch_shapes=[
                pltpu.VMEM((nb * n_tiles, 1, ATTN_TILE), F32),
                pltpu.VMEM((nb * n_tiles, FOX_HEAD_DIM + SUM_ROWS, ATTN_TILE), F32),
                pltpu.VMEM((n_tiles, blk, ATTN_TILE), F32),
                pltpu.VMEM((n_tiles, 1, ATTN_TILE), F32),
                pltpu.VMEM((n_tiles, blk, ATTN_TILE), F32),
                pltpu.VMEM((n_tiles, 1, ATTN_TILE), F32),
            ],
        ),
        out_shape=jax.ShapeDtypeStruct((B, L, FOX_WIDTH), F32),
        compiler_params=_params(2),
        name="fox_attn_prompt",
    )(n_off, vq, vj, q, f, kb, e, vt)


def _attn_sample_kernel(q_ref, kn_ref, vn_ref, ck_ref, cv_ref, cq_ref, ct_ref, o_ref, *, past, ls):
    hd = FOX_HEAD_DIM
    pad_k = jnp.zeros((LANES - ls, LANES), BF16)
    cqb = cq_ref[...]
    lane = lax.broadcasted_iota(jnp.int32, cqb.shape, 1)
    r = lax.broadcasted_iota(jnp.int32, (2 * ls, LANES), 0) & (ls - 1)
    c = lax.broadcasted_iota(jnp.int32, (2 * ls, LANES), 1)
    lane_o = lax.broadcasted_iota(jnp.int32, (ls, LANES), 1)

    for p in range(FOX_PAIRS):
        cols = slice(p * LANES, (p + 1) * LANES)
        qst = _stack_heads(q_ref[:, cols])
        kc_t = ck_ref[cols, :].astype(BF16)
        vc_t = cv_ref[cols, :].astype(BF16)
        kn = jnp.concatenate([kn_ref[:, cols], pad_k], axis=0)
        vn = jnp.concatenate([vn_ref[:, cols].astype(BF16), pad_k], axis=0)
        cq = jnp.concatenate(
            [jnp.sum(jnp.where(lane == 2 * p + h2, cqb, 0.0), axis=1, keepdims=True) for h2 in range(2)],
            axis=0)
        cka = ct_ref[2 * p:2 * p + 1, :]
        ckb = ct_ref[2 * p + 1:2 * p + 2, :]

        def bias(lo, hi):
            return jnp.concatenate([cq[:ls] - cka[:, lo:hi], cq[ls:] - ckb[:, lo:hi]], axis=0)

        s_c = _dot(qst, kc_t) + bias(0, past)
        s_n = _dot_nt(qst, kn) + bias(past, past + LANES)
        s_n = jnp.where(c <= r, s_n, NEG_INF)
        m = jnp.maximum(jnp.max(s_c, axis=1, keepdims=True), jnp.max(s_n, axis=1, keepdims=True))
        p_c = jnp.exp2(s_c - m)
        p_n = jnp.exp2(s_n - m)
        l = jnp.sum(p_c, axis=1, keepdims=True) + jnp.sum(p_n, axis=1, keepdims=True)
        o = (_dot_nt(p_c.astype(BF16), vc_t) + _dot(p_n.astype(BF16), vn)) / l
        o_ref[:, cols] = jnp.where(lane_o < hd, o[:ls], o[ls:])


def _attn_sample(q, kb, v, cache_k, cache_v, cq, ct, *, past):
    B, ls, _ = q.shape
    tot = ct.shape[2]
    rows = lambda n: pl.BlockSpec((None, n, FOX_WIDTH), lambda b: (b, 0, 0))
    cache_t = pl.BlockSpec((None, FOX_WIDTH, past), lambda b: (b, 0, 0))
    return pl.pallas_call(
        functools.partial(_attn_sample_kernel, past=past, ls=ls),
        grid=(B,),
        in_specs=[
            rows(ls), rows(ls), rows(ls), cache_t, cache_t,
            pl.BlockSpec((None, ls, FOX_HEADS), lambda b: (b, 0, 0)),
            pl.BlockSpec((None, FOX_HEADS, tot), lambda b: (b, 0, 0)),
        ],
        out_specs=rows(ls),
        out_shape=jax.ShapeDtypeStruct((B, ls, FOX_WIDTH), F32),
        compiler_params=_params(1),
        name="fox_attn_sample",
    )(q, kb, v, cache_k, cache_v, cq, ct)


def _mid_kernel(o_ref, sg_ref, x_ref, wo_ref, g_ref, w_ref, wa1_ref, wa2_ref, ba_ref,
                y_ref, q_ref, k_ref, v_ref, sg2_ref, ga_ref):
    kw = GLA_KEY_WIDTH
    u = (o_ref[...] * sg_ref[...]).astype(BF16)
    y = x_ref[...] + _dot(u, wo_ref[...])
    y_ref[...] = y
    hb = _rms(y, g_ref[...]).astype(BF16)
    q_ref[...] = _dot_nt(hb, w_ref[0:kw, :]) * (GLA_HEAD_K ** -0.5)
    k_ref[...] = _dot_nt(hb, w_ref[kw:2 * kw, :])
    v_ref[...] = _dot_nt(hb, w_ref[2 * kw:2 * kw + GLA_VAL_WIDTH, :]).astype(BF16)
    gate = _dot_nt(hb, w_ref[2 * kw + GLA_VAL_WIDTH:2 * kw + 2 * GLA_VAL_WIDTH, :])
    sg2_ref[...] = gate * jax.nn.sigmoid(gate)
    a1 = _dot_nt(hb, wa1_ref[...]).astype(BF16)
    a = _dot(a1, wa2_ref[...]) + ba_ref[...]
    ga_ref[...] = _log_sigmoid(a) * (1.0 / GLA_GATE_TEMP)


def _mid(o, sg, x, w_out, g, w_main, w_a1, w_a2, b_a, *, tm):
    T = o.shape[0]
    tok = lambda w: pl.BlockSpec((tm, w), lambda i: (i, 0))
    full = lambda a: pl.BlockSpec(a.shape, lambda i: (0,) * a.ndim)
    return pl.pallas_call(
        _mid_kernel,
        grid=(T // tm,),
        in_specs=[tok(D_MODEL), tok(D_MODEL), tok(D_MODEL), full(w_out), full(g), full(w_main),
                  full(w_a1), full(w_a2), full(b_a)],
        out_specs=[tok(D_MODEL), tok(GLA_KEY_WIDTH), tok(GLA_KEY_WIDTH), tok(GLA_VAL_WIDTH),
                   tok(GLA_VAL_WIDTH), tok(GLA_KEY_WIDTH)],
        out_shape=[
            jax.ShapeDtypeStruct((T, D_MODEL), F32),
            jax.ShapeDtypeStruct((T, GLA_KEY_WIDTH), F32),
            jax.ShapeDtypeStruct((T, GLA_KEY_WIDTH), F32),
            jax.ShapeDtypeStruct((T, GLA_VAL_WIDTH), BF16),
            jax.ShapeDtypeStruct((T, GLA_VAL_WIDTH), F32),
            jax.ShapeDtypeStruct((T, GLA_KEY_WIDTH), F32),
        ],
        compiler_params=_params(1),
        name="fox_out_gla_proj",
    )(o, sg, x, w_out, g, w_main, w_a1, w_a2, b_a)


def _gla_kernel(q_ref, k_ref, v_ref, ga_ref, s0_ref, o_ref, s_ref, *, n_chunks, chained):
    i = pl.program_id(1)
    ch = GLA_CHUNK
    dk, dv = GLA_HEAD_K, GLA_HEAD_V

    @pl.when(i == 0)
    def _():
        s_ref[...] = s0_ref[...]

    t = GLA_GROUP * ch
    r = lax.broadcasted_iota(jnp.int32, (t, t), 0)
    c = lax.broadcasted_iota(jnp.int32, (t, t), 1)
    causal = ((r >> 6) == (c >> 6)) & (r >= c)
    tril = jnp.where(causal, 1.0, 0.0).astype(BF16)
    eye = (lax.broadcasted_iota(jnp.int32, (dk, dk), 0) == lax.broadcasted_iota(jnp.int32, (dk, dk), 1))

    for g0 in range(0, n_chunks, GLA_GROUP):
        grp = slice(g0 * ch, (g0 + GLA_GROUP) * ch)
        b = _tril_cumsum(tril, ga_ref[grp, :])
        lasts = [b[(ci + 1) * ch - 1:(ci + 1) * ch, :] for ci in range(GLA_GROUP)]
        b_last = jnp.concatenate([jnp.broadcast_to(x, (ch, GLA_KEY_WIDTH)) for x in lasts], axis=0)
        kk = k_ref[grp, :]
        qe = (q_ref[grp, :] * jnp.exp(b)).astype(BF16)
        ke = (kk * jnp.exp(-b)).astype(BF16)
        kd = (kk * jnp.exp(b_last - b)).astype(BF16)
        decs = [jnp.exp(x) for x in lasts]

        for h in range(GLA_HEADS):
            ks = slice(h * dk, (h + 1) * dk)
            vs = slice(h * dv, (h + 1) * dv)
            vh = v_ref[grp, vs]
            a = jnp.where(causal, _dot_nt(qe[:, ks], ke[:, ks]), 0.0)
            o_intra = _dot(a.astype(BF16), vh)
            s = s_ref[0, h]
            for ci in range(GLA_GROUP):
                rows = slice(ci * ch, (ci + 1) * ch)
                out_rows = slice((g0 + ci) * ch, (g0 + ci + 1) * ch)
                if not chained:
                    s = s_ref[g0 + ci, h]
                o_ref[out_rows, vs] = o_intra[rows] + _dot(qe[rows, ks], s.astype(BF16))
                dec_col = jnp.sum(jnp.where(eye, jnp.broadcast_to(decs[ci][:, ks], (dk, dk)), 0.0),
                                  axis=1, keepdims=True)
                s = dec_col * s + _dot_tn(kd[rows, ks], vh[rows])
                if not chained:
                    s_ref[g0 + ci, h] = s
            if chained:
                s_ref[0, h] = s


def _gla(q, k, v, ga, s0, *, tc):
    B, L, _ = q.shape
    n_chunks = tc // GLA_CHUNK
    chained = L > GLA_CHUNK
    if chained:
        n_states = 1
    else:
        n_states = n_chunks
        fold = lambda a: a.reshape(B // n_chunks, tc, a.shape[-1])
        q, k, v, ga = fold(q), fold(k), fold(v), fold(ga)
    G, T, _ = q.shape
    tok = lambda w: pl.BlockSpec((None, tc, w), lambda b, i: (b, i, 0))
    st = pl.BlockSpec((n_states, GLA_HEADS, GLA_HEAD_K, GLA_HEAD_V), lambda b, i: (b, 0, 0, 0))
    o, s = pl.pallas_call(
        functools.partial(_gla_kernel, n_chunks=n_chunks, chained=chained),
        grid=(G, T // tc),
        in_specs=[tok(GLA_KEY_WIDTH), tok(GLA_KEY_WIDTH), tok(GLA_VAL_WIDTH), tok(GLA_KEY_WIDTH), st],
        out_specs=[tok(GLA_VAL_WIDTH), st],
        out_shape=[
            jax.ShapeDtypeStruct((G, T, GLA_VAL_WIDTH), F32),
            jax.ShapeDtypeStruct((B, GLA_HEADS, GLA_HEAD_K, GLA_HEAD_V), F32),
        ],
        compiler_params=_params(2),
        name="gla_chunk",
    )(q, k, v, ga, s0)
    return o.reshape(B, L, GLA_VAL_WIDTH), s


def _gla_out_kernel(o_ref, sg_ref, y_ref, go_ref, wo_ref, gf_ref, out_ref, *, chunk_major):
    dv = GLA_HEAD_V
    if chunk_major:
        o = jnp.concatenate([o_ref[:, t, :] for t in range(o_ref.shape[1])], axis=0)
    else:
        o = o_ref[...]
    parts = []
    for h in range(GLA_HEADS):
        vs = slice(h * dv, (h + 1) * dv)
        parts.append((_rms(o[:, vs], go_ref[...]) * sg_ref[:, vs]).astype(BF16))
    u = jnp.concatenate(parts, axis=1)
    y = y_ref[...] + _dot(u, wo_ref[...])
    out_ref[...] = _rms(y, gf_ref[...])


def _gla_out(o, sg, y, g_o, w_out, g_f, *, tm):
    B, L, _ = o.shape
    n = L // GLA_CHUNK
    chunk_major = n > 1
    if chunk_major:
        tm = 8 * n
        o = o.reshape(B, n, GLA_CHUNK, D_MODEL)
        o_spec = pl.BlockSpec((None, n, 8, D_MODEL), lambda b, i: (b, 0, i, 0))
    else:
        flat = lambda a: a.reshape(1, B * L, D_MODEL)
        o, sg, y = flat(o), flat(sg), flat(y)
        o_spec = pl.BlockSpec((None, tm, D_MODEL), lambda b, i: (b, i, 0))
    nb, rows, _ = sg.shape
    tok = pl.BlockSpec((None, tm, D_MODEL), lambda b, i: (b, i, 0))
    full = lambda a: pl.BlockSpec(a.shape, lambda b, i: (0,) * a.ndim)
    return pl.pallas_call(
        functools.partial(_gla_out_kernel, chunk_major=chunk_major),
        grid=(nb, rows // tm),
        in_specs=[o_spec, tok, tok, full(g_o), full(w_out), full(g_f)],
        out_specs=tok,
        out_shape=jax.ShapeDtypeStruct((nb, rows, D_MODEL), F32),
        compiler_params=_params(2),
        name="gla_out",
    )(o, sg, y, g_o, w_out, g_f)


def _pad_cols(w, n):
    return jnp.pad(w, ((0, 0), (0, n - w.shape[1])))


def kernel(x_prompt, x_sample, cache_fox_k, cache_fox_v, cache_fox_logf, state_gla,
           g_norm_fox, w_in_fox, b_fox_f, w_out_fox,
           g_norm_gla, w_in_gla, w_gla_a2, b_gla_a, g_gla_o, w_out_gla, g_final):
    B, L, D = x_prompt.shape
    Bs, Ls, _ = x_sample.shape
    P = cache_fox_k.shape[1]
    H, hd = FOX_HEADS, FOX_HEAD_DIM

    row = lambda a: a.reshape(1, -1).astype(F32)
    pad_rows = lambda w: jnp.pad(w, ((0, LANES - w.shape[0]), (0, 0)))
    w_fox_t = w_in_fox.T
    w_fox = w_fox_t[:4 * FOX_WIDTH].astype(BF16)
    w_fox_f = pad_rows(w_fox_t[4 * FOX_WIDTH:]).astype(BF16)
    b_f = _pad_cols(row(b_fox_f), LANES)
    w_o_fox = w_out_fox.astype(BF16)
    n_main = 2 * GLA_KEY_WIDTH + 2 * GLA_VAL_WIDTH
    w_gla_t = w_in_gla.T
    w_gla = w_gla_t[:n_main].astype(BF16)
    w_a1 = pad_rows(w_gla_t[n_main:]).astype(BF16)
    w_a2 = pad_rows(w_gla_a2).astype(BF16)
    w_o_gla = w_out_gla.astype(BF16)

    q_p, k_p, v_p, kb_p, sg_p, lft_p, vt_p, nrm_p = _fox_proj(
        x_prompt, row(g_norm_fox), w_fox, w_fox_f, b_f, tm=ATTN_BLOCK, want_vt=True)
    e_p, f_p, cmin_p, cmax_p = _bias_tiles(lft_p, rows=ATTN_BLOCK)
    first_p = _skip_plan(nrm_p[:, :, 0, :H], cmin_p, nrm_p[:, :, 1, :H], cmax_p)
    o_p = _attn_prompt(first_p, q_p, f_p, kb_p, e_p, vt_p, blk=ATTN_BLOCK)
    lf_p = jnp.swapaxes(lft_p, 1, 2)

    xs = x_sample.reshape(1, Bs * Ls, D)
    q_s, k_s, v_s, kb_s, sg_s, lft_s = _fox_proj(
        xs, row(g_norm_fox), w_fox, w_fox_f, b_f, tm=PROJ_TM, want_vt=False)
    sh = lambda a: a.reshape(Bs, Ls, a.shape[-1])
    q_s, k_s, v_s, kb_s = sh(q_s), sh(k_s), sh(v_s), sh(kb_s)
    lft_s = jnp.swapaxes(lft_s.reshape(H, Bs, Ls), 0, 1)
    lf_s = jnp.swapaxes(lft_s, 1, 2)
    lft_all = jnp.concatenate(
        [jnp.swapaxes(cache_fox_logf.astype(F32), 1, 2), lft_s, jnp.zeros((Bs, H, LANES - Ls), F32)], axis=2)
    ct_s = _cumsum_short(lft_all, cols=(P + LANES) // 3, per_step=4)
    cache_t = lambda a: jnp.transpose(a, (0, 2, 3, 1)).reshape(Bs, FOX_WIDTH, P)
    o_s = _attn_sample(q_s, kb_s, v_s, cache_t(cache_fox_k), cache_t(cache_fox_v),
                       jnp.swapaxes(ct_s[:, :, P:P + Ls], 1, 2), ct_s, past=P)

    def gla_layer(o, sg, x, s0, tc):
        b, l, _ = x.shape
        flat = lambda a: a.reshape(b * l, a.shape[-1])
        y1, q, k, v, sg2, ga = _mid(flat(o), flat(sg), flat(x), w_o_fox, row(g_norm_gla), w_gla,
                                    w_a1, w_a2, row(b_gla_a), tm=PROJ_TM)
        un = lambda a: a.reshape(b, l, a.shape[-1])
        og, s_out = _gla(un(q), un(k), un(v), un(ga), s0, tc=tc)
        y = _gla_out(og, un(sg2), un(y1), row(g_gla_o), w_o_gla, row(g_final), tm=PROJ_TM)
        return y.reshape(b, l, D), s_out

    s0_p = jnp.zeros((B, GLA_HEADS, GLA_HEAD_K, GLA_HEAD_V), F32)
    y_p, s_p = gla_layer(o_p, sg_p, x_prompt, s0_p, GLA_TC)
    y_s, s_s = gla_layer(o_s, sg_s.reshape(Bs, Ls, D), x_sample, state_gla.astype(F32), GLA_TC)

    return (y_p, y_s,
            k_p.reshape(B, L, H, hd), v_p.reshape(B, L, H, hd), lf_p, s_p.astype(state_gla.dtype),
            k_s.reshape(Bs, Ls, H, hd), v_s.reshape(Bs, Ls, H, hd), lf_s, s_s.astype(state_gla.dtype))
```

```python
import functools

import jax
import jax.numpy as jnp
from jax import lax
from jax.experimental import pallas as pl
from jax.experimental.pallas import tpu as pltpu

F32 = jnp.float32
BF16 = jnp.bfloat16

D_MODEL = 1024
EPS = 1e-6
NEG_INF = -1e30
LOG2E = 1.4426950408889634
NORM_SLACK = 1.01
SKIP_LOG2 = 152.0

FOX_HEADS = 16
FOX_HEAD_DIM = 64
FOX_WIDTH = FOX_HEADS * FOX_HEAD_DIM
FOX_PAIRS = FOX_HEADS // 2

GLA_HEADS = 4
GLA_KEY_WIDTH = 512
GLA_VAL_WIDTH = 1024
GLA_HEAD_K = 128
GLA_HEAD_V = 256
GLA_GATE_RANK = 16
GLA_GATE_TEMP = 16.0
GLA_CHUNK = 64

LANES = 128
VMEM_LIMIT = 56 * 1024 * 1024

PROJ_TM = 512
ATTN_BLOCK = 512
GLA_GROUP = 4
GLA_TC = 512


def _params(n_axes, flags=None):
    return pltpu.CompilerParams(
        dimension_semantics=("arbitrary",) * n_axes,
        vmem_limit_bytes=VMEM_LIMIT,
        flags=flags,
    )


def _dot(a, b):
    return jnp.dot(a, b, preferred_element_type=F32)


def _dot_nt(a, b):
    return lax.dot_general(a, b, (((1,), (1,)), ((), ())), preferred_element_type=F32)


def _dot_tn(a, b):
    return lax.dot_general(a, b, (((0,), (0,)), ((), ())), preferred_element_type=F32)


def _log_sigmoid(z):
    return jnp.minimum(z, 0.0) - jnp.log1p(jnp.exp(-jnp.abs(z)))


def _rms(x, g):
    ms = jnp.mean(x * x, axis=-1, keepdims=True)
    return (x * lax.rsqrt(ms + EPS)) * g


def _split3(x):
    hi = x.astype(BF16)
    r1 = x - hi.astype(F32)
    mid = r1.astype(BF16)
    lo = (r1 - mid.astype(F32)).astype(BF16)
    return hi, mid, lo


def _tril_cumsum(tril_bf16, x):
    hi, mid, lo = _split3(x)
    return (_dot(tril_bf16, hi) + _dot(tril_bf16, mid)) + _dot(tril_bf16, lo)


def _fox_proj_kernel(x_ref, g_ref, w_ref, wf_ref, bf_ref, *rest, want_vt):
    if want_vt:
        q_ref, k_ref, v_ref, kb_ref, sg_ref, lft_ref, vt_ref, nrm_ref = rest
    else:
        q_ref, k_ref, v_ref, kb_ref, sg_ref, lft_ref = rest
    fw = FOX_WIDTH
    hb = _rms(x_ref[...], g_ref[...]).astype(BF16)
    q = _dot_nt(hb, w_ref[0:fw, :])
    qb = (q * (LOG2E * FOX_HEAD_DIM ** -0.5)).astype(BF16)
    q_ref[...] = qb
    k = _dot_nt(hb, w_ref[fw:2 * fw, :])
    k_ref[...] = k
    kb = k.astype(BF16)
    kb_ref[...] = kb
    if want_vt:
        row = lax.broadcasted_iota(jnp.int32, (FOX_WIDTH, LANES), 0)
        col = lax.broadcasted_iota(jnp.int32, (FOX_WIDTH, LANES), 1)
        head_of = jnp.where((row >> 6) == col, 1.0, 0.0).astype(BF16)

        def bound(xb):
            xf = xb.astype(F32)
            sq = _dot((xf * xf).astype(BF16), head_of)
            return jnp.sqrt(jnp.max(sq, axis=0, keepdims=True)) * NORM_SLACK

        nrm_ref[...] = jnp.concatenate(
            [bound(kb), bound(qb), jnp.zeros((6, LANES), F32)], axis=0)
    v = _dot_nt(hb, w_ref[2 * fw:3 * fw, :])
    v_ref[...] = v
    if want_vt:
        vt_ref[...] = v.astype(BF16).T
    gate = _dot_nt(hb, w_ref[3 * fw:4 * fw, :])
    sg_ref[...] = gate * jax.nn.sigmoid(gate)
    fl = _dot_nt(hb, wf_ref[...]) + bf_ref[...]
    lft_ref[...] = _log_sigmoid(fl).T[:FOX_HEADS, :]


def _fox_proj(x, g, w_main, w_f, b_f, *, tm, want_vt):
    B, L, _ = x.shape
    n = L // tm
    tok = lambda w: pl.BlockSpec((None, tm, w), lambda b, i: (b, i, 0))
    full = lambda a: pl.BlockSpec(a.shape, lambda b, i: (0,) * a.ndim)
    in_specs = [tok(D_MODEL), full(g), full(w_main), full(w_f), full(b_f)]
    args = [x, g, w_main, w_f, b_f]
    out_specs = [tok(FOX_WIDTH)] * 5 + [pl.BlockSpec((None, FOX_HEADS, tm), lambda b, i: (b, 0, i))]
    out_shape = [
        jax.ShapeDtypeStruct((B, L, FOX_WIDTH), BF16),
        jax.ShapeDtypeStruct((B, L, FOX_WIDTH), F32),
        jax.ShapeDtypeStruct((B, L, FOX_WIDTH), F32),
        jax.ShapeDtypeStruct((B, L, FOX_WIDTH), BF16),
        jax.ShapeDtypeStruct((B, L, FOX_WIDTH), F32),
        jax.ShapeDtypeStruct((B, FOX_HEADS, L), F32),
    ]
    if want_vt:
        out_specs.append(pl.BlockSpec((None, None, FOX_WIDTH, tm), lambda b, i: (b, i, 0, 0)))
        out_shape.append(jax.ShapeDtypeStruct((B, n, FOX_WIDTH, tm), BF16))
        out_specs.append(pl.BlockSpec((None, None, 8, LANES), lambda b, i: (b, i, 0, 0)))
        out_shape.append(jax.ShapeDtypeStruct((B, n, 8, LANES), F32))
    return pl.pallas_call(
        functools.partial(_fox_proj_kernel, want_vt=want_vt),
        grid=(B, n),
        in_specs=in_specs,
        out_specs=out_specs,
        out_shape=out_shape,
        compiler_params=_params(2),
        name="fox_proj_vt" if want_vt else "fox_proj",
    )(*args)


BIAS_SLOT = 8


def _bias_selectors():
    shape = (LANES, 3 * FOX_HEADS)
    lane = lax.broadcasted_iota(jnp.int32, shape, 0)
    col = lax.broadcasted_iota(jnp.int32, shape, 1)
    slot = BIAS_SLOT * (col & (FOX_HEADS - 1))
    piece = col >> 4
    sel_f = jnp.where(lane == slot + piece, 1.0, 0.0).astype(BF16)
    sel_e = jnp.where(lane == slot + 3 + piece, -1.0, 0.0).astype(BF16)
    in_slot = lax.broadcasted_iota(jnp.int32, (LANES, 1), 0) & (BIAS_SLOT - 1)
    one_f = jnp.where((in_slot >= 3) & (in_slot < 6), 1.0, 0.0)
    one_e = jnp.where(in_slot < 3, 1.0, 0.0)
    return sel_f, sel_e, one_f, one_e


def _stack3(x):
    return jnp.concatenate(_split3(x), axis=0)


def _bias_kernel(lft_ref, e_ref, f_ref, rng_ref, carry_ref, *, rows):
    r = lax.broadcasted_iota(jnp.int32, (rows, rows), 0)
    c = lax.broadcasted_iota(jnp.int32, (rows, rows), 1)
    upper = jnp.where(r <= c, 1.0, 0.0).astype(BF16)
    sel_f, sel_e, one_f, one_e = _bias_selectors()
    nh = FOX_HEADS

    @pl.when(pl.program_id(1) == 0)
    def _():
        carry_ref[...] = jnp.zeros_like(carry_ref)

    carry = carry_ref[:, 0:1]
    for sb in range(rng_ref.shape[0]):
        cols = slice(sb * rows, (sb + 1) * rows)
        st = _dot(_stack3(lft_ref[:, cols]), upper)
        cs = ((st[0:nh] + st[nh:2 * nh]) + st[2 * nh:3 * nh]) + carry
        carry = cs[:, rows - 1:rows]
        c2 = cs * LOG2E
        pieces = _stack3(c2)
        f_ref[cols, :] = (_dot(sel_f, pieces) + one_f).T.astype(BF16)
        e_ref[cols, :] = (_dot(sel_e, pieces) + one_e).T.astype(BF16)
        rng_ref[sb] = jnp.concatenate(
            [jnp.broadcast_to(jnp.min(c2, axis=1, keepdims=True), (nh, LANES)),
             jnp.broadcast_to(jnp.max(c2, axis=1, keepdims=True), (nh, LANES))], axis=0)
    carry_ref[...] = jnp.broadcast_to(carry, carry_ref.shape)


def _bias_tiles(lft, *, rows):
    B, H, L = lft.shape
    per_step = 4
    span = per_step * rows
    wide = pl.BlockSpec((None, span, LANES), lambda b, i: (b, i, 0))
    e, f, rng = pl.pallas_call(
        functools.partial(_bias_kernel, rows=rows),
        grid=(B, L // span),
        in_specs=[pl.BlockSpec((None, H, span), lambda b, i: (b, 0, i))],
        out_specs=[wide, wide, pl.BlockSpec((None, per_step, 2 * H, LANES), lambda b, i: (b, i, 0, 0))],
        out_shape=[jax.ShapeDtypeStruct((B, L, LANES), BF16)] * 2
        + [jax.ShapeDtypeStruct((B, L // rows, 2 * H, LANES), F32)],
        scratch_shapes=[pltpu.VMEM((H, LANES), F32)],
        compiler_params=_params(2),
        name="fox_cumsum_bias",
    )(lft)
    return e, f, rng[:, :, :H, 0], rng[:, :, H:, 0]


def _cumsum_short_kernel(lft_ref, ct_ref, *, cols):
    nbatch, nh, length = lft_ref.shape
    r = lax.broadcasted_iota(jnp.int32, (cols, cols), 0)
    c = lax.broadcasted_iota(jnp.int32, (cols, cols), 1)
    upper = jnp.where(r <= c, 1.0, 0.0).astype(BF16)
    carry = [jnp.zeros((nh, 1), F32)] * nbatch
    for c0 in range(0, length, cols):
        span = slice(c0, c0 + cols)
        st = _dot(jnp.concatenate([_stack3(lft_ref[bi, :, span]) for bi in range(nbatch)], axis=0), upper)
        for bi in range(nbatch):
            base = 3 * nh * bi
            cs = ((st[base:base + nh] + st[base + nh:base + 2 * nh]) + st[base + 2 * nh:base + 3 * nh]) + carry[bi]
            ct_ref[bi, :, span] = cs * LOG2E
            carry[bi] = cs[:, cols - 1:cols]


def _cumsum_short(lft, *, cols, per_step):
    B, H, L = lft.shape
    spec = pl.BlockSpec((per_step, H, L), lambda b: (b, 0, 0))
    return pl.pallas_call(
        functools.partial(_cumsum_short_kernel, cols=cols),
        grid=(B // per_step,),
        in_specs=[spec],
        out_specs=spec,
        out_shape=jax.ShapeDtypeStruct((B, H, L), F32),
        compiler_params=_params(1),
        name="fox_cumsum",
    )(lft)


def _skip_plan_kernel(kn_ref, cmin_ref, qn_ref, cmax_ref, knq_ref, first_ref, *, nb):
    j = lax.broadcasted_iota(jnp.int32, (nb, nb), 0).astype(F32)
    q = lax.broadcasted_iota(jnp.int32, (nb, nb), 1).astype(F32)
    rows = []
    for p in range(FOX_PAIRS):
        skippable = None
        for h in (2 * p, 2 * p + 1):
            qn = qn_ref[h:h + 1, :]
            bound = (kn_ref[:, h:h + 1] * qn + (cmax_ref[h:h + 1, :] - cmin_ref[:, h:h + 1])
                     + qn * knq_ref[h:h + 1, :])
            ok = bound < -SKIP_LOG2
            skippable = ok if skippable is None else (skippable & ok)
        must = (j < q) & jnp.logical_not(skippable)
        rows.append(jnp.min(jnp.where(must, j, q), axis=0, keepdims=True))
    first_ref[...] = jnp.concatenate(rows, axis=0).astype(jnp.int32)


def _skip_plan(kn, cmin, qn, cmax):
    B, nb, H = kn.shape
    by_block = pl.BlockSpec((None, nb, H), lambda b: (b, 0, 0))
    by_head = pl.BlockSpec((None, H, nb), lambda b: (b, 0, 0))
    t = lambda a: jnp.swapaxes(a, 1, 2)
    return pl.pallas_call(
        functools.partial(_skip_plan_kernel, nb=nb),
        grid=(B,),
        in_specs=[by_block, by_block, by_head, by_head, by_head],
        out_specs=pl.BlockSpec((None, FOX_PAIRS, nb), lambda b: (b, 0, 0)),
        out_shape=jax.ShapeDtypeStruct((B, FOX_PAIRS, nb), jnp.int32),
        compiler_params=_params(1),
        name="fox_skip_plan",
    )(kn, cmin, t(qn), t(cmax), t(kn))


def _stack_heads(q2):
    lane = lax.broadcasted_iota(jnp.int32, q2.shape, 1)
    zero = jnp.zeros_like(q2)
    return jnp.concatenate(
        [jnp.where(lane < FOX_HEAD_DIM, q2, zero), jnp.where(lane >= FOX_HEAD_DIM, q2, zero)], axis=0)


ATTN_TILE = 256
SUM_ROWS = 16


def _attn_prompt_kernel(n_ref, vq_ref, vj_ref, q_ref, f_ref, k_ref, e_ref, vt_ref, o_ref,
                        m_ref, acc_ref, sa_ref, xa_ref, sb_ref, xb_ref, *, blk, nb):
    b_id = pl.program_id(0)
    p_id = pl.program_id(1)
    hd = FOX_HEAD_DIM
    tw = ATTN_TILE
    per_head = blk // tw
    n_tiles = 2 * per_head

    lane = lax.broadcasted_iota(jnp.int32, (blk, LANES), 1)
    zero = jnp.zeros((blk, LANES), BF16)
    slot_a = 2 * BIAS_SLOT * p_id
    slot_b = slot_a + BIAS_SLOT
    ones_rows = jnp.ones((SUM_ROWS, blk), BF16)

    def query_tiles(qi):
        r0 = pl.multiple_of(qi * blk, blk)
        q2 = q_ref[pl.ds(r0, blk), :]
        f2 = f_ref[pl.ds(r0, blk), :]
        heads = [
            jnp.concatenate([jnp.where(lane < hd, q2, zero),
                             jnp.where((lane >= slot_a) & (lane < slot_b), f2, zero)], axis=1),
            jnp.concatenate([jnp.where(lane >= hd, q2, zero),
                             jnp.where((lane >= slot_b) & (lane < slot_b + BIAS_SLOT), f2, zero)], axis=1),
        ]
        return [heads[t // per_head][(t % per_head) * tw:(t % per_head + 1) * tw] for t in range(n_tiles)]

    def stage(nxt, cur):
        if nxt is not None:
            qi_n, j_n, (sn_ref, xn_ref), diag_n = nxt
            q_tiles = query_tiles(qi_n)
            k0 = pl.multiple_of(j_n * blk, blk)
            kaug = jnp.concatenate([k_ref[pl.ds(k0, blk), :], e_ref[pl.ds(k0, blk), :]], axis=1)
        if cur is not None:
            qi_c, j_c, (sc_ref, xc_ref), diag_c = cur
            vt = vt_ref[j_c]
            v_heads = [jnp.concatenate([vt[h * hd:(h + 1) * hd, :], ones_rows], axis=0) for h in range(2)]
        for t in range(n_tiles):
            c0 = (t % per_head) * tw
            if nxt is not None:
                live = c0 + tw if diag_n else blk
                s = _dot_nt(kaug[:live], q_tiles[t])
                if diag_n:
                    r = lax.broadcasted_iota(jnp.int32, (live, tw), 0)
                    c = lax.broadcasted_iota(jnp.int32, (live, tw), 1) + c0
                    s = jnp.where(r <= c, s, NEG_INF)
                sn_ref[t, 0:live, :] = s
                xn_ref[t] = jnp.max(s, axis=0, keepdims=True)
            if cur is not None:
                live = c0 + tw if diag_c else blk
                idx = qi_c * n_tiles + t
                if diag_c:
                    m_new = xc_ref[t]
                else:
                    m_old = m_ref[idx]
                    m_new = jnp.maximum(m_old, xc_ref[t])
                    alpha = jnp.exp2(m_old - m_new)
                p = jnp.exp2(sc_ref[t, 0:live, :] - m_new)
                m_ref[idx] = m_new
                pv = _dot(v_heads[t // per_head][:, 0:live], p.astype(BF16))
                acc_ref[idx] = pv if diag_c else acc_ref[idx] * alpha + pv

    buf_a = (sa_ref, xa_ref)
    buf_b = (sb_ref, xb_ref)

    diag = lambda n, buf: (n, n, buf, True)
    n_off = n_ref[b_id, p_id]
    off = lambda v, buf: (vq_ref[b_id, p_id, v], vj_ref[b_id, p_id, v], buf, False)

    stage(diag(0, buf_a), None)

    def diag_pair(n, carry):
        i0 = 2 * n
        stage(diag(i0 + 1, buf_b), diag(i0, buf_a))
        stage(diag(i0 + 2, buf_a), diag(i0 + 1, buf_b))
        return carry

    lax.fori_loop(0, nb // 2 - 1, diag_pair, 0)
    stage(diag(nb - 1, buf_b), diag(nb - 2, buf_a))

    @pl.when(n_off > 0)
    def _():
        stage(off(0, buf_a), diag(nb - 1, buf_b))

    @pl.when(n_off == 0)
    def _():
        stage(None, diag(nb - 1, buf_b))

    def off_pair(n, carry):
        v0 = 2 * n
        stage(off(v0 + 1, buf_b), off(v0, buf_a))
        stage(off(v0 + 2, buf_a), off(v0 + 1, buf_b))
        return carry

    lax.fori_loop(0, jnp.maximum(n_off - 1, 0) // 2, off_pair, 0)

    @pl.when((n_off > 0) & (n_off % 2 == 0))
    def _():
        stage(off(n_off - 1, buf_b), off(n_off - 2, buf_a))
        stage(None, off(n_off - 1, buf_b))

    @pl.when(n_off % 2 == 1)
    def _():
        stage(None, off(n_off - 1, buf_a))

    def finish(qi, carry):
        halves = [jnp.concatenate([acc_ref[qi * n_tiles + t, 0:hd, :] / acc_ref[qi * n_tiles + t, hd:hd + 1, :]
                                   for t in range(h * per_head, (h + 1) * per_head)], axis=1)
                  for h in range(2)]
        o_ref[pl.ds(pl.multiple_of(qi * blk, blk), blk), :] = jnp.concatenate(halves, axis=0).T
        return carry

    lax.fori_loop(0, nb, finish, 0)


def _visit_lists(first):
    B, P, nb = first.shape
    vmax = nb * (nb - 1) // 2 + 8
    q_idx = jnp.arange(nb, dtype=jnp.int32)
    cnt = q_idx - first
    incl = jnp.cumsum(cnt, axis=-1)
    excl = incl - cnt
    v = jnp.arange(vmax, dtype=jnp.int32)
    vq = jnp.minimum(jnp.sum(v[None, None, :, None] >= incl[:, :, None, :], axis=-1), nb - 1).astype(jnp.int32)
    pick = vq[..., None] == q_idx
    take = lambda a: jnp.sum(jnp.where(pick, a[:, :, None, :], 0), axis=-1)
    vj = jnp.clip(take(first) + (v - take(excl)), 0, nb - 1).astype(jnp.int32)
    return incl[..., -1].astype(jnp.int32), vq, vj


def _attn_prompt(first, q, f, kb, e, vt, *, blk):
    B, L, _ = q.shape
    nb = L // blk
    n_tiles = 2 * blk // ATTN_TILE
    n_off, vq, vj = _visit_lists(first)
    by_pair = pl.BlockSpec((None, L, LANES), lambda b, p, *_: (b, 0, p))
    shared = pl.BlockSpec((None, L, LANES), lambda b, p, *_: (b, 0, 0))
    return pl.pallas_call(
        functools.partial(_attn_prompt_kernel, blk=blk, nb=nb),
        grid_spec=pltpu.PrefetchScalarGridSpec(
            num_scalar_prefetch=3,
            grid=(B, FOX_PAIRS),
            in_specs=[by_pair, shared, by_pair, shared,
                      pl.BlockSpec((None, nb, LANES, blk), lambda b, p, *_: (b, 0, p, 0))],
            out_specs=by_pair,
            scratch_shapes=[
                pltpu.VMEM((nb * n_tiles, 1, ATTN_TILE), F32),
                pltpu.VMEM((nb * n_tiles, FOX_HEAD_DIM + SUM_ROWS, ATTN_TILE), F32),
                pltpu.VMEM((n_tiles, blk, ATTN_TILE), F32),
                pltpu.VMEM((n_tiles, 1, ATTN_TILE), F32),
                pltpu.VMEM((n_tiles, blk, ATTN_TILE), F32),
                pltpu.VMEM((n_tiles, 1, ATTN_TILE), F32),
            ],
        ),
        out_shape=jax.ShapeDtypeStruct((B, L, FOX_WIDTH), F32),
        compiler_params=_params(2),
        name="fox_attn_prompt",
    )(n_off, vq, vj, q, f, kb, e, vt)


def _attn_sample_kernel(q_ref, kn_ref, vn_ref, ck_ref, cv_ref, cq_ref, ct_ref, o_ref, *, past, ls):
    hd = FOX_HEAD_DIM
    pad_k = jnp.zeros((LANES - ls, LANES), BF16)
    cqb = cq_ref[...]
    lane = lax.broadcasted_iota(jnp.int32, cqb.shape, 1)
    r = lax.broadcasted_iota(jnp.int32, (2 * ls, LANES), 0) & (ls - 1)
    c = lax.broadcasted_iota(jnp.int32, (2 * ls, LANES), 1)
    lane_o = lax.broadcasted_iota(jnp.int32, (ls, LANES), 1)

    for p in range(FOX_PAIRS):
        cols = slice(p * LANES, (p + 1) * LANES)
        qst = _stack_heads(q_ref[:, cols])
        kc_t = ck_ref[cols, :].astype(BF16)
        vc_t = cv_ref[cols, :].astype(BF16)
        kn = jnp.concatenate([kn_ref[:, cols], pad_k], axis=0)
        vn = jnp.concatenate([vn_ref[:, cols].astype(BF16), pad_k], axis=0)
        cq = jnp.concatenate(
            [jnp.sum(jnp.where(lane == 2 * p + h2, cqb, 0.0), axis=1, keepdims=True) for h2 in range(2)],
            axis=0)
        cka = ct_ref[2 * p:2 * p + 1, :]
        ckb = ct_ref[2 * p + 1:2 * p + 2, :]

        def bias(lo, hi):
            return jnp.concatenate([cq[:ls] - cka[:, lo:hi], cq[ls:] - ckb[:, lo:hi]], axis=0)

        s_c = _dot(qst, kc_t) + bias(0, past)
        s_n = _dot_nt(qst, kn) + bias(past, past + LANES)
        s_n = jnp.where(c <= r, s_n, NEG_INF)
        m = jnp.maximum(jnp.max(s_c, axis=1, keepdims=True), jnp.max(s_n, axis=1, keepdims=True))
        p_c = jnp.exp2(s_c - m)
        p_n = jnp.exp2(s_n - m)
        l = jnp.sum(p_c, axis=1, keepdims=True) + jnp.sum(p_n, axis=1, keepdims=True)
        o = (_dot_nt(p_c.astype(BF16), vc_t) + _dot(p_n.astype(BF16), vn)) / l
        o_ref[:, cols] = jnp.where(lane_o < hd, o[:ls], o[ls:])


def _attn_sample(q, kb, v, cache_k, cache_v, cq, ct, *, past):
    B, ls, _ = q.shape
    tot = ct.shape[2]
    rows = lambda n: pl.BlockSpec((None, n, FOX_WIDTH), lambda b: (b, 0, 0))
    cache_t = pl.BlockSpec((None, FOX_WIDTH, past), lambda b: (b, 0, 0))
    return pl.pallas_call(
        functools.partial(_attn_sample_kernel, past=past, ls=ls),
        grid=(B,),
        in_specs=[
            rows(ls), rows(ls), rows(ls), cache_t, cache_t,
            pl.BlockSpec((None, ls, FOX_HEADS), lambda b: (b, 0, 0)),
            pl.BlockSpec((None, FOX_HEADS, tot), lambda b: (b, 0, 0)),
        ],
        out_specs=rows(ls),
        out_shape=jax.ShapeDtypeStruct((B, ls, FOX_WIDTH), F32),
        compiler_params=_params(1),
        name="fox_attn_sample",
    )(q, kb, v, cache_k, cache_v, cq, ct)


def _mid_kernel(o_ref, sg_ref, x_ref, wo_ref, g_ref, w_ref, wa1_ref, wa2_ref, ba_ref,
                y_ref, q_ref, k_ref, v_ref, sg2_ref, ga_ref):
    kw = GLA_KEY_WIDTH
    u = (o_ref[...] * sg_ref[...]).astype(BF16)
    y = x_ref[...] + _dot(u, wo_ref[...])
    y_ref[...] = y
    hb = _rms(y, g_ref[...]).astype(BF16)
    q_ref[...] = _dot_nt(hb, w_ref[0:kw, :]) * (GLA_HEAD_K ** -0.5)
    k_ref[...] = _dot_nt(hb, w_ref[kw:2 * kw, :])
    v_ref[...] = _dot_nt(hb, w_ref[2 * kw:2 * kw + GLA_VAL_WIDTH, :]).astype(BF16)
    gate = _dot_nt(hb, w_ref[2 * kw + GLA_VAL_WIDTH:2 * kw + 2 * GLA_VAL_WIDTH, :])
    sg2_ref[...] = gate * jax.nn.sigmoid(gate)
    a1 = _dot_nt(hb, wa1_ref[...]).astype(BF16)
    a = _dot(a1, wa2_ref[...]) + ba_ref[...]
    ga_ref[...] = _log_sigmoid(a) * (1.0 / GLA_GATE_TEMP)


def _mid(o, sg, x, w_out, g, w_main, w_a1, w_a2, b_a, *, tm):
    T = o.shape[0]
    tok = lambda w: pl.BlockSpec((tm, w), lambda i: (i, 0))
    full = lambda a: pl.BlockSpec(a.shape, lambda i: (0,) * a.ndim)
    return pl.pallas_call(
        _mid_kernel,
        grid=(T // tm,),
        in_specs=[tok(D_MODEL), tok(D_MODEL), tok(D_MODEL), full(w_out), full(g), full(w_main),
                  full(w_a1), full(w_a2), full(b_a)],
        out_specs=[tok(D_MODEL), tok(GLA_KEY_WIDTH), tok(GLA_KEY_WIDTH), tok(GLA_VAL_WIDTH),
                   tok(GLA_VAL_WIDTH), tok(GLA_KEY_WIDTH)],
        out_shape=[
            jax.ShapeDtypeStruct((T, D_MODEL), F32),
            jax.ShapeDtypeStruct((T, GLA_KEY_WIDTH), F32),
            jax.ShapeDtypeStruct((T, GLA_KEY_WIDTH), F32),
            jax.ShapeDtypeStruct((T, GLA_VAL_WIDTH), BF16),
            jax.ShapeDtypeStruct((T, GLA_VAL_WIDTH), F32),
            jax.ShapeDtypeStruct((T, GLA_KEY_WIDTH), F32),
        ],
        compiler_params=_params(1),
        name="fox_out_gla_proj",
    )(o, sg, x, w_out, g, w_main, w_a1, w_a2, b_a)


def _gla_kernel(q_ref, k_ref, v_ref, ga_ref, s0_ref, o_ref, s_ref, *, n_chunks, chained):
    i = pl.program_id(1)
    ch = GLA_CHUNK
    dk, dv = GLA_HEAD_K, GLA_HEAD_V

    @pl.when(i == 0)
    def _():
        s_ref[...] = s0_ref[...]

    t = GLA_GROUP * ch
    r = lax.broadcasted_iota(jnp.int32, (t, t), 0)
    c = lax.broadcasted_iota(jnp.int32, (t, t), 1)
    causal = ((r >> 6) == (c >> 6)) & (r >= c)
    tril = jnp.where(causal, 1.0, 0.0).astype(BF16)
    eye = (lax.broadcasted_iota(jnp.int32, (dk, dk), 0) == lax.broadcasted_iota(jnp.int32, (dk, dk), 1))

    for g0 in range(0, n_chunks, GLA_GROUP):
        grp = slice(g0 * ch, (g0 + GLA_GROUP) * ch)
        b = _tril_cumsum(tril, ga_ref[grp, :])
        lasts = [b[(ci + 1) * ch - 1:(ci + 1) * ch, :] for ci in range(GLA_GROUP)]
        b_last = jnp.concatenate([jnp.broadcast_to(x, (ch, GLA_KEY_WIDTH)) for x in lasts], axis=0)
        kk = k_ref[grp, :]
        qe = (q_ref[grp, :] * jnp.exp(b)).astype(BF16)
        ke = (kk * jnp.exp(-b)).astype(BF16)
        kd = (kk * jnp.exp(b_last - b)).astype(BF16)
        decs = [jnp.exp(x) for x in lasts]

        for h in range(GLA_HEADS):
            ks = slice(h * dk, (h + 1) * dk)
            vs = slice(h * dv, (h + 1) * dv)
            vh = v_ref[grp, vs]
            a = jnp.where(causal, _dot_nt(qe[:, ks], ke[:, ks]), 0.0)
            o_intra = _dot(a.astype(BF16), vh)
            s = s_ref[0, h]
            for ci in range(GLA_GROUP):
                rows = slice(ci * ch, (ci + 1) * ch)
                out_rows = slice((g0 + ci) * ch, (g0 + ci + 1) * ch)
                if not chained:
                    s = s_ref[g0 + ci, h]
                o_ref[out_rows, vs] = o_intra[rows] + _dot(qe[rows, ks], s.astype(BF16))
                dec_col = jnp.sum(jnp.where(eye, jnp.broadcast_to(decs[ci][:, ks], (dk, dk)), 0.0),
                                  axis=1, keepdims=True)
                s = dec_col * s + _dot_tn(kd[rows, ks], vh[rows])
                if not chained:
                    s_ref[g0 + ci, h] = s
            if chained:
                s_ref[0, h] = s


def _gla(q, k, v, ga, s0, *, tc):
    B, L, _ = q.shape
    n_chunks = tc // GLA_CHUNK
    chained = L > GLA_CHUNK
    if chained:
        n_states = 1
    else:
        n_states = n_chunks
        fold = lambda a: a.reshape(B // n_chunks, tc, a.shape[-1])
        q, k, v, ga = fold(q), fold(k), fold(v), fold(ga)
    G, T, _ = q.shape
    tok = lambda w: pl.BlockSpec((None, tc, w), lambda b, i: (b, i, 0))
    st = pl.BlockSpec((n_states, GLA_HEADS, GLA_HEAD_K, GLA_HEAD_V), lambda b, i: (b, 0, 0, 0))
    o, s = pl.pallas_call(
        functools.partial(_gla_kernel, n_chunks=n_chunks, chained=chained),
        grid=(G, T // tc),
        in_specs=[tok(GLA_KEY_WIDTH), tok(GLA_KEY_WIDTH), tok(GLA_VAL_WIDTH), tok(GLA_KEY_WIDTH), st],
        out_specs=[tok(GLA_VAL_WIDTH), st],
        out_shape=[
            jax.ShapeDtypeStruct((G, T, GLA_VAL_WIDTH), F32),
            jax.ShapeDtypeStruct((B, GLA_HEADS, GLA_HEAD_K, GLA_HEAD_V), F32),
        ],
        compiler_params=_params(2),
        name="gla_chunk",
    )(q, k, v, ga, s0)
    return o.reshape(B, L, GLA_VAL_WIDTH), s


def _gla_out_kernel(o_ref, sg_ref, y_ref, go_ref, wo_ref, gf_ref, out_ref, *, chunk_major):
    dv = GLA_HEAD_V
    if chunk_major:
        o = jnp.concatenate([o_ref[:, t, :] for t in range(o_ref.shape[1])], axis=0)
    else:
        o = o_ref[...]
    parts = []
    for h in range(GLA_HEADS):
        vs = slice(h * dv, (h + 1) * dv)
        parts.append((_rms(o[:, vs], go_ref[...]) * sg_ref[:, vs]).astype(BF16))
    u = jnp.concatenate(parts, axis=1)
    y = y_ref[...] + _dot(u, wo_ref[...])
    out_ref[...] = _rms(y, gf_ref[...])


def _gla_out(o, sg, y, g_o, w_out, g_f, *, tm):
    B, L, _ = o.shape
    n = L // GLA_CHUNK
    chunk_major = n > 1
    if chunk_major:
        tm = 8 * n
        o = o.reshape(B, n, GLA_CHUNK, D_MODEL)
        o_spec = pl.BlockSpec((None, n, 8, D_MODEL), lambda b, i: (b, 0, i, 0))
    else:
        flat = lambda a: a.reshape(1, B * L, D_MODEL)
        o, sg, y = flat(o), flat(sg), flat(y)
        o_spec = pl.BlockSpec((None, tm, D_MODEL), lambda b, i: (b, i, 0))
    nb, rows, _ = sg.shape
    tok = pl.BlockSpec((None, tm, D_MODEL), lambda b, i: (b, i, 0))
    full = lambda a: pl.BlockSpec(a.shape, lambda b, i: (0,) * a.ndim)
    return pl.pallas_call(
        functools.partial(_gla_out_kernel, chunk_major=chunk_major),
        grid=(nb, rows // tm),
        in_specs=[o_spec, tok, tok, full(g_o), full(w_out), full(g_f)],
        out_specs=tok,
        out_shape=jax.ShapeDtypeStruct((nb, rows, D_MODEL), F32),
        compiler_params=_params(2),
        name="gla_out",
    )(o, sg, y, g_o, w_out, g_f)


def _pad_cols(w, n):
    return jnp.pad(w, ((0, 0), (0, n - w.shape[1])))


def kernel(x_prompt, x_sample, cache_fox_k, cache_fox_v, cache_fox_logf, state_gla,
           g_norm_fox, w_in_fox, b_fox_f, w_out_fox,
           g_norm_gla, w_in_gla, w_gla_a2, b_gla_a, g_gla_o, w_out_gla, g_final):
    B, L, D = x_prompt.shape
    Bs, Ls, _ = x_sample.shape
    P = cache_fox_k.shape[1]
    H, hd = FOX_HEADS, FOX_HEAD_DIM

    row = lambda a: a.reshape(1, -1).astype(F32)
    pad_rows = lambda w: jnp.pad(w, ((0, LANES - w.shape[0]), (0, 0)))
    w_fox_t = w_in_fox.T
    w_fox = w_fox_t[:4 * FOX_WIDTH].astype(BF16)
    w_fox_f = pad_rows(w_fox_t[4 * FOX_WIDTH:]).astype(BF16)
    b_f = _pad_cols(row(b_fox_f), LANES)
    w_o_fox = w_out_fox.astype(BF16)
    n_main = 2 * GLA_KEY_WIDTH + 2 * GLA_VAL_WIDTH
    w_gla_t = w_in_gla.T
    w_gla = w_gla_t[:n_main].astype(BF16)
    w_a1 = pad_rows(w_gla_t[n_main:]).astype(BF16)
    w_a2 = pad_rows(w_gla_a2).astype(BF16)
    w_o_gla = w_out_gla.astype(BF16)

    q_p, k_p, v_p, kb_p, sg_p, lft_p, vt_p, nrm_p = _fox_proj(
        x_prompt, row(g_norm_fox), w_fox, w_fox_f, b_f, tm=ATTN_BLOCK, want_vt=True)
    e_p, f_p, cmin_p, cmax_p = _bias_tiles(lft_p, rows=ATTN_BLOCK)
    first_p = _skip_plan(nrm_p[:, :, 0, :H], cmin_p, nrm_p[:, :, 1, :H], cmax_p)
    o_p = _attn_prompt(first_p, q_p, f_p, kb_p, e_p, vt_p, blk=ATTN_BLOCK)
    lf_p = jnp.swapaxes(lft_p, 1, 2)

    xs = x_sample.reshape(1, Bs * Ls, D)
    q_s, k_s, v_s, kb_s, sg_s, lft_s = _fox_proj(
        xs, row(g_norm_fox), w_fox, w_fox_f, b_f, tm=PROJ_TM, want_vt=False)
    sh = lambda a: a.reshape(Bs, Ls, a.shape[-1])
    q_s, k_s, v_s, kb_s = sh(q_s), sh(k_s), sh(v_s), sh(kb_s)
    lft_s = jnp.swapaxes(lft_s.reshape(H, Bs, Ls), 0, 1)
    lf_s = jnp.swapaxes(lft_s, 1, 2)
    lft_all = jnp.concatenate(
        [jnp.swapaxes(cache_fox_logf.astype(F32), 1, 2), lft_s, jnp.zeros((Bs, H, LANES - Ls), F32)], axis=2)
    ct_s = _cumsum_short(lft_all, cols=(P + LANES) // 3, per_step=4)
    cache_t = lambda a: jnp.transpose(a, (0, 2, 3, 1)).reshape(Bs, FOX_WIDTH, P)
    o_s = _attn_sample(q_s, kb_s, v_s, cache_t(cache_fox_k), cache_t(cache_fox_v),
                       jnp.swapaxes(ct_s[:, :, P:P + Ls], 1, 2), ct_s, past=P)

    def gla_layer(o, sg, x, s0, tc):
        b, l, _ = x.shape
        flat = lambda a: a.reshape(b * l, a.shape[-1])
        y1, q, k, v, sg2, ga = _mid(flat(o), flat(sg), flat(x), w_o_fox, row(g_norm_gla), w_gla,
                                    w_a1, w_a2, row(b_gla_a), tm=PROJ_TM)
        un = lambda a: a.reshape(b, l, a.shape[-1])
        og, s_out = _gla(un(q), un(k), un(v), un(ga), s0, tc=tc)
        y = _gla_out(og, un(sg2), un(y1), row(g_gla_o), w_o_gla, row(g_final), tm=PROJ_TM)
        return y.reshape(b, l, D), s_out

    s0_p = jnp.zeros((B, GLA_HEADS, GLA_HEAD_K, GLA_HEAD_V), F32)
    y_p, s_p = gla_layer(o_p, sg_p, x_prompt, s0_p, GLA_TC)
    y_s, s_s = gla_layer(o_s, sg_s.reshape(Bs, Ls, D), x_sample, state_gla.astype(F32), GLA_TC)

    return (y_p, y_s,
            k_p.reshape(B, L, H, hd), v_p.reshape(B, L, H, hd), lf_p, s_p.astype(state_gla.dtype),
            k_s.reshape(Bs, Ls, H, hd), v_s.reshape(Bs, Ls, H, hd), lf_s, s_s.astype(state_gla.dtype))
```

```python
import functools

import jax
import jax.numpy as jnp
from jax import lax
from jax.experimental import pallas as pl
from jax.experimental.pallas import tpu as pltpu

F32 = jnp.float32
BF16 = jnp.bfloat16

D_MODEL = 1024
EPS = 1e-6
NEG_INF = -1e30
LOG2E = 1.4426950408889634
NORM_SLACK = 1.01
SKIP_LOG2 = 152.0

FOX_HEADS = 16
FOX_HEAD_DIM = 64
FOX_WIDTH = FOX_HEADS * FOX_HEAD_DIM
FOX_PAIRS = FOX_HEADS // 2

GLA_HEADS = 4
GLA_KEY_WIDTH = 512
GLA_VAL_WIDTH = 1024
GLA_HEAD_K = 128
GLA_HEAD_V = 256
GLA_GATE_RANK = 16
GLA_GATE_TEMP = 16.0
GLA_CHUNK = 64

LANES = 128
SUBLANES = 8
VMEM_LIMIT = 56 * 1024 * 1024

PROJ_TM = 512
ATTN_BLOCK = 512
GLA_GROUP = 4
GLA_TC = 512


def _params(n_axes):
    return pltpu.CompilerParams(
        dimension_semantics=("arbitrary",) * n_axes,
        vmem_limit_bytes=VMEM_LIMIT,
    )


def _dot(a, b):
    return jnp.dot(a, b, preferred_element_type=F32)


def _dot_nt(a, b):
    return lax.dot_general(a, b, (((1,), (1,)), ((), ())), preferred_element_type=F32)


def _dot_tn(a, b):
    return lax.dot_general(a, b, (((0,), (0,)), ((), ())), preferred_element_type=F32)


def _log_sigmoid(z):
    return jnp.minimum(z, 0.0) - jnp.log1p(jnp.exp(-jnp.abs(z)))


def _rms(x, g):
    ms = jnp.mean(x * x, axis=-1, keepdims=True)
    return (x * lax.rsqrt(ms + EPS)) * g


def _split3(x):
    hi = x.astype(BF16)
    r1 = x - hi.astype(F32)
    mid = r1.astype(BF16)
    lo = (r1 - mid.astype(F32)).astype(BF16)
    return hi, mid, lo


def _tril_cumsum(tril_bf16, x):
    hi, mid, lo = _split3(x)
    return (_dot(tril_bf16, hi) + _dot(tril_bf16, mid)) + _dot(tril_bf16, lo)


def _fox_proj_kernel(x_ref, g_ref, w_ref, wf_ref, bf_ref, *rest, want_vt):
    if want_vt:
        q_ref, k_ref, v_ref, kb_ref, sg_ref, lft_ref, vt_ref, nrm_ref = rest
    else:
        q_ref, k_ref, v_ref, kb_ref, sg_ref, lft_ref = rest
    fw = FOX_WIDTH
    hb = _rms(x_ref[...], g_ref[...]).astype(BF16)
    q = _dot_nt(hb, w_ref[0:fw, :])
    qb = (q * (LOG2E * FOX_HEAD_DIM ** -0.5)).astype(BF16)
    q_ref[...] = qb
    k = _dot_nt(hb, w_ref[fw:2 * fw, :])
    k_ref[...] = k
    kb = k.astype(BF16)
    kb_ref[...] = kb
    if want_vt:
        row = lax.broadcasted_iota(jnp.int32, (FOX_WIDTH, LANES), 0)
        col = lax.broadcasted_iota(jnp.int32, (FOX_WIDTH, LANES), 1)
        head_of = jnp.where((row >> 6) == col, 1.0, 0.0).astype(BF16)

        def bound(xb):
            xf = xb.astype(F32)
            sq = _dot((xf * xf).astype(BF16), head_of)
            return jnp.sqrt(jnp.max(sq, axis=0, keepdims=True)) * NORM_SLACK

        nrm_ref[...] = jnp.concatenate(
            [bound(kb), bound(qb), jnp.zeros((SUBLANES - 2, LANES), F32)], axis=0)
    v = _dot_nt(hb, w_ref[2 * fw:3 * fw, :])
    v_ref[...] = v
    if want_vt:
        vt_ref[...] = v.astype(BF16).T
    gate = _dot_nt(hb, w_ref[3 * fw:4 * fw, :])
    sg_ref[...] = gate * jax.nn.sigmoid(gate)
    fl = _dot_nt(hb, wf_ref[...]) + bf_ref[...]
    lft_ref[...] = _log_sigmoid(fl).T[:FOX_HEADS, :]


def _fox_proj(x, g, w_main, w_f, b_f, *, tm, want_vt):
    B, L, _ = x.shape
    n = L // tm
    tok = lambda w: pl.BlockSpec((None, tm, w), lambda b, i: (b, i, 0))
    full = lambda a: pl.BlockSpec(a.shape, lambda b, i: (0,) * a.ndim)
    in_specs = [tok(D_MODEL), full(g), full(w_main), full(w_f), full(b_f)]
    args = [x, g, w_main, w_f, b_f]
    out_specs = [tok(FOX_WIDTH)] * 5 + [pl.BlockSpec((None, FOX_HEADS, tm), lambda b, i: (b, 0, i))]
    out_shape = [
        jax.ShapeDtypeStruct((B, L, FOX_WIDTH), BF16),
        jax.ShapeDtypeStruct((B, L, FOX_WIDTH), F32),
        jax.ShapeDtypeStruct((B, L, FOX_WIDTH), F32),
        jax.ShapeDtypeStruct((B, L, FOX_WIDTH), BF16),
        jax.ShapeDtypeStruct((B, L, FOX_WIDTH), F32),
        jax.ShapeDtypeStruct((B, FOX_HEADS, L), F32),
    ]
    if want_vt:
        out_specs.append(pl.BlockSpec((None, None, FOX_WIDTH, tm), lambda b, i: (b, i, 0, 0)))
        out_shape.append(jax.ShapeDtypeStruct((B, n, FOX_WIDTH, tm), BF16))
        out_specs.append(pl.BlockSpec((None, None, SUBLANES, LANES), lambda b, i: (b, i, 0, 0)))
        out_shape.append(jax.ShapeDtypeStruct((B, n, SUBLANES, LANES), F32))
    return pl.pallas_call(
        functools.partial(_fox_proj_kernel, want_vt=want_vt),
        grid=(B, n),
        in_specs=in_specs,
        out_specs=out_specs,
        out_shape=out_shape,
        compiler_params=_params(2),
        name="fox_proj_vt" if want_vt else "fox_proj",
    )(*args)


BIAS_SLOT = 8


def _bias_selectors():
    shape = (LANES, 3 * FOX_HEADS)
    lane = lax.broadcasted_iota(jnp.int32, shape, 0)
    col = lax.broadcasted_iota(jnp.int32, shape, 1)
    slot = BIAS_SLOT * (col & (FOX_HEADS - 1))
    piece = col >> 4
    sel_f = jnp.where(lane == slot + piece, 1.0, 0.0).astype(BF16)
    sel_e = jnp.where(lane == slot + 3 + piece, -1.0, 0.0).astype(BF16)
    in_slot = lax.broadcasted_iota(jnp.int32, (LANES, 1), 0) & (BIAS_SLOT - 1)
    one_f = jnp.where((in_slot >= 3) & (in_slot < 6), 1.0, 0.0)
    one_e = jnp.where(in_slot < 3, 1.0, 0.0)
    return sel_f, sel_e, one_f, one_e


def _stack3(x):
    return jnp.concatenate(_split3(x), axis=0)


def _bias_kernel(lft_ref, e_ref, f_ref, rng_ref, carry_ref, *, rows):
    r = lax.broadcasted_iota(jnp.int32, (rows, rows), 0)
    c = lax.broadcasted_iota(jnp.int32, (rows, rows), 1)
    upper = jnp.where(r <= c, 1.0, 0.0).astype(BF16)
    sel_f, sel_e, one_f, one_e = _bias_selectors()
    nh = FOX_HEADS

    @pl.when(pl.program_id(1) == 0)
    def _():
        carry_ref[...] = jnp.zeros_like(carry_ref)

    carry = carry_ref[:, 0:1]
    for sb in range(rng_ref.shape[0]):
        cols = slice(sb * rows, (sb + 1) * rows)
        st = _dot(_stack3(lft_ref[:, cols]), upper)
        cs = ((st[0:nh] + st[nh:2 * nh]) + st[2 * nh:3 * nh]) + carry
        carry = cs[:, rows - 1:rows]
        c2 = cs * LOG2E
        pieces = _stack3(c2)
        f_ref[cols, :] = (_dot(sel_f, pieces) + one_f).T.astype(BF16)
        e_ref[cols, :] = (_dot(sel_e, pieces) + one_e).T.astype(BF16)
        rng_ref[sb] = jnp.concatenate(
            [jnp.broadcast_to(jnp.min(c2, axis=1, keepdims=True), (nh, LANES)),
             jnp.broadcast_to(jnp.max(c2, axis=1, keepdims=True), (nh, LANES))], axis=0)
    carry_ref[...] = jnp.broadcast_to(carry, carry_ref.shape)


def _bias_tiles(lft, *, rows):
    B, H, L = lft.shape
    per_step = 4
    span = per_step * rows
    wide = pl.BlockSpec((None, span, LANES), lambda b, i: (b, i, 0))
    e, f, rng = pl.pallas_call(
        functools.partial(_bias_kernel, rows=rows),
        grid=(B, L // span),
        in_specs=[pl.BlockSpec((None, H, span), lambda b, i: (b, 0, i))],
        out_specs=[wide, wide, pl.BlockSpec((None, per_step, 2 * H, LANES), lambda b, i: (b, i, 0, 0))],
        out_shape=[jax.ShapeDtypeStruct((B, L, LANES), BF16)] * 2
        + [jax.ShapeDtypeStruct((B, L // rows, 2 * H, LANES), F32)],
        scratch_shapes=[pltpu.VMEM((H, LANES), F32)],
        compiler_params=_params(2),
        name="fox_cumsum_bias",
    )(lft)
    return e, f, rng[:, :, :H, 0], rng[:, :, H:, 0]


def _cumsum_short_kernel(lft_ref, ct_ref, *, cols):
    nbatch, nh, length = lft_ref.shape
    r = lax.broadcasted_iota(jnp.int32, (cols, cols), 0)
    c = lax.broadcasted_iota(jnp.int32, (cols, cols), 1)
    upper = jnp.where(r <= c, 1.0, 0.0).astype(BF16)
    carry = [jnp.zeros((nh, 1), F32)] * nbatch
    for c0 in range(0, length, cols):
        span = slice(c0, c0 + cols)
        st = _dot(jnp.concatenate([_stack3(lft_ref[bi, :, span]) for bi in range(nbatch)], axis=0), upper)
        for bi in range(nbatch):
            base = 3 * nh * bi
            cs = ((st[base:base + nh] + st[base + nh:base + 2 * nh]) + st[base + 2 * nh:base + 3 * nh]) + carry[bi]
            ct_ref[bi, :, span] = cs * LOG2E
            carry[bi] = cs[:, cols - 1:cols]


def _cumsum_short(lft, *, cols, per_step):
    B, H, L = lft.shape
    spec = pl.BlockSpec((per_step, H, L), lambda b: (b, 0, 0))
    return pl.pallas_call(
        functools.partial(_cumsum_short_kernel, cols=cols),
        grid=(B // per_step,),
        in_specs=[spec],
        out_specs=spec,
        out_shape=jax.ShapeDtypeStruct((B, H, L), F32),
        compiler_params=_params(1),
        name="fox_cumsum",
    )(lft)


def _skip_plan_kernel(kn_ref, cmin_ref, qn_ref, cmax_ref, knq_ref, first_ref, *, nb):
    j = lax.broadcasted_iota(jnp.int32, (nb, nb), 0).astype(F32)
    q = lax.broadcasted_iota(jnp.int32, (nb, nb), 1).astype(F32)
    rows = []
    for p in range(FOX_PAIRS):
        skippable = None
        for h in (2 * p, 2 * p + 1):
            qn = qn_ref[h:h + 1, :]
            bound = (kn_ref[:, h:h + 1] * qn + (cmax_ref[h:h + 1, :] - cmin_ref[:, h:h + 1])
                     + qn * knq_ref[h:h + 1, :])
            ok = bound < -SKIP_LOG2
            skippable = ok if skippable is None else (skippable & ok)
        must = (j < q) & jnp.logical_not(skippable)
        rows.append(jnp.min(jnp.where(must, j, q), axis=0, keepdims=True))
    first_ref[...] = jnp.concatenate(rows, axis=0).astype(jnp.int32)


def _skip_plan(kn, cmin, qn, cmax):
    B, nb, H = kn.shape
    by_block = pl.BlockSpec((None, nb, H), lambda b: (b, 0, 0))
    by_head = pl.BlockSpec((None, H, nb), lambda b: (b, 0, 0))
    t = lambda a: jnp.swapaxes(a, 1, 2)
    return pl.pallas_call(
        functools.partial(_skip_plan_kernel, nb=nb),
        grid=(B,),
        in_specs=[by_block, by_block, by_head, by_head, by_head],
        out_specs=pl.BlockSpec((None, FOX_PAIRS, nb), lambda b: (b, 0, 0)),
        out_shape=jax.ShapeDtypeStruct((B, FOX_PAIRS, nb), jnp.int32),
        compiler_params=_params(1),
        name="fox_skip_plan",
    )(kn, cmin, t(qn), t(cmax), t(kn))


def _stack_heads(q2):
    lane = lax.broadcasted_iota(jnp.int32, q2.shape, 1)
    zero = jnp.zeros_like(q2)
    return jnp.concatenate(
        [jnp.where(lane < FOX_HEAD_DIM, q2, zero), jnp.where(lane >= FOX_HEAD_DIM, q2, zero)], axis=0)


ATTN_TILE = 256
SUM_ROWS = 16


def _attn_prompt_kernel(n_ref, vq_ref, vj_ref, q_ref, f_ref, k_ref, e_ref, vt_ref, o_ref,
                        m_ref, acc_ref, sa_ref, xa_ref, sb_ref, xb_ref, *, blk, nb):
    b_id = pl.program_id(0)
    p_id = pl.program_id(1)
    hd = FOX_HEAD_DIM
    tw = ATTN_TILE
    per_head = blk // tw
    n_tiles = 2 * per_head

    lane = lax.broadcasted_iota(jnp.int32, (blk, LANES), 1)
    zero = jnp.zeros((blk, LANES), BF16)
    slot_a = 2 * BIAS_SLOT * p_id
    slot_b = slot_a + BIAS_SLOT
    ones_rows = jnp.ones((SUM_ROWS, blk), BF16)

    def query_tiles(qi):
        r0 = pl.multiple_of(qi * blk, blk)
        q2 = q_ref[pl.ds(r0, blk), :]
        f2 = f_ref[pl.ds(r0, blk), :]
        heads = [
            jnp.concatenate([jnp.where(lane < hd, q2, zero),
                             jnp.where((lane >= slot_a) & (lane < slot_b), f2, zero)], axis=1),
            jnp.concatenate([jnp.where(lane >= hd, q2, zero),
                             jnp.where((lane >= slot_b) & (lane < slot_b + BIAS_SLOT), f2, zero)], axis=1),
        ]
        return [heads[t // per_head][(t % per_head) * tw:(t % per_head + 1) * tw] for t in range(n_tiles)]

    def stage(nxt, cur):
        if nxt is not None:
            qi_n, j_n, (sn_ref, xn_ref), diag_n = nxt
            q_tiles = query_tiles(qi_n)
            k0 = pl.multiple_of(j_n * blk, blk)
            kaug = jnp.concatenate([k_ref[pl.ds(k0, blk), :], e_ref[pl.ds(k0, blk), :]], axis=1)
        if cur is not None:
            qi_c, j_c, (sc_ref, xc_ref), diag_c = cur
            vt = vt_ref[j_c]
            v_heads = [jnp.concatenate([vt[h * hd:(h + 1) * hd, :], ones_rows], axis=0) for h in range(2)]
        for t in range(n_tiles):
            c0 = (t % per_head) * tw
            if nxt is not None:
                live = c0 + tw if diag_n else blk
                s = _dot_nt(kaug[:live], q_tiles[t])
                if diag_n:
                    r = lax.broadcasted_iota(jnp.int32, (live, tw), 0)
                    c = lax.broadcasted_iota(jnp.int32, (live, tw), 1) + c0
                    s = jnp.where(r <= c, s, NEG_INF)
                sn_ref[t, 0:live, :] = s
                xn_ref[t] = jnp.max(s, axis=0, keepdims=True)
            if cur is not None:
                live = c0 + tw if diag_c else blk
                idx = qi_c * n_tiles + t
                if diag_c:
                    m_new = xc_ref[t]
                else:
                    m_old = m_ref[idx]
                    m_new = jnp.maximum(m_old, xc_ref[t])
                    alpha = jnp.exp2(m_old - m_new)
                p = jnp.exp2(sc_ref[t, 0:live, :] - m_new)
                m_ref[idx] = m_new
                pv = _dot(v_heads[t // per_head][:, 0:live], p.astype(BF16))
                acc_ref[idx] = pv if diag_c else acc_ref[idx] * alpha + pv

    buf_a = (sa_ref, xa_ref)
    buf_b = (sb_ref, xb_ref)

    diag = lambda n, buf: (n, n, buf, True)
    n_off = n_ref[b_id, p_id]
    off = lambda v, buf: (vq_ref[b_id, p_id, v], vj_ref[b_id, p_id, v], buf, False)

    stage(diag(0, buf_a), None)

    def diag_pair(n, carry):
        i0 = 2 * n
        stage(diag(i0 + 1, buf_b), diag(i0, buf_a))
        stage(diag(i0 + 2, buf_a), diag(i0 + 1, buf_b))
        return carry

    lax.fori_loop(0, nb // 2 - 1, diag_pair, 0)
    stage(diag(nb - 1, buf_b), diag(nb - 2, buf_a))

    @pl.when(n_off > 0)
    def _():
        stage(off(0, buf_a), diag(nb - 1, buf_b))

    @pl.when(n_off == 0)
    def _():
        stage(None, diag(nb - 1, buf_b))

    def off_pair(n, carry):
        v0 = 2 * n
        stage(off(v0 + 1, buf_b), off(v0, buf_a))
        stage(off(v0 + 2, buf_a), off(v0 + 1, buf_b))
        return carry

    lax.fori_loop(0, jnp.maximum(n_off - 1, 0) // 2, off_pair, 0)

    @pl.when((n_off > 0) & (n_off % 2 == 0))
    def _():
        stage(off(n_off - 1, buf_b), off(n_off - 2, buf_a))
        stage(None, off(n_off - 1, buf_b))

    @pl.when(n_off % 2 == 1)
    def _():
        stage(None, off(n_off - 1, buf_a))

    def finish(qi, carry):
        halves = [jnp.concatenate([acc_ref[qi * n_tiles + t, 0:hd, :] / acc_ref[qi * n_tiles + t, hd:hd + 1, :]
                                   for t in range(h * per_head, (h + 1) * per_head)], axis=1)
                  for h in range(2)]
        o_ref[pl.ds(pl.multiple_of(qi * blk, blk), blk), :] = jnp.concatenate(halves, axis=0).T
        return carry

    lax.fori_loop(0, nb, finish, 0)


def _visit_lists(first):
    B, P, nb = first.shape
    vmax = nb * (nb - 1) // 2 + SUBLANES
    q_idx = jnp.arange(nb, dtype=jnp.int32)
    cnt = q_idx - first
    incl = jnp.cumsum(cnt, axis=-1)
    excl = incl - cnt
    v = jnp.arange(vmax, dtype=jnp.int32)
    vq = jnp.minimum(jnp.sum(v[None, None, :, None] >= incl[:, :, None, :], axis=-1), nb - 1).astype(jnp.int32)
    pick = vq[..., None] == q_idx
    take = lambda a: jnp.sum(jnp.where(pick, a[:, :, None, :], 0), axis=-1)
    vj = jnp.clip(take(first) + (v - take(excl)), 0, nb - 1).astype(jnp.int32)
    return incl[..., -1].astype(jnp.int32), vq, vj


def _attn_prompt(first, q, f, kb, e, vt, *, blk):
    B, L, _ = q.shape
    nb = L // blk
    n_tiles = 2 * blk // ATTN_TILE
    n_off, vq, vj = _visit_lists(first)
    by_pair = pl.BlockSpec((None, L, LANES), lambda b, p, *_: (b, 0, p))
    shared = pl.BlockSpec((None, L, LANES), lambda b, p, *_: (b, 0, 0))
    return pl.pallas_call(
        functools.partial(_attn_prompt_kernel, blk=blk, nb=nb),
        grid_spec=pltpu.PrefetchScalarGridSpec(
            num_scalar_prefetch=3,
            grid=(B, FOX_PAIRS),
            in_specs=[by_pair, shared, by_pair, shared,
                      pl.BlockSpec((None, nb, LANES, blk), lambda b, p, *_: (b, 0, p, 0))],
            out_specs=by_pair,
            scratch_shapes=[
                pltpu.VMEM((nb * n_tiles, 1, ATTN_TILE), F32),
                pltpu.VMEM((nb * n_tiles, FOX_HEAD_DIM + SUM_ROWS, ATTN_TILE), F32),
                pltpu.VMEM((n_tiles, blk, ATTN_TILE), F32),
                pltpu.VMEM((n_tiles, 1, ATTN_TILE), F32),
                pltpu.VMEM((n_tiles, blk, ATTN_TILE), F32),
                pltpu.VMEM((n_tiles, 1, ATTN_TILE), F32),
            ],
        ),
        out_shape=jax.ShapeDtypeStruct((B, L, FOX_WIDTH), F32),
        compiler_params=_params(2),
        name="fox_attn_prompt",
    )(n_off, vq, vj, q, f, kb, e, vt)


def _attn_sample_kernel(q_ref, kn_ref, vn_ref, ck_ref, cv_ref, cq_ref, ct_ref, o_ref, *, past, ls):
    hd = FOX_HEAD_DIM
    pad_k = jnp.zeros((LANES - ls, LANES), BF16)
    cqb = cq_ref[...]
    lane = lax.broadcasted_iota(jnp.int32, cqb.shape, 1)
    r = lax.broadcasted_iota(jnp.int32, (2 * ls, LANES), 0) & (ls - 1)
    c = lax.broadcasted_iota(jnp.int32, (2 * ls, LANES), 1)
    lane_o = lax.broadcasted_iota(jnp.int32, (ls, LANES), 1)

    for p in range(FOX_PAIRS):
        cols = slice(p * LANES, (p + 1) * LANES)
        qst = _stack_heads(q_ref[:, cols])
        kc_t = ck_ref[cols, :].astype(BF16)
        vc_t = cv_ref[cols, :].astype(BF16)
        kn = jnp.concatenate([kn_ref[:, cols], pad_k], axis=0)
        vn = jnp.concatenate([vn_ref[:, cols].astype(BF16), pad_k], axis=0)
        cq = jnp.concatenate(
            [jnp.sum(jnp.where(lane == 2 * p + h2, cqb, 0.0), axis=1, keepdims=True) for h2 in range(2)],
            axis=0)
        cka = ct_ref[2 * p:2 * p + 1, :]
        ckb = ct_ref[2 * p + 1:2 * p + 2, :]

        def bias(lo, hi):
            return jnp.concatenate([cq[:ls] - cka[:, lo:hi], cq[ls:] - ckb[:, lo:hi]], axis=0)

        s_c = _dot(qst, kc_t) + bias(0, past)
        s_n = _dot_nt(qst, kn) + bias(past, past + LANES)
        s_n = jnp.where(c <= r, s_n, NEG_INF)
        m = jnp.maximum(jnp.max(s_c, axis=1, keepdims=True), jnp.max(s_n, axis=1, keepdims=True))
        p_c = jnp.exp2(s_c - m)
        p_n = jnp.exp2(s_n - m)
        l = jnp.sum(p_c, axis=1, keepdims=True) + jnp.sum(p_n, axis=1, keepdims=True)
        o = (_dot_nt(p_c.astype(BF16), vc_t) + _dot(p_n.astype(BF16), vn)) / l
        o_ref[:, cols] = jnp.where(lane_o < hd, o[:ls], o[ls:])


def _attn_sample(q, kb, v, cache_k, cache_v, cq, ct, *, past):
    B, ls, _ = q.shape
    tot = ct.shape[2]
    rows = lambda n: pl.BlockSpec((None, n, FOX_WIDTH), lambda b: (b, 0, 0))
    cache_t = pl.BlockSpec((None, FOX_WIDTH, past), lambda b: (b, 0, 0))
    return pl.pallas_call(
        functools.partial(_attn_sample_kernel, past=past, ls=ls),
        grid=(B,),
        in_specs=[
            rows(ls), rows(ls), rows(ls), cache_t, cache_t,
            pl.BlockSpec((None, ls, FOX_HEADS), lambda b: (b, 0, 0)),
            pl.BlockSpec((None, FOX_HEADS, tot), lambda b: (b, 0, 0)),
        ],
        out_specs=rows(ls),
        out_shape=jax.ShapeDtypeStruct((B, ls, FOX_WIDTH), F32),
        compiler_params=_params(1),
        name="fox_attn_sample",
    )(q, kb, v, cache_k, cache_v, cq, ct)


def _mid_kernel(o_ref, sg_ref, x_ref, wo_ref, g_ref, w_ref, wa1_ref, wa2_ref, ba_ref,
                y_ref, q_ref, k_ref, v_ref, sg2_ref, ga_ref):
    kw = GLA_KEY_WIDTH
    u = (o_ref[...] * sg_ref[...]).astype(BF16)
    y = x_ref[...] + _dot(u, wo_ref[...])
    y_ref[...] = y
    hb = _rms(y, g_ref[...]).astype(BF16)
    q_ref[...] = _dot_nt(hb, w_ref[0:kw, :]) * (GLA_HEAD_K ** -0.5)
    k_ref[...] = _dot_nt(hb, w_ref[kw:2 * kw, :])
    v_ref[...] = _dot_nt(hb, w_ref[2 * kw:2 * kw + GLA_VAL_WIDTH, :]).astype(BF16)
    gate = _dot_nt(hb, w_ref[2 * kw + GLA_VAL_WIDTH:2 * kw + 2 * GLA_VAL_WIDTH, :])
    sg2_ref[...] = gate * jax.nn.sigmoid(gate)
    a1 = _dot_nt(hb, wa1_ref[...]).astype(BF16)
    a = _dot(a1, wa2_ref[...]) + ba_ref[...]
    ga_ref[...] = _log_sigmoid(a) * (1.0 / GLA_GATE_TEMP)


def _mid(o, sg, x, w_out, g, w_main, w_a1, w_a2, b_a, *, tm):
    T = o.shape[0]
    tok = lambda w: pl.BlockSpec((tm, w), lambda i: (i, 0))
    full = lambda a: pl.BlockSpec(a.shape, lambda i: (0,) * a.ndim)
    return pl.pallas_call(
        _mid_kernel,
        grid=(T // tm,),
        in_specs=[tok(D_MODEL), tok(D_MODEL), tok(D_MODEL), full(w_out), full(g), full(w_main),
                  full(w_a1), full(w_a2), full(b_a)],
        out_specs=[tok(D_MODEL), tok(GLA_KEY_WIDTH), tok(GLA_KEY_WIDTH), tok(GLA_VAL_WIDTH),
                   tok(GLA_VAL_WIDTH), tok(GLA_KEY_WIDTH)],
        out_shape=[
            jax.ShapeDtypeStruct((T, D_MODEL), F32),
            jax.ShapeDtypeStruct((T, GLA_KEY_WIDTH), F32),
            jax.ShapeDtypeStruct((T, GLA_KEY_WIDTH), F32),
            jax.ShapeDtypeStruct((T, GLA_VAL_WIDTH), BF16),
            jax.ShapeDtypeStruct((T, GLA_VAL_WIDTH), F32),
            jax.ShapeDtypeStruct((T, GLA_KEY_WIDTH), F32),
        ],
        compiler_params=_params(1),
        name="fox_out_gla_proj",
    )(o, sg, x, w_out, g, w_main, w_a1, w_a2, b_a)


def _gla_kernel(q_ref, k_ref, v_ref, ga_ref, s0_ref, o_ref, s_ref, *, n_chunks, chained):
    i = pl.program_id(1)
    ch = GLA_CHUNK
    dk, dv = GLA_HEAD_K, GLA_HEAD_V

    @pl.when(i == 0)
    def _():
        s_ref[...] = s0_ref[...]

    t = GLA_GROUP * ch
    r = lax.broadcasted_iota(jnp.int32, (t, t), 0)
    c = lax.broadcasted_iota(jnp.int32, (t, t), 1)
    causal = ((r >> 6) == (c >> 6)) & (r >= c)
    tril = jnp.where(causal, 1.0, 0.0).astype(BF16)
    eye = (lax.broadcasted_iota(jnp.int32, (dk, dk), 0) == lax.broadcasted_iota(jnp.int32, (dk, dk), 1))

    for g0 in range(0, n_chunks, GLA_GROUP):
        grp = slice(g0 * ch, (g0 + GLA_GROUP) * ch)
        b = _tril_cumsum(tril, ga_ref[grp, :])
        lasts = [b[(ci + 1) * ch - 1:(ci + 1) * ch, :] for ci in range(GLA_GROUP)]
        b_last = jnp.concatenate([jnp.broadcast_to(x, (ch, GLA_KEY_WIDTH)) for x in lasts], axis=0)
        kk = k_ref[grp, :]
        qe = (q_ref[grp, :] * jnp.exp(b)).astype(BF16)
        ke = (kk * jnp.exp(-b)).astype(BF16)
        kd = (kk * jnp.exp(b_last - b)).astype(BF16)
        decs = [jnp.exp(x) for x in lasts]

        for h in range(GLA_HEADS):
            ks = slice(h * dk, (h + 1) * dk)
            vs = slice(h * dv, (h + 1) * dv)
            vh = v_ref[grp, vs]
            a = jnp.where(causal, _dot_nt(qe[:, ks], ke[:, ks]), 0.0)
            o_intra = _dot(a.astype(BF16), vh)
            s = s_ref[0, h]
            for ci in range(GLA_GROUP):
                rows = slice(ci * ch, (ci + 1) * ch)
                out_rows = slice((g0 + ci) * ch, (g0 + ci + 1) * ch)
                if not chained:
                    s = s_ref[g0 + ci, h]
                o_ref[out_rows, vs] = o_intra[rows] + _dot(qe[rows, ks], s.astype(BF16))
                dec_col = jnp.sum(jnp.where(eye, jnp.broadcast_to(decs[ci][:, ks], (dk, dk)), 0.0),
                                  axis=1, keepdims=True)
                s = dec_col * s + _dot_tn(kd[rows, ks], vh[rows])
                if not chained:
                    s_ref[g0 + ci, h] = s
            if chained:
                s_ref[0, h] = s


def _gla(q, k, v, ga, s0, *, tc):
    B, L, _ = q.shape
    n_chunks = tc // GLA_CHUNK
    chained = L > GLA_CHUNK
    if chained:
        n_states = 1
    else:
        n_states = n_chunks
        fold = lambda a: a.reshape(B // n_chunks, tc, a.shape[-1])
        q, k, v, ga = fold(q), fold(k), fold(v), fold(ga)
    G, T, _ = q.shape
    tok = lambda w: pl.BlockSpec((None, tc, w), lambda b, i: (b, i, 0))
    st = pl.BlockSpec((n_states, GLA_HEADS, GLA_HEAD_K, GLA_HEAD_V), lambda b, i: (b, 0, 0, 0))
    o, s = pl.pallas_call(
        functools.partial(_gla_kernel, n_chunks=n_chunks, chained=chained),
        grid=(G, T // tc),
        in_specs=[tok(GLA_KEY_WIDTH), tok(GLA_KEY_WIDTH), tok(GLA_VAL_WIDTH), tok(GLA_KEY_WIDTH), st],
        out_specs=[tok(GLA_VAL_WIDTH), st],
        out_shape=[
            jax.ShapeDtypeStruct((G, T, GLA_VAL_WIDTH), F32),
            jax.ShapeDtypeStruct((B, GLA_HEADS, GLA_HEAD_K, GLA_HEAD_V), F32),
        ],
        compiler_params=_params(2),
        name="gla_chunk",
    )(q, k, v, ga, s0)
    return o.reshape(B, L, GLA_VAL_WIDTH), s


def _gla_out_kernel(o_ref, sg_ref, y_ref, go_ref, wo_ref, gf_ref, out_ref, *, chunk_major):
    dv = GLA_HEAD_V
    if chunk_major:
        o = jnp.concatenate([o_ref[:, t, :] for t in range(o_ref.shape[1])], axis=0)
    else:
        o = o_ref[...]
    parts = []
    for h in range(GLA_HEADS):
        vs = slice(h * dv, (h + 1) * dv)
        parts.append((_rms(o[:, vs], go_ref[...]) * sg_ref[:, vs]).astype(BF16))
    u = jnp.concatenate(parts, axis=1)
    y = y_ref[...] + _dot(u, wo_ref[...])
    out_ref[...] = _rms(y, gf_ref[...])


def _gla_out(o, sg, y, g_o, w_out, g_f):
    B, L, _ = o.shape
    n = L // GLA_CHUNK
    chunk_major = n > 1
    if chunk_major:
        tm = SUBLANES * n
        o = o.reshape(B, n, GLA_CHUNK, D_MODEL)
        o_spec = pl.BlockSpec((None, n, SUBLANES, D_MODEL), lambda b, i: (b, 0, i, 0))
    else:
        tm = PROJ_TM
        flat = lambda a: a.reshape(1, B * L, D_MODEL)
        o, sg, y = flat(o), flat(sg), flat(y)
        o_spec = pl.BlockSpec((None, tm, D_MODEL), lambda b, i: (b, i, 0))
    nb, rows, _ = sg.shape
    tok = pl.BlockSpec((None, tm, D_MODEL), lambda b, i: (b, i, 0))
    full = lambda a: pl.BlockSpec(a.shape, lambda b, i: (0,) * a.ndim)
    return pl.pallas_call(
        functools.partial(_gla_out_kernel, chunk_major=chunk_major),
        grid=(nb, rows // tm),
        in_specs=[o_spec, tok, tok, full(g_o), full(w_out), full(g_f)],
        out_specs=tok,
        out_shape=jax.ShapeDtypeStruct((nb, rows, D_MODEL), F32),
        compiler_params=_params(2),
        name="gla_out",
    )(o, sg, y, g_o, w_out, g_f)


def _pad_cols(w, n):
    return jnp.pad(w, ((0, 0), (0, n - w.shape[1])))


def kernel(x_prompt, x_sample, cache_fox_k, cache_fox_v, cache_fox_logf, state_gla,
           g_norm_fox, w_in_fox, b_fox_f, w_out_fox,
           g_norm_gla, w_in_gla, w_gla_a2, b_gla_a, g_gla_o, w_out_gla, g_final):
    B, L, D = x_prompt.shape
    Bs, Ls, _ = x_sample.shape
    P = cache_fox_k.shape[1]
    H, hd = FOX_HEADS, FOX_HEAD_DIM

    assert D == D_MODEL and L % (2 * GLA_TC) == 0 and L % ATTN_BLOCK == 0
    assert Ls == GLA_CHUNK and (Bs * Ls) % PROJ_TM == 0 and Bs % (GLA_TC // GLA_CHUNK) == 0
    assert (P + LANES) % 3 == 0 and P % LANES == 0 and cache_fox_k.shape[2:] == (H, hd)

    row = lambda a: a.reshape(1, -1).astype(F32)
    pad_rows = lambda w: jnp.pad(w, ((0, LANES - w.shape[0]), (0, 0)))
    w_fox_t = w_in_fox.T
    w_fox = w_fox_t[:4 * FOX_WIDTH].astype(BF16)
    w_fox_f = pad_rows(w_fox_t[4 * FOX_WIDTH:]).astype(BF16)
    b_f = _pad_cols(row(b_fox_f), LANES)
    w_o_fox = w_out_fox.astype(BF16)
    n_main = 2 * GLA_KEY_WIDTH + 2 * GLA_VAL_WIDTH
    w_gla_t = w_in_gla.T
    w_gla = w_gla_t[:n_main].astype(BF16)
    w_a1 = pad_rows(w_gla_t[n_main:]).astype(BF16)
    w_a2 = pad_rows(w_gla_a2).astype(BF16)
    w_o_gla = w_out_gla.astype(BF16)

    q_p, k_p, v_p, kb_p, sg_p, lft_p, vt_p, nrm_p = _fox_proj(
        x_prompt, row(g_norm_fox), w_fox, w_fox_f, b_f, tm=ATTN_BLOCK, want_vt=True)
    e_p, f_p, cmin_p, cmax_p = _bias_tiles(lft_p, rows=ATTN_BLOCK)
    first_p = _skip_plan(nrm_p[:, :, 0, :H], cmin_p, nrm_p[:, :, 1, :H], cmax_p)
    o_p = _attn_prompt(first_p, q_p, f_p, kb_p, e_p, vt_p, blk=ATTN_BLOCK)
    lf_p = jnp.swapaxes(lft_p, 1, 2)

    xs = x_sample.reshape(1, Bs * Ls, D)
    q_s, k_s, v_s, kb_s, sg_s, lft_s = _fox_proj(
        xs, row(g_norm_fox), w_fox, w_fox_f, b_f, tm=PROJ_TM, want_vt=False)
    sh = lambda a: a.reshape(Bs, Ls, a.shape[-1])
    q_s, k_s, v_s, kb_s = sh(q_s), sh(k_s), sh(v_s), sh(kb_s)
    lft_s = jnp.swapaxes(lft_s.reshape(H, Bs, Ls), 0, 1)
    lf_s = jnp.swapaxes(lft_s, 1, 2)
    lft_all = jnp.concatenate(
        [jnp.swapaxes(cache_fox_logf.astype(F32), 1, 2), lft_s, jnp.zeros((Bs, H, LANES - Ls), F32)], axis=2)
    ct_s = _cumsum_short(lft_all, cols=(P + LANES) // 3, per_step=4)
    cache_t = lambda a: jnp.transpose(a, (0, 2, 3, 1)).reshape(Bs, FOX_WIDTH, P)
    o_s = _attn_sample(q_s, kb_s, v_s, cache_t(cache_fox_k), cache_t(cache_fox_v),
                       jnp.swapaxes(ct_s[:, :, P:P + Ls], 1, 2), ct_s, past=P)

    def gla_layer(o, sg, x, s0, tc):
        b, l, _ = x.shape
        flat = lambda a: a.reshape(b * l, a.shape[-1])
        y1, q, k, v, sg2, ga = _mid(flat(o), flat(sg), flat(x), w_o_fox, row(g_norm_gla), w_gla,
                                    w_a1, w_a2, row(b_gla_a), tm=PROJ_TM)
        un = lambda a: a.reshape(b, l, a.shape[-1])
        og, s_out = _gla(un(q), un(k), un(v), un(ga), s0, tc=tc)
        y = _gla_out(og, un(sg2), un(y1), row(g_gla_o), w_o_gla, row(g_final))
        return y.reshape(b, l, D), s_out

    s0_p = jnp.zeros((B, GLA_HEADS, GLA_HEAD_K, GLA_HEAD_V), F32)
    y_p, s_p = gla_layer(o_p, sg_p, x_prompt, s0_p, GLA_TC)
    y_s, s_s = gla_layer(o_s, sg_s.reshape(Bs, Ls, D), x_sample, state_gla.astype(F32), GLA_TC)

    return (y_p, y_s,
            k_p.reshape(B, L, H, hd), v_p.reshape(B, L, H, hd), lf_p, s_p.astype(state_gla.dtype),
            k_s.reshape(Bs, Ls, H, hd), v_s.reshape(Bs, Ls, H, hd), lf_s, s_s.astype(state_gla.dtype))
```

```python
import functools

import jax
import jax.numpy as jnp
from jax import lax
from jax.experimental import pallas as pl
from jax.experimental.pallas import tpu as pltpu

F32 = jnp.float32
BF16 = jnp.bfloat16

D_MODEL = 1024
EPS = 1e-6
NEG_INF = -1e30
LOG2E = 1.4426950408889634
NORM_SLACK = 1.01
SKIP_LOG2 = 152.0

FOX_HEADS = 16
FOX_HEAD_DIM = 64
FOX_WIDTH = FOX_HEADS * FOX_HEAD_DIM
FOX_PAIRS = FOX_HEADS // 2

GLA_HEADS = 4
GLA_KEY_WIDTH = 512
GLA_VAL_WIDTH = 1024
GLA_HEAD_K = 128
GLA_HEAD_V = 256
GLA_GATE_RANK = 16
GLA_GATE_TEMP = 16.0
GLA_CHUNK = 64

LANES = 128
SUBLANES = 8
VMEM_LIMIT = 56 * 1024 * 1024

PROJ_TM = 512
ATTN_BLOCK = 512
GLA_GROUP = 4
GLA_TC = 512


def _params(n_axes):
    return pltpu.CompilerParams(
        dimension_semantics=("arbitrary",) * n_axes,
        vmem_limit_bytes=VMEM_LIMIT,
    )


def _dot(a, b):
    return jnp.dot(a, b, preferred_element_type=F32)


def _dot_nt(a, b):
    return lax.dot_general(a, b, (((1,), (1,)), ((), ())), preferred_element_type=F32)


def _dot_tn(a, b):
    return lax.dot_general(a, b, (((0,), (0,)), ((), ())), preferred_element_type=F32)


def _log_sigmoid(z):
    return jnp.minimum(z, 0.0) - jnp.log1p(jnp.exp(-jnp.abs(z)))


def _rms(x, g):
    ms = jnp.mean(x * x, axis=-1, keepdims=True)
    return (x * lax.rsqrt(ms + EPS)) * g


def _split3(x):
    hi = x.astype(BF16)
    r1 = x - hi.astype(F32)
    mid = r1.astype(BF16)
    lo = (r1 - mid.astype(F32)).astype(BF16)
    return hi, mid, lo


def _tril_cumsum(tril_bf16, x):
    hi, mid, lo = _split3(x)
    return (_dot(tril_bf16, hi) + _dot(tril_bf16, mid)) + _dot(tril_bf16, lo)


def _fox_proj_kernel(x_ref, g_ref, w_ref, wf_ref, bf_ref, *rest, want_vt):
    if want_vt:
        q_ref, k_ref, v_ref, kb_ref, sg_ref, lft_ref, vt_ref, nrm_ref = rest
    else:
        q_ref, k_ref, v_ref, kb_ref, sg_ref, lft_ref = rest
    fw = FOX_WIDTH
    hb = _rms(x_ref[...], g_ref[...]).astype(BF16)
    q = _dot_nt(hb, w_ref[0:fw, :])
    qb = (q * (LOG2E * FOX_HEAD_DIM ** -0.5)).astype(BF16)
    q_ref[...] = qb
    k = _dot_nt(hb, w_ref[fw:2 * fw, :])
    k_ref[...] = k
    kb = k.astype(BF16)
    kb_ref[...] = kb
    if want_vt:
        row = lax.broadcasted_iota(jnp.int32, (FOX_WIDTH, LANES), 0)
        col = lax.broadcasted_iota(jnp.int32, (FOX_WIDTH, LANES), 1)
        head_of = jnp.where((row >> 6) == col, 1.0, 0.0).astype(BF16)

        def bound(xb):
            xf = xb.astype(F32)
            sq = _dot((xf * xf).astype(BF16), head_of)
            return jnp.sqrt(jnp.max(sq, axis=0, keepdims=True)) * NORM_SLACK

        nrm_ref[...] = jnp.concatenate(
            [bound(kb), bound(qb), jnp.zeros((SUBLANES - 2, LANES), F32)], axis=0)
    v = _dot_nt(hb, w_ref[2 * fw:3 * fw, :])
    v_ref[...] = v
    if want_vt:
        vt_ref[...] = v.astype(BF16).T
    gate = _dot_nt(hb, w_ref[3 * fw:4 * fw, :])
    sg_ref[...] = gate * jax.nn.sigmoid(gate)
    fl = _dot_nt(hb, wf_ref[...]) + bf_ref[...]
    lft_ref[...] = _log_sigmoid(fl).T[:FOX_HEADS, :]


def _fox_proj(x, g, w_main, w_f, b_f, *, tm, want_vt):
    B, L, _ = x.shape
    n = L // tm
    tok = lambda w: pl.BlockSpec((None, tm, w), lambda b, i: (b, i, 0))
    full = lambda a: pl.BlockSpec(a.shape, lambda b, i: (0,) * a.ndim)
    in_specs = [tok(D_MODEL), full(g), full(w_main), full(w_f), full(b_f)]
    args = [x, g, w_main, w_f, b_f]
    out_specs = [tok(FOX_WIDTH)] * 5 + [pl.BlockSpec((None, FOX_HEADS, tm), lambda b, i: (b, 0, i))]
    out_shape = [
        jax.ShapeDtypeStruct((B, L, FOX_WIDTH), BF16),
        jax.ShapeDtypeStruct((B, L, FOX_WIDTH), F32),
        jax.ShapeDtypeStruct((B, L, FOX_WIDTH), F32),
        jax.ShapeDtypeStruct((B, L, FOX_WIDTH), BF16),
        jax.ShapeDtypeStruct((B, L, FOX_WIDTH), F32),
        jax.ShapeDtypeStruct((B, FOX_HEADS, L), F32),
    ]
    if want_vt:
        out_specs.append(pl.BlockSpec((None, None, FOX_WIDTH, tm), lambda b, i: (b, i, 0, 0)))
        out_shape.append(jax.ShapeDtypeStruct((B, n, FOX_WIDTH, tm), BF16))
        out_specs.append(pl.BlockSpec((None, None, SUBLANES, LANES), lambda b, i: (b, i, 0, 0)))
        out_shape.append(jax.ShapeDtypeStruct((B, n, SUBLANES, LANES), F32))
    return pl.pallas_call(
        functools.partial(_fox_proj_kernel, want_vt=want_vt),
        grid=(B, n),
        in_specs=in_specs,
        out_specs=out_specs,
        out_shape=out_shape,
        compiler_params=_params(2),
        name="fox_proj_vt" if want_vt else "fox_proj",
    )(*args)


BIAS_SLOT = 8


def _bias_selectors():
    shape = (LANES, 3 * FOX_HEADS)
    lane = lax.broadcasted_iota(jnp.int32, shape, 0)
    col = lax.broadcasted_iota(jnp.int32, shape, 1)
    slot = BIAS_SLOT * (col & (FOX_HEADS - 1))
    piece = col >> 4
    sel_f = jnp.where(lane == slot + piece, 1.0, 0.0).astype(BF16)
    sel_e = jnp.where(lane == slot + 3 + piece, -1.0, 0.0).astype(BF16)
    in_slot = lax.broadcasted_iota(jnp.int32, (LANES, 1), 0) & (BIAS_SLOT - 1)
    one_f = jnp.where((in_slot >= 3) & (in_slot < 6), 1.0, 0.0)
    one_e = jnp.where(in_slot < 3, 1.0, 0.0)
    return sel_f, sel_e, one_f, one_e


def _stack3(x):
    return jnp.concatenate(_split3(x), axis=0)


def _bias_kernel(lft_ref, e_ref, f_ref, rng_ref, carry_ref, *, rows):
    r = lax.broadcasted_iota(jnp.int32, (rows, rows), 0)
    c = lax.broadcasted_iota(jnp.int32, (rows, rows), 1)
    upper = jnp.where(r <= c, 1.0, 0.0).astype(BF16)
    sel_f, sel_e, one_f, one_e = _bias_selectors()
    nh = FOX_HEADS

    @pl.when(pl.program_id(1) == 0)
    def _():
        carry_ref[...] = jnp.zeros_like(carry_ref)

    carry = carry_ref[:, 0:1]
    for sb in range(rng_ref.shape[0]):
        cols = slice(sb * rows, (sb + 1) * rows)
        st = _dot(_stack3(lft_ref[:, cols]), upper)
        cs = ((st[0:nh] + st[nh:2 * nh]) + st[2 * nh:3 * nh]) + carry
        carry = cs[:, rows - 1:rows]
        c2 = cs * LOG2E
        pieces = _stack3(c2)
        f_ref[cols, :] = (_dot(sel_f, pieces) + one_f).T.astype(BF16)
        e_ref[cols, :] = (_dot(sel_e, pieces) + one_e).T.astype(BF16)
        rng_ref[sb] = jnp.concatenate(
            [jnp.broadcast_to(jnp.min(c2, axis=1, keepdims=True), (nh, LANES)),
             jnp.broadcast_to(jnp.max(c2, axis=1, keepdims=True), (nh, LANES))], axis=0)
    carry_ref[...] = jnp.broadcast_to(carry, carry_ref.shape)


def _bias_tiles(lft, *, rows):
    B, H, L = lft.shape
    per_step = 4
    span = per_step * rows
    wide = pl.BlockSpec((None, span, LANES), lambda b, i: (b, i, 0))
    e, f, rng = pl.pallas_call(
        functools.partial(_bias_kernel, rows=rows),
        grid=(B, L // span),
        in_specs=[pl.BlockSpec((None, H, span), lambda b, i: (b, 0, i))],
        out_specs=[wide, wide, pl.BlockSpec((None, per_step, 2 * H, LANES), lambda b, i: (b, i, 0, 0))],
        out_shape=[jax.ShapeDtypeStruct((B, L, LANES), BF16)] * 2
        + [jax.ShapeDtypeStruct((B, L // rows, 2 * H, LANES), F32)],
        scratch_shapes=[pltpu.VMEM((H, LANES), F32)],
        compiler_params=_params(2),
        name="fox_cumsum_bias",
    )(lft)
    return e, f, rng[:, :, :H, 0], rng[:, :, H:, 0]


def _cumsum_short_kernel(lft_ref, ct_ref, *, cols):
    nbatch, nh, length = lft_ref.shape
    r = lax.broadcasted_iota(jnp.int32, (cols, cols), 0)
    c = lax.broadcasted_iota(jnp.int32, (cols, cols), 1)
    upper = jnp.where(r <= c, 1.0, 0.0).astype(BF16)
    carry = [jnp.zeros((nh, 1), F32)] * nbatch
    for c0 in range(0, length, cols):
        span = slice(c0, c0 + cols)
        st = _dot(jnp.concatenate([_stack3(lft_ref[bi, :, span]) for bi in range(nbatch)], axis=0), upper)
        for bi in range(nbatch):
            base = 3 * nh * bi
            cs = ((st[base:base + nh] + st[base + nh:base + 2 * nh]) + st[base + 2 * nh:base + 3 * nh]) + carry[bi]
            ct_ref[bi, :, span] = cs * LOG2E
            carry[bi] = cs[:, cols - 1:cols]


def _cumsum_short(lft, *, cols, per_step):
    B, H, L = lft.shape
    spec = pl.BlockSpec((per_step, H, L), lambda b: (b, 0, 0))
    return pl.pallas_call(
        functools.partial(_cumsum_short_kernel, cols=cols),
        grid=(B // per_step,),
        in_specs=[spec],
        out_specs=spec,
        out_shape=jax.ShapeDtypeStruct((B, H, L), F32),
        compiler_params=_params(1),
        name="fox_cumsum",
    )(lft)


def _skip_plan_kernel(kn_ref, cmin_ref, qn_ref, cmax_ref, knq_ref, first_ref, *, nb):
    j = lax.broadcasted_iota(jnp.int32, (nb, nb), 0).astype(F32)
    q = lax.broadcasted_iota(jnp.int32, (nb, nb), 1).astype(F32)
    rows = []
    for p in range(FOX_PAIRS):
        skippable = None
        for h in (2 * p, 2 * p + 1):
            qn = qn_ref[h:h + 1, :]
            bound = (kn_ref[:, h:h + 1] * qn + (cmax_ref[h:h + 1, :] - cmin_ref[:, h:h + 1])
                     + qn * knq_ref[h:h + 1, :])
            ok = bound < -SKIP_LOG2
            skippable = ok if skippable is None else (skippable & ok)
        must = (j < q) & jnp.logical_not(skippable)
        rows.append(jnp.min(jnp.where(must, j, q), axis=0, keepdims=True))
    first_ref[...] = jnp.concatenate(rows, axis=0).astype(jnp.int32)


def _skip_plan(kn, cmin, qn, cmax):
    B, nb, H = kn.shape
    by_block = pl.BlockSpec((None, nb, H), lambda b: (b, 0, 0))
    by_head = pl.BlockSpec((None, H, nb), lambda b: (b, 0, 0))
    t = lambda a: jnp.swapaxes(a, 1, 2)
    return pl.pallas_call(
        functools.partial(_skip_plan_kernel, nb=nb),
        grid=(B,),
        in_specs=[by_block, by_block, by_head, by_head, by_head],
        out_specs=pl.BlockSpec((None, FOX_PAIRS, nb), lambda b: (b, 0, 0)),
        out_shape=jax.ShapeDtypeStruct((B, FOX_PAIRS, nb), jnp.int32),
        compiler_params=_params(1),
        name="fox_skip_plan",
    )(kn, cmin, t(qn), t(cmax), t(kn))


def _stack_heads(q2):
    lane = lax.broadcasted_iota(jnp.int32, q2.shape, 1)
    zero = jnp.zeros_like(q2)
    return jnp.concatenate(
        [jnp.where(lane < FOX_HEAD_DIM, q2, zero), jnp.where(lane >= FOX_HEAD_DIM, q2, zero)], axis=0)


ATTN_TILE = 256
SUM_ROWS = 16


def _attn_prompt_kernel(n_ref, vq_ref, vj_ref, q_ref, f_ref, k_ref, e_ref, vt_ref, o_ref,
                        m_ref, acc_ref, sa_ref, xa_ref, sb_ref, xb_ref, *, blk, nb):
    b_id = pl.program_id(0)
    p_id = pl.program_id(1)
    hd = FOX_HEAD_DIM
    tw = ATTN_TILE
    per_head = blk // tw
    n_tiles = 2 * per_head

    lane = lax.broadcasted_iota(jnp.int32, (blk, LANES), 1)
    zero = jnp.zeros((blk, LANES), BF16)
    slot_a = 2 * BIAS_SLOT * p_id
    slot_b = slot_a + BIAS_SLOT
    ones_rows = jnp.ones((SUM_ROWS, blk), BF16)

    def query_tiles(qi):
        r0 = pl.multiple_of(qi * blk, blk)
        q2 = q_ref[pl.ds(r0, blk), :]
        f2 = f_ref[pl.ds(r0, blk), :]
        heads = [
            jnp.concatenate([jnp.where(lane < hd, q2, zero),
                             jnp.where((lane >= slot_a) & (lane < slot_b), f2, zero)], axis=1),
            jnp.concatenate([jnp.where(lane >= hd, q2, zero),
                             jnp.where((lane >= slot_b) & (lane < slot_b + BIAS_SLOT), f2, zero)], axis=1),
        ]
        return [heads[t // per_head][(t % per_head) * tw:(t % per_head + 1) * tw] for t in range(n_tiles)]

    def stage(nxt, cur):
        if nxt is not None:
            qi_n, j_n, (sn_ref, xn_ref), diag_n = nxt
            q_tiles = query_tiles(qi_n)
            k0 = pl.multiple_of(j_n * blk, blk)
            kaug = jnp.concatenate([k_ref[pl.ds(k0, blk), :], e_ref[pl.ds(k0, blk), :]], axis=1)
        if cur is not None:
            qi_c, j_c, (sc_ref, xc_ref), diag_c = cur
            vt = vt_ref[j_c]
            v_heads = [jnp.concatenate([vt[h * hd:(h + 1) * hd, :], ones_rows], axis=0) for h in range(2)]
        for t in range(n_tiles):
            c0 = (t % per_head) * tw
            if nxt is not None:
                live = c0 + tw if diag_n else blk
                s = _dot_nt(kaug[:live], q_tiles[t])
                if diag_n:
                    r = lax.broadcasted_iota(jnp.int32, (live, tw), 0)
                    c = lax.broadcasted_iota(jnp.int32, (live, tw), 1) + c0
                    s = jnp.where(r <= c, s, NEG_INF)
                sn_ref[t, 0:live, :] = s
                xn_ref[t] = jnp.max(s, axis=0, keepdims=True)
            if cur is not None:
                live = c0 + tw if diag_c else blk
                idx = qi_c * n_tiles + t
                if diag_c:
                    m_new = xc_ref[t]
                else:
                    m_old = m_ref[idx]
                    m_new = jnp.maximum(m_old, xc_ref[t])
                    alpha = jnp.exp2(m_old - m_new)
                p = jnp.exp2(sc_ref[t, 0:live, :] - m_new)
                m_ref[idx] = m_new
                pv = _dot(v_heads[t // per_head][:, 0:live], p.astype(BF16))
                acc_ref[idx] = pv if diag_c else acc_ref[idx] * alpha + pv

    buf_a = (sa_ref, xa_ref)
    buf_b = (sb_ref, xb_ref)

    diag = lambda n, buf: (n, n, buf, True)
    n_off = n_ref[b_id, p_id]
    off = lambda v, buf: (vq_ref[b_id, p_id, v], vj_ref[b_id, p_id, v], buf, False)

    stage(diag(0, buf_a), None)

    def diag_pair(n, carry):
        i0 = 2 * n
        stage(diag(i0 + 1, buf_b), diag(i0, buf_a))
        stage(diag(i0 + 2, buf_a), diag(i0 + 1, buf_b))
        return carry

    lax.fori_loop(0, nb // 2 - 1, diag_pair, 0)
    stage(diag(nb - 1, buf_b), diag(nb - 2, buf_a))

    @pl.when(n_off > 0)
    def _():
        stage(off(0, buf_a), diag(nb - 1, buf_b))

    @pl.when(n_off == 0)
    def _():
        stage(None, diag(nb - 1, buf_b))

    def off_pair(n, carry):
        v0 = 2 * n
        stage(off(v0 + 1, buf_b), off(v0, buf_a))
        stage(off(v0 + 2, buf_a), off(v0 + 1, buf_b))
        return carry

    lax.fori_loop(0, jnp.maximum(n_off - 1, 0) // 2, off_pair, 0)

    @pl.when((n_off > 0) & (n_off % 2 == 0))
    def _():
        stage(off(n_off - 1, buf_b), off(n_off - 2, buf_a))
        stage(None, off(n_off - 1, buf_b))

    @pl.when(n_off % 2 == 1)
    def _():
        stage(None, off(n_off - 1, buf_a))

    def finish(qi, carry):
        halves = [jnp.concatenate([acc_ref[qi * n_tiles + t, 0:hd, :] / acc_ref[qi * n_tiles + t, hd:hd + 1, :]
                                   for t in range(h * per_head, (h + 1) * per_head)], axis=1)
                  for h in range(2)]
        o_ref[pl.ds(pl.multiple_of(qi * blk, blk), blk), :] = jnp.concatenate(halves, axis=0).T
        return carry

    lax.fori_loop(0, nb, finish, 0)


def _visit_lists(first):
    B, P, nb = first.shape
    vmax = nb * (nb - 1) // 2 + SUBLANES
    q_idx = jnp.arange(nb, dtype=jnp.int32)
    cnt = q_idx - first
    incl = jnp.cumsum(cnt, axis=-1)
    excl = incl - cnt
    v = jnp.arange(vmax, dtype=jnp.int32)
    vq = jnp.minimum(jnp.sum(v[None, None, :, None] >= incl[:, :, None, :], axis=-1), nb - 1).astype(jnp.int32)
    pick = vq[..., None] == q_idx
    take = lambda a: jnp.sum(jnp.where(pick, a[:, :, None, :], 0), axis=-1)
    vj = jnp.clip(take(first) + (v - take(excl)), 0, nb - 1).astype(jnp.int32)
    return incl[..., -1].astype(jnp.int32), vq, vj


def _attn_prompt(first, q, f, kb, e, vt, *, blk):
    B, L, _ = q.shape
    nb = L // blk
    n_tiles = 2 * blk // ATTN_TILE
    n_off, vq, vj = _visit_lists(first)
    by_pair = pl.BlockSpec((None, L, LANES), lambda b, p, *_: (b, 0, p))
    shared = pl.BlockSpec((None, L, LANES), lambda b, p, *_: (b, 0, 0))
    return pl.pallas_call(
        functools.partial(_attn_prompt_kernel, blk=blk, nb=nb),
        grid_spec=pltpu.PrefetchScalarGridSpec(
            num_scalar_prefetch=3,
            grid=(B, FOX_PAIRS),
            in_specs=[by_pair, shared, by_pair, shared,
                      pl.BlockSpec((None, nb, LANES, blk), lambda b, p, *_: (b, 0, p, 0))],
            out_specs=by_pair,
            scratch_shapes=[
                pltpu.VMEM((nb * n_tiles, 1, ATTN_TILE), F32),
                pltpu.VMEM((nb * n_tiles, FOX_HEAD_DIM + SUM_ROWS, ATTN_TILE), F32),
                pltpu.VMEM((n_tiles, blk, ATTN_TILE), F32),
                pltpu.VMEM((n_tiles, 1, ATTN_TILE), F32),
                pltpu.VMEM((n_tiles, blk, ATTN_TILE), F32),
                pltpu.VMEM((n_tiles, 1, ATTN_TILE), F32),
            ],
        ),
        out_shape=jax.ShapeDtypeStruct((B, L, FOX_WIDTH), F32),
        compiler_params=_params(2),
        name="fox_attn_prompt",
    )(n_off, vq, vj, q, f, kb, e, vt)


def _attn_sample_kernel(q_ref, kn_ref, vn_ref, ck_ref, cv_ref, cq_ref, ct_ref, o_ref, *, past, ls):
    hd = FOX_HEAD_DIM
    pad_k = jnp.zeros((LANES - ls, LANES), BF16)
    cqb = cq_ref[...]
    lane = lax.broadcasted_iota(jnp.int32, cqb.shape, 1)
    r = lax.broadcasted_iota(jnp.int32, (2 * ls, LANES), 0) & (ls - 1)
    c = lax.broadcasted_iota(jnp.int32, (2 * ls, LANES), 1)
    lane_o = lax.broadcasted_iota(jnp.int32, (ls, LANES), 1)

    scored = []
    for p in range(FOX_PAIRS):
        cols = slice(p * LANES, (p + 1) * LANES)
        qst = _stack_heads(q_ref[:, cols])
        kc_t = ck_ref[cols, :].astype(BF16)
        kn = jnp.concatenate([kn_ref[:, cols], pad_k], axis=0)
        cq = jnp.concatenate(
            [jnp.sum(jnp.where(lane == 2 * p + h2, cqb, 0.0), axis=1, keepdims=True) for h2 in range(2)],
            axis=0)
        cka = ct_ref[2 * p:2 * p + 1, :]
        ckb = ct_ref[2 * p + 1:2 * p + 2, :]

        def bias(lo, hi):
            return jnp.concatenate([cq[:ls] - cka[:, lo:hi], cq[ls:] - ckb[:, lo:hi]], axis=0)

        s_c = _dot(qst, kc_t) + bias(0, past)
        s_n = _dot_nt(qst, kn) + bias(past, past + LANES)
        s_n = jnp.where(c <= r, s_n, NEG_INF)
        m = jnp.maximum(jnp.max(s_c, axis=1, keepdims=True), jnp.max(s_n, axis=1, keepdims=True))
        scored.append((s_c, s_n, m))

    for p in range(FOX_PAIRS):
        cols = slice(p * LANES, (p + 1) * LANES)
        s_c, s_n, m = scored[p]
        vc_t = cv_ref[cols, :].astype(BF16)
        vn = jnp.concatenate([vn_ref[:, cols].astype(BF16), pad_k], axis=0)
        p_c = jnp.exp2(s_c - m)
        p_n = jnp.exp2(s_n - m)
        l = jnp.sum(p_c, axis=1, keepdims=True) + jnp.sum(p_n, axis=1, keepdims=True)
        o = (_dot_nt(p_c.astype(BF16), vc_t) + _dot(p_n.astype(BF16), vn)) / l
        o_ref[:, cols] = jnp.where(lane_o < hd, o[:ls], o[ls:])


def _attn_sample(q, kb, v, cache_k, cache_v, cq, ct, *, past):
    B, ls, _ = q.shape
    tot = ct.shape[2]
    rows = lambda n: pl.BlockSpec((None, n, FOX_WIDTH), lambda b: (b, 0, 0))
    cache_t = pl.BlockSpec((None, FOX_WIDTH, past), lambda b: (b, 0, 0))
    return pl.pallas_call(
        functools.partial(_attn_sample_kernel, past=past, ls=ls),
        grid=(B,),
        in_specs=[
            rows(ls), rows(ls), rows(ls), cache_t, cache_t,
            pl.BlockSpec((None, ls, FOX_HEADS), lambda b: (b, 0, 0)),
            pl.BlockSpec((None, FOX_HEADS, tot), lambda b: (b, 0, 0)),
        ],
        out_specs=rows(ls),
        out_shape=jax.ShapeDtypeStruct((B, ls, FOX_WIDTH), F32),
        compiler_params=_params(1),
        name="fox_attn_sample",
    )(q, kb, v, cache_k, cache_v, cq, ct)


def _mid_kernel(o_ref, sg_ref, x_ref, wo_ref, g_ref, w_ref, wa1_ref, wa2_ref, ba_ref,
                y_ref, q_ref, k_ref, v_ref, sg2_ref, ga_ref):
    kw = GLA_KEY_WIDTH
    u = (o_ref[...] * sg_ref[...]).astype(BF16)
    y = x_ref[...] + _dot(u, wo_ref[...])
    y_ref[...] = y
    hb = _rms(y, g_ref[...]).astype(BF16)
    q_ref[...] = _dot_nt(hb, w_ref[0:kw, :]) * (GLA_HEAD_K ** -0.5)
    k_ref[...] = _dot_nt(hb, w_ref[kw:2 * kw, :])
    v_ref[...] = _dot_nt(hb, w_ref[2 * kw:2 * kw + GLA_VAL_WIDTH, :]).astype(BF16)
    gate = _dot_nt(hb, w_ref[2 * kw + GLA_VAL_WIDTH:2 * kw + 2 * GLA_VAL_WIDTH, :])
    sg2_ref[...] = gate * jax.nn.sigmoid(gate)
    a1 = _dot_nt(hb, wa1_ref[...]).astype(BF16)
    a = _dot(a1, wa2_ref[...]) + ba_ref[...]
    ga_ref[...] = _log_sigmoid(a) * (1.0 / GLA_GATE_TEMP)


def _mid(o, sg, x, w_out, g, w_main, w_a1, w_a2, b_a, *, tm):
    T = o.shape[0]
    tok = lambda w: pl.BlockSpec((tm, w), lambda i: (i, 0))
    full = lambda a: pl.BlockSpec(a.shape, lambda i: (0,) * a.ndim)
    return pl.pallas_call(
        _mid_kernel,
        grid=(T // tm,),
        in_specs=[tok(D_MODEL), tok(D_MODEL), tok(D_MODEL), full(w_out), full(g), full(w_main),
                  full(w_a1), full(w_a2), full(b_a)],
        out_specs=[tok(D_MODEL), tok(GLA_KEY_WIDTH), tok(GLA_KEY_WIDTH), tok(GLA_VAL_WIDTH),
                   tok(GLA_VAL_WIDTH), tok(GLA_KEY_WIDTH)],
        out_shape=[
            jax.ShapeDtypeStruct((T, D_MODEL), F32),
            jax.ShapeDtypeStruct((T, GLA_KEY_WIDTH), F32),
            jax.ShapeDtypeStruct((T, GLA_KEY_WIDTH), F32),
            jax.ShapeDtypeStruct((T, GLA_VAL_WIDTH), BF16),
            jax.ShapeDtypeStruct((T, GLA_VAL_WIDTH), F32),
            jax.ShapeDtypeStruct((T, GLA_KEY_WIDTH), F32),
        ],
        compiler_params=_params(1),
        name="fox_out_gla_proj",
    )(o, sg, x, w_out, g, w_main, w_a1, w_a2, b_a)


def _gla_kernel(q_ref, k_ref, v_ref, ga_ref, s0_ref, o_ref, s_ref, *, n_chunks, chained):
    i = pl.program_id(1)
    ch = GLA_CHUNK
    dk, dv = GLA_HEAD_K, GLA_HEAD_V

    @pl.when(i == 0)
    def _():
        s_ref[...] = s0_ref[...]

    t = GLA_GROUP * ch
    r = lax.broadcasted_iota(jnp.int32, (t, t), 0)
    c = lax.broadcasted_iota(jnp.int32, (t, t), 1)
    causal = ((r >> 6) == (c >> 6)) & (r >= c)
    tril = jnp.where(causal, 1.0, 0.0).astype(BF16)
    eye = (lax.broadcasted_iota(jnp.int32, (dk, dk), 0) == lax.broadcasted_iota(jnp.int32, (dk, dk), 1))

    for g0 in range(0, n_chunks, GLA_GROUP):
        grp = slice(g0 * ch, (g0 + GLA_GROUP) * ch)
        b = _tril_cumsum(tril, ga_ref[grp, :])
        lasts = [b[(ci + 1) * ch - 1:(ci + 1) * ch, :] for ci in range(GLA_GROUP)]
        b_last = jnp.concatenate([jnp.broadcast_to(x, (ch, GLA_KEY_WIDTH)) for x in lasts], axis=0)
        kk = k_ref[grp, :]
        qe = (q_ref[grp, :] * jnp.exp(b)).astype(BF16)
        ke = (kk * jnp.exp(-b)).astype(BF16)
        kd = (kk * jnp.exp(b_last - b)).astype(BF16)
        decs = [jnp.exp(x) for x in lasts]

        for h in range(GLA_HEADS):
            ks = slice(h * dk, (h + 1) * dk)
            vs = slice(h * dv, (h + 1) * dv)
            vh = v_ref[grp, vs]
            a = jnp.where(causal, _dot_nt(qe[:, ks], ke[:, ks]), 0.0)
            o_intra = _dot(a.astype(BF16), vh)
            s = s_ref[0, h]
            for ci in range(GLA_GROUP):
                rows = slice(ci * ch, (ci + 1) * ch)
                out_rows = slice((g0 + ci) * ch, (g0 + ci + 1) * ch)
                if not chained:
                    s = s_ref[g0 + ci, h]
                o_ref[out_rows, vs] = o_intra[rows] + _dot(qe[rows, ks], s.astype(BF16))
                dec_col = jnp.sum(jnp.where(eye, jnp.broadcast_to(decs[ci][:, ks], (dk, dk)), 0.0),
                                  axis=1, keepdims=True)
                s = dec_col * s + _dot_tn(kd[rows, ks], vh[rows])
                if not chained:
                    s_ref[g0 + ci, h] = s
            if chained:
                s_ref[0, h] = s


def _gla(q, k, v, ga, s0, *, tc):
    B, L, _ = q.shape
    n_chunks = tc // GLA_CHUNK
    chained = L > GLA_CHUNK
    if chained:
        n_states = 1
    else:
        n_states = n_chunks
        fold = lambda a: a.reshape(B // n_chunks, tc, a.shape[-1])
        q, k, v, ga = fold(q), fold(k), fold(v), fold(ga)
    G, T, _ = q.shape
    tok = lambda w: pl.BlockSpec((None, tc, w), lambda b, i: (b, i, 0))
    st = pl.BlockSpec((n_states, GLA_HEADS, GLA_HEAD_K, GLA_HEAD_V), lambda b, i: (b, 0, 0, 0))
    o, s = pl.pallas_call(
        functools.partial(_gla_kernel, n_chunks=n_chunks, chained=chained),
        grid=(G, T // tc),
        in_specs=[tok(GLA_KEY_WIDTH), tok(GLA_KEY_WIDTH), tok(GLA_VAL_WIDTH), tok(GLA_KEY_WIDTH), st],
        out_specs=[tok(GLA_VAL_WIDTH), st],
        out_shape=[
            jax.ShapeDtypeStruct((G, T, GLA_VAL_WIDTH), F32),
            jax.ShapeDtypeStruct((B, GLA_HEADS, GLA_HEAD_K, GLA_HEAD_V), F32),
        ],
        compiler_params=_params(2),
        name="gla_chunk",
    )(q, k, v, ga, s0)
    return o.reshape(B, L, GLA_VAL_WIDTH), s


def _gla_out_kernel(o_ref, sg_ref, y_ref, go_ref, wo_ref, gf_ref, out_ref, *, chunk_major):
    dv = GLA_HEAD_V
    if chunk_major:
        o = jnp.concatenate([o_ref[:, t, :] for t in range(o_ref.shape[1])], axis=0)
    else:
        o = o_ref[...]
    parts = []
    for h in range(GLA_HEADS):
        vs = slice(h * dv, (h + 1) * dv)
        parts.append((_rms(o[:, vs], go_ref[...]) * sg_ref[:, vs]).astype(BF16))
    u = jnp.concatenate(parts, axis=1)
    y = y_ref[...] + _dot(u, wo_ref[...])
    out_ref[...] = _rms(y, gf_ref[...])


def _gla_out(o, sg, y, g_o, w_out, g_f):
    B, L, _ = o.shape
    n = L // GLA_CHUNK
    chunk_major = n > 1
    if chunk_major:
        tm = SUBLANES * n
        o = o.reshape(B, n, GLA_CHUNK, D_MODEL)
        o_spec = pl.BlockSpec((None, n, SUBLANES, D_MODEL), lambda b, i: (b, 0, i, 0))
    else:
        tm = PROJ_TM
        flat = lambda a: a.reshape(1, B * L, D_MODEL)
        o, sg, y = flat(o), flat(sg), flat(y)
        o_spec = pl.BlockSpec((None, tm, D_MODEL), lambda b, i: (b, i, 0))
    nb, rows, _ = sg.shape
    tok = pl.BlockSpec((None, tm, D_MODEL), lambda b, i: (b, i, 0))
    full = lambda a: pl.BlockSpec(a.shape, lambda b, i: (0,) * a.ndim)
    return pl.pallas_call(
        functools.partial(_gla_out_kernel, chunk_major=chunk_major),
        grid=(nb, rows // tm),
        in_specs=[o_spec, tok, tok, full(g_o), full(w_out), full(g_f)],
        out_specs=tok,
        out_shape=jax.ShapeDtypeStruct((nb, rows, D_MODEL), F32),
        compiler_params=_params(2),
        name="gla_out",
    )(o, sg, y, g_o, w_out, g_f)


def _pad_cols(w, n):
    return jnp.pad(w, ((0, 0), (0, n - w.shape[1])))


def kernel(x_prompt, x_sample, cache_fox_k, cache_fox_v, cache_fox_logf, state_gla,
           g_norm_fox, w_in_fox, b_fox_f, w_out_fox,
           g_norm_gla, w_in_gla, w_gla_a2, b_gla_a, g_gla_o, w_out_gla, g_final):
    B, L, D = x_prompt.shape
    Bs, Ls, _ = x_sample.shape
    P = cache_fox_k.shape[1]
    H, hd = FOX_HEADS, FOX_HEAD_DIM

    assert D == D_MODEL and L % (2 * GLA_TC) == 0 and L % ATTN_BLOCK == 0
    assert Ls == GLA_CHUNK and (Bs * Ls) % PROJ_TM == 0 and Bs % (GLA_TC // GLA_CHUNK) == 0
    assert (P + LANES) % 3 == 0 and P % LANES == 0 and cache_fox_k.shape[2:] == (H, hd)

    row = lambda a: a.reshape(1, -1).astype(F32)
    pad_rows = lambda w: jnp.pad(w, ((0, LANES - w.shape[0]), (0, 0)))
    w_fox_t = w_in_fox.T
    w_fox = w_fox_t[:4 * FOX_WIDTH].astype(BF16)
    w_fox_f = pad_rows(w_fox_t[4 * FOX_WIDTH:]).astype(BF16)
    b_f = _pad_cols(row(b_fox_f), LANES)
    w_o_fox = w_out_fox.astype(BF16)
    n_main = 2 * GLA_KEY_WIDTH + 2 * GLA_VAL_WIDTH
    w_gla_t = w_in_gla.T
    w_gla = w_gla_t[:n_main].astype(BF16)
    w_a1 = pad_rows(w_gla_t[n_main:]).astype(BF16)
    w_a2 = pad_rows(w_gla_a2).astype(BF16)
    w_o_gla = w_out_gla.astype(BF16)

    q_p, k_p, v_p, kb_p, sg_p, lft_p, vt_p, nrm_p = _fox_proj(
        x_prompt, row(g_norm_fox), w_fox, w_fox_f, b_f, tm=ATTN_BLOCK, want_vt=True)
    e_p, f_p, cmin_p, cmax_p = _bias_tiles(lft_p, rows=ATTN_BLOCK)
    first_p = _skip_plan(nrm_p[:, :, 0, :H], cmin_p, nrm_p[:, :, 1, :H], cmax_p)
    o_p = _attn_prompt(first_p, q_p, f_p, kb_p, e_p, vt_p, blk=ATTN_BLOCK)
    lf_p = jnp.swapaxes(lft_p, 1, 2)

    xs = x_sample.reshape(1, Bs * Ls, D)
    q_s, k_s, v_s, kb_s, sg_s, lft_s = _fox_proj(
        xs, row(g_norm_fox), w_fox, w_fox_f, b_f, tm=PROJ_TM, want_vt=False)
    sh = lambda a: a.reshape(Bs, Ls, a.shape[-1])
    q_s, k_s, v_s, kb_s = sh(q_s), sh(k_s), sh(v_s), sh(kb_s)
    lft_s = jnp.swapaxes(lft_s.reshape(H, Bs, Ls), 0, 1)
    lf_s = jnp.swapaxes(lft_s, 1, 2)
    lft_all = jnp.concatenate(
        [jnp.swapaxes(cache_fox_logf.astype(F32), 1, 2), lft_s, jnp.zeros((Bs, H, LANES - Ls), F32)], axis=2)
    ct_s = _cumsum_short(lft_all, cols=(P + LANES) // 3, per_step=4)
    cache_t = lambda a: jnp.transpose(a, (0, 2, 3, 1)).reshape(Bs, FOX_WIDTH, P)
    o_s = _attn_sample(q_s, kb_s, v_s, cache_t(cache_fox_k), cache_t(cache_fox_v),
                       jnp.swapaxes(ct_s[:, :, P:P + Ls], 1, 2), ct_s, past=P)

    def gla_layer(o, sg, x, s0, tc):
        b, l, _ = x.shape
        flat = lambda a: a.reshape(b * l, a.shape[-1])
        y1, q, k, v, sg2, ga = _mid(flat(o), flat(sg), flat(x), w_o_fox, row(g_norm_gla), w_gla,
                                    w_a1, w_a2, row(b_gla_a), tm=PROJ_TM)
        un = lambda a: a.reshape(b, l, a.shape[-1])
        og, s_out = _gla(un(q), un(k), un(v), un(ga), s0, tc=tc)
        y = _gla_out(og, un(sg2), un(y1), row(g_gla_o), w_o_gla, row(g_final))
        return y.reshape(b, l, D), s_out

    s0_p = jnp.zeros((B, GLA_HEADS, GLA_HEAD_K, GLA_HEAD_V), F32)
    y_p, s_p = gla_layer(o_p, sg_p, x_prompt, s0_p, GLA_TC)
    y_s, s_s = gla_layer(o_s, sg_s.reshape(Bs, Ls, D), x_sample, state_gla.astype(F32), GLA_TC)

    return (y_p, y_s,
            k_p.reshape(B, L, H, hd), v_p.reshape(B, L, H, hd), lf_p, s_p.astype(state_gla.dtype),
            k_s.reshape(Bs, Ls, H, hd), v_s.reshape(Bs, Ls, H, hd), lf_s, s_s.astype(state_gla.dtype))
```

```python
import functools

import jax
import jax.numpy as jnp
from jax import lax
from jax.experimental import pallas as pl
from jax.experimental.pallas import tpu as pltpu

F32 = jnp.float32
BF16 = jnp.bfloat16

D_MODEL = 1024
EPS = 1e-6
NEG_INF = -1e30
LOG2E = 1.4426950408889634
NORM_SLACK = 1.01
SKIP_LOG2 = 152.0

FOX_HEADS = 16
FOX_HEAD_DIM = 64
FOX_WIDTH = FOX_HEADS * FOX_HEAD_DIM
FOX_PAIRS = FOX_HEADS // 2

GLA_HEADS = 4
GLA_KEY_WIDTH = 512
GLA_VAL_WIDTH = 1024
GLA_HEAD_K = 128
GLA_HEAD_V = 256
GLA_GATE_RANK = 16
GLA_GATE_TEMP = 16.0
GLA_CHUNK = 64

LANES = 128
SUBLANES = 8
VMEM_LIMIT = 56 * 1024 * 1024

PROJ_TM = 512
ATTN_BLOCK = 512
GLA_GROUP = 4
GLA_TC = 512


def _params(n_axes):
    return pltpu.CompilerParams(
        dimension_semantics=("arbitrary",) * n_axes,
        vmem_limit_bytes=VMEM_LIMIT,
    )


def _dot(a, b):
    return jnp.dot(a, b, preferred_element_type=F32)


def _dot_nt(a, b):
    return lax.dot_general(a, b, (((1,), (1,)), ((), ())), preferred_element_type=F32)


def _dot_tn(a, b):
    return lax.dot_general(a, b, (((0,), (0,)), ((), ())), preferred_element_type=F32)


def _log_sigmoid(z):
    return jnp.minimum(z, 0.0) - jnp.log1p(jnp.exp(-jnp.abs(z)))


def _rms(x, g):
    ms = jnp.mean(x * x, axis=-1, keepdims=True)
    return (x * lax.rsqrt(ms + EPS)) * g


def _split3(x):
    hi = x.astype(BF16)
    r1 = x - hi.astype(F32)
    mid = r1.astype(BF16)
    lo = (r1 - mid.astype(F32)).astype(BF16)
    return hi, mid, lo


def _tril_cumsum(tril_bf16, x):
    hi, mid, lo = _split3(x)
    return (_dot(tril_bf16, hi) + _dot(tril_bf16, mid)) + _dot(tril_bf16, lo)


def _fox_proj_kernel(x_ref, g_ref, w_ref, wf_ref, bf_ref, *rest, want_vt):
    if want_vt:
        q_ref, k_ref, v_ref, kb_ref, sg_ref, lft_ref, vt_ref, nrm_ref = rest
    else:
        q_ref, k_ref, v_ref, kb_ref, sg_ref, lft_ref = rest
    fw = FOX_WIDTH
    hb = _rms(x_ref[...], g_ref[...]).astype(BF16)
    q = _dot_nt(hb, w_ref[0:fw, :])
    qb = (q * (LOG2E * FOX_HEAD_DIM ** -0.5)).astype(BF16)
    q_ref[...] = qb
    k = _dot_nt(hb, w_ref[fw:2 * fw, :])
    k_ref[...] = k
    kb = k.astype(BF16)
    kb_ref[...] = kb
    if want_vt:
        row = lax.broadcasted_iota(jnp.int32, (FOX_WIDTH, LANES), 0)
        col = lax.broadcasted_iota(jnp.int32, (FOX_WIDTH, LANES), 1)
        head_of = jnp.where((row >> 6) == col, 1.0, 0.0).astype(BF16)

        def bound(xb):
            xf = xb.astype(F32)
            sq = _dot((xf * xf).astype(BF16), head_of)
            return jnp.sqrt(jnp.max(sq, axis=0, keepdims=True)) * NORM_SLACK

        nrm_ref[...] = jnp.concatenate(
            [bound(kb), bound(qb), jnp.zeros((SUBLANES - 2, LANES), F32)], axis=0)
    v = _dot_nt(hb, w_ref[2 * fw:3 * fw, :])
    v_ref[...] = v
    if want_vt:
        vt_ref[...] = v.astype(BF16).T
    gate = _dot_nt(hb, w_ref[3 * fw:4 * fw, :])
    sg_ref[...] = gate * jax.nn.sigmoid(gate)
    fl = _dot_nt(hb, wf_ref[...]) + bf_ref[...]
    lft_ref[...] = _log_sigmoid(fl).T[:FOX_HEADS, :]


def _fox_proj(x, g, w_main, w_f, b_f, *, tm, want_vt):
    B, L, _ = x.shape
    n = L // tm
    tok = lambda w: pl.BlockSpec((None, tm, w), lambda b, i: (b, i, 0))
    full = lambda a: pl.BlockSpec(a.shape, lambda b, i: (0,) * a.ndim)
    in_specs = [tok(D_MODEL), full(g), full(w_main), full(w_f), full(b_f)]
    args = [x, g, w_main, w_f, b_f]
    out_specs = [tok(FOX_WIDTH)] * 5 + [pl.BlockSpec((None, FOX_HEADS, tm), lambda b, i: (b, 0, i))]
    out_shape = [
        jax.ShapeDtypeStruct((B, L, FOX_WIDTH), BF16),
        jax.ShapeDtypeStruct((B, L, FOX_WIDTH), F32),
        jax.ShapeDtypeStruct((B, L, FOX_WIDTH), F32),
        jax.ShapeDtypeStruct((B, L, FOX_WIDTH), BF16),
        jax.ShapeDtypeStruct((B, L, FOX_WIDTH), F32),
        jax.ShapeDtypeStruct((B, FOX_HEADS, L), F32),
    ]
    if want_vt:
        out_specs.append(pl.BlockSpec((None, None, FOX_WIDTH, tm), lambda b, i: (b, i, 0, 0)))
        out_shape.append(jax.ShapeDtypeStruct((B, n, FOX_WIDTH, tm), BF16))
        out_specs.append(pl.BlockSpec((None, None, SUBLANES, LANES), lambda b, i: (b, i, 0, 0)))
        out_shape.append(jax.ShapeDtypeStruct((B, n, SUBLANES, LANES), F32))
    return pl.pallas_call(
        functools.partial(_fox_proj_kernel, want_vt=want_vt),
        grid=(B, n),
        in_specs=in_specs,
        out_specs=out_specs,
        out_shape=out_shape,
        compiler_params=_params(2),
        name="fox_proj_vt" if want_vt else "fox_proj",
    )(*args)


BIAS_SLOT = 8


def _bias_selectors():
    shape = (LANES, 3 * FOX_HEADS)
    lane = lax.broadcasted_iota(jnp.int32, shape, 0)
    col = lax.broadcasted_iota(jnp.int32, shape, 1)
    slot = BIAS_SLOT * (col & (FOX_HEADS - 1))
    piece = col >> 4
    sel_f = jnp.where(lane == slot + piece, 1.0, 0.0).astype(BF16)
    sel_e = jnp.where(lane == slot + 3 + piece, -1.0, 0.0).astype(BF16)
    in_slot = lax.broadcasted_iota(jnp.int32, (LANES, 1), 0) & (BIAS_SLOT - 1)
    one_f = jnp.where((in_slot >= 3) & (in_slot < 6), 1.0, 0.0)
    one_e = jnp.where(in_slot < 3, 1.0, 0.0)
    return sel_f, sel_e, one_f, one_e


def _stack3(x):
    return jnp.concatenate(_split3(x), axis=0)


def _bias_kernel(lft_ref, e_ref, f_ref, rng_ref, carry_ref, *, rows):
    r = lax.broadcasted_iota(jnp.int32, (rows, rows), 0)
    c = lax.broadcasted_iota(jnp.int32, (rows, rows), 1)
    upper = jnp.where(r <= c, 1.0, 0.0).astype(BF16)
    sel_f, sel_e, one_f, one_e = _bias_selectors()
    nh = FOX_HEADS

    @pl.when(pl.program_id(1) == 0)
    def _():
        carry_ref[...] = jnp.zeros_like(carry_ref)

    carry = carry_ref[:, 0:1]
    for sb in range(rng_ref.shape[0]):
        cols = slice(sb * rows, (sb + 1) * rows)
        st = _dot(_stack3(lft_ref[:, cols]), upper)
        cs = ((st[0:nh] + st[nh:2 * nh]) + st[2 * nh:3 * nh]) + carry
        carry = cs[:, rows - 1:rows]
        c2 = cs * LOG2E
        pieces = _stack3(c2)
        f_ref[cols, :] = (_dot(sel_f, pieces) + one_f).T.astype(BF16)
        e_ref[cols, :] = (_dot(sel_e, pieces) + one_e).T.astype(BF16)
        rng_ref[sb] = jnp.concatenate(
            [jnp.broadcast_to(jnp.min(c2, axis=1, keepdims=True), (nh, LANES)),
             jnp.broadcast_to(jnp.max(c2, axis=1, keepdims=True), (nh, LANES))], axis=0)
    carry_ref[...] = jnp.broadcast_to(carry, carry_ref.shape)


def _bias_tiles(lft, *, rows):
    B, H, L = lft.shape
    per_step = 4
    span = per_step * rows
    wide = pl.BlockSpec((None, span, LANES), lambda b, i: (b, i, 0))
    e, f, rng = pl.pallas_call(
        functools.partial(_bias_kernel, rows=rows),
        grid=(B, L // span),
        in_specs=[pl.BlockSpec((None, H, span), lambda b, i: (b, 0, i))],
        out_specs=[wide, wide, pl.BlockSpec((None, per_step, 2 * H, LANES), lambda b, i: (b, i, 0, 0))],
        out_shape=[jax.ShapeDtypeStruct((B, L, LANES), BF16)] * 2
        + [jax.ShapeDtypeStruct((B, L // rows, 2 * H, LANES), F32)],
        scratch_shapes=[pltpu.VMEM((H, LANES), F32)],
        compiler_params=_params(2),
        name="fox_cumsum_bias",
    )(lft)
    return e, f, rng[:, :, :H, 0], rng[:, :, H:, 0]


def _cumsum_short_kernel(lft_ref, ct_ref, *, cols):
    nbatch, nh, length = lft_ref.shape
    r = lax.broadcasted_iota(jnp.int32, (cols, cols), 0)
    c = lax.broadcasted_iota(jnp.int32, (cols, cols), 1)
    upper = jnp.where(r <= c, 1.0, 0.0).astype(BF16)
    carry = [jnp.zeros((nh, 1), F32)] * nbatch
    for c0 in range(0, length, cols):
        span = slice(c0, c0 + cols)
        st = _dot(jnp.concatenate([_stack3(lft_ref[bi, :, span]) for bi in range(nbatch)], axis=0), upper)
        for bi in range(nbatch):
            base = 3 * nh * bi
            cs = ((st[base:base + nh] + st[base + nh:base + 2 * nh]) + st[base + 2 * nh:base + 3 * nh]) + carry[bi]
            ct_ref[bi, :, span] = cs * LOG2E
            carry[bi] = cs[:, cols - 1:cols]


def _cumsum_short(lft, *, cols, per_step):
    B, H, L = lft.shape
    spec = pl.BlockSpec((per_step, H, L), lambda b: (b, 0, 0))
    return pl.pallas_call(
        functools.partial(_cumsum_short_kernel, cols=cols),
        grid=(B // per_step,),
        in_specs=[spec],
        out_specs=spec,
        out_shape=jax.ShapeDtypeStruct((B, H, L), F32),
        compiler_params=_params(1),
        name="fox_cumsum",
    )(lft)


def _skip_plan_kernel(kn_ref, cmin_ref, qn_ref, cmax_ref, knq_ref, first_ref, *, nb):
    j = lax.broadcasted_iota(jnp.int32, (nb, nb), 0).astype(F32)
    q = lax.broadcasted_iota(jnp.int32, (nb, nb), 1).astype(F32)
    rows = []
    for p in range(FOX_PAIRS):
        skippable = None
        for h in (2 * p, 2 * p + 1):
            qn = qn_ref[h:h + 1, :]
            bound = (kn_ref[:, h:h + 1] * qn + (cmax_ref[h:h + 1, :] - cmin_ref[:, h:h + 1])
                     + qn * knq_ref[h:h + 1, :])
            ok = bound < -SKIP_LOG2
            skippable = ok if skippable is None else (skippable & ok)
        must = (j < q) & jnp.logical_not(skippable)
        rows.append(jnp.min(jnp.where(must, j, q), axis=0, keepdims=True))
    first_ref[...] = jnp.concatenate(rows, axis=0).astype(jnp.int32)


def _skip_plan(kn, cmin, qn, cmax):
    B, nb, H = kn.shape
    by_block = pl.BlockSpec((None, nb, H), lambda b: (b, 0, 0))
    by_head = pl.BlockSpec((None, H, nb), lambda b: (b, 0, 0))
    t = lambda a: jnp.swapaxes(a, 1, 2)
    return pl.pallas_call(
        functools.partial(_skip_plan_kernel, nb=nb),
        grid=(B,),
        in_specs=[by_block, by_block, by_head, by_head, by_head],
        out_specs=pl.BlockSpec((None, FOX_PAIRS, nb), lambda b: (b, 0, 0)),
        out_shape=jax.ShapeDtypeStruct((B, FOX_PAIRS, nb), jnp.int32),
        compiler_params=_params(1),
        name="fox_skip_plan",
    )(kn, cmin, t(qn), t(cmax), t(kn))


def _stack_heads(q2):
    lane = lax.broadcasted_iota(jnp.int32, q2.shape, 1)
    zero = jnp.zeros_like(q2)
    return jnp.concatenate(
        [jnp.where(lane < FOX_HEAD_DIM, q2, zero), jnp.where(lane >= FOX_HEAD_DIM, q2, zero)], axis=0)


ATTN_TILE = 256
SUM_ROWS = 16


def _attn_prompt_kernel(n_ref, vq_ref, vj_ref, q_ref, f_ref, k_ref, e_ref, vt_ref, o_ref,
                        m_ref, acc_ref, sa_ref, xa_ref, sb_ref, xb_ref, *, blk, nb):
    b_id = pl.program_id(0)
    p_id = pl.program_id(1)
    hd = FOX_HEAD_DIM
    tw = ATTN_TILE
    per_head = blk // tw
    n_tiles = 2 * per_head

    lane = lax.broadcasted_iota(jnp.int32, (blk, LANES), 1)
    zero = jnp.zeros((blk, LANES), BF16)
    slot_a = 2 * BIAS_SLOT * p_id
    slot_b = slot_a + BIAS_SLOT
    ones_rows = jnp.ones((SUM_ROWS, blk), BF16)

    def query_tiles(qi):
        r0 = pl.multiple_of(qi * blk, blk)
        q2 = q_ref[pl.ds(r0, blk), :]
        f2 = f_ref[pl.ds(r0, blk), :]
        heads = [
            jnp.concatenate([jnp.where(lane < hd, q2, zero),
                             jnp.where((lane >= slot_a) & (lane < slot_b), f2, zero)], axis=1),
            jnp.concatenate([jnp.where(lane >= hd, q2, zero),
                             jnp.where((lane >= slot_b) & (lane < slot_b + BIAS_SLOT), f2, zero)], axis=1),
        ]
        return [heads[t // per_head][(t % per_head) * tw:(t % per_head + 1) * tw] for t in range(n_tiles)]

    def stage(nxt, cur):
        if nxt is not None:
            qi_n, j_n, (sn_ref, xn_ref), diag_n = nxt
            q_tiles = query_tiles(qi_n)
            k0 = pl.multiple_of(j_n * blk, blk)
            kaug = jnp.concatenate([k_ref[pl.ds(k0, blk), :], e_ref[pl.ds(k0, blk), :]], axis=1)
        if cur is not None:
            qi_c, j_c, (sc_ref, xc_ref), diag_c = cur
            vt = vt_ref[j_c]
            v_heads = [jnp.concatenate([vt[h * hd:(h + 1) * hd, :], ones_rows], axis=0) for h in range(2)]
        for t in range(n_tiles):
            c0 = (t % per_head) * tw
            if nxt is not None:
                live = c0 + tw if diag_n else blk
                s = _dot_nt(kaug[:live], q_tiles[t])
                if diag_n:
                    r = lax.broadcasted_iota(jnp.int32, (live, tw), 0)
                    c = lax.broadcasted_iota(jnp.int32, (live, tw), 1) + c0
                    s = jnp.where(r <= c, s, NEG_INF)
                sn_ref[t, 0:live, :] = s
                xn_ref[t] = jnp.max(s, axis=0, keepdims=True)
            if cur is not None:
                live = c0 + tw if diag_c else blk
                idx = qi_c * n_tiles + t
                if diag_c:
                    m_new = xc_ref[t]
                else:
                    m_old = m_ref[idx]
                    m_new = jnp.maximum(m_old, xc_ref[t])
                    alpha = jnp.exp2(m_old - m_new)
                p = jnp.exp2(sc_ref[t, 0:live, :] - m_new)
                m_ref[idx] = m_new
                pv = _dot(v_heads[t // per_head][:, 0:live], p.astype(BF16))
                acc_ref[idx] = pv if diag_c else acc_ref[idx] * alpha + pv

    buf_a = (sa_ref, xa_ref)
    buf_b = (sb_ref, xb_ref)

    diag = lambda n, buf: (n, n, buf, True)
    n_off = n_ref[b_id, p_id]
    off = lambda v, buf: (vq_ref[b_id, p_id, v], vj_ref[b_id, p_id, v], buf, False)

    stage(diag(0, buf_a), None)

    def diag_pair(n, carry):
        i0 = 2 * n
        stage(diag(i0 + 1, buf_b), diag(i0, buf_a))
        stage(diag(i0 + 2, buf_a), diag(i0 + 1, buf_b))
        return carry

    lax.fori_loop(0, nb // 2 - 1, diag_pair, 0)
    stage(diag(nb - 1, buf_b), diag(nb - 2, buf_a))

    @pl.when(n_off > 0)
    def _():
        stage(off(0, buf_a), diag(nb - 1, buf_b))

    @pl.when(n_off == 0)
    def _():
        stage(None, diag(nb - 1, buf_b))

    def off_pair(n, carry):
        v0 = 2 * n
        stage(off(v0 + 1, buf_b), off(v0, buf_a))
        stage(off(v0 + 2, buf_a), off(v0 + 1, buf_b))
        return carry

    lax.fori_loop(0, jnp.maximum(n_off - 1, 0) // 2, off_pair, 0)

    @pl.when((n_off > 0) & (n_off % 2 == 0))
    def _():
        stage(off(n_off - 1, buf_b), off(n_off - 2, buf_a))
        stage(None, off(n_off - 1, buf_b))

    @pl.when(n_off % 2 == 1)
    def _():
        stage(None, off(n_off - 1, buf_a))

    def finish(qi, carry):
        halves = [jnp.concatenate([acc_ref[qi * n_tiles + t, 0:hd, :] / acc_ref[qi * n_tiles + t, hd:hd + 1, :]
                                   for t in range(h * per_head, (h + 1) * per_head)], axis=1)
                  for h in range(2)]
        o_ref[pl.ds(pl.multiple_of(qi * blk, blk), blk), :] = jnp.concatenate(halves, axis=0).T
        return carry

    lax.fori_loop(0, nb, finish, 0)


def _visit_lists(first):
    B, P, nb = first.shape
    vmax = nb * (nb - 1) // 2 + SUBLANES
    q_idx = jnp.arange(nb, dtype=jnp.int32)
    cnt = q_idx - first
    incl = jnp.cumsum(cnt, axis=-1)
    excl = incl - cnt
    v = jnp.arange(vmax, dtype=jnp.int32)
    vq = jnp.minimum(jnp.sum(v[None, None, :, None] >= incl[:, :, None, :], axis=-1), nb - 1).astype(jnp.int32)
    pick = vq[..., None] == q_idx
    take = lambda a: jnp.sum(jnp.where(pick, a[:, :, None, :], 0), axis=-1)
    vj = jnp.clip(take(first) + (v - take(excl)), 0, nb - 1).astype(jnp.int32)
    return incl[..., -1].astype(jnp.int32), vq, vj


def _attn_prompt(first, q, f, kb, e, vt, *, blk):
    B, L, _ = q.shape
    nb = L // blk
    n_tiles = 2 * blk // ATTN_TILE
    n_off, vq, vj = _visit_lists(first)
    by_pair = pl.BlockSpec((None, L, LANES), lambda b, p, *_: (b, 0, p))
    shared = pl.BlockSpec((None, L, LANES), lambda b, p, *_: (b, 0, 0))
    return pl.pallas_call(
        functools.partial(_attn_prompt_kernel, blk=blk, nb=nb),
        grid_spec=pltpu.PrefetchScalarGridSpec(
            num_scalar_prefetch=3,
            grid=(B, FOX_PAIRS),
            in_specs=[by_pair, shared, by_pair, shared,
                      pl.BlockSpec((None, nb, LANES, blk), lambda b, p, *_: (b, 0, p, 0))],
            out_specs=by_pair,
            scratch_shapes=[
                pltpu.VMEM((nb * n_tiles, 1, ATTN_TILE), F32),
                pltpu.VMEM((nb * n_tiles, FOX_HEAD_DIM + SUM_ROWS, ATTN_TILE), F32),
                pltpu.VMEM((n_tiles, blk, ATTN_TILE), F32),
                pltpu.VMEM((n_tiles, 1, ATTN_TILE), F32),
                pltpu.VMEM((n_tiles, blk, ATTN_TILE), F32),
                pltpu.VMEM((n_tiles, 1, ATTN_TILE), F32),
            ],
        ),
        out_shape=jax.ShapeDtypeStruct((B, L, FOX_WIDTH), F32),
        compiler_params=_params(2),
        name="fox_attn_prompt",
    )(n_off, vq, vj, q, f, kb, e, vt)


def _attn_sample_kernel(q_ref, kn_ref, vn_ref, ck_ref, cv_ref, cq_ref, ct_ref, o_ref, *, past, ls):
    hd = FOX_HEAD_DIM
    pad_k = jnp.zeros((LANES - ls, LANES), BF16)
    cqb = cq_ref[...]
    lane = lax.broadcasted_iota(jnp.int32, cqb.shape, 1)
    r = lax.broadcasted_iota(jnp.int32, (2 * ls, LANES), 0) & (ls - 1)
    c = lax.broadcasted_iota(jnp.int32, (2 * ls, LANES), 1)
    lane_o = lax.broadcasted_iota(jnp.int32, (ls, LANES), 1)

    scored = []
    for p in range(FOX_PAIRS):
        cols = slice(p * LANES, (p + 1) * LANES)
        qst = _stack_heads(q_ref[:, cols])
        kc_t = ck_ref[cols, :].astype(BF16)
        kn = jnp.concatenate([kn_ref[:, cols], pad_k], axis=0)
        cq = jnp.concatenate(
            [jnp.sum(jnp.where(lane == 2 * p + h2, cqb, 0.0), axis=1, keepdims=True) for h2 in range(2)],
            axis=0)
        cka = ct_ref[2 * p:2 * p + 1, :]
        ckb = ct_ref[2 * p + 1:2 * p + 2, :]

        def bias(lo, hi):
            return jnp.concatenate([cq[:ls] - cka[:, lo:hi], cq[ls:] - ckb[:, lo:hi]], axis=0)

        s_c = _dot(qst, kc_t) + bias(0, past)
        s_n = _dot_nt(qst, kn) + bias(past, past + LANES)
        s_n = jnp.where(c <= r, s_n, NEG_INF)
        m = jnp.maximum(jnp.max(s_c, axis=1, keepdims=True), jnp.max(s_n, axis=1, keepdims=True))
        scored.append((s_c, s_n, m))

    for p in range(FOX_PAIRS):
        cols = slice(p * LANES, (p + 1) * LANES)
        s_c, s_n, m = scored[p]
        vc_t = cv_ref[cols, :].astype(BF16)
        vn = jnp.concatenate([vn_ref[:, cols].astype(BF16), pad_k], axis=0)
        p_c = jnp.exp2(s_c - m)
        p_n = jnp.exp2(s_n - m)
        l = jnp.sum(p_c, axis=1, keepdims=True) + jnp.sum(p_n, axis=1, keepdims=True)
        o = (_dot_nt(p_c.astype(BF16), vc_t) + _dot(p_n.astype(BF16), vn)) / l
        o_ref[:, cols] = jnp.where(lane_o < hd, o[:ls], o[ls:])


def _attn_sample(q, kb, v, cache_k, cache_v, cq, ct, *, past):
    B, ls, _ = q.shape
    tot = ct.shape[2]
    rows = lambda n: pl.BlockSpec((None, n, FOX_WIDTH), lambda b: (b, 0, 0))
    cache_t = pl.BlockSpec((None, FOX_WIDTH, past), lambda b: (b, 0, 0))
    return pl.pallas_call(
        functools.partial(_attn_sample_kernel, past=past, ls=ls),
        grid=(B,),
        in_specs=[
            rows(ls), rows(ls), rows(ls), cache_t, cache_t,
            pl.BlockSpec((None, ls, FOX_HEADS), lambda b: (b, 0, 0)),
            pl.BlockSpec((None, FOX_HEADS, tot), lambda b: (b, 0, 0)),
        ],
        out_specs=rows(ls),
        out_shape=jax.ShapeDtypeStruct((B, ls, FOX_WIDTH), F32),
        compiler_params=_params(1),
        name="fox_attn_sample",
    )(q, kb, v, cache_k, cache_v, cq, ct)


def _mid_kernel(o_ref, sg_ref, x_ref, wo_ref, g_ref, w_ref, wa1_ref, wa2_ref, ba_ref,
                y_ref, q_ref, k_ref, v_ref, sg2_ref, ga_ref):
    kw = GLA_KEY_WIDTH
    tm = o_ref.shape[0]
    halves = []
    for rows in (slice(0, tm // 2), slice(tm // 2, tm)):
        u = (o_ref[rows, :] * sg_ref[rows, :]).astype(BF16)
        y = x_ref[rows, :] + _dot(u, wo_ref[...])
        y_ref[rows, :] = y
        halves.append(_rms(y, g_ref[...]).astype(BF16))
    hb = jnp.concatenate(halves, axis=0)
    q_ref[...] = _dot_nt(hb, w_ref[0:kw, :]) * (GLA_HEAD_K ** -0.5)
    k_ref[...] = _dot_nt(hb, w_ref[kw:2 * kw, :])
    v_ref[...] = _dot_nt(hb, w_ref[2 * kw:2 * kw + GLA_VAL_WIDTH, :]).astype(BF16)
    gate = _dot_nt(hb, w_ref[2 * kw + GLA_VAL_WIDTH:2 * kw + 2 * GLA_VAL_WIDTH, :])
    sg2_ref[...] = gate * jax.nn.sigmoid(gate)
    a1 = _dot_nt(hb, wa1_ref[...]).astype(BF16)
    a = _dot(a1, wa2_ref[...]) + ba_ref[...]
    ga_ref[...] = _log_sigmoid(a) * (1.0 / GLA_GATE_TEMP)


def _mid(o, sg, x, w_out, g, w_main, w_a1, w_a2, b_a, *, tm):
    T = o.shape[0]
    tok = lambda w: pl.BlockSpec((tm, w), lambda i: (i, 0))
    full = lambda a: pl.BlockSpec(a.shape, lambda i: (0,) * a.ndim)
    return pl.pallas_call(
        _mid_kernel,
        grid=(T // tm,),
        in_specs=[tok(D_MODEL), tok(D_MODEL), tok(D_MODEL), full(w_out), full(g), full(w_main),
                  full(w_a1), full(w_a2), full(b_a)],
        out_specs=[tok(D_MODEL), tok(GLA_KEY_WIDTH), tok(GLA_KEY_WIDTH), tok(GLA_VAL_WIDTH),
                   tok(GLA_VAL_WIDTH), tok(GLA_KEY_WIDTH)],
        out_shape=[
            jax.ShapeDtypeStruct((T, D_MODEL), F32),
            jax.ShapeDtypeStruct((T, GLA_KEY_WIDTH), F32),
            jax.ShapeDtypeStruct((T, GLA_KEY_WIDTH), F32),
            jax.ShapeDtypeStruct((T, GLA_VAL_WIDTH), BF16),
            jax.ShapeDtypeStruct((T, GLA_VAL_WIDTH), F32),
            jax.ShapeDtypeStruct((T, GLA_KEY_WIDTH), F32),
        ],
        compiler_params=_params(1),
        name="fox_out_gla_proj",
    )(o, sg, x, w_out, g, w_main, w_a1, w_a2, b_a)


def _gla_kernel(q_ref, k_ref, v_ref, ga_ref, s0_ref, o_ref, s_ref, *, n_chunks, chained):
    i = pl.program_id(1)
    ch = GLA_CHUNK
    dk, dv = GLA_HEAD_K, GLA_HEAD_V

    @pl.when(i == 0)
    def _():
        s_ref[...] = s0_ref[...]

    t = GLA_GROUP * ch
    r = lax.broadcasted_iota(jnp.int32, (t, t), 0)
    c = lax.broadcasted_iota(jnp.int32, (t, t), 1)
    causal = ((r >> 6) == (c >> 6)) & (r >= c)
    tril = jnp.where(causal, 1.0, 0.0).astype(BF16)
    eye = (lax.broadcasted_iota(jnp.int32, (dk, dk), 0) == lax.broadcasted_iota(jnp.int32, (dk, dk), 1))

    for g0 in range(0, n_chunks, GLA_GROUP):
        grp = slice(g0 * ch, (g0 + GLA_GROUP) * ch)
        b = _tril_cumsum(tril, ga_ref[grp, :])
        lasts = [b[(ci + 1) * ch - 1:(ci + 1) * ch, :] for ci in range(GLA_GROUP)]
        b_last = jnp.concatenate([jnp.broadcast_to(x, (ch, GLA_KEY_WIDTH)) for x in lasts], axis=0)
        kk = k_ref[grp, :]
        qe = (q_ref[grp, :] * jnp.exp(b)).astype(BF16)
        ke = (kk * jnp.exp(-b)).astype(BF16)
        kd = (kk * jnp.exp(b_last - b)).astype(BF16)
        decs = [jnp.exp(x) for x in lasts]

        for h in range(GLA_HEADS):
            ks = slice(h * dk, (h + 1) * dk)
            vs = slice(h * dv, (h + 1) * dv)
            vh = v_ref[grp, vs]
            a = jnp.where(causal, _dot_nt(qe[:, ks], ke[:, ks]), 0.0)
            o_intra = _dot(a.astype(BF16), vh)
            s = s_ref[0, h]
            for ci in range(GLA_GROUP):
                rows = slice(ci * ch, (ci + 1) * ch)
                out_rows = slice((g0 + ci) * ch, (g0 + ci + 1) * ch)
                if not chained:
                    s = s_ref[g0 + ci, h]
                o_ref[out_rows, vs] = o_intra[rows] + _dot(qe[rows, ks], s.astype(BF16))
                dec_col = jnp.sum(jnp.where(eye, jnp.broadcast_to(decs[ci][:, ks], (dk, dk)), 0.0),
                                  axis=1, keepdims=True)
                s = dec_col * s + _dot_tn(kd[rows, ks], vh[rows])
                if not chained:
                    s_ref[g0 + ci, h] = s
            if chained:
                s_ref[0, h] = s


def _gla(q, k, v, ga, s0, *, tc):
    B, L, _ = q.shape
    n_chunks = tc // GLA_CHUNK
    chained = L > GLA_CHUNK
    if chained:
        n_states = 1
    else:
        n_states = n_chunks
        fold = lambda a: a.reshape(B // n_chunks, tc, a.shape[-1])
        q, k, v, ga = fold(q), fold(k), fold(v), fold(ga)
    G, T, _ = q.shape
    tok = lambda w: pl.BlockSpec((None, tc, w), lambda b, i: (b, i, 0))
    st = pl.BlockSpec((n_states, GLA_HEADS, GLA_HEAD_K, GLA_HEAD_V), lambda b, i: (b, 0, 0, 0))
    o, s = pl.pallas_call(
        functools.partial(_gla_kernel, n_chunks=n_chunks, chained=chained),
        grid=(G, T // tc),
        in_specs=[tok(GLA_KEY_WIDTH), tok(GLA_KEY_WIDTH), tok(GLA_VAL_WIDTH), tok(GLA_KEY_WIDTH), st],
        out_specs=[tok(GLA_VAL_WIDTH), st],
        out_shape=[
            jax.ShapeDtypeStruct((G, T, GLA_VAL_WIDTH), F32),
            jax.ShapeDtypeStruct((B, GLA_HEADS, GLA_HEAD_K, GLA_HEAD_V), F32),
        ],
        compiler_params=_params(2),
        name="gla_chunk",
    )(q, k, v, ga, s0)
    return o.reshape(B, L, GLA_VAL_WIDTH), s


def _gla_out_kernel(o_ref, sg_ref, y_ref, go_ref, wo_ref, gf_ref, out_ref, *, chunk_major):
    dv = GLA_HEAD_V
    if chunk_major:
        o = jnp.concatenate([o_ref[:, t, :] for t in range(o_ref.shape[1])], axis=0)
    else:
        o = o_ref[...]
    parts = []
    for h in range(GLA_HEADS):
        vs = slice(h * dv, (h + 1) * dv)
        parts.append((_rms(o[:, vs], go_ref[...]) * sg_ref[:, vs]).astype(BF16))
    u = jnp.concatenate(parts, axis=1)
    y = y_ref[...] + _dot(u, wo_ref[...])
    out_ref[...] = _rms(y, gf_ref[...])


def _gla_out(o, sg, y, g_o, w_out, g_f):
    B, L, _ = o.shape
    n = L // GLA_CHUNK
    chunk_major = n > 1
    if chunk_major:
        tm = SUBLANES * n
        o = o.reshape(B, n, GLA_CHUNK, D_MODEL)
        o_spec = pl.BlockSpec((None, n, SUBLANES, D_MODEL), lambda b, i: (b, 0, i, 0))
    else:
        tm = PROJ_TM
        flat = lambda a: a.reshape(1, B * L, D_MODEL)
        o, sg, y = flat(o), flat(sg), flat(y)
        o_spec = pl.BlockSpec((None, tm, D_MODEL), lambda b, i: (b, i, 0))
    nb, rows, _ = sg.shape
    tok = pl.BlockSpec((None, tm, D_MODEL), lambda b, i: (b, i, 0))
    full = lambda a: pl.BlockSpec(a.shape, lambda b, i: (0,) * a.ndim)
    return pl.pallas_call(
        functools.partial(_gla_out_kernel, chunk_major=chunk_major),
        grid=(nb, rows // tm),
        in_specs=[o_spec, tok, tok, full(g_o), full(w_out), full(g_f)],
        out_specs=tok,
        out_shape=jax.ShapeDtypeStruct((nb, rows, D_MODEL), F32),
        compiler_params=_params(2),
        name="gla_out",
    )(o, sg, y, g_o, w_out, g_f)


def _pad_cols(w, n):
    return jnp.pad(w, ((0, 0), (0, n - w.shape[1])))


def kernel(x_prompt, x_sample, cache_fox_k, cache_fox_v, cache_fox_logf, state_gla,
           g_norm_fox, w_in_fox, b_fox_f, w_out_fox,
           g_norm_gla, w_in_gla, w_gla_a2, b_gla_a, g_gla_o, w_out_gla, g_final):
    B, L, D = x_prompt.shape
    Bs, Ls, _ = x_sample.shape
    P = cache_fox_k.shape[1]
    H, hd = FOX_HEADS, FOX_HEAD_DIM

    assert D == D_MODEL and L % (2 * GLA_TC) == 0 and L % ATTN_BLOCK == 0
    assert Ls == GLA_CHUNK and (Bs * Ls) % PROJ_TM == 0 and Bs % (GLA_TC // GLA_CHUNK) == 0
    assert (P + LANES) % 3 == 0 and P % LANES == 0 and cache_fox_k.shape[2:] == (H, hd)

    row = lambda a: a.reshape(1, -1).astype(F32)
    pad_rows = lambda w: jnp.pad(w, ((0, LANES - w.shape[0]), (0, 0)))
    w_fox_t = w_in_fox.T
    w_fox = w_fox_t[:4 * FOX_WIDTH].astype(BF16)
    w_fox_f = pad_rows(w_fox_t[4 * FOX_WIDTH:]).astype(BF16)
    b_f = _pad_cols(row(b_fox_f), LANES)
    w_o_fox = w_out_fox.astype(BF16)
    n_main = 2 * GLA_KEY_WIDTH + 2 * GLA_VAL_WIDTH
    w_gla_t = w_in_gla.T
    w_gla = w_gla_t[:n_main].astype(BF16)
    w_a1 = pad_rows(w_gla_t[n_main:]).astype(BF16)
    w_a2 = pad_rows(w_gla_a2).astype(BF16)
    w_o_gla = w_out_gla.astype(BF16)

    q_p, k_p, v_p, kb_p, sg_p, lft_p, vt_p, nrm_p = _fox_proj(
        x_prompt, row(g_norm_fox), w_fox, w_fox_f, b_f, tm=ATTN_BLOCK, want_vt=True)
    e_p, f_p, cmin_p, cmax_p = _bias_tiles(lft_p, rows=ATTN_BLOCK)
    first_p = _skip_plan(nrm_p[:, :, 0, :H], cmin_p, nrm_p[:, :, 1, :H], cmax_p)
    o_p = _attn_prompt(first_p, q_p, f_p, kb_p, e_p, vt_p, blk=ATTN_BLOCK)
    lf_p = jnp.swapaxes(lft_p, 1, 2)

    xs = x_sample.reshape(1, Bs * Ls, D)
    q_s, k_s, v_s, kb_s, sg_s, lft_s = _fox_proj(
        xs, row(g_norm_fox), w_fox, w_fox_f, b_f, tm=PROJ_TM, want_vt=False)
    sh = lambda a: a.reshape(Bs, Ls, a.shape[-1])
    q_s, k_s, v_s, kb_s = sh(q_s), sh(k_s), sh(v_s), sh(kb_s)
    lft_s = jnp.swapaxes(lft_s.reshape(H, Bs, Ls), 0, 1)
    lf_s = jnp.swapaxes(lft_s, 1, 2)
    lft_all = jnp.concatenate(
        [jnp.swapaxes(cache_fox_logf.astype(F32), 1, 2), lft_s, jnp.zeros((Bs, H, LANES - Ls), F32)], axis=2)
    ct_s = _cumsum_short(lft_all, cols=(P + LANES) // 3, per_step=4)
    cache_t = lambda a: jnp.transpose(a, (0, 2, 3, 1)).reshape(Bs, FOX_WIDTH, P)
    o_s = _attn_sample(q_s, kb_s, v_s, cache_t(cache_fox_k), cache_t(cache_fox_v),
                       jnp.swapaxes(ct_s[:, :, P:P + Ls], 1, 2), ct_s, past=P)

    def gla_layer(o, sg, x, s0, tc):
        b, l, _ = x.shape
        flat = lambda a: a.reshape(b * l, a.shape[-1])
        y1, q, k, v, sg2, ga = _mid(flat(o), flat(sg), flat(x), w_o_fox, row(g_norm_gla), w_gla,
                                    w_a1, w_a2, row(b_gla_a), tm=PROJ_TM)
        un = lambda a: a.reshape(b, l, a.shape[-1])
        og, s_out = _gla(un(q), un(k), un(v), un(ga), s0, tc=tc)
        y = _gla_out(og, un(sg2), un(y1), row(g_gla_o), w_o_gla, row(g_final))
        return y.reshape(b, l, D), s_out

    s0_p = jnp.zeros((B, GLA_HEADS, GLA_HEAD_K, GLA_HEAD_V), F32)
    y_p, s_p = gla_layer(o_p, sg_p, x_prompt, s0_p, GLA_TC)
    y_s, s_s = gla_layer(o_s, sg_s.reshape(Bs, Ls, D), x_sample, state_gla.astype(F32), GLA_TC)

    return (y_p, y_s,
            k_p.reshape(B, L, H, hd), v_p.reshape(B, L, H, hd), lf_p, s_p.astype(state_gla.dtype),
            k_s.reshape(Bs, Ls, H, hd), v_s.reshape(Bs, Ls, H, hd), lf_s, s_s.astype(state_gla.dtype))
```

```python
import functools

import jax
import jax.numpy as jnp
from jax import lax
from jax.experimental import pallas as pl
from jax.experimental.pallas import tpu as pltpu

F32 = jnp.float32
BF16 = jnp.bfloat16

D_MODEL = 1024
EPS = 1e-6
NEG_INF = -1e30
LOG2E = 1.4426950408889634
NORM_SLACK = 1.01
SKIP_LOG2 = 152.0

FOX_HEADS = 16
FOX_HEAD_DIM = 64
FOX_WIDTH = FOX_HEADS * FOX_HEAD_DIM
FOX_PAIRS = FOX_HEADS // 2

GLA_HEADS = 4
GLA_KEY_WIDTH = 512
GLA_VAL_WIDTH = 1024
GLA_HEAD_K = 128
GLA_HEAD_V = 256
GLA_GATE_RANK = 16
GLA_GATE_TEMP = 16.0
GLA_CHUNK = 64

LANES = 128
SUBLANES = 8
VMEM_LIMIT = 56 * 1024 * 1024

PROJ_TM = 512
ATTN_BLOCK = 512
GLA_GROUP = 4
GLA_TC = 512


def _params(n_axes):
    return pltpu.CompilerParams(
        dimension_semantics=("arbitrary",) * n_axes,
        vmem_limit_bytes=VMEM_LIMIT,
    )


def _dot(a, b):
    return jnp.dot(a, b, preferred_element_type=F32)


def _dot_nt(a, b):
    return lax.dot_general(a, b, (((1,), (1,)), ((), ())), preferred_element_type=F32)


def _dot_tn(a, b):
    return lax.dot_general(a, b, (((0,), (0,)), ((), ())), preferred_element_type=F32)


def _log_sigmoid(z):
    return jnp.minimum(z, 0.0) - jnp.log1p(jnp.exp(-jnp.abs(z)))


def _rms(x, g):
    ms = jnp.mean(x * x, axis=-1, keepdims=True)
    return (x * lax.rsqrt(ms + EPS)) * g


def _split3(x):
    hi = x.astype(BF16)
    r1 = x - hi.astype(F32)
    mid = r1.astype(BF16)
    lo = (r1 - mid.astype(F32)).astype(BF16)
    return hi, mid, lo


def _tril_cumsum(tril_bf16, x):
    hi, mid, lo = _split3(x)
    return (_dot(tril_bf16, hi) + _dot(tril_bf16, mid)) + _dot(tril_bf16, lo)


def _fox_proj_kernel(x_ref, g_ref, w_ref, wf_ref, bf_ref, *rest, want_vt):
    if want_vt:
        q_ref, k_ref, v_ref, kb_ref, sg_ref, lft_ref, vt_ref, nrm_ref = rest
    else:
        q_ref, k_ref, v_ref, kb_ref, sg_ref, lft_ref = rest
    fw = FOX_WIDTH
    hb = _rms(x_ref[...], g_ref[...]).astype(BF16)
    q = _dot_nt(hb, w_ref[0:fw, :])
    qb = (q * (LOG2E * FOX_HEAD_DIM ** -0.5)).astype(BF16)
    q_ref[...] = qb
    k = _dot_nt(hb, w_ref[fw:2 * fw, :])
    k_ref[...] = k
    kb = k.astype(BF16)
    kb_ref[...] = kb
    if want_vt:
        row = lax.broadcasted_iota(jnp.int32, (FOX_WIDTH, LANES), 0)
        col = lax.broadcasted_iota(jnp.int32, (FOX_WIDTH, LANES), 1)
        head_of = jnp.where((row >> 6) == col, 1.0, 0.0).astype(BF16)

        def bound(xb):
            xf = xb.astype(F32)
            sq = _dot((xf * xf).astype(BF16), head_of)
            return jnp.sqrt(jnp.max(sq, axis=0, keepdims=True)) * NORM_SLACK

        nrm_ref[...] = jnp.concatenate(
            [bound(kb), bound(qb), jnp.zeros((SUBLANES - 2, LANES), F32)], axis=0)
    v = _dot_nt(hb, w_ref[2 * fw:3 * fw, :])
    v_ref[...] = v
    if want_vt:
        vt_ref[...] = v.astype(BF16).T
    gate = _dot_nt(hb, w_ref[3 * fw:4 * fw, :])
    sg_ref[...] = gate * jax.nn.sigmoid(gate)
    fl = _dot_nt(hb, wf_ref[...]) + bf_ref[...]
    lft_ref[...] = _log_sigmoid(fl).T[:FOX_HEADS, :]


def _fox_proj(x, g, w_main, w_f, b_f, *, tm, want_vt):
    B, L, _ = x.shape
    n = L // tm
    tok = lambda w: pl.BlockSpec((None, tm, w), lambda b, i: (b, i, 0))
    full = lambda a: pl.BlockSpec(a.shape, lambda b, i: (0,) * a.ndim)
    in_specs = [tok(D_MODEL), full(g), full(w_main), full(w_f), full(b_f)]
    args = [x, g, w_main, w_f, b_f]
    out_specs = [tok(FOX_WIDTH)] * 5 + [pl.BlockSpec((None, FOX_HEADS, tm), lambda b, i: (b, 0, i))]
    out_shape = [
        jax.ShapeDtypeStruct((B, L, FOX_WIDTH), BF16),
        jax.ShapeDtypeStruct((B, L, FOX_WIDTH), F32),
        jax.ShapeDtypeStruct((B, L, FOX_WIDTH), F32),
        jax.ShapeDtypeStruct((B, L, FOX_WIDTH), BF16),
        jax.ShapeDtypeStruct((B, L, FOX_WIDTH), F32),
        jax.ShapeDtypeStruct((B, FOX_HEADS, L), F32),
    ]
    if want_vt:
        out_specs.append(pl.BlockSpec((None, None, FOX_WIDTH, tm), lambda b, i: (b, i, 0, 0)))
        out_shape.append(jax.ShapeDtypeStruct((B, n, FOX_WIDTH, tm), BF16))
        out_specs.append(pl.BlockSpec((None, None, SUBLANES, LANES), lambda b, i: (b, i, 0, 0)))
        out_shape.append(jax.ShapeDtypeStruct((B, n, SUBLANES, LANES), F32))
    return pl.pallas_call(
        functools.partial(_fox_proj_kernel, want_vt=want_vt),
        grid=(B, n),
        in_specs=in_specs,
        out_specs=out_specs,
        out_shape=out_shape,
        compiler_params=_params(2),
        name="fox_proj_vt" if want_vt else "fox_proj",
    )(*args)


BIAS_SLOT = 8


def _bias_selectors():
    shape = (LANES, 3 * FOX_HEADS)
    lane = lax.broadcasted_iota(jnp.int32, shape, 0)
    col = lax.broadcasted_iota(jnp.int32, shape, 1)
    slot = BIAS_SLOT * (col & (FOX_HEADS - 1))
    piece = col >> 4
    sel_f = jnp.where(lane == slot + piece, 1.0, 0.0).astype(BF16)
    sel_e = jnp.where(lane == slot + 3 + piece, -1.0, 0.0).astype(BF16)
    in_slot = lax.broadcasted_iota(jnp.int32, (LANES, 1), 0) & (BIAS_SLOT - 1)
    one_f = jnp.where((in_slot >= 3) & (in_slot < 6), 1.0, 0.0)
    one_e = jnp.where(in_slot < 3, 1.0, 0.0)
    return sel_f, sel_e, one_f, one_e


def _stack3(x):
    return jnp.concatenate(_split3(x), axis=0)


def _bias_kernel(lft_ref, e_ref, f_ref, rng_ref, carry_ref, *, rows):
    r = lax.broadcasted_iota(jnp.int32, (rows, rows), 0)
    c = lax.broadcasted_iota(jnp.int32, (rows, rows), 1)
    upper = jnp.where(r <= c, 1.0, 0.0).astype(BF16)
    sel_f, sel_e, one_f, one_e = _bias_selectors()
    nh = FOX_HEADS

    @pl.when(pl.program_id(1) == 0)
    def _():
        carry_ref[...] = jnp.zeros_like(carry_ref)

    carry = carry_ref[:, 0:1]
    for sb in range(rng_ref.shape[0]):
        cols = slice(sb * rows, (sb + 1) * rows)
        st = _dot(_stack3(lft_ref[:, cols]), upper)
        cs = ((st[0:nh] + st[nh:2 * nh]) + st[2 * nh:3 * nh]) + carry
        carry = cs[:, rows - 1:rows]
        c2 = cs * LOG2E
        pieces = _stack3(c2)
        f_ref[cols, :] = (_dot(sel_f, pieces) + one_f).T.astype(BF16)
        e_ref[cols, :] = (_dot(sel_e, pieces) + one_e).T.astype(BF16)
        rng_ref[sb] = jnp.concatenate(
            [jnp.broadcast_to(jnp.min(c2, axis=1, keepdims=True), (nh, LANES)),
             jnp.broadcast_to(jnp.max(c2, axis=1, keepdims=True), (nh, LANES))], axis=0)
    carry_ref[...] = jnp.broadcast_to(carry, carry_ref.shape)


def _bias_tiles(lft, *, rows):
    B, H, L = lft.shape
    per_step = 4
    span = per_step * rows
    wide = pl.BlockSpec((None, span, LANES), lambda b, i: (b, i, 0))
    e, f, rng = pl.pallas_call(
        functools.partial(_bias_kernel, rows=rows),
        grid=(B, L // span),
        in_specs=[pl.BlockSpec((None, H, span), lambda b, i: (b, 0, i))],
        out_specs=[wide, wide, pl.BlockSpec((None, per_step, 2 * H, LANES), lambda b, i: (b, i, 0, 0))],
        out_shape=[jax.ShapeDtypeStruct((B, L, LANES), BF16)] * 2
        + [jax.ShapeDtypeStruct((B, L // rows, 2 * H, LANES), F32)],
        scratch_shapes=[pltpu.VMEM((H, LANES), F32)],
        compiler_params=_params(2),
        name="fox_cumsum_bias",
    )(lft)
    return e, f, rng[:, :, :H, 0], rng[:, :, H:, 0]


def _cumsum_short_kernel(lft_ref, ct_ref, *, cols):
    nbatch, nh, length = lft_ref.shape
    r = lax.broadcasted_iota(jnp.int32, (cols, cols), 0)
    c = lax.broadcasted_iota(jnp.int32, (cols, cols), 1)
    upper = jnp.where(r <= c, 1.0, 0.0).astype(BF16)
    carry = [jnp.zeros((nh, 1), F32)] * nbatch
    for c0 in range(0, length, cols):
        span = slice(c0, c0 + cols)
        st = _dot(jnp.concatenate([_stack3(lft_ref[bi, :, span]) for bi in range(nbatch)], axis=0), upper)
        for bi in range(nbatch):
            base = 3 * nh * bi
            cs = ((st[base:base + nh] + st[base + nh:base + 2 * nh]) + st[base + 2 * nh:base + 3 * nh]) + carry[bi]
            ct_ref[bi, :, span] = cs * LOG2E
            carry[bi] = cs[:, cols - 1:cols]


def _cumsum_short(lft, *, cols, per_step):
    B, H, L = lft.shape
    spec = pl.BlockSpec((per_step, H, L), lambda b: (b, 0, 0))
    return pl.pallas_call(
        functools.partial(_cumsum_short_kernel, cols=cols),
        grid=(B // per_step,),
        in_specs=[spec],
        out_specs=spec,
        out_shape=jax.ShapeDtypeStruct((B, H, L), F32),
        compiler_params=_params(1),
        name="fox_cumsum",
    )(lft)


def _skip_plan_kernel(kn_ref, cmin_ref, qn_ref, cmax_ref, knq_ref, first_ref, *, nb):
    j = lax.broadcasted_iota(jnp.int32, (nb, nb), 0).astype(F32)
    q = lax.broadcasted_iota(jnp.int32, (nb, nb), 1).astype(F32)
    rows = []
    for p in range(FOX_PAIRS):
        skippable = None
        for h in (2 * p, 2 * p + 1):
            qn = qn_ref[h:h + 1, :]
            bound = (kn_ref[:, h:h + 1] * qn + (cmax_ref[h:h + 1, :] - cmin_ref[:, h:h + 1])
                     + qn * knq_ref[h:h + 1, :])
            ok = bound < -SKIP_LOG2
            skippable = ok if skippable is None else (skippable & ok)
        must = (j < q) & jnp.logical_not(skippable)
        rows.append(jnp.min(jnp.where(must, j, q), axis=0, keepdims=True))
    first_ref[...] = jnp.concatenate(rows, axis=0).astype(jnp.int32)


def _skip_plan(kn, cmin, qn, cmax):
    B, nb, H = kn.shape
    by_block = pl.BlockSpec((None, nb, H), lambda b: (b, 0, 0))
    by_head = pl.BlockSpec((None, H, nb), lambda b: (b, 0, 0))
    t = lambda a: jnp.swapaxes(a, 1, 2)
    return pl.pallas_call(
        functools.partial(_skip_plan_kernel, nb=nb),
        grid=(B,),
        in_specs=[by_block, by_block, by_head, by_head, by_head],
        out_specs=pl.BlockSpec((None, FOX_PAIRS, nb), lambda b: (b, 0, 0)),
        out_shape=jax.ShapeDtypeStruct((B, FOX_PAIRS, nb), jnp.int32),
        compiler_params=_params(1),
        name="fox_skip_plan",
    )(kn, cmin, t(qn), t(cmax), t(kn))


def _stack_heads(q2):
    lane = lax.broadcasted_iota(jnp.int32, q2.shape, 1)
    zero = jnp.zeros_like(q2)
    return jnp.concatenate(
        [jnp.where(lane < FOX_HEAD_DIM, q2, zero), jnp.where(lane >= FOX_HEAD_DIM, q2, zero)], axis=0)


ATTN_TILE = 256
SUM_ROWS = 16


def _attn_prompt_kernel(n_ref, vq_ref, vj_ref, q_ref, f_ref, k_ref, e_ref, vt_ref, o_ref,
                        m_ref, acc_ref, sa_ref, xa_ref, sb_ref, xb_ref, *, blk, nb):
    b_id = pl.program_id(0)
    p_id = pl.program_id(1)
    hd = FOX_HEAD_DIM
    tw = ATTN_TILE
    per_head = blk // tw
    n_tiles = 2 * per_head

    lane = lax.broadcasted_iota(jnp.int32, (blk, LANES), 1)
    zero = jnp.zeros((blk, LANES), BF16)
    slot_a = 2 * BIAS_SLOT * p_id
    slot_b = slot_a + BIAS_SLOT
    ones_rows = jnp.ones((SUM_ROWS, blk), BF16)

    def query_tiles(qi):
        r0 = pl.multiple_of(qi * blk, blk)
        q2 = q_ref[pl.ds(r0, blk), :]
        f2 = f_ref[pl.ds(r0, blk), :]
        heads = [
            jnp.concatenate([jnp.where(lane < hd, q2, zero),
                             jnp.where((lane >= slot_a) & (lane < slot_b), f2, zero)], axis=1),
            jnp.concatenate([jnp.where(lane >= hd, q2, zero),
                             jnp.where((lane >= slot_b) & (lane < slot_b + BIAS_SLOT), f2, zero)], axis=1),
        ]
        return [heads[t // per_head][(t % per_head) * tw:(t % per_head + 1) * tw] for t in range(n_tiles)]

    def stage(nxt, cur):
        if nxt is not None:
            qi_n, j_n, (sn_ref, xn_ref), diag_n = nxt
            q_tiles = query_tiles(qi_n)
            k0 = pl.multiple_of(j_n * blk, blk)
            kaug = jnp.concatenate([k_ref[pl.ds(k0, blk), :], e_ref[pl.ds(k0, blk), :]], axis=1)
        if cur is not None:
            qi_c, j_c, (sc_ref, xc_ref), diag_c = cur
            vt = vt_ref[j_c]
            v_heads = [jnp.concatenate([vt[h * hd:(h + 1) * hd, :], ones_rows], axis=0) for h in range(2)]
        for t in range(n_tiles):
            c0 = (t % per_head) * tw
            if nxt is not None:
                live = c0 + tw if diag_n else blk
                s = _dot_nt(kaug[:live], q_tiles[t])
                if diag_n:
                    r = lax.broadcasted_iota(jnp.int32, (live, tw), 0)
                    c = lax.broadcasted_iota(jnp.int32, (live, tw), 1) + c0
                    s = jnp.where(r <= c, s, NEG_INF)
                sn_ref[t, 0:live, :] = s
                xn_ref[t] = jnp.max(s, axis=0, keepdims=True)
            if cur is not None:
                live = c0 + tw if diag_c else blk
                idx = qi_c * n_tiles + t
                if diag_c:
                    m_new = xc_ref[t]
                else:
                    m_old = m_ref[idx]
                    m_new = jnp.maximum(m_old, xc_ref[t])
                    alpha = jnp.exp2(m_old - m_new)
                p = jnp.exp2(sc_ref[t, 0:live, :] - m_new)
                m_ref[idx] = m_new
                pv = _dot(v_heads[t // per_head][:, 0:live], p.astype(BF16))
                acc_ref[idx] = pv if diag_c else acc_ref[idx] * alpha + pv

    buf_a = (sa_ref, xa_ref)
    buf_b = (sb_ref, xb_ref)

    diag = lambda n, buf: (n, n, buf, True)
    n_off = n_ref[b_id, p_id]
    off = lambda v, buf: (vq_ref[b_id, p_id, v], vj_ref[b_id, p_id, v], buf, False)

    stage(diag(0, buf_a), None)

    def diag_pair(n, carry):
        i0 = 2 * n
        stage(diag(i0 + 1, buf_b), diag(i0, buf_a))
        stage(diag(i0 + 2, buf_a), diag(i0 + 1, buf_b))
        return carry

    lax.fori_loop(0, nb // 2 - 1, diag_pair, 0)
    stage(diag(nb - 1, buf_b), diag(nb - 2, buf_a))

    @pl.when(n_off > 0)
    def _():
        stage(off(0, buf_a), diag(nb - 1, buf_b))

    @pl.when(n_off == 0)
    def _():
        stage(None, diag(nb - 1, buf_b))

    def off_pair(n, carry):
        v0 = 2 * n
        stage(off(v0 + 1, buf_b), off(v0, buf_a))
        stage(off(v0 + 2, buf_a), off(v0 + 1, buf_b))
        return carry

    lax.fori_loop(0, jnp.maximum(n_off - 1, 0) // 2, off_pair, 0)

    @pl.when((n_off > 0) & (n_off % 2 == 0))
    def _():
        stage(off(n_off - 1, buf_b), off(n_off - 2, buf_a))
        stage(None, off(n_off - 1, buf_b))

    @pl.when(n_off % 2 == 1)
    def _():
        stage(None, off(n_off - 1, buf_a))

    def finish(qi, carry):
        halves = [jnp.concatenate([acc_ref[qi * n_tiles + t, 0:hd, :] / acc_ref[qi * n_tiles + t, hd:hd + 1, :]
                                   for t in range(h * per_head, (h + 1) * per_head)], axis=1)
                  for h in range(2)]
        o_ref[pl.ds(pl.multiple_of(qi * blk, blk), blk), :] = jnp.concatenate(halves, axis=0).T
        return carry

    lax.fori_loop(0, nb, finish, 0)


def _visit_lists(first):
    B, P, nb = first.shape
    vmax = nb * (nb - 1) // 2 + SUBLANES
    q_idx = jnp.arange(nb, dtype=jnp.int32)
    cnt = q_idx - first
    incl = jnp.cumsum(cnt, axis=-1)
    excl = incl - cnt
    v = jnp.arange(vmax, dtype=jnp.int32)
    vq = jnp.minimum(jnp.sum(v[None, None, :, None] >= incl[:, :, None, :], axis=-1), nb - 1).astype(jnp.int32)
    pick = vq[..., None] == q_idx
    take = lambda a: jnp.sum(jnp.where(pick, a[:, :, None, :], 0), axis=-1)
    vj = jnp.clip(take(first) + (v - take(excl)), 0, nb - 1).astype(jnp.int32)
    return incl[..., -1].astype(jnp.int32), vq, vj


def _attn_prompt(first, q, f, kb, e, vt, *, blk):
    B, L, _ = q.shape
    nb = L // blk
    n_tiles = 2 * blk // ATTN_TILE
    n_off, vq, vj = _visit_lists(first)
    by_pair = pl.BlockSpec((None, L, LANES), lambda b, p, *_: (b, 0, p))
    shared = pl.BlockSpec((None, L, LANES), lambda b, p, *_: (b, 0, 0))
    return pl.pallas_call(
        functools.partial(_attn_prompt_kernel, blk=blk, nb=nb),
        grid_spec=pltpu.PrefetchScalarGridSpec(
            num_scalar_prefetch=3,
            grid=(B, FOX_PAIRS),
            in_specs=[by_pair, shared, by_pair, shared,
                      pl.BlockSpec((None, nb, LANES, blk), lambda b, p, *_: (b, 0, p, 0))],
            out_specs=by_pair,
            scratch_shapes=[
                pltpu.VMEM((nb * n_tiles, 1, ATTN_TILE), F32),
                pltpu.VMEM((nb * n_tiles, FOX_HEAD_DIM + SUM_ROWS, ATTN_TILE), F32),
                pltpu.VMEM((n_tiles, blk, ATTN_TILE), F32),
                pltpu.VMEM((n_tiles, 1, ATTN_TILE), F32),
                pltpu.VMEM((n_tiles, blk, ATTN_TILE), F32),
                pltpu.VMEM((n_tiles, 1, ATTN_TILE), F32),
            ],
        ),
        out_shape=jax.ShapeDtypeStruct((B, L, FOX_WIDTH), F32),
        compiler_params=_params(2),
        name="fox_attn_prompt",
    )(n_off, vq, vj, q, f, kb, e, vt)


def _attn_sample_kernel(q_ref, kn_ref, vn_ref, ck_ref, cv_ref, cq_ref, ct_ref, o_ref, *, past, ls):
    hd = FOX_HEAD_DIM
    pad_k = jnp.zeros((LANES - ls, LANES), BF16)
    cqb = cq_ref[...]
    lane = lax.broadcasted_iota(jnp.int32, cqb.shape, 1)
    r = lax.broadcasted_iota(jnp.int32, (2 * ls, LANES), 0) & (ls - 1)
    c = lax.broadcasted_iota(jnp.int32, (2 * ls, LANES), 1)
    lane_o = lax.broadcasted_iota(jnp.int32, (ls, LANES), 1)

    scored = []
    for p in range(FOX_PAIRS):
        cols = slice(p * LANES, (p + 1) * LANES)
        qst = _stack_heads(q_ref[:, cols])
        kc_t = ck_ref[cols, :].astype(BF16)
        kn = jnp.concatenate([kn_ref[:, cols], pad_k], axis=0)
        cq = jnp.concatenate(
            [jnp.sum(jnp.where(lane == 2 * p + h2, cqb, 0.0), axis=1, keepdims=True) for h2 in range(2)],
            axis=0)
        cka = ct_ref[2 * p:2 * p + 1, :]
        ckb = ct_ref[2 * p + 1:2 * p + 2, :]

        def bias(lo, hi):
            return jnp.concatenate([cq[:ls] - cka[:, lo:hi], cq[ls:] - ckb[:, lo:hi]], axis=0)

        s_c = _dot(qst, kc_t) + bias(0, past)
        s_n = _dot_nt(qst, kn) + bias(past, past + LANES)
        s_n = jnp.where(c <= r, s_n, NEG_INF)
        m = jnp.maximum(jnp.max(s_c, axis=1, keepdims=True), jnp.max(s_n, axis=1, keepdims=True))
        scored.append((s_c, s_n, m))

    for p in range(FOX_PAIRS):
        cols = slice(p * LANES, (p + 1) * LANES)
        s_c, s_n, m = scored[p]
        vc_t = cv_ref[cols, :].astype(BF16)
        vn = jnp.concatenate([vn_ref[:, cols].astype(BF16), pad_k], axis=0)
        p_c = jnp.exp2(s_c - m)
        p_n = jnp.exp2(s_n - m)
        l = jnp.sum(p_c, axis=1, keepdims=True) + jnp.sum(p_n, axis=1, keepdims=True)
        o = (_dot_nt(p_c.astype(BF16), vc_t) + _dot(p_n.astype(BF16), vn)) / l
        o_ref[:, cols] = jnp.where(lane_o < hd, o[:ls], o[ls:])


def _attn_sample(q, kb, v, cache_k, cache_v, cq, ct, *, past):
    B, ls, _ = q.shape
    tot = ct.shape[2]
    rows = lambda n: pl.BlockSpec((None, n, FOX_WIDTH), lambda b: (b, 0, 0))
    cache_t = pl.BlockSpec((None, FOX_WIDTH, past), lambda b: (b, 0, 0))
    return pl.pallas_call(
        functools.partial(_attn_sample_kernel, past=past, ls=ls),
        grid=(B,),
        in_specs=[
            rows(ls), rows(ls), rows(ls), cache_t, cache_t,
            pl.BlockSpec((None, ls, FOX_HEADS), lambda b: (b, 0, 0)),
            pl.BlockSpec((None, FOX_HEADS, tot), lambda b: (b, 0, 0)),
        ],
        out_specs=rows(ls),
        out_shape=jax.ShapeDtypeStruct((B, ls, FOX_WIDTH), F32),
        compiler_params=_params(1),
        name="fox_attn_sample",
    )(q, kb, v, cache_k, cache_v, cq, ct)


def _mid_kernel(o_ref, sg_ref, x_ref, wo_ref, g_ref, w_ref, wa1_ref, wa2_ref, ba_ref,
                y_ref, q_ref, k_ref, v_ref, sg2_ref, ga_ref):
    kw = GLA_KEY_WIDTH
    tm = o_ref.shape[0]
    halves = []
    for rows in (slice(0, tm // 2), slice(tm // 2, tm)):
        u = (o_ref[rows, :] * sg_ref[rows, :]).astype(BF16)
        y = x_ref[rows, :] + _dot(u, wo_ref[...])
        y_ref[rows, :] = y
        halves.append(_rms(y, g_ref[...]).astype(BF16))
    hb = jnp.concatenate(halves, axis=0)
    q_ref[...] = _dot_nt(hb, w_ref[0:kw, :]) * (GLA_HEAD_K ** -0.5)
    k_ref[...] = _dot_nt(hb, w_ref[kw:2 * kw, :])
    v_ref[...] = _dot_nt(hb, w_ref[2 * kw:2 * kw + GLA_VAL_WIDTH, :]).astype(BF16)
    gate = _dot_nt(hb, w_ref[2 * kw + GLA_VAL_WIDTH:2 * kw + 2 * GLA_VAL_WIDTH, :])
    sg2_ref[...] = gate * jax.nn.sigmoid(gate)
    a1 = _dot_nt(hb, wa1_ref[...]).astype(BF16)
    a = _dot(a1, wa2_ref[...]) + ba_ref[...]
    ga_ref[...] = _log_sigmoid(a) * (1.0 / GLA_GATE_TEMP)


def _mid(o, sg, x, w_out, g, w_main, w_a1, w_a2, b_a, *, tm):
    T = o.shape[0]
    tok = lambda w: pl.BlockSpec((tm, w), lambda i: (i, 0))
    full = lambda a: pl.BlockSpec(a.shape, lambda i: (0,) * a.ndim)
    return pl.pallas_call(
        _mid_kernel,
        grid=(T // tm,),
        in_specs=[tok(D_MODEL), tok(D_MODEL), tok(D_MODEL), full(w_out), full(g), full(w_main),
                  full(w_a1), full(w_a2), full(b_a)],
        out_specs=[tok(D_MODEL), tok(GLA_KEY_WIDTH), tok(GLA_KEY_WIDTH), tok(GLA_VAL_WIDTH),
                   tok(GLA_VAL_WIDTH), tok(GLA_KEY_WIDTH)],
        out_shape=[
            jax.ShapeDtypeStruct((T, D_MODEL), F32),
            jax.ShapeDtypeStruct((T, GLA_KEY_WIDTH), F32),
            jax.ShapeDtypeStruct((T, GLA_KEY_WIDTH), F32),
            jax.ShapeDtypeStruct((T, GLA_VAL_WIDTH), BF16),
            jax.ShapeDtypeStruct((T, GLA_VAL_WIDTH), F32),
            jax.ShapeDtypeStruct((T, GLA_KEY_WIDTH), F32),
        ],
        compiler_params=_params(1),
        name="fox_out_gla_proj",
    )(o, sg, x, w_out, g, w_main, w_a1, w_a2, b_a)


def _gla_kernel(q_ref, k_ref, v_ref, ga_ref, s0_ref, o_ref, s_ref, *, n_chunks, chained):
    i = pl.program_id(1)
    ch = GLA_CHUNK
    dk, dv = GLA_HEAD_K, GLA_HEAD_V

    @pl.when(i == 0)
    def _():
        s_ref[...] = s0_ref[...]

    t = GLA_GROUP * ch
    r = lax.broadcasted_iota(jnp.int32, (t, t), 0)
    c = lax.broadcasted_iota(jnp.int32, (t, t), 1)
    causal = ((r >> 6) == (c >> 6)) & (r >= c)
    tril = jnp.where(causal, 1.0, 0.0).astype(BF16)
    eye = (lax.broadcasted_iota(jnp.int32, (dk, dk), 0) == lax.broadcasted_iota(jnp.int32, (dk, dk), 1))

    for g0 in range(0, n_chunks, GLA_GROUP):
        grp = slice(g0 * ch, (g0 + GLA_GROUP) * ch)
        b = _tril_cumsum(tril, ga_ref[grp, :])
        lasts = [b[(ci + 1) * ch - 1:(ci + 1) * ch, :] for ci in range(GLA_GROUP)]
        b_last = jnp.concatenate([jnp.broadcast_to(x, (ch, GLA_KEY_WIDTH)) for x in lasts], axis=0)
        kk = k_ref[grp, :]
        qe = (q_ref[grp, :] * jnp.exp(b)).astype(BF16)
        ke = (kk * jnp.exp(-b)).astype(BF16)
        kd = (kk * jnp.exp(b_last - b)).astype(BF16)
        decs = [jnp.exp(x) for x in lasts]

        for h in range(GLA_HEADS):
            ks = slice(h * dk, (h + 1) * dk)
            vs = slice(h * dv, (h + 1) * dv)
            vh = v_ref[grp, vs]
            a = jnp.where(causal, _dot_nt(qe[:, ks], ke[:, ks]), 0.0)
            o_intra = _dot(a.astype(BF16), vh)
            s = s_ref[0, h]
            for ci in range(GLA_GROUP):
                rows = slice(ci * ch, (ci + 1) * ch)
                out_rows = slice((g0 + ci) * ch, (g0 + ci + 1) * ch)
                if not chained:
                    s = s_ref[g0 + ci, h]
                o_ref[out_rows, vs] = o_intra[rows] + _dot(qe[rows, ks], s.astype(BF16))
                dec_col = jnp.sum(jnp.where(eye, jnp.broadcast_to(decs[ci][:, ks], (dk, dk)), 0.0),
                                  axis=1, keepdims=True)
                s = dec_col * s + _dot_tn(kd[rows, ks], vh[rows])
                if not chained:
                    s_ref[g0 + ci, h] = s
            if chained:
                s_ref[0, h] = s


def _gla(q, k, v, ga, s0, *, tc):
    B, L, _ = q.shape
    n_chunks = tc // GLA_CHUNK
    chained = L > GLA_CHUNK
    if chained:
        n_states = 1
    else:
        n_states = n_chunks
        fold = lambda a: a.reshape(B // n_chunks, tc, a.shape[-1])
        q, k, v, ga = fold(q), fold(k), fold(v), fold(ga)
    G, T, _ = q.shape
    tok = lambda w: pl.BlockSpec((None, tc, w), lambda b, i: (b, i, 0))
    st = pl.BlockSpec((n_states, GLA_HEADS, GLA_HEAD_K, GLA_HEAD_V), lambda b, i: (b, 0, 0, 0))
    o, s = pl.pallas_call(
        functools.partial(_gla_kernel, n_chunks=n_chunks, chained=chained),
        grid=(G, T // tc),
        in_specs=[tok(GLA_KEY_WIDTH), tok(GLA_KEY_WIDTH), tok(GLA_VAL_WIDTH), tok(GLA_KEY_WIDTH), st],
        out_specs=[tok(GLA_VAL_WIDTH), st],
        out_shape=[
            jax.ShapeDtypeStruct((G, T, GLA_VAL_WIDTH), F32),
            jax.ShapeDtypeStruct((B, GLA_HEADS, GLA_HEAD_K, GLA_HEAD_V), F32),
        ],
        compiler_params=_params(2),
        name="gla_chunk",
    )(q, k, v, ga, s0)
    return o.reshape(B, L, GLA_VAL_WIDTH), s


def _gla_out_kernel(o_ref, sg_ref, y_ref, go_ref, wo_ref, gf_ref, out_ref, u_ref, *, chunk_major):
    dv = GLA_HEAD_V
    tm = out_ref.shape[0]
    if chunk_major:
        o_all = jnp.concatenate([o_ref[:, t, :] for t in range(o_ref.shape[1])], axis=0)
    n_slabs = SUBLANES
    slab = tm // n_slabs
    for t in range(n_slabs):
        rows = slice(t * slab, (t + 1) * slab)
        o = o_all[rows] if chunk_major else o_ref[rows, :]
        for h in range(GLA_HEADS):
            vs = slice(h * dv, (h + 1) * dv)
            u_ref[rows, vs] = (_rms(o[:, vs], go_ref[...]) * sg_ref[rows, vs]).astype(BF16)
    y = y_ref[...] + _dot(u_ref[...], wo_ref[...])
    out_ref[...] = _rms(y, gf_ref[...])


def _gla_out(o, sg, y, g_o, w_out, g_f):
    B, L, _ = o.shape
    n = L // GLA_CHUNK
    chunk_major = n > 1
    if chunk_major:
        tm = SUBLANES * n
        o = o.reshape(B, n, GLA_CHUNK, D_MODEL)
        o_spec = pl.BlockSpec((None, n, SUBLANES, D_MODEL), lambda b, i: (b, 0, i, 0))
    else:
        tm = PROJ_TM
        flat = lambda a: a.reshape(1, B * L, D_MODEL)
        o, sg, y = flat(o), flat(sg), flat(y)
        o_spec = pl.BlockSpec((None, tm, D_MODEL), lambda b, i: (b, i, 0))
    nb, rows, _ = sg.shape
    tok = pl.BlockSpec((None, tm, D_MODEL), lambda b, i: (b, i, 0))
    full = lambda a: pl.BlockSpec(a.shape, lambda b, i: (0,) * a.ndim)
    return pl.pallas_call(
        functools.partial(_gla_out_kernel, chunk_major=chunk_major),
        grid=(nb, rows // tm),
        in_specs=[o_spec, tok, tok, full(g_o), full(w_out), full(g_f)],
        out_specs=tok,
        out_shape=jax.ShapeDtypeStruct((nb, rows, D_MODEL), F32),
        scratch_shapes=[pltpu.VMEM((tm, D_MODEL), BF16)],
        compiler_params=_params(2),
        name="gla_out",
    )(o, sg, y, g_o, w_out, g_f)


def _pad_cols(w, n):
    return jnp.pad(w, ((0, 0), (0, n - w.shape[1])))


def kernel(x_prompt, x_sample, cache_fox_k, cache_fox_v, cache_fox_logf, state_gla,
           g_norm_fox, w_in_fox, b_fox_f, w_out_fox,
           g_norm_gla, w_in_gla, w_gla_a2, b_gla_a, g_gla_o, w_out_gla, g_final):
    B, L, D = x_prompt.shape
    Bs, Ls, _ = x_sample.shape
    P = cache_fox_k.shape[1]
    H, hd = FOX_HEADS, FOX_HEAD_DIM

    assert D == D_MODEL and L % (2 * GLA_TC) == 0 and L % ATTN_BLOCK == 0
    assert Ls == GLA_CHUNK and (Bs * Ls) % PROJ_TM == 0 and Bs % (GLA_TC // GLA_CHUNK) == 0
    assert (P + LANES) % 3 == 0 and P % LANES == 0 and cache_fox_k.shape[2:] == (H, hd)

    row = lambda a: a.reshape(1, -1).astype(F32)
    pad_rows = lambda w: jnp.pad(w, ((0, LANES - w.shape[0]), (0, 0)))
    w_fox_t = w_in_fox.T
    w_fox = w_fox_t[:4 * FOX_WIDTH].astype(BF16)
    w_fox_f = pad_rows(w_fox_t[4 * FOX_WIDTH:]).astype(BF16)
    b_f = _pad_cols(row(b_fox_f), LANES)
    w_o_fox = w_out_fox.astype(BF16)
    n_main = 2 * GLA_KEY_WIDTH + 2 * GLA_VAL_WIDTH
    w_gla_t = w_in_gla.T
    w_gla = w_gla_t[:n_main].astype(BF16)
    w_a1 = pad_rows(w_gla_t[n_main:]).astype(BF16)
    w_a2 = pad_rows(w_gla_a2).astype(BF16)
    w_o_gla = w_out_gla.astype(BF16)

    q_p, k_p, v_p, kb_p, sg_p, lft_p, vt_p, nrm_p = _fox_proj(
        x_prompt, row(g_norm_fox), w_fox, w_fox_f, b_f, tm=ATTN_BLOCK, want_vt=True)
    e_p, f_p, cmin_p, cmax_p = _bias_tiles(lft_p, rows=ATTN_BLOCK)
    first_p = _skip_plan(nrm_p[:, :, 0, :H], cmin_p, nrm_p[:, :, 1, :H], cmax_p)
    o_p = _attn_prompt(first_p, q_p, f_p, kb_p, e_p, vt_p, blk=ATTN_BLOCK)
    lf_p = jnp.swapaxes(lft_p, 1, 2)

    xs = x_sample.reshape(1, Bs * Ls, D)
    q_s, k_s, v_s, kb_s, sg_s, lft_s = _fox_proj(
        xs, row(g_norm_fox), w_fox, w_fox_f, b_f, tm=PROJ_TM, want_vt=False)
    sh = lambda a: a.reshape(Bs, Ls, a.shape[-1])
    q_s, k_s, v_s, kb_s = sh(q_s), sh(k_s), sh(v_s), sh(kb_s)
    lft_s = jnp.swapaxes(lft_s.reshape(H, Bs, Ls), 0, 1)
    lf_s = jnp.swapaxes(lft_s, 1, 2)
    lft_all = jnp.concatenate(
        [jnp.swapaxes(cache_fox_logf.astype(F32), 1, 2), lft_s, jnp.zeros((Bs, H, LANES - Ls), F32)], axis=2)
    ct_s = _cumsum_short(lft_all, cols=(P + LANES) // 3, per_step=4)
    cache_t = lambda a: jnp.transpose(a, (0, 2, 3, 1)).reshape(Bs, FOX_WIDTH, P)
    o_s = _attn_sample(q_s, kb_s, v_s, cache_t(cache_fox_k), cache_t(cache_fox_v),
                       jnp.swapaxes(ct_s[:, :, P:P + Ls], 1, 2), ct_s, past=P)

    def gla_layer(o, sg, x, s0, tc):
        b, l, _ = x.shape
        flat = lambda a: a.reshape(b * l, a.shape[-1])
        y1, q, k, v, sg2, ga = _mid(flat(o), flat(sg), flat(x), w_o_fox, row(g_norm_gla), w_gla,
                                    w_a1, w_a2, row(b_gla_a), tm=PROJ_TM)
        un = lambda a: a.reshape(b, l, a.shape[-1])
        og, s_out = _gla(un(q), un(k), un(v), un(ga), s0, tc=tc)
        y = _gla_out(og, un(sg2), un(y1), row(g_gla_o), w_o_gla, row(g_final))
        return y.reshape(b, l, D), s_out

    s0_p = jnp.zeros((B, GLA_HEADS, GLA_HEAD_K, GLA_HEAD_V), F32)
    y_p, s_p = gla_layer(o_p, sg_p, x_prompt, s0_p, GLA_TC)
    y_s, s_s = gla_layer(o_s, sg_s.reshape(Bs, Ls, D), x_sample, state_gla.astype(F32), GLA_TC)

    return (y_p, y_s,
            k_p.reshape(B, L, H, hd), v_p.reshape(B, L, H, hd), lf_p, s_p.astype(state_gla.dtype),
            k_s.reshape(Bs, Ls, H, hd), v_s.reshape(Bs, Ls, H, hd), lf_s, s_s.astype(state_gla.dtype))
```
